```python
import jax, jax.numpy as jnp
from jax import lax
import numpy as np

D_MODEL = 2048
BATCH = 4
SEQ = 4096
DEPTH = 1

D_MIX = D_MODEL
D_A = D_MIX // 2
CHUNK = 128
A_GROUPS = 8
A_GROUP_W = D_A // A_GROUPS
D_B = D_MIX - D_A
HEAD_DIM = 64
N_Q_HEADS = D_B // HEAD_DIM
N_KV_HEADS = 4
Q_PER_KV = N_Q_HEADS // N_KV_HEADS
D_KV = N_KV_HEADS * HEAD_DIM
WINDOW = 128
BLOCK = WINDOW
ROPE_THETA = 10000.0
NORM_EPS = 1e-5
SPLIT_SIZES = (D_A, D_A, D_A, D_B, D_KV, D_KV, D_B)
D_IN = sum(SPLIT_SIZES)

kernel_name = "hymba_gmlp_swa_sink_adaln"


def rms_norm(x, g):
    xf = x.astype(jnp.float32)
    y = xf * lax.rsqrt(jnp.mean(xf * xf, axis=-1, keepdims=True) + NORM_EPS)
    return (y * g.astype(jnp.float32)).astype(x.dtype)


def layer_norm(x, g, b):
    xf = x.astype(jnp.float32)
    mu = jnp.mean(xf, axis=-1, keepdims=True)
    xc = xf - mu
    var = jnp.mean(xc * xc, axis=-1, keepdims=True)
    y = xc * lax.rsqrt(var + NORM_EPS)
    return (y * g.astype(jnp.float32) + b.astype(jnp.float32)).astype(x.dtype)


def modulate(h, shift, scale):
    return h * (1.0 + scale[:, None, :]) + shift[:, None, :]


def rope_tables(seq, dtype):
    inv_freq = ROPE_THETA ** (-jnp.arange(0, HEAD_DIM, 2, dtype=jnp.float32) / HEAD_DIM)
    ang = jnp.arange(seq, dtype=jnp.float32)[:, None] * inv_freq[None, :]
    return jnp.cos(ang).astype(dtype), jnp.sin(ang).astype(dtype)


def apply_rope(x, cos, sin):
    x1, x2 = jnp.split(x, 2, axis=-1)
    c = cos[None, :, None, :]
    s = sin[None, :, None, :]
    return jnp.concatenate([x1 * c - x2 * s, x2 * c + x1 * s], axis=-1)


def chunked_spatial_gating(u, v, ln_g, ln_b, w_s, b_s):
    bsz, seq, _ = v.shape
    n_chunks = seq // CHUNK
    v = layer_norm(v, ln_g, ln_b)
    vg = v.reshape(bsz, n_chunks, CHUNK, A_GROUPS, A_GROUP_W)
    w = w_s * jnp.tril(jnp.ones((CHUNK, CHUNK), w_s.dtype))[None]
    s = jnp.einsum('gts,bcsgd->bctgd', w, vg) + b_s.T[None, None, :, :, None]
    return u * s.reshape(bsz, seq, D_A)


def sliding_window_sink_attention(q, k, v, sinks):
    bsz, seq = q.shape[0], q.shape[1]
    nb = seq // BLOCK
    qb = q.reshape(bsz, nb, BLOCK, N_KV_HEADS, Q_PER_KV, HEAD_DIM)
    kb = k.reshape(bsz, nb, BLOCK, N_KV_HEADS, HEAD_DIM)
    vb = v.reshape(bsz, nb, BLOCK, N_KV_HEADS, HEAD_DIM)
    pad = ((0, 0), (1, 0), (0, 0), (0, 0), (0, 0))
    k_band = jnp.concatenate([jnp.pad(kb, pad)[:, :-1], kb], axis=2)
    v_band = jnp.concatenate([jnp.pad(vb, pad)[:, :-1], vb], axis=2)
    scale = HEAD_DIM ** -0.5
    scores = jnp.einsum('bnqkgd,bnjkd->bnkgqj', qb, k_band).astype(jnp.float32) * scale
    blk = jnp.arange(nb)[:, None]
    qpos = blk * BLOCK + jnp.arange(BLOCK)[None, :]
    kpos = (blk - 1) * BLOCK + jnp.arange(2 * BLOCK)[None, :]
    rel = qpos[:, :, None] - kpos[:, None, :]
    valid = (rel >= 0) & (rel < WINDOW) & (kpos[:, None, :] >= 0)
    scores = jnp.where(valid[None, :, None, None, :, :], scores, -jnp.inf)
    sink = sinks.astype(jnp.float32).reshape(N_KV_HEADS, Q_PER_KV)[None, None, :, :, None, None]
    m = jnp.maximum(jnp.max(scores, axis=-1, keepdims=True), sink)
    p = jnp.exp(scores - m)
    denom = jnp.sum(p, axis=-1, keepdims=True) + jnp.exp(sink - m)
    probs = (p / denom).astype(v.dtype)
    out = jnp.einsum('bnkgqj,bnjkd->bnqkgd', probs, v_band)
    return out.reshape(bsz, seq, N_Q_HEADS * HEAD_DIM)


def setup_inputs(seed: int = 0) -> dict:
    key = jax.random.key(seed)
    ks = jax.random.split(key, 16)
    f32 = jnp.float32
    ada_std = 0.2 * D_MODEL ** -0.5
    return {
        "x": jax.random.normal(ks[0], (BATCH, SEQ, D_MODEL), f32),
        "c": jax.random.normal(ks[1], (BATCH, D_MODEL), f32),
        "w_ada": jax.random.normal(ks[2], (DEPTH, D_MODEL, 3 * D_MODEL), f32) * ada_std,
        "b_ada": jax.random.normal(ks[3], (DEPTH, 3 * D_MODEL), f32) * 0.02,
        "norm_g": 1.0 + 0.02 * jax.random.normal(ks[4], (DEPTH, D_MODEL), f32),
        "w_in": jax.random.normal(ks[5], (DEPTH, D_MODEL, D_IN), f32) * D_MODEL ** -0.5,
        "ln_v_g": 1.0 + 0.02 * jax.random.normal(ks[6], (DEPTH, D_A), f32),
        "ln_v_b": 0.02 * jax.random.normal(ks[7], (DEPTH, D_A), f32),
        "w_spatial": jax.random.normal(ks[8], (DEPTH, A_GROUPS, CHUNK, CHUNK), f32) * (0.5 * CHUNK ** -0.5),
        "b_spatial": 1.0 + 0.02 * jax.random.normal(ks[9], (DEPTH, A_GROUPS, CHUNK), f32),
        "sinks": 0.5 * jax.random.normal(ks[10], (DEPTH, N_Q_HEADS), f32),
        "w_out": jax.random.normal(ks[11], (DEPTH, D_MIX, D_MODEL), f32) * D_MIX ** -0.5,
        "w_ada_final": jax.random.normal(ks[12], (D_MODEL, 2 * D_MODEL), f32) * ada_std,
        "b_ada_final": jax.random.normal(ks[13], (2 * D_MODEL,), f32) * 0.02,
        "final_norm_g": 1.0 + 0.02 * jax.random.normal(ks[14], (D_MODEL,), f32),
    }


def reference(x, c, w_ada, b_ada, norm_g, w_in, ln_v_g, ln_v_b, w_spatial, b_spatial,
              sinks, w_out, w_ada_final, b_ada_final, final_norm_g):
    bsz, seq, _ = x.shape
    c_act = jax.nn.silu(c)
    cos, sin = rope_tables(seq, x.dtype)
    offs = np.cumsum((0,) + SPLIT_SIZES)[:-1].tolist()[1:]
    for l in range(DEPTH):
        mod = c_act @ w_ada[l] + b_ada[l]
        shift, scale, gate = jnp.split(mod, 3, axis=-1)
        h = modulate(rms_norm(x, norm_g[l]), shift, scale)
        proj = h @ w_in[l]
        u_a, v_a, z_a, q, k, v, z_b = jnp.split(proj, offs, axis=-1)
        y_a = chunked_spatial_gating(u_a, v_a, ln_v_g[l], ln_v_b[l], w_spatial[l], b_spatial[l])
        q = apply_rope(q.reshape(bsz, seq, N_Q_HEADS, HEAD_DIM), cos, sin)
        k = apply_rope(k.reshape(bsz, seq, N_KV_HEADS, HEAD_DIM), cos, sin)
        v = v.reshape(bsz, seq, N_KV_HEADS, HEAD_DIM)
        y_b = sliding_window_sink_attention(q, k, v, sinks[l])
        y = jnp.concatenate([y_a * jax.nn.silu(z_a), y_b * jax.nn.silu(z_b)], axis=-1)
        x = x + gate[:, None, :] * (y @ w_out[l])
    mod_f = c_act @ w_ada_final + b_ada_final
    shift_f, scale_f = jnp.split(mod_f, 2, axis=-1)
    return modulate(rms_norm(x, final_norm_g), shift_f, scale_f)
```

```python
import functools

import jax
import jax.numpy as jnp
from jax import lax
from jax.experimental import pallas as pl
from jax.experimental.pallas import tpu as pltpu

F32 = jnp.float32
BF16 = jnp.bfloat16

CHUNK = 128
A_GROUPS = 8
A_GROUP_W = 128
HEAD_DIM = 64
N_KV_HEADS = 4
Q_PER_KV = 4
ROPE_THETA = 10000.0
NORM_EPS = 1e-5

LANES = 128
HEADS_PER_VREG = LANES // HEAD_DIM

TM_IN = 256
TM_OUT = 512
TN_DOT = 512
TN_ADA = 512
VMEM_LIMIT_PROJ = 56 * 1024 * 1024


def _silu(z):
    return z * (1.0 / (1.0 + jnp.exp(-z)))


def _ada_kernel(c_ref, w_ref, b_ref, o_ref):
    ca = _silu(c_ref[...])
    acc = jnp.dot(ca.astype(BF16), w_ref[...].astype(BF16), preferred_element_type=F32)
    o_ref[...] = acc + b_ref[...]


def _ada_mod(c, w, b):
    bsz, d = c.shape
    n = w.shape[-1]
    w3 = w.reshape((1,) + w.shape[-2:])
    b2 = b.reshape(1, n)
    return pl.pallas_call(
        _ada_kernel,
        grid=(n // TN_ADA,),
        in_specs=[
            pl.BlockSpec((bsz, d), lambda j: (0, 0)),
            pl.BlockSpec((None, d, TN_ADA), lambda j: (0, 0, j)),
            pl.BlockSpec((1, TN_ADA), lambda j: (0, j)),
        ],
        out_specs=pl.BlockSpec((bsz, TN_ADA), lambda j: (0, j)),
        out_shape=jax.ShapeDtypeStruct((bsz, n), F32),
        compiler_params=pltpu.CompilerParams(dimension_semantics=("arbitrary",)),
        name="ada_mod",
    )(c, w3, b2)


def _inproj_kernel(x_ref, mod_ref, g_ref, w_ref, o_ref, *, d_model, d_in):
    x = x_ref[...]
    ms = jnp.mean(x * x, axis=-1, keepdims=True)
    y = x * lax.rsqrt(ms + NORM_EPS) * g_ref[...]
    shift = mod_ref[:, 0:d_model]
    scale = mod_ref[:, d_model:2 * d_model]
    h = (y * (1.0 + scale) + shift).astype(BF16)
    for n0 in range(0, d_in, TN_DOT):
        acc = jnp.dot(h, w_ref[:, n0:n0 + TN_DOT], preferred_element_type=F32)
        o_ref[:, n0:n0 + TN_DOT] = acc.astype(BF16)


def _inproj(x2, mod3, norm_g, w_in_bf16, seq):
    rows, d_model = x2.shape
    d_in = w_in_bf16.shape[-1]
    tiles_per_batch = seq // TM_IN
    return pl.pallas_call(
        functools.partial(_inproj_kernel, d_model=d_model, d_in=d_in),
        grid=(rows // TM_IN,),
        in_specs=[
            pl.BlockSpec((TM_IN, d_model), lambda i: (i, 0)),
            pl.BlockSpec((None, 1, mod3.shape[-1]), lambda i: (i // tiles_per_batch, 0, 0)),
            pl.BlockSpec((1, d_model), lambda i: (0, 0)),
            pl.BlockSpec((d_model, d_in), lambda i: (0, 0), pipeline_mode=pl.Buffered(1)),
        ],
        out_specs=pl.BlockSpec((TM_IN, d_in), lambda i: (i, 0)),
        out_shape=jax.ShapeDtypeStruct((rows, d_in), BF16),
        compiler_params=pltpu.CompilerParams(
            dimension_semantics=("arbitrary",), vmem_limit_bytes=VMEM_LIMIT_PROJ),
        name="in_proj",
    )(x2, mod3, norm_g.reshape(1, d_model), w_in_bf16)


def _mixer_kernel(sink_ref, proj_ref, cos_ref, sin_ref, lng_ref, lnb_ref, ws_ref, bst_ref,
                  y_ref, kprev_ref, vprev_ref, *, d_a, d_b, d_kv):
    blk = pl.program_id(1)
    o_u, o_v, o_za = 0, d_a, 2 * d_a
    o_q = 3 * d_a
    o_k = o_q + d_b
    o_vv = o_k + d_kv
    o_zb = o_vv + d_kv

    @pl.when(blk == 0)
    def _():
        kprev_ref[...] = jnp.zeros_like(kprev_ref)
        vprev_ref[...] = jnp.zeros_like(vprev_ref)

    row = lax.broadcasted_iota(jnp.int32, (CHUNK, CHUNK), 0)
    col = lax.broadcasted_iota(jnp.int32, (CHUNK, CHUNK), 1)
    causal = col <= row

    va = proj_ref[:, o_v:o_v + d_a].astype(F32)
    mu = jnp.mean(va, axis=-1, keepdims=True)
    vc = va - mu
    var = jnp.mean(vc * vc, axis=-1, keepdims=True)
    vn = (vc * lax.rsqrt(var + NORM_EPS) * lng_ref[...] + lnb_ref[...]).astype(BF16)
    tril = causal.astype(F32)
    for g in range(A_GROUPS):
        sl = slice(g * A_GROUP_W, (g + 1) * A_GROUP_W)
        w = (ws_ref[g] * tril).astype(BF16)
        s = jnp.dot(w, vn[:, sl], preferred_element_type=F32) + bst_ref[:, g:g + 1]
        u = proj_ref[:, o_u + g * A_GROUP_W:o_u + (g + 1) * A_GROUP_W].astype(F32)
        za = proj_ref[:, o_za + g * A_GROUP_W:o_za + (g + 1) * A_GROUP_W].astype(F32)
        y_ref[:, sl] = (u * s * _silu(za)).astype(BF16)

    cos = cos_ref[...]
    sin = sin_ref[...]
    first_half = (col & (HEAD_DIM - 1)) < (HEAD_DIM // 2)

    def rope(xv):
        rot = jnp.where(first_half,
                        pltpu.roll(xv, LANES - HEAD_DIM // 2, 1),
                        pltpu.roll(xv, HEAD_DIM // 2, 1))
        return xv * cos + rot * sin

    low_half = col < HEAD_DIM
    ke_cur = [None] * (N_KV_HEADS * HEADS_PER_VREG)
    ve_cur = [None] * (N_KV_HEADS * HEADS_PER_VREG)
    for c in range(d_kv // LANES):
        kc = rope(proj_ref[:, o_k + c * LANES:o_k + (c + 1) * LANES].astype(F32))
        vcol = proj_ref[:, o_vv + c * LANES:o_vv + (c + 1) * LANES].astype(F32)
        kc_sw = pltpu.roll(kc, HEAD_DIM, 1)
        vcol_sw = pltpu.roll(vcol, HEAD_DIM, 1)
        for j in range(HEADS_PER_VREG):
            kv_head = c * HEADS_PER_VREG + j
            for o in range(HEADS_PER_VREG):
                mask = low_half if o == 0 else jnp.logical_not(low_half)
                e = kv_head * HEADS_PER_VREG + o
                ke_cur[e] = jnp.where(mask, kc if o == j else kc_sw, 0.0).astype(BF16)
                ve_cur[e] = jnp.where(mask, vcol if o == j else vcol_sw, 0.0).astype(BF16)

    scale = HEAD_DIM ** -0.5
    has_prev = blk > 0
    for p in range(d_b // LANES):
        kv_head = (p * HEADS_PER_VREG) // Q_PER_KV
        q2 = (rope(proj_ref[:, o_q + p * LANES:o_q + (p + 1) * LANES].astype(F32)) * scale
              ).astype(BF16)
        probs = []
        vband = []
        for o in range(HEADS_PER_VREG):
            h = p * HEADS_PER_VREG + o
            e = kv_head * HEADS_PER_VREG + o
            kband = jnp.concatenate([kprev_ref[e], ke_cur[e]], axis=0)
            s = lax.dot_general(q2, kband, (((1,), (1,)), ((), ())),
                                preferred_element_type=F32)
            s_prev = jnp.where(has_prev, s[:, :CHUNK], -jnp.inf)
            comb = jnp.where(causal, s[:, CHUNK:], s_prev)
            sink = sink_ref[h]
            m = jnp.maximum(jnp.max(comb, axis=-1, keepdims=True), sink)
            pexp = jnp.exp(comb - m)
            denom = jnp.sum(pexp, axis=-1, keepdims=True) + jnp.exp(sink - m)
            pn = pexp * (1.0 / denom)
            probs.append(jnp.where(causal, 0.0, pn).astype(BF16))
            probs.append(jnp.where(causal, pn, 0.0).astype(BF16))
            vband.append(vprev_ref[e])
            vband.append(ve_cur[e])
        out = jnp.dot(jnp.concatenate(probs, axis=1), jnp.concatenate(vband, axis=0),
                      preferred_element_type=F32)
        zb = proj_ref[:, o_zb + p * LANES:o_zb + (p + 1) * LANES].astype(F32)
        y_ref[:, d_a + p * LANES:d_a + (p + 1) * LANES] = (out * _silu(zb)).astype(BF16)

    for e in range(N_KV_HEADS * HEADS_PER_VREG):
        kprev_ref[e] = ke_cur[e]
        vprev_ref[e] = ve_cur[e]


def _mixer(proj3, sinks, cos_t, sin_t, ln_g, ln_b, w_sp, b_sp_t, d_a, d_b, d_kv):
    bsz, seq, d_in = proj3.shape
    n_exp = N_KV_HEADS * HEADS_PER_VREG
    return pl.pallas_call(
        functools.partial(_mixer_kernel, d_a=d_a, d_b=d_b, d_kv=d_kv),
        grid=(bsz, seq // CHUNK),
        in_specs=[
            pl.BlockSpec(memory_space=pltpu.SMEM),
            pl.BlockSpec((None, CHUNK, d_in), lambda b, i: (b, i, 0)),
            pl.BlockSpec((CHUNK, LANES), lambda b, i: (i, 0)),
            pl.BlockSpec((CHUNK, LANES), lambda b, i: (i, 0)),
            pl.BlockSpec((1, d_a), lambda b, i: (0, 0)),
            pl.BlockSpec((1, d_a), lambda b, i: (0, 0)),
            pl.BlockSpec((A_GROUPS, CHUNK, CHUNK), lambda b, i: (0, 0, 0)),
            pl.BlockSpec((CHUNK, A_GROUPS), lambda b, i: (0, 0)),
        ],
        out_specs=pl.BlockSpec((None, CHUNK, d_a + d_b), lambda b, i: (b, i, 0)),
        out_shape=jax.ShapeDtypeStruct((bsz, seq, d_a + d_b), BF16),
        scratch_shapes=[
            pltpu.VMEM((n_exp, CHUNK, LANES), BF16),
            pltpu.VMEM((n_exp, CHUNK, LANES), BF16),
        ],
        compiler_params=pltpu.CompilerParams(dimension_semantics=("arbitrary", "arbitrary")),
        name="mixers",
    )(sinks, proj3, cos_t, sin_t, ln_g, ln_b, w_sp, b_sp_t)


def _outproj_kernel(y_ref, x_ref, mod_ref, modf_ref, g_ref, w_ref, o_ref, *, d_model):
    gate = mod_ref[:, 2 * d_model:3 * d_model]
    y = y_ref[...]
    ssq = jnp.zeros((y.shape[0], 1), F32)
    for n0 in range(0, d_model, TN_DOT):
        sl = slice(n0, n0 + TN_DOT)
        acc = jnp.dot(y, w_ref[:, sl], preferred_element_type=F32)
        xn = x_ref[:, sl] + gate[:, sl] * acc
        ssq = ssq + jnp.sum(xn * xn, axis=-1, keepdims=True)
        o_ref[:, sl] = xn
    inv = lax.rsqrt(ssq * (1.0 / d_model) + NORM_EPS)
    shift_f = modf_ref[:, 0:d_model]
    scale_f = modf_ref[:, d_model:2 * d_model]
    o_ref[...] = (o_ref[...] * inv * g_ref[...]) * (1.0 + scale_f) + shift_f


def _outproj(y2, x2, mod3, modf3, final_g, w_out_bf16, seq):
    rows, d_model = x2.shape
    d_mix = y2.shape[-1]
    tiles_per_batch = seq // TM_OUT
    return pl.pallas_call(
        functools.partial(_outproj_kernel, d_model=d_model),
        grid=(rows // TM_OUT,),
        in_specs=[
            pl.BlockSpec((TM_OUT, d_mix), lambda i: (i, 0)),
            pl.BlockSpec((TM_OUT, d_model), lambda i: (i, 0)),
            pl.BlockSpec((None, 1, mod3.shape[-1]), lambda i: (i // tiles_per_batch, 0, 0)),
            pl.BlockSpec((None, 1, modf3.shape[-1]), lambda i: (i // tiles_per_batch, 0, 0)),
            pl.BlockSpec((1, d_model), lambda i: (0, 0)),
            pl.BlockSpec((d_mix, d_model), lambda i: (0, 0), pipeline_mode=pl.Buffered(1)),
        ],
        out_specs=pl.BlockSpec((TM_OUT, d_model), lambda i: (i, 0)),
        out_shape=jax.ShapeDtypeStruct((rows, d_model), F32),
        compiler_params=pltpu.CompilerParams(
            dimension_semantics=("arbitrary",), vmem_limit_bytes=VMEM_LIMIT_PROJ),
        name="out_proj",
    )(y2, x2, mod3, modf3, final_g.reshape(1, d_model), w_out_bf16)


def _rope_tables(seq):
    half = HEAD_DIM // 2
    inv_freq = ROPE_THETA ** (-jnp.arange(0, HEAD_DIM, 2, dtype=F32) / HEAD_DIM)
    ang = jnp.arange(seq, dtype=F32)[:, None] * inv_freq[None, :]
    cos = jnp.cos(ang)
    sin = jnp.sin(ang)
    cos_t = jnp.tile(cos, (1, LANES // half))
    sin_t = jnp.tile(jnp.concatenate([-sin, sin], axis=1), (1, HEADS_PER_VREG))
    return cos_t, sin_t


def kernel(x, c, w_ada, b_ada, norm_g, w_in, ln_v_g, ln_v_b, w_spatial, b_spatial, sinks,
           w_out, w_ada_final, b_ada_final, final_norm_g):
    bsz, seq, d_model = x.shape
    assert w_ada.shape[0] == 1, "single-layer stack"
    d_a = ln_v_g.shape[-1]
    d_mix = w_out.shape[-2]
    d_b = d_mix - d_a
    d_kv = N_KV_HEADS * HEAD_DIM
    d_in = w_in.shape[-1]
    assert d_in == 3 * d_a + 2 * d_b + 2 * d_kv
    assert d_a == A_GROUPS * A_GROUP_W and d_b == N_KV_HEADS * Q_PER_KV * HEAD_DIM

    x2 = x.reshape(bsz * seq, d_model)
    mod = _ada_mod(c, w_ada, b_ada)
    mod_f = _ada_mod(c, w_ada_final, b_ada_final)
    mod3 = mod.reshape(bsz, 1, 3 * d_model)
    modf3 = mod_f.reshape(bsz, 1, 2 * d_model)

    proj = _inproj(x2, mod3, norm_g, w_in.reshape(d_model, d_in).astype(BF16), seq)
    cos_t, sin_t = _rope_tables(seq)
    y = _mixer(proj.reshape(bsz, seq, d_in), sinks.reshape(-1), cos_t, sin_t,
               ln_v_g.reshape(1, d_a), ln_v_b.reshape(1, d_a),
               w_spatial.reshape(A_GROUPS, CHUNK, CHUNK),
               b_spatial.reshape(A_GROUPS, CHUNK).T, d_a, d_b, d_kv)
    out = _outproj(y.reshape(bsz * seq, d_mix), x2, mod3, modf3, final_norm_g,
                   w_out.reshape(d_mix, d_model).astype(BF16), seq)
    return out.reshape(bsz, seq, d_model)
```

```python
import functools

import jax
import jax.numpy as jnp
from jax import lax
from jax.experimental import pallas as pl
from jax.experimental.pallas import tpu as pltpu

F32 = jnp.float32
BF16 = jnp.bfloat16

CHUNK = 128
A_GROUPS = 8
A_GROUP_W = 128
HEAD_DIM = 64
N_KV_HEADS = 4
Q_PER_KV = 4
ROPE_THETA = 10000.0
NORM_EPS = 1e-5

LANES = 128
HEADS_PER_VREG = LANES // HEAD_DIM
N_EXP = N_KV_HEADS * HEADS_PER_VREG

T_BLK = 256
TN_DOT = 512
PAIRS_PER_SLOT = 2
TN_ADA = 512
VMEM_LIMIT_LAYER = 60 * 1024 * 1024


def _silu(z):
    hz = 0.5 * z
    return hz + hz * jnp.tanh(hz)


def _ada_kernel(c_ref, w_ref, b_ref, o_ref):
    c = c_ref[...]
    ca = c * (1.0 / (1.0 + jnp.exp(-c)))
    acc = jnp.dot(ca.astype(BF16), w_ref[...].astype(BF16), preferred_element_type=F32)
    o_ref[...] = acc + b_ref[...]


def _ada_mod(c, w, b):
    bsz, d = c.shape
    n = w.shape[-1]
    w3 = w.reshape((1,) + w.shape[-2:])
    b2 = b.reshape(1, n)
    return pl.pallas_call(
        _ada_kernel,
        grid=(n // TN_ADA,),
        in_specs=[
            pl.BlockSpec((bsz, d), lambda j: (0, 0)),
            pl.BlockSpec((None, d, TN_ADA), lambda j: (0, 0, j)),
            pl.BlockSpec((1, TN_ADA), lambda j: (0, j)),
        ],
        out_specs=pl.BlockSpec((bsz, TN_ADA), lambda j: (0, j)),
        out_shape=jax.ShapeDtypeStruct((bsz, n), F32),
        compiler_params=pltpu.CompilerParams(dimension_semantics=("arbitrary",)),
        name="ada_mod",
    )(c, w3, b2)


class _MixerBlock:
    def __init__(self, load, store, cos, sin, lng, lnb, ws_ref, bst, sink_ref, kprev, vprev,
                 has_prev, *, d_a, d_b, d_kv):
        self.load, self.store = load, store
        self.cos, self.sin, self.lng, self.lnb = cos, sin, lng, lnb
        self.ws_ref, self.bst, self.sink_ref = ws_ref, bst, sink_ref
        self.kprev, self.vprev, self.has_prev = kprev, vprev, has_prev
        self.d_a, self.d_b, self.d_kv = d_a, d_b, d_kv
        self.o_u, self.o_v, self.o_za = 0, d_a, 2 * d_a
        self.o_q = 3 * d_a
        self.o_k = self.o_q + d_b
        self.o_vv = self.o_k + d_kv
        self.o_zb = self.o_vv + d_kv
        row = lax.broadcasted_iota(jnp.int32, (CHUNK, CHUNK), 0)
        col = lax.broadcasted_iota(jnp.int32, (CHUNK, CHUNK), 1)
        self.col = col
        self.causal = col <= row
        self.first_half = (col & (HEAD_DIM - 1)) < (HEAD_DIM // 2)
        self.probs = {}

    def _rope(self, xv):
        rot = jnp.where(self.first_half,
                        pltpu.roll(xv, LANES - HEAD_DIM // 2, 1),
                        pltpu.roll(xv, HEAD_DIM // 2, 1))
        return xv * self.cos + rot * self.sin

    def prep(self):
        va = self.load(self.o_v, self.o_v + self.d_a).astype(F32)
        mu = jnp.mean(va, axis=-1, keepdims=True)
        vc = va - mu
        var = jnp.mean(vc * vc, axis=-1, keepdims=True)
        self.vn = (vc * lax.rsqrt(var + NORM_EPS) * self.lng + self.lnb).astype(BF16)
        self.tril = self.causal.astype(F32)
        low_half = self.col < HEAD_DIM
        self.ke = [None] * N_EXP
        self.ve = [None] * N_EXP
        for c in range(self.d_kv // LANES):
            kc = self._rope(self.load(self.o_k + c * LANES, self.o_k + (c + 1) * LANES)
                            .astype(F32))
            vcol = self.load(self.o_vv + c * LANES, self.o_vv + (c + 1) * LANES).astype(F32)
            kc_sw = pltpu.roll(kc, HEAD_DIM, 1)
            vcol_sw = pltpu.roll(vcol, HEAD_DIM, 1)
            for j in range(HEADS_PER_VREG):
                kv_head = c * HEADS_PER_VREG + j
                for o in range(HEADS_PER_VREG):
                    mask = low_half if o == 0 else jnp.logical_not(low_half)
                    e = kv_head * HEADS_PER_VREG + o
                    self.ke[e] = jnp.where(mask, kc if o == j else kc_sw, 0.0).astype(BF16)
                    self.ve[e] = jnp.where(mask, vcol if o == j else vcol_sw, 0.0).astype(BF16)

    def group(self, g):
        c0 = g * A_GROUP_W
        w = (self.ws_ref[g] * self.tril).astype(BF16)
        s = (jnp.dot(w, self.vn[:, c0:c0 + A_GROUP_W], preferred_element_type=F32)
             + self.bst[:, g:g + 1])
        u = self.load(self.o_u + c0, self.o_u + c0 + A_GROUP_W).astype(F32)
        za = self.load(self.o_za + c0, self.o_za + c0 + A_GROUP_W).astype(F32)
        self.store(c0, (u * s * _silu(za)).astype(BF16))

    def scores(self, p):
        kv_head = (p * HEADS_PER_VREG) // Q_PER_KV
        kprev = self.kprev()
        q2 = (self._rope(self.load(self.o_q + p * LANES, self.o_q + (p + 1) * LANES)
                         .astype(F32)) * (HEAD_DIM ** -0.5)).astype(BF16)
        probs = []
        for o in range(HEADS_PER_VREG):
            h = p * HEADS_PER_VREG + o
            e = kv_head * HEADS_PER_VREG + o
            kband = jnp.concatenate([kprev[e], self.ke[e]], axis=0)
            s = lax.dot_general(q2, kband, (((1,), (1,)), ((), ())),
                                preferred_element_type=F32)
            s_prev = s[:, :CHUNK]
            if self.has_prev is not True:
                s_prev = jnp.where(self.has_prev, s_prev, -jnp.inf)
            comb = jnp.where(self.causal, s[:, CHUNK:], s_prev)
            sink = self.sink_ref[h]
            m = jnp.maximum(jnp.max(comb, axis=-1, keepdims=True), sink)
            pexp = jnp.exp(comb - m)
            denom = jnp.sum(pexp, axis=-1, keepdims=True) + jnp.exp(sink - m)
            pn = pexp * (1.0 / denom)
            probs.append(jnp.where(self.causal, 0.0, pn).astype(BF16))
            probs.append(jnp.where(self.causal, pn, 0.0).astype(BF16))
        self.probs[p] = jnp.concatenate(probs, axis=1)

    def values(self, p):
        kv_head = (p * HEADS_PER_VREG) // Q_PER_KV
        vprev = self.vprev()
        vband = []
        for o in range(HEADS_PER_VREG):
            e = kv_head * HEADS_PER_VREG + o
            vband.append(vprev[e])
            vband.append(self.ve[e])
        out = jnp.dot(self.probs.pop(p), jnp.concatenate(vband, axis=0),
                      preferred_element_type=F32)
        zb = self.load(self.o_zb + p * LANES, self.o_zb + (p + 1) * LANES).astype(F32)
        self.store(self.d_a + p * LANES, (out * _silu(zb)).astype(BF16))


def _layer_kernel(sink_ref, x_ref, moda_ref, modc_ref, modf_ref, ng_ref, fg_ref,
                  cos_ref, sin_ref, lng_ref, lnb_ref, ws_ref, bst_ref, win_ref, wout_ref,
                  o_ref, proj0_scr, proj1_scr, y_scr, xprev_scr, kprev_scr, vprev_scr,
                  *, d_model, d_in, d_a, d_b, d_kv, blocks_per_batch):
    step = pl.program_id(0)

    @pl.when(step == 0)
    def _():
        proj1_scr[...] = jnp.zeros_like(proj1_scr)
        xprev_scr[...] = jnp.zeros_like(xprev_scr)
        kprev_scr[...] = jnp.zeros_like(kprev_scr)
        vprev_scr[...] = jnp.zeros_like(vprev_scr)

    args = (sink_ref, x_ref, moda_ref, modc_ref, modf_ref, ng_ref, fg_ref, cos_ref, sin_ref,
            lng_ref, lnb_ref, ws_ref, bst_ref, win_ref, wout_ref, o_ref)
    scr = (y_scr, xprev_scr, kprev_scr, vprev_scr)
    dims = dict(d_model=d_model, d_in=d_in, d_a=d_a, d_b=d_b, d_kv=d_kv,
                blocks_per_batch=blocks_per_batch)

    @pl.when(step % 2 == 0)
    def _():
        _layer_step(step, *args, proj0_scr, proj1_scr, *scr, **dims)

    @pl.when(step % 2 == 1)
    def _():
        _layer_step(step, *args, proj1_scr, proj0_scr, *scr, **dims)


def _layer_step(step, sink_ref, x_ref, moda_ref, modc_ref, modf_ref, ng_ref, fg_ref,
                cos_ref, sin_ref, lng_ref, lnb_ref, ws_ref, bst_ref, win_ref, wout_ref,
                o_ref, proj_w, proj_r, y_scr, xprev_scr, kprev_scr, vprev_scr,
                *, d_model, d_in, d_a, d_b, d_kv, blocks_per_batch):
    blk_c = jnp.maximum(step - 1, 0)
    first_in_batch = (blk_c % blocks_per_batch) == 0

    lng = lng_ref[...]
    lnb = lnb_ref[...]
    bst = bst_ref[...]
    n_sub = T_BLK // CHUNK
    blocks = []
    for sb in range(n_sub):
        r0 = sb * CHUNK

        def load(c0, c1, r0=r0):
            return proj_r[r0:r0 + CHUNK, c0:c1]

        def store(c0, val, r0=r0):
            y_scr[r0:r0 + CHUNK, c0:c0 + val.shape[1]] = val

        if sb == 0:
            kprev = lambda: [kprev_scr[e] for e in range(N_EXP)]
            vprev = lambda: [vprev_scr[e] for e in range(N_EXP)]
            has_prev = jnp.logical_not(first_in_batch)
        else:
            kprev = lambda b=blocks[sb - 1]: b.ke
            vprev = lambda b=blocks[sb - 1]: b.ve
            has_prev = True
        blocks.append(_MixerBlock(
            load, store, cos_ref[r0:r0 + CHUNK, :], sin_ref[r0:r0 + CHUNK, :], lng, lnb,
            ws_ref, bst, sink_ref, kprev, vprev, has_prev, d_a=d_a, d_b=d_b, d_kv=d_kv))

    x = x_ref[...]
    ms = jnp.mean(x * x, axis=-1, keepdims=True)
    xn = x * lax.rsqrt(ms + NORM_EPS) * ng_ref[...]
    shift = moda_ref[:, 0:d_model]
    scale = moda_ref[:, d_model:2 * d_model]
    h = (xn * (1.0 + scale) + shift).astype(BF16)

    def proj_chunk(n0):
        acc = jnp.dot(h, win_ref[:, n0:n0 + TN_DOT], preferred_element_type=F32)
        proj_w[:, n0:n0 + TN_DOT] = acc.astype(BF16)

    chunks = list(range(0, d_in, TN_DOT))
    next_chunk = [0]

    def emit_chunks(n):
        for _ in range(n):
            if next_chunk[0] < len(chunks):
                proj_chunk(chunks[next_chunk[0]])
                next_chunk[0] += 1

    for b in blocks:
        b.prep()
    n_pairs = d_b // LANES
    slots = [(b, list(range(p0, min(p0 + PAIRS_PER_SLOT, n_pairs))))
             for b in blocks for p0 in range(0, n_pairs, PAIRS_PER_SLOT)]
    groups_per_slot = -(-A_GROUPS * n_sub // len(slots))
    group_list = [(b, g) for b in blocks for g in range(A_GROUPS)]
    for p in slots[0][1]:
        slots[0][0].scores(p)
    for j, (b, pairs) in enumerate(slots):
        emit_chunks(1)
        if j + 1 < len(slots):
            for p in slots[j + 1][1]:
                slots[j + 1][0].scores(p)
        for bg, g in group_list[j * groups_per_slot:(j + 1) * groups_per_slot]:
            bg.group(g)
        for p in pairs:
            b.values(p)
    for e in range(N_EXP):
        kprev_scr[e] = blocks[-1].ke[e]
        vprev_scr[e] = blocks[-1].ve[e]
    emit_chunks(len(chunks) - next_chunk[0] - 2)

    gate = modc_ref[:, 2 * d_model:3 * d_model]
    y = y_scr[...]
    ssq = jnp.zeros((T_BLK, 1), F32)
    for n0 in range(0, d_model, TN_DOT):
        sl = slice(n0, n0 + TN_DOT)
        acc = jnp.dot(y, wout_ref[:, sl], preferred_element_type=F32)
        xr = xprev_scr[:, sl] + gate[:, sl] * acc
        ssq = ssq + jnp.sum(xr * xr, axis=-1, keepdims=True)
        o_ref[:, sl] = xr
    emit_chunks(len(chunks))
    inv = lax.rsqrt(ssq * (1.0 / d_model) + NORM_EPS)
    shift_f = modf_ref[:, 0:d_model]
    scale_f = modf_ref[:, d_model:2 * d_model]
    o_ref[...] = (o_ref[...] * inv * fg_ref[...]) * (1.0 + scale_f) + shift_f

    xprev_scr[...] = x


def _layer(x2, sinks, mod3, modf3, norm_g, final_g, cos_t, sin_t, ln_g, ln_b, w_sp, b_sp_t,
           w_in_bf16, w_out_bf16, seq, d_a, d_b, d_kv):
    rows, d_model = x2.shape
    d_in = w_in_bf16.shape[-1]
    d_mix = d_a + d_b
    n_blk = rows // T_BLK
    bpb = seq // T_BLK

    def blk_a(i):
        return jnp.minimum(i, n_blk - 1)

    def blk_c(i):
        return jnp.maximum(i - 1, 0)

    const2 = lambda i: (0, 0)
    kern = functools.partial(_layer_kernel, d_model=d_model, d_in=d_in, d_a=d_a, d_b=d_b,
                             d_kv=d_kv, blocks_per_batch=bpb)
    return pl.pallas_call(
        kern,
        grid=(n_blk + 1,),
        in_specs=[
            pl.BlockSpec(memory_space=pltpu.SMEM),
            pl.BlockSpec((T_BLK, d_model), lambda i: (blk_a(i), 0)),
            pl.BlockSpec((None, 1, 3 * d_model), lambda i: (blk_a(i) // bpb, 0, 0)),
            pl.BlockSpec((None, 1, 3 * d_model), lambda i: (blk_c(i) // bpb, 0, 0)),
            pl.BlockSpec((None, 1, 2 * d_model), lambda i: (blk_c(i) // bpb, 0, 0)),
            pl.BlockSpec((1, d_model), const2),
            pl.BlockSpec((1, d_model), const2),
            pl.BlockSpec((T_BLK, LANES), lambda i: (blk_c(i) % bpb, 0)),
            pl.BlockSpec((T_BLK, LANES), lambda i: (blk_c(i) % bpb, 0)),
            pl.BlockSpec((1, d_a), const2),
            pl.BlockSpec((1, d_a), const2),
            pl.BlockSpec((A_GROUPS, CHUNK, CHUNK), lambda i: (0, 0, 0)),
            pl.BlockSpec((CHUNK, A_GROUPS), const2),
            pl.BlockSpec((d_model, d_in), const2, pipeline_mode=pl.Buffered(1)),
            pl.BlockSpec((d_mix, d_model), const2, pipeline_mode=pl.Buffered(1)),
        ],
        out_specs=pl.BlockSpec((T_BLK, d_model), lambda i: (blk_c(i), 0)),
        out_shape=jax.ShapeDtypeStruct((rows, d_model), F32),
        scratch_shapes=[
            pltpu.VMEM((T_BLK, d_in), BF16),
            pltpu.VMEM((T_BLK, d_in), BF16),
            pltpu.VMEM((T_BLK, d_mix), BF16),
            pltpu.VMEM((T_BLK, d_model), F32),
            pltpu.VMEM((N_EXP, CHUNK, LANES), BF16),
            pltpu.VMEM((N_EXP, CHUNK, LANES), BF16),
        ],
        compiler_params=pltpu.CompilerParams(
            dimension_semantics=("arbitrary",), vmem_limit_bytes=VMEM_LIMIT_LAYER),
        name="layer",
    )(sinks, x2, mod3, mod3, modf3, norm_g.reshape(1, d_model), final_g.reshape(1, d_model),
      cos_t, sin_t, ln_g, ln_b, w_sp, b_sp_t, w_in_bf16, w_out_bf16)


def _rope_tables(seq):
    half = HEAD_DIM // 2
    inv_freq = ROPE_THETA ** (-jnp.arange(0, HEAD_DIM, 2, dtype=F32) / HEAD_DIM)
    ang = jnp.arange(seq, dtype=F32)[:, None] * inv_freq[None, :]
    cos = jnp.cos(ang)
    sin = jnp.sin(ang)
    cos_t = jnp.tile(cos, (1, LANES // half))
    sin_t = jnp.tile(jnp.concatenate([-sin, sin], axis=1), (1, HEADS_PER_VREG))
    return cos_t, sin_t


def kernel(x, c, w_ada, b_ada, norm_g, w_in, ln_v_g, ln_v_b, w_spatial, b_spatial, sinks,
           w_out, w_ada_final, b_ada_final, final_norm_g):
    bsz, seq, d_model = x.shape
    assert w_ada.shape[0] == 1, "single-layer stack"
    d_a = ln_v_g.shape[-1]
    d_mix = w_out.shape[-2]
    d_b = d_mix - d_a
    d_kv = N_KV_HEADS * HEAD_DIM
    d_in = w_in.shape[-1]
    assert d_in == 3 * d_a + 2 * d_b + 2 * d_kv
    assert d_a == A_GROUPS * A_GROUP_W and d_b == N_KV_HEADS * Q_PER_KV * HEAD_DIM
    assert seq % T_BLK == 0 and T_BLK % CHUNK == 0

    x2 = x.reshape(bsz * seq, d_model)
    mod = _ada_mod(c, w_ada, b_ada)
    mod_f = _ada_mod(c, w_ada_final, b_ada_final)
    mod3 = mod.reshape(bsz, 1, 3 * d_model)
    modf3 = mod_f.reshape(bsz, 1, 2 * d_model)
    cos_t, sin_t = _rope_tables(seq)
    out = _layer(x2, sinks.reshape(-1), mod3, modf3, norm_g, final_norm_g, cos_t, sin_t,
                 ln_v_g.reshape(1, d_a), ln_v_b.reshape(1, d_a),
                 w_spatial.reshape(A_GROUPS, CHUNK, CHUNK),
                 b_spatial.reshape(A_GROUPS, CHUNK).T,
                 w_in.reshape(d_model, d_in).astype(BF16),
                 w_out.reshape(d_mix, d_model).astype(BF16), seq, d_a, d_b, d_kv)
    return out.reshape(bsz, seq, d_model)
```

```python
import functools

import jax
import jax.numpy as jnp
from jax import lax
from jax.experimental import pallas as pl
from jax.experimental.pallas import tpu as pltpu

F32 = jnp.float32
BF16 = jnp.bfloat16

CHUNK = 128
A_GROUPS = 8
A_GROUP_W = 128
HEAD_DIM = 64
N_KV_HEADS = 4
Q_PER_KV = 4
ROPE_THETA = 10000.0
NORM_EPS = 1e-5

LANES = 128
HEADS_PER_VREG = LANES // HEAD_DIM
N_EXP = N_KV_HEADS * HEADS_PER_VREG

T_BLK = 256
TN_DOT = 512
PAIRS_PER_SLOT = 2
W_TILES = 32
TN_ADA = 1024
VMEM_LIMIT_ADA = 40 * 1024 * 1024
VMEM_LIMIT_LAYER = 60 * 1024 * 1024


def _silu(z):
    hz = 0.5 * z
    return hz + hz * jnp.tanh(hz)


def _ada_kernel(c_ref, wa_top, wa_bot, wf_top, wf_bot, ba_ref, bf_ref, oa_ref, of_ref,
                *, n_a_tiles):
    j = pl.program_id(0)
    c = c_ref[...]
    ca = (c * (1.0 / (1.0 + jnp.exp(-c)))).astype(BF16)
    half = ca.shape[1] // 2

    def mod(w_top, w_bot, b_ref, o_ref):
        acc = jnp.dot(ca[:, :half], w_top[...].astype(BF16), preferred_element_type=F32)
        acc = acc + jnp.dot(ca[:, half:], w_bot[...].astype(BF16),
                            preferred_element_type=F32)
        o_ref[...] = acc + b_ref[...]

    @pl.when(j < n_a_tiles)
    def _():
        mod(wa_top, wa_bot, ba_ref, oa_ref)

    @pl.when(j >= n_a_tiles)
    def _():
        mod(wf_top, wf_bot, bf_ref, of_ref)


def _ada_mod(c, w_a, b_a, w_f, b_f):
    bsz, d = c.shape
    n_a, n_f = w_a.shape[-1], w_f.shape[-1]
    w_a = w_a.reshape(d, n_a)
    w_f = w_f.reshape(d, n_f)
    ta, tf = n_a // TN_ADA, n_f // TN_ADA
    half = d // 2
    a_tile = lambda j: jnp.minimum(j, ta - 1)
    f_tile = lambda j: jnp.maximum(j - ta, 0)
    return pl.pallas_call(
        functools.partial(_ada_kernel, n_a_tiles=ta),
        grid=(ta + tf,),
        in_specs=[
            pl.BlockSpec((bsz, d), lambda j: (0, 0)),
            pl.BlockSpec((half, TN_ADA), lambda j: (0, a_tile(j))),
            pl.BlockSpec((half, TN_ADA), lambda j: (1, a_tile(j))),
            pl.BlockSpec((half, TN_ADA), lambda j: (0, f_tile(j))),
            pl.BlockSpec((half, TN_ADA), lambda j: (1, f_tile(j))),
            pl.BlockSpec((1, TN_ADA), lambda j: (0, a_tile(j))),
            pl.BlockSpec((1, TN_ADA), lambda j: (0, f_tile(j))),
        ],
        out_specs=[
            pl.BlockSpec((bsz, TN_ADA), lambda j: (0, a_tile(j))),
            pl.BlockSpec((bsz, TN_ADA), lambda j: (0, f_tile(j))),
        ],
        out_shape=[jax.ShapeDtypeStruct((bsz, n_a), F32),
                   jax.ShapeDtypeStruct((bsz, n_f), F32)],
        compiler_params=pltpu.CompilerParams(
            dimension_semantics=("arbitrary",), vmem_limit_bytes=VMEM_LIMIT_ADA),
        name="ada_mod",
    )(c, w_a, w_a, w_f, w_f, b_a.reshape(1, n_a), b_f.reshape(1, n_f))


class _MixerBlock:
    def __init__(self, load, store, cos, sin, lng, lnb, ws_ref, bst, sink_ref, kprev, vprev,
                 has_prev, *, d_a, d_b, d_kv):
        self.load, self.store = load, store
        self.cos, self.sin, self.lng, self.lnb = cos, sin, lng, lnb
        self.ws_ref, self.bst, self.sink_ref = ws_ref, bst, sink_ref
        self.kprev, self.vprev, self.has_prev = kprev, vprev, has_prev
        self.d_a, self.d_b, self.d_kv = d_a, d_b, d_kv
        self.o_u, self.o_v, self.o_za = 0, d_a, 2 * d_a
        self.o_q = 3 * d_a
        self.o_k = self.o_q + d_b
        self.o_vv = self.o_k + d_kv
        self.o_zb = self.o_vv + d_kv
        row = lax.broadcasted_iota(jnp.int32, (CHUNK, CHUNK), 0)
        col = lax.broadcasted_iota(jnp.int32, (CHUNK, CHUNK), 1)
        self.col = col
        self.causal = col <= row
        self.first_half = (col & (HEAD_DIM - 1)) < (HEAD_DIM // 2)
        self.probs = {}

    def _rope(self, xv):
        rot = jnp.where(self.first_half,
                        pltpu.roll(xv, LANES - HEAD_DIM // 2, 1),
                        pltpu.roll(xv, HEAD_DIM // 2, 1))
        return xv * self.cos + rot * self.sin

    def prep(self):
        va = self.load(self.o_v, self.o_v + self.d_a).astype(F32)
        mu = jnp.mean(va, axis=-1, keepdims=True)
        vc = va - mu
        var = jnp.mean(vc * vc, axis=-1, keepdims=True)
        self.vn = (vc * lax.rsqrt(var + NORM_EPS) * self.lng + self.lnb).astype(BF16)
        self.tril = self.causal.astype(F32)
        low_half = self.col < HEAD_DIM
        self.ke = [None] * N_EXP
        self.ve = [None] * N_EXP
        for c in range(self.d_kv // LANES):
            kc = self._rope(self.load(self.o_k + c * LANES, self.o_k + (c + 1) * LANES)
                            .astype(F32))
            vcol = self.load(self.o_vv + c * LANES, self.o_vv + (c + 1) * LANES).astype(F32)
            kc_sw = pltpu.roll(kc, HEAD_DIM, 1)
            vcol_sw = pltpu.roll(vcol, HEAD_DIM, 1)
            for j in range(HEADS_PER_VREG):
                kv_head = c * HEADS_PER_VREG + j
                for o in range(HEADS_PER_VREG):
                    mask = low_half if o == 0 else jnp.logical_not(low_half)
                    e = kv_head * HEADS_PER_VREG + o
                    self.ke[e] = jnp.where(mask, kc if o == j else kc_sw, 0.0).astype(BF16)
                    self.ve[e] = jnp.where(mask, vcol if o == j else vcol_sw, 0.0).astype(BF16)

    def group(self, g):
        c0 = g * A_GROUP_W
        w = (self.ws_ref[g] * self.tril).astype(BF16)
        s = (jnp.dot(w, self.vn[:, c0:c0 + A_GROUP_W], preferred_element_type=F32)
             + self.bst[:, g:g + 1])
        u = self.load(self.o_u + c0, self.o_u + c0 + A_GROUP_W).astype(F32)
        za = self.load(self.o_za + c0, self.o_za + c0 + A_GROUP_W).astype(F32)
        self.store(c0, (u * s * _silu(za)).astype(BF16))

    def scores(self, p):
        kv_head = (p * HEADS_PER_VREG) // Q_PER_KV
        kprev = self.kprev()
        q2 = (self._rope(self.load(self.o_q + p * LANES, self.o_q + (p + 1) * LANES)
                         .astype(F32)) * (HEAD_DIM ** -0.5)).astype(BF16)
        probs = []
        for o in range(HEADS_PER_VREG):
            h = p * HEADS_PER_VREG + o
            e = kv_head * HEADS_PER_VREG + o
            kband = jnp.concatenate([kprev[e], self.ke[e]], axis=0)
            s = lax.dot_general(q2, kband, (((1,), (1,)), ((), ())),
                                preferred_element_type=F32)
            s_prev = s[:, :CHUNK]
            if self.has_prev is not True:
                s_prev = jnp.where(self.has_prev, s_prev, -jnp.inf)
            comb = jnp.where(self.causal, s[:, CHUNK:], s_prev)
            sink = self.sink_ref[h]
            m = jnp.maximum(jnp.max(comb, axis=-1, keepdims=True), sink)
            pexp = jnp.exp(comb - m)
            denom = jnp.sum(pexp, axis=-1, keepdims=True) + jnp.exp(sink - m)
            pn = pexp * (1.0 / denom)
            probs.append(jnp.where(self.causal, 0.0, pn).astype(BF16))
            probs.append(jnp.where(self.causal, pn, 0.0).astype(BF16))
        self.probs[p] = jnp.concatenate(probs, axis=1)

    def values(self, p):
        kv_head = (p * HEADS_PER_VREG) // Q_PER_KV
        vprev = self.vprev()
        vband = []
        for o in range(HEADS_PER_VREG):
            e = kv_head * HEADS_PER_VREG + o
            vband.append(vprev[e])
            vband.append(self.ve[e])
        out = jnp.dot(self.probs.pop(p), jnp.concatenate(vband, axis=0),
                      preferred_element_type=F32)
        zb = self.load(self.o_zb + p * LANES, self.o_zb + (p + 1) * LANES).astype(F32)
        self.store(self.d_a + p * LANES, (out * _silu(zb)).astype(BF16))


def _layer_kernel(sink_ref, x_ref, moda_ref, modc_ref, modf_ref, ng_ref, fg_ref,
                  cos_ref, sin_ref, lng_ref, lnb_ref, ws_ref, bst_ref, win32_ref, wout32_ref,
                  o_ref, win_scr, wout_scr, proj0_scr, proj1_scr, y_scr, xprev_scr, h_scr,
                  kprev_scr, vprev_scr,
                  *, d_model, d_in, d_a, d_b, d_kv, blocks_per_batch, n_blk):
    i = pl.program_id(0)

    @pl.when(i < W_TILES)
    def _():
        r_in = pl.multiple_of(i * win32_ref.shape[0], win32_ref.shape[0])
        win_scr[pl.ds(r_in, win32_ref.shape[0]), :] = win32_ref[...].astype(BF16)
        r_out = pl.multiple_of(i * wout32_ref.shape[0], wout32_ref.shape[0])
        wout_scr[pl.ds(r_out, wout32_ref.shape[0]), :] = wout32_ref[...].astype(BF16)

    step = i - W_TILES

    args = (sink_ref, x_ref, moda_ref, modc_ref, modf_ref, ng_ref, fg_ref, cos_ref, sin_ref,
            lng_ref, lnb_ref, ws_ref, bst_ref, win_scr, wout_scr, o_ref)
    scr = (y_scr, xprev_scr, h_scr, kprev_scr, vprev_scr)
    dims = dict(d_model=d_model, d_in=d_in, d_a=d_a, d_b=d_b, d_kv=d_kv,
                blocks_per_batch=blocks_per_batch, n_blk=n_blk)
    bufs = ((proj0_scr, proj1_scr), (proj1_scr, proj0_scr))
    inner = jnp.logical_and(step > 0, step < n_blk)

    @pl.when(step == 0)
    def _():
        kprev_scr[...] = jnp.zeros_like(kprev_scr)
        vprev_scr[...] = jnp.zeros_like(vprev_scr)
        _layer_step(step, *args, *bufs[0], *scr, **dims, do_bc=False)

    @pl.when(jnp.logical_and(inner, step % 2 == 0))
    def _():
        _layer_step(step, *args, *bufs[0], *scr, **dims)

    @pl.when(jnp.logical_and(inner, step % 2 == 1))
    def _():
        _layer_step(step, *args, *bufs[1], *scr, **dims)

    @pl.when(step == n_blk)
    def _():
        _layer_step(step, *args, *bufs[n_blk % 2], *scr, **dims, do_a=False)


def _layer_step(step, sink_ref, x_ref, moda_ref, modc_ref, modf_ref, ng_ref, fg_ref,
                cos_ref, sin_ref, lng_ref, lnb_ref, ws_ref, bst_ref, win_ref, wout_ref,
                o_ref, proj_w, proj_r, y_scr, xprev_scr, h_scr, kprev_scr, vprev_scr,
                *, d_model, d_in, d_a, d_b, d_kv, blocks_per_batch, n_blk,
                do_a=True, do_bc=True):
    blk_c = jnp.clip(step - 1, 0, n_blk - 1)
    first_in_batch = (blk_c % blocks_per_batch) == 0

    if do_a:
        x = x_ref[...]
        ms = jnp.mean(x * x, axis=-1, keepdims=True)
        xn = x * lax.rsqrt(ms + NORM_EPS) * ng_ref[...]
        shift = moda_ref[:, 0:d_model]
        scale1 = 1.0 + moda_ref[:, d_model:2 * d_model]
        h_scr[...] = (xn * scale1 + shift).astype(BF16)

    lng = lng_ref[...]
    lnb = lnb_ref[...]
    bst = bst_ref[...]
    n_sub = T_BLK // CHUNK
    blocks = []
    for sb in range(n_sub):
        r0 = sb * CHUNK

        def load(c0, c1, r0=r0):
            return proj_r[r0:r0 + CHUNK, c0:c1]

        def store(c0, val, r0=r0):
            y_scr[r0:r0 + CHUNK, c0:c0 + val.shape[1]] = val

        if sb == 0:
            kprev = lambda: [kprev_scr[e] for e in range(N_EXP)]
            vprev = lambda: [vprev_scr[e] for e in range(N_EXP)]
            has_prev = jnp.logical_not(first_in_batch)
        else:
            kprev = lambda b=blocks[sb - 1]: b.ke
            vprev = lambda b=blocks[sb - 1]: b.ve
            has_prev = True
        blocks.append(_MixerBlock(
            load, store, cos_ref[r0:r0 + CHUNK, :], sin_ref[r0:r0 + CHUNK, :], lng, lnb,
            ws_ref, bst, sink_ref, kprev, vprev, has_prev, d_a=d_a, d_b=d_b, d_kv=d_kv))

    def proj_chunk(n0):
        acc = jnp.dot(h_scr[...], win_ref[:, n0:n0 + TN_DOT], preferred_element_type=F32)
        proj_w[:, n0:n0 + TN_DOT] = acc.astype(BF16)

    chunks = list(range(0, d_in, TN_DOT))
    next_chunk = [0]

    def emit_chunks(n):
        for _ in range(n):
            if do_a and next_chunk[0] < len(chunks):
                proj_chunk(chunks[next_chunk[0]])
                next_chunk[0] += 1

    if not do_bc:
        emit_chunks(len(chunks))
        xprev_scr[...] = x
        return

    for b in blocks:
        b.prep()
    n_pairs = d_b // LANES
    slots = [(b, list(range(p0, min(p0 + PAIRS_PER_SLOT, n_pairs))))
             for b in blocks for p0 in range(0, n_pairs, PAIRS_PER_SLOT)]
    groups_per_slot = -(-A_GROUPS * n_sub // len(slots))
    group_list = [(b, g) for b in blocks for g in range(A_GROUPS)]
    for p in slots[0][1]:
        slots[0][0].scores(p)
    for j, (b, pairs) in enumerate(slots):
        emit_chunks(1)
        if j + 1 < len(slots):
            for p in slots[j + 1][1]:
                slots[j + 1][0].scores(p)
        for bg, g in group_list[j * groups_per_slot:(j + 1) * groups_per_slot]:
            bg.group(g)
        for p in pairs:
            b.values(p)
    for e in range(N_EXP):
        kprev_scr[e] = blocks[-1].ke[e]
        vprev_scr[e] = blocks[-1].ve[e]
    emit_chunks(len(chunks) - next_chunk[0] - 2)

    gate = modc_ref[:, 2 * d_model:3 * d_model]
    ssq = jnp.zeros((T_BLK, 1), F32)
    for n0 in range(0, d_model, TN_DOT):
        sl = slice(n0, n0 + TN_DOT)
        acc = jnp.dot(y_scr[...], wout_ref[:, sl], preferred_element_type=F32)
        xr = xprev_scr[:, sl] + gate[:, sl] * acc
        ssq = ssq + jnp.sum(xr * xr, axis=-1, keepdims=True)
        o_ref[:, sl] = xr
    emit_chunks(len(chunks))
    inv = lax.rsqrt(ssq * (1.0 / d_model) + NORM_EPS)
    shift_f = modf_ref[:, 0:d_model]
    scale_f = modf_ref[:, d_model:2 * d_model]
    o_ref[...] = (o_ref[...] * inv * fg_ref[...]) * (1.0 + scale_f) + shift_f

    if do_a:
        xprev_scr[...] = x


def _layer(x2, sinks, mod3, modf3, norm_g, final_g, cos_t, sin_t, ln_g, ln_b, w_sp, b_sp_t,
           w_in, w_out, seq, d_a, d_b, d_kv):
    rows, d_model = x2.shape
    d_in = w_in.shape[-1]
    d_mix = d_a + d_b
    n_blk = rows // T_BLK
    bpb = seq // T_BLK
    assert d_model % W_TILES == 0 and d_mix % W_TILES == 0

    def blk_a(i):
        return jnp.clip(i - W_TILES, 0, n_blk - 1)

    def blk_c(i):
        return jnp.clip(i - W_TILES - 1, 0, n_blk - 1)

    def w_tile(i):
        return (jnp.minimum(i, W_TILES - 1), 0)

    const2 = lambda i: (0, 0)
    kern = functools.partial(_layer_kernel, d_model=d_model, d_in=d_in, d_a=d_a, d_b=d_b,
                             d_kv=d_kv, blocks_per_batch=bpb, n_blk=n_blk)
    return pl.pallas_call(
        kern,
        grid=(W_TILES + n_blk + 1,),
        in_specs=[
            pl.BlockSpec(memory_space=pltpu.SMEM),
            pl.BlockSpec((T_BLK, d_model), lambda i: (blk_a(i), 0)),
            pl.BlockSpec((None, 1, 3 * d_model), lambda i: (blk_a(i) // bpb, 0, 0)),
            pl.BlockSpec((None, 1, 3 * d_model), lambda i: (blk_c(i) // bpb, 0, 0)),
            pl.BlockSpec((None, 1, 2 * d_model), lambda i: (blk_c(i) // bpb, 0, 0)),
            pl.BlockSpec((1, d_model), const2),
            pl.BlockSpec((1, d_model), const2),
            pl.BlockSpec((T_BLK, LANES), lambda i: (blk_c(i) % bpb, 0)),
            pl.BlockSpec((T_BLK, LANES), lambda i: (blk_c(i) % bpb, 0)),
            pl.BlockSpec((1, d_a), const2),
            pl.BlockSpec((1, d_a), const2),
            pl.BlockSpec((A_GROUPS, CHUNK, CHUNK), lambda i: (0, 0, 0)),
            pl.BlockSpec((CHUNK, A_GROUPS), const2),
            pl.BlockSpec((d_model // W_TILES, d_in), w_tile),
            pl.BlockSpec((d_mix // W_TILES, d_model), w_tile),
        ],
        out_specs=pl.BlockSpec((T_BLK, d_model), lambda i: (blk_c(i), 0)),
        out_shape=jax.ShapeDtypeStruct((rows, d_model), F32),
        scratch_shapes=[
            pltpu.VMEM((d_model, d_in), BF16),
            pltpu.VMEM((d_mix, d_model), BF16),
            pltpu.VMEM((T_BLK, d_in), BF16),
            pltpu.VMEM((T_BLK, d_in), BF16),
            pltpu.VMEM((T_BLK, d_mix), BF16),
            pltpu.VMEM((T_BLK, d_model), F32),
            pltpu.VMEM((T_BLK, d_model), BF16),
            pltpu.VMEM((N_EXP, CHUNK, LANES), BF16),
            pltpu.VMEM((N_EXP, CHUNK, LANES), BF16),
        ],
        compiler_params=pltpu.CompilerParams(
            dimension_semantics=("arbitrary",), vmem_limit_bytes=VMEM_LIMIT_LAYER),
        name="layer",
    )(sinks, x2, mod3, mod3, modf3, norm_g.reshape(1, d_model), final_g.reshape(1, d_model),
      cos_t, sin_t, ln_g, ln_b, w_sp, b_sp_t, w_in, w_out)


def _rope_tables(seq):
    half = HEAD_DIM // 2
    inv_freq = ROPE_THETA ** (-jnp.arange(0, HEAD_DIM, 2, dtype=F32) / HEAD_DIM)
    ang = jnp.arange(seq, dtype=F32)[:, None] * inv_freq[None, :]
    cos = jnp.cos(ang)
    sin = jnp.sin(ang)
    cos_t = jnp.tile(cos, (1, LANES // half))
    sin_t = jnp.tile(jnp.concatenate([-sin, sin], axis=1), (1, HEADS_PER_VREG))
    return cos_t, sin_t


def kernel(x, c, w_ada, b_ada, norm_g, w_in, ln_v_g, ln_v_b, w_spatial, b_spatial, sinks,
           w_out, w_ada_final, b_ada_final, final_norm_g):
    bsz, seq, d_model = x.shape
    assert w_ada.shape[0] == 1, "single-layer stack"
    d_a = ln_v_g.shape[-1]
    d_mix = w_out.shape[-2]
    d_b = d_mix - d_a
    d_kv = N_KV_HEADS * HEAD_DIM
    d_in = w_in.shape[-1]
    assert d_in == 3 * d_a + 2 * d_b + 2 * d_kv
    assert d_a == A_GROUPS * A_GROUP_W and d_b == N_KV_HEADS * Q_PER_KV * HEAD_DIM
    assert seq % T_BLK == 0 and T_BLK % CHUNK == 0

    x2 = x.reshape(bsz * seq, d_model)
    mod, mod_f = _ada_mod(c, w_ada, b_ada, w_ada_final, b_ada_final)
    mod3 = mod.reshape(bsz, 1, 3 * d_model)
    modf3 = mod_f.reshape(bsz, 1, 2 * d_model)
    cos_t, sin_t = _rope_tables(seq)
    out = _layer(x2, sinks.reshape(-1), mod3, modf3, norm_g, final_norm_g, cos_t, sin_t,
                 ln_v_g.reshape(1, d_a), ln_v_b.reshape(1, d_a),
                 w_spatial.reshape(A_GROUPS, CHUNK, CHUNK),
                 b_spatial.reshape(A_GROUPS, CHUNK).T,
                 w_in.reshape(d_model, d_in), w_out.reshape(d_mix, d_model),
                 seq, d_a, d_b, d_kv)
    return out.reshape(bsz, seq, d_model)
```

```python
import functools

import jax
import jax.numpy as jnp
from jax import lax
from jax.experimental import pallas as pl
from jax.experimental.pallas import tpu as pltpu

F32 = jnp.float32
BF16 = jnp.bfloat16

CHUNK = 128
A_GROUPS = 8
A_GROUP_W = 128
HEAD_DIM = 64
N_KV_HEADS = 4
Q_PER_KV = 4
ROPE_THETA = 10000.0
NORM_EPS = 1e-5

LANES = 128
HEADS_PER_VREG = LANES // HEAD_DIM
N_EXP = N_KV_HEADS * HEADS_PER_VREG

T_BLK = 256
TN_DOT = 512
PAIRS_PER_SLOT = 2
STAGE_ROWS_IN = 32
STAGE_SLOTS_IN = 5
STAGE_ROWS_OUT = 64
TN_ADA = 1024
VMEM_LIMIT_ADA = 40 * 1024 * 1024
VMEM_LIMIT_LAYER = 60 * 1024 * 1024


def _silu(z):
    hz = 0.5 * z
    return hz + hz * jnp.tanh(hz)


def _ada_kernel(c_ref, wa_top, wa_bot, wf_top, wf_bot, ba_ref, bf_ref, oa_ref, of_ref,
                *, n_a_tiles):
    j = pl.program_id(0)
    c = c_ref[...]
    ca = (c * (1.0 / (1.0 + jnp.exp(-c)))).astype(BF16)
    half = ca.shape[1] // 2

    def mod(w_top, w_bot, b_ref, o_ref):
        acc = jnp.dot(ca[:, :half], w_top[...].astype(BF16), preferred_element_type=F32)
        acc = acc + jnp.dot(ca[:, half:], w_bot[...].astype(BF16),
                            preferred_element_type=F32)
        o_ref[...] = acc + b_ref[...]

    @pl.when(j < n_a_tiles)
    def _():
        mod(wa_top, wa_bot, ba_ref, oa_ref)

    @pl.when(j >= n_a_tiles)
    def _():
        mod(wf_top, wf_bot, bf_ref, of_ref)


def _ada_mod(c, w_a, b_a, w_f, b_f):
    bsz, d = c.shape
    n_a, n_f = w_a.shape[-1], w_f.shape[-1]
    w_a = w_a.reshape(d, n_a)
    w_f = w_f.reshape(d, n_f)
    ta, tf = n_a // TN_ADA, n_f // TN_ADA
    half = d // 2
    a_tile = lambda j: jnp.minimum(j, ta - 1)
    f_tile = lambda j: jnp.maximum(j - ta, 0)
    return pl.pallas_call(
        functools.partial(_ada_kernel, n_a_tiles=ta),
        grid=(ta + tf,),
        in_specs=[
            pl.BlockSpec((bsz, d), lambda j: (0, 0)),
            pl.BlockSpec((half, TN_ADA), lambda j: (0, a_tile(j))),
            pl.BlockSpec((half, TN_ADA), lambda j: (1, a_tile(j))),
            pl.BlockSpec((half, TN_ADA), lambda j: (0, f_tile(j))),
            pl.BlockSpec((half, TN_ADA), lambda j: (1, f_tile(j))),
            pl.BlockSpec((1, TN_ADA), lambda j: (0, a_tile(j))),
            pl.BlockSpec((1, TN_ADA), lambda j: (0, f_tile(j))),
        ],
        out_specs=[
            pl.BlockSpec((bsz, TN_ADA), lambda j: (0, a_tile(j))),
            pl.BlockSpec((bsz, TN_ADA), lambda j: (0, f_tile(j))),
        ],
        out_shape=[jax.ShapeDtypeStruct((bsz, n_a), F32),
                   jax.ShapeDtypeStruct((bsz, n_f), F32)],
        compiler_params=pltpu.CompilerParams(
            dimension_semantics=("arbitrary",), vmem_limit_bytes=VMEM_LIMIT_ADA),
        name="ada_mod",
    )(c, w_a, w_a, w_f, w_f, b_a.reshape(1, n_a), b_f.reshape(1, n_f))


class _MixerBlock:
    def __init__(self, load, store, cos, sin, lng, lnb, ws_ref, bst, sink_ref, kprev, vprev,
                 has_prev, *, d_a, d_b, d_kv):
        self.load, self.store = load, store
        self.cos, self.sin, self.lng, self.lnb = cos, sin, lng, lnb
        self.ws_ref, self.bst, self.sink_ref = ws_ref, bst, sink_ref
        self.kprev, self.vprev, self.has_prev = kprev, vprev, has_prev
        self.d_a, self.d_b, self.d_kv = d_a, d_b, d_kv
        self.o_u, self.o_v, self.o_za = 0, d_a, 2 * d_a
        self.o_q = 3 * d_a
        self.o_k = self.o_q + d_b
        self.o_vv = self.o_k + d_kv
        self.o_zb = self.o_vv + d_kv
        row = lax.broadcasted_iota(jnp.int32, (CHUNK, CHUNK), 0)
        col = lax.broadcasted_iota(jnp.int32, (CHUNK, CHUNK), 1)
        self.col = col
        self.causal = col <= row
        self.first_half = (col & (HEAD_DIM - 1)) < (HEAD_DIM // 2)
        self.probs = {}

    def _rope(self, xv):
        rot = jnp.where(self.first_half,
                        pltpu.roll(xv, LANES - HEAD_DIM // 2, 1),
                        pltpu.roll(xv, HEAD_DIM // 2, 1))
        return xv * self.cos + rot * self.sin

    def prep(self):
        va = self.load(self.o_v, self.o_v + self.d_a).astype(F32)
        mu = jnp.mean(va, axis=-1, keepdims=True)
        vc = va - mu
        var = jnp.mean(vc * vc, axis=-1, keepdims=True)
        self.vn = (vc * lax.rsqrt(var + NORM_EPS) * self.lng + self.lnb).astype(BF16)
        self.tril = self.causal.astype(F32)
        low_half = self.col < HEAD_DIM
        self.ke = [None] * N_EXP
        self.ve = [None] * N_EXP
        for c in range(self.d_kv // LANES):
            kc = self._rope(self.load(self.o_k + c * LANES, self.o_k + (c + 1) * LANES)
                            .astype(F32))
            vcol = self.load(self.o_vv + c * LANES, self.o_vv + (c + 1) * LANES).astype(F32)
            kc_sw = pltpu.roll(kc, HEAD_DIM, 1)
            vcol_sw = pltpu.roll(vcol, HEAD_DIM, 1)
            for j in range(HEADS_PER_VREG):
                kv_head = c * HEADS_PER_VREG + j
                for o in range(HEADS_PER_VREG):
                    mask = low_half if o == 0 else jnp.logical_not(low_half)
                    e = kv_head * HEADS_PER_VREG + o
                    self.ke[e] = jnp.where(mask, kc if o == j else kc_sw, 0.0).astype(BF16)
                    self.ve[e] = jnp.where(mask, vcol if o == j else vcol_sw, 0.0).astype(BF16)

    def group(self, g):
        c0 = g * A_GROUP_W
        w = (self.ws_ref[g] * self.tril).astype(BF16)
        s = (jnp.dot(w, self.vn[:, c0:c0 + A_GROUP_W], preferred_element_type=F32)
             + self.bst[:, g:g + 1])
        u = self.load(self.o_u + c0, self.o_u + c0 + A_GROUP_W).astype(F32)
        za = self.load(self.o_za + c0, self.o_za + c0 + A_GROUP_W).astype(F32)
        self.store(c0, (u * s * _silu(za)).astype(BF16))

    def scores(self, p):
        kv_head = (p * HEADS_PER_VREG) // Q_PER_KV
        kprev = self.kprev()
        q2 = (self._rope(self.load(self.o_q + p * LANES, self.o_q + (p + 1) * LANES)
                         .astype(F32)) * (HEAD_DIM ** -0.5)).astype(BF16)
        probs = []
        for o in range(HEADS_PER_VREG):
            h = p * HEADS_PER_VREG + o
            e = kv_head * HEADS_PER_VREG + o
            kband = jnp.concatenate([kprev[e], self.ke[e]], axis=0)
            s = lax.dot_general(q2, kband, (((1,), (1,)), ((), ())),
                                preferred_element_type=F32)
            s_prev = s[:, :CHUNK]
            if self.has_prev is not True:
                s_prev = jnp.where(self.has_prev, s_prev, -jnp.inf)
            comb = jnp.where(self.causal, s[:, CHUNK:], s_prev)
            sink = self.sink_ref[h]
            m = jnp.maximum(jnp.max(comb, axis=-1, keepdims=True), sink)
            pexp = jnp.exp(comb - m)
            denom = jnp.sum(pexp, axis=-1, keepdims=True) + jnp.exp(sink - m)
            pn = pexp * (1.0 / denom)
            probs.append(jnp.where(self.causal, 0.0, pn).astype(BF16))
            probs.append(jnp.where(self.causal, pn, 0.0).astype(BF16))
        self.probs[p] = jnp.concatenate(probs, axis=1)

    def values(self, p):
        kv_head = (p * HEADS_PER_VREG) // Q_PER_KV
        vprev = self.vprev()
        vband = []
        for o in range(HEADS_PER_VREG):
            e = kv_head * HEADS_PER_VREG + o
            vband.append(vprev[e])
            vband.append(self.ve[e])
        out = jnp.dot(self.probs.pop(p), jnp.concatenate(vband, axis=0),
                      preferred_element_type=F32)
        zb = self.load(self.o_zb + p * LANES, self.o_zb + (p + 1) * LANES).astype(F32)
        self.store(self.d_a + p * LANES, (out * _silu(zb)).astype(BF16))


def _stage_weights(w_hbm, w_scr, ring, rows, sem):
    n_slots = ring.shape[0] // rows
    n_chunks = w_hbm.shape[0] // rows
    assert ring.shape[0] % rows == 0 and w_hbm.shape[0] % rows == 0

    def slot_ref(c):
        return ring.at[pl.ds((c % n_slots) * rows, rows)]

    def copy(c):
        return pltpu.make_async_copy(w_hbm.at[pl.ds(c * rows, rows)], slot_ref(c),
                                     sem.at[c % n_slots])

    for c in range(min(n_slots, n_chunks)):
        copy(c).start()
    for c in range(n_chunks):
        copy(c).wait()
        w_scr[c * rows:(c + 1) * rows, :] = slot_ref(c)[...].astype(BF16)
        if c + n_slots < n_chunks:
            copy(c + n_slots).start()


def _layer_kernel(sink_ref, x_ref, moda_ref, modc_ref, modf_ref, ng_ref, fg_ref,
                  cos_ref, sin_ref, lng_ref, lnb_ref, ws_ref, bst_ref, win_hbm, wout_hbm,
                  o_ref, win_scr, wout_scr, proj0_scr, proj1_scr, y_scr, xprev_scr, h_scr,
                  kprev_scr, vprev_scr, stage_scr, sem_in, sem_out,
                  *, d_model, d_in, d_a, d_b, d_kv, blocks_per_batch, n_blk):
    step = pl.program_id(0)

    @pl.when(step == 0)
    def _():
        _stage_weights(win_hbm, win_scr, stage_scr, STAGE_ROWS_IN, sem_in)
        _stage_weights(wout_hbm, wout_scr, xprev_scr, STAGE_ROWS_OUT, sem_out)

    args = (sink_ref, x_ref, moda_ref, modc_ref, modf_ref, ng_ref, fg_ref, cos_ref, sin_ref,
            lng_ref, lnb_ref, ws_ref, bst_ref, win_scr, wout_scr, o_ref)
    scr = (y_scr, xprev_scr, h_scr, kprev_scr, vprev_scr)
    dims = dict(d_model=d_model, d_in=d_in, d_a=d_a, d_b=d_b, d_kv=d_kv,
                blocks_per_batch=blocks_per_batch, n_blk=n_blk)
    bufs = ((proj0_scr, proj1_scr), (proj1_scr, proj0_scr))
    inner = jnp.logical_and(step > 0, step < n_blk)

    @pl.when(step == 0)
    def _():
        kprev_scr[...] = jnp.zeros_like(kprev_scr)
        vprev_scr[...] = jnp.zeros_like(vprev_scr)
        _layer_step(step, *args, *bufs[0], *scr, **dims, do_bc=False)

    @pl.when(jnp.logical_and(inner, step % 2 == 0))
    def _():
        _layer_step(step, *args, *bufs[0], *scr, **dims)

    @pl.when(jnp.logical_and(inner, step % 2 == 1))
    def _():
        _layer_step(step, *args, *bufs[1], *scr, **dims)

    @pl.when(step == n_blk)
    def _():
        _layer_step(step, *args, *bufs[n_blk % 2], *scr, **dims, do_a=False)


def _layer_step(step, sink_ref, x_ref, moda_ref, modc_ref, modf_ref, ng_ref, fg_ref,
                cos_ref, sin_ref, lng_ref, lnb_ref, ws_ref, bst_ref, win_ref, wout_ref,
                o_ref, proj_w, proj_r, y_scr, xprev_scr, h_scr, kprev_scr, vprev_scr,
                *, d_model, d_in, d_a, d_b, d_kv, blocks_per_batch, n_blk,
                do_a=True, do_bc=True):
    blk_c = jnp.clip(step - 1, 0, n_blk - 1)
    first_in_batch = (blk_c % blocks_per_batch) == 0

    if do_a:
        x = x_ref[...]
        ms = jnp.mean(x * x, axis=-1, keepdims=True)
        xn = x * lax.rsqrt(ms + NORM_EPS) * ng_ref[...]
        shift = moda_ref[:, 0:d_model]
        scale1 = 1.0 + moda_ref[:, d_model:2 * d_model]
        h_scr[...] = (xn * scale1 + shift).astype(BF16)

    lng = lng_ref[...]
    lnb = lnb_ref[...]
    bst = bst_ref[...]
    n_sub = T_BLK // CHUNK
    blocks = []
    for sb in range(n_sub):
        r0 = sb * CHUNK

        def load(c0, c1, r0=r0):
            return proj_r[r0:r0 + CHUNK, c0:c1]

        def store(c0, val, r0=r0):
            y_scr[r0:r0 + CHUNK, c0:c0 + val.shape[1]] = val

        if sb == 0:
            kprev = lambda: [kprev_scr[e] for e in range(N_EXP)]
            vprev = lambda: [vprev_scr[e] for e in range(N_EXP)]
            has_prev = jnp.logical_not(first_in_batch)
        else:
            kprev = lambda b=blocks[sb - 1]: b.ke
            vprev = lambda b=blocks[sb - 1]: b.ve
            has_prev = True
        blocks.append(_MixerBlock(
            load, store, cos_ref[r0:r0 + CHUNK, :], sin_ref[r0:r0 + CHUNK, :], lng, lnb,
            ws_ref, bst, sink_ref, kprev, vprev, has_prev, d_a=d_a, d_b=d_b, d_kv=d_kv))

    def proj_chunk(n0):
        acc = jnp.dot(h_scr[...], win_ref[:, n0:n0 + TN_DOT], preferred_element_type=F32)
        proj_w[:, n0:n0 + TN_DOT] = acc.astype(BF16)

    chunks = list(range(0, d_in, TN_DOT))
    next_chunk = [0]

    def emit_chunks(n):
        for _ in range(n):
            if do_a and next_chunk[0] < len(chunks):
                proj_chunk(chunks[next_chunk[0]])
                next_chunk[0] += 1

    if not do_bc:
        emit_chunks(len(chunks))
        xprev_scr[...] = x
        return

    for b in blocks:
        b.prep()
    n_pairs = d_b // LANES
    slots = [(b, list(range(p0, min(p0 + PAIRS_PER_SLOT, n_pairs))))
             for b in blocks for p0 in range(0, n_pairs, PAIRS_PER_SLOT)]
    groups_per_slot = -(-A_GROUPS * n_sub // len(slots))
    group_list = [(b, g) for b in blocks for g in range(A_GROUPS)]
    for p in slots[0][1]:
        slots[0][0].scores(p)
    for j, (b, pairs) in enumerate(slots):
        emit_chunks(1)
        if j + 1 < len(slots):
            for p in slots[j + 1][1]:
                slots[j + 1][0].scores(p)
        for bg, g in group_list[j * groups_per_slot:(j + 1) * groups_per_slot]:
            bg.group(g)
        for p in pairs:
            b.values(p)
    for e in range(N_EXP):
        kprev_scr[e] = blocks[-1].ke[e]
        vprev_scr[e] = blocks[-1].ve[e]
    emit_chunks(len(chunks) - next_chunk[0] - 2)

    gate = modc_ref[:, 2 * d_model:3 * d_model]
    ssq = jnp.zeros((T_BLK, 1), F32)
    for n0 in range(0, d_model, TN_DOT):
        sl = slice(n0, n0 + TN_DOT)
        acc = jnp.dot(y_scr[...], wout_ref[:, sl], preferred_element_type=F32)
        xr = xprev_scr[:, sl] + gate[:, sl] * acc
        ssq = ssq + jnp.sum(xr * xr, axis=-1, keepdims=True)
        o_ref[:, sl] = xr
    emit_chunks(len(chunks))
    inv = lax.rsqrt(ssq * (1.0 / d_model) + NORM_EPS)
    shift_f = modf_ref[:, 0:d_model]
    scale_f = modf_ref[:, d_model:2 * d_model]
    o_ref[...] = (o_ref[...] * inv * fg_ref[...]) * (1.0 + scale_f) + shift_f

    if do_a:
        xprev_scr[...] = x


def _layer(x2, sinks, mod3, modf3, norm_g, final_g, cos_t, sin_t, ln_g, ln_b, w_sp, b_sp_t,
           w_in, w_out, seq, d_a, d_b, d_kv):
    rows, d_model = x2.shape
    d_in = w_in.shape[-1]
    d_mix = d_a + d_b
    n_blk = rows // T_BLK
    bpb = seq // T_BLK
    assert d_mix == d_model, "w_out is staged through the x-history scratch"

    def blk_a(i):
        return jnp.minimum(i, n_blk - 1)

    def blk_c(i):
        return jnp.clip(i - 1, 0, n_blk - 1)

    const2 = lambda i: (0, 0)
    kern = functools.partial(_layer_kernel, d_model=d_model, d_in=d_in, d_a=d_a, d_b=d_b,
                             d_kv=d_kv, blocks_per_batch=bpb, n_blk=n_blk)
    return pl.pallas_call(
        kern,
        grid=(n_blk + 1,),
        in_specs=[
            pl.BlockSpec(memory_space=pltpu.SMEM),
            pl.BlockSpec((T_BLK, d_model), lambda i: (blk_a(i), 0)),
            pl.BlockSpec((None, 1, 3 * d_model), lambda i: (blk_a(i) // bpb, 0, 0)),
            pl.BlockSpec((None, 1, 3 * d_model), lambda i: (blk_c(i) // bpb, 0, 0)),
            pl.BlockSpec((None, 1, 2 * d_model), lambda i: (blk_c(i) // bpb, 0, 0)),
            pl.BlockSpec((1, d_model), const2),
            pl.BlockSpec((1, d_model), const2),
            pl.BlockSpec((T_BLK, LANES), lambda i: (blk_c(i) % bpb, 0)),
            pl.BlockSpec((T_BLK, LANES), lambda i: (blk_c(i) % bpb, 0)),
            pl.BlockSpec((1, d_a), const2),
            pl.BlockSpec((1, d_a), const2),
            pl.BlockSpec((A_GROUPS, CHUNK, CHUNK), lambda i: (0, 0, 0)),
            pl.BlockSpec((CHUNK, A_GROUPS), const2),
            pl.BlockSpec(memory_space=pl.ANY),
            pl.BlockSpec(memory_space=pl.ANY),
        ],
        out_specs=pl.BlockSpec((T_BLK, d_model), lambda i: (blk_c(i), 0)),
        out_shape=jax.ShapeDtypeStruct((rows, d_model), F32),
        scratch_shapes=[
            pltpu.VMEM((d_model, d_in), BF16),
            pltpu.VMEM((d_mix, d_model), BF16),
            pltpu.VMEM((T_BLK, d_in), BF16),
            pltpu.VMEM((T_BLK, d_in), BF16),
            pltpu.VMEM((T_BLK, d_mix), BF16),
            pltpu.VMEM((T_BLK, d_model), F32),
            pltpu.VMEM((T_BLK, d_model), BF16),
            pltpu.VMEM((N_EXP, CHUNK, LANES), BF16),
            pltpu.VMEM((N_EXP, CHUNK, LANES), BF16),
            pltpu.VMEM((STAGE_SLOTS_IN * STAGE_ROWS_IN, d_in), F32),
            pltpu.SemaphoreType.DMA((STAGE_SLOTS_IN,)),
            pltpu.SemaphoreType.DMA((T_BLK // STAGE_ROWS_OUT,)),
        ],
        compiler_params=pltpu.CompilerParams(
            dimension_semantics=("arbitrary",), vmem_limit_bytes=VMEM_LIMIT_LAYER),
        name="layer",
    )(sinks, x2, mod3, mod3, modf3, norm_g.reshape(1, d_model), final_g.reshape(1, d_model),
      cos_t, sin_t, ln_g, ln_b, w_sp, b_sp_t, w_in, w_out)


def _rope_tables(seq):
    half = HEAD_DIM // 2
    inv_freq = ROPE_THETA ** (-jnp.arange(0, HEAD_DIM, 2, dtype=F32) / HEAD_DIM)
    ang = jnp.arange(seq, dtype=F32)[:, None] * inv_freq[None, :]
    cos = jnp.cos(ang)
    sin = jnp.sin(ang)
    cos_t = jnp.tile(cos, (1, LANES // half))
    sin_t = jnp.tile(jnp.concatenate([-sin, sin], axis=1), (1, HEADS_PER_VREG))
    return cos_t, sin_t


def kernel(x, c, w_ada, b_ada, norm_g, w_in, ln_v_g, ln_v_b, w_spatial, b_spatial, sinks,
           w_out, w_ada_final, b_ada_final, final_norm_g):
    bsz, seq, d_model = x.shape
    assert w_ada.shape[0] == 1, "single-layer stack"
    d_a = ln_v_g.shape[-1]
    d_mix = w_out.shape[-2]
    d_b = d_mix - d_a
    d_kv = N_KV_HEADS * HEAD_DIM
    d_in = w_in.shape[-1]
    assert d_in == 3 * d_a + 2 * d_b + 2 * d_kv
    assert d_a == A_GROUPS * A_GROUP_W and d_b == N_KV_HEADS * Q_PER_KV * HEAD_DIM
    assert seq % T_BLK == 0 and T_BLK % CHUNK == 0

    x2 = x.reshape(bsz * seq, d_model)
    mod, mod_f = _ada_mod(c, w_ada, b_ada, w_ada_final, b_ada_final)
    mod3 = mod.reshape(bsz, 1, 3 * d_model)
    modf3 = mod_f.reshape(bsz, 1, 2 * d_model)
    cos_t, sin_t = _rope_tables(seq)
    out = _layer(x2, sinks.reshape(-1), mod3, modf3, norm_g, final_norm_g, cos_t, sin_t,
                 ln_v_g.reshape(1, d_a), ln_v_b.reshape(1, d_a),
                 w_spatial.reshape(A_GROUPS, CHUNK, CHUNK),
                 b_spatial.reshape(A_GROUPS, CHUNK).T,
                 w_in.reshape(d_model, d_in), w_out.reshape(d_mix, d_model),
                 seq, d_a, d_b, d_kv)
    return out.reshape(bsz, seq, d_model)
```

```python
import functools

import jax
import jax.numpy as jnp
from jax import lax
from jax.experimental import pallas as pl
from jax.experimental.pallas import tpu as pltpu

F32 = jnp.float32
BF16 = jnp.bfloat16

CHUNK = 128
A_GROUPS = 8
A_GROUP_W = 128
HEAD_DIM = 64
N_KV_HEADS = 4
Q_PER_KV = 4
ROPE_THETA = 10000.0
NORM_EPS = 1e-5

LANES = 128
HEADS_PER_VREG = LANES // HEAD_DIM
N_EXP = N_KV_HEADS * HEADS_PER_VREG

T_BLK = 256
TN_DOT = 512
PAIRS_PER_SLOT = 2
STAGE_ROWS_IN = 64
STAGE_SLOTS_IN = 3
STAGE_ROWS_OUT = 64
TN_ADA = 1024
VMEM_LIMIT_ADA = 40 * 1024 * 1024
VMEM_LIMIT_LAYER = 60 * 1024 * 1024


def _silu(z):
    hz = 0.5 * z
    return hz + hz * jnp.tanh(hz)


def _ada_kernel(c_ref, wa_top, wa_bot, wf_top, wf_bot, ba_ref, bf_ref, oa_ref, of_ref,
                *, n_a_tiles):
    j = pl.program_id(0)
    c = c_ref[...]
    ca = (c * (1.0 / (1.0 + jnp.exp(-c)))).astype(BF16)
    half = ca.shape[1] // 2

    def mod(w_top, w_bot, b_ref, o_ref):
        acc = jnp.dot(ca[:, :half], w_top[...].astype(BF16), preferred_element_type=F32)
        acc = acc + jnp.dot(ca[:, half:], w_bot[...].astype(BF16),
                            preferred_element_type=F32)
        o_ref[...] = acc + b_ref[...]

    @pl.when(j < n_a_tiles)
    def _():
        mod(wa_top, wa_bot, ba_ref, oa_ref)

    @pl.when(j >= n_a_tiles)
    def _():
        mod(wf_top, wf_bot, bf_ref, of_ref)


def _ada_mod(c, w_a, b_a, w_f, b_f):
    bsz, d = c.shape
    n_a, n_f = w_a.shape[-1], w_f.shape[-1]
    w_a = w_a.reshape(d, n_a)
    w_f = w_f.reshape(d, n_f)
    ta, tf = n_a // TN_ADA, n_f // TN_ADA
    half = d // 2
    a_tile = lambda j: jnp.minimum(j, ta - 1)
    f_tile = lambda j: jnp.maximum(j - ta, 0)
    return pl.pallas_call(
        functools.partial(_ada_kernel, n_a_tiles=ta),
        grid=(ta + tf,),
        in_specs=[
            pl.BlockSpec((bsz, d), lambda j: (0, 0)),
            pl.BlockSpec((half, TN_ADA), lambda j: (0, a_tile(j))),
            pl.BlockSpec((half, TN_ADA), lambda j: (1, a_tile(j))),
            pl.BlockSpec((half, TN_ADA), lambda j: (0, f_tile(j))),
            pl.BlockSpec((half, TN_ADA), lambda j: (1, f_tile(j))),
            pl.BlockSpec((1, TN_ADA), lambda j: (0, a_tile(j))),
            pl.BlockSpec((1, TN_ADA), lambda j: (0, f_tile(j))),
        ],
        out_specs=[
            pl.BlockSpec((bsz, TN_ADA), lambda j: (0, a_tile(j))),
            pl.BlockSpec((bsz, TN_ADA), lambda j: (0, f_tile(j))),
        ],
        out_shape=[jax.ShapeDtypeStruct((bsz, n_a), F32),
                   jax.ShapeDtypeStruct((bsz, n_f), F32)],
        compiler_params=pltpu.CompilerParams(
            dimension_semantics=("arbitrary",), vmem_limit_bytes=VMEM_LIMIT_ADA),
        name="ada_mod",
    )(c, w_a, w_a, w_f, w_f, b_a.reshape(1, n_a), b_f.reshape(1, n_f))


class _MixerBlock:
    def __init__(self, load, store, cos, sin, lng, lnb, ws_ref, bst, sink_ref, kprev, vprev,
                 has_prev, *, d_a, d_b, d_kv):
        self.load, self.store = load, store
        self.cos, self.sin, self.lng, self.lnb = cos, sin, lng, lnb
        self.ws_ref, self.bst, self.sink_ref = ws_ref, bst, sink_ref
        self.kprev, self.vprev, self.has_prev = kprev, vprev, has_prev
        self.d_a, self.d_b, self.d_kv = d_a, d_b, d_kv
        self.o_u, self.o_v, self.o_za = 0, d_a, 2 * d_a
        self.o_q = 3 * d_a
        self.o_k = self.o_q + d_b
        self.o_vv = self.o_k + d_kv
        self.o_zb = self.o_vv + d_kv
        row = lax.broadcasted_iota(jnp.int32, (CHUNK, CHUNK), 0)
        col = lax.broadcasted_iota(jnp.int32, (CHUNK, CHUNK), 1)
        self.col = col
        self.causal = col <= row
        self.first_half = (col & (HEAD_DIM - 1)) < (HEAD_DIM // 2)
        self.probs = {}

    def _rope(self, xv):
        rot = jnp.where(self.first_half,
                        pltpu.roll(xv, LANES - HEAD_DIM // 2, 1),
                        pltpu.roll(xv, HEAD_DIM // 2, 1))
        return xv * self.cos + rot * self.sin

    def prep(self):
        va = self.load(self.o_v, self.o_v + self.d_a).astype(F32)
        mu = jnp.mean(va, axis=-1, keepdims=True)
        vc = va - mu
        var = jnp.mean(vc * vc, axis=-1, keepdims=True)
        self.vn = (vc * lax.rsqrt(var + NORM_EPS) * self.lng + self.lnb).astype(BF16)
        self.tril = self.causal.astype(F32)
        low_half = self.col < HEAD_DIM
        self.ke = [None] * N_EXP
        self.ve = [None] * N_EXP
        for c in range(self.d_kv // LANES):
            kc = self._rope(self.load(self.o_k + c * LANES, self.o_k + (c + 1) * LANES)
                            .astype(F32))
            vcol = self.load(self.o_vv + c * LANES, self.o_vv + (c + 1) * LANES).astype(F32)
            kc_sw = pltpu.roll(kc, HEAD_DIM, 1)
            vcol_sw = pltpu.roll(vcol, HEAD_DIM, 1)
            for j in range(HEADS_PER_VREG):
                kv_head = c * HEADS_PER_VREG + j
                for o in range(HEADS_PER_VREG):
                    mask = low_half if o == 0 else jnp.logical_not(low_half)
                    e = kv_head * HEADS_PER_VREG + o
                    self.ke[e] = jnp.where(mask, kc if o == j else kc_sw, 0.0).astype(BF16)
                    self.ve[e] = jnp.where(mask, vcol if o == j else vcol_sw, 0.0).astype(BF16)

    def group(self, g):
        c0 = g * A_GROUP_W
        w = (self.ws_ref[g] * self.tril).astype(BF16)
        s = (jnp.dot(w, self.vn[:, c0:c0 + A_GROUP_W], preferred_element_type=F32)
             + self.bst[:, g:g + 1])
        u = self.load(self.o_u + c0, self.o_u + c0 + A_GROUP_W).astype(F32)
        za = self.load(self.o_za + c0, self.o_za + c0 + A_GROUP_W).astype(F32)
        self.store(c0, (u * s * _silu(za)).astype(BF16))

    def scores(self, p):
        kv_head = (p * HEADS_PER_VREG) // Q_PER_KV
        kprev = self.kprev()
        q2 = (self._rope(self.load(self.o_q + p * LANES, self.o_q + (p + 1) * LANES)
                         .astype(F32)) * (HEAD_DIM ** -0.5)).astype(BF16)
        probs = []
        for o in range(HEADS_PER_VREG):
            h = p * HEADS_PER_VREG + o
            e = kv_head * HEADS_PER_VREG + o
            kband = jnp.concatenate([kprev[e], self.ke[e]], axis=0)
            s = lax.dot_general(q2, kband, (((1,), (1,)), ((), ())),
                                preferred_element_type=F32)
            s_prev = s[:, :CHUNK]
            if self.has_prev is not True:
                s_prev = jnp.where(self.has_prev, s_prev, -jnp.inf)
            comb = jnp.where(self.causal, s[:, CHUNK:], s_prev)
            sink = self.sink_ref[h]
            m = jnp.maximum(jnp.max(comb, axis=-1, keepdims=True), sink)
            pexp = jnp.exp(comb - m)
            denom = jnp.sum(pexp, axis=-1, keepdims=True) + jnp.exp(sink - m)
            pn = pexp * (1.0 / denom)
            probs.append(jnp.where(self.causal, 0.0, pn).astype(BF16))
            probs.append(jnp.where(self.causal, pn, 0.0).astype(BF16))
        self.probs[p] = jnp.concatenate(probs, axis=1)

    def values(self, p):
        kv_head = (p * HEADS_PER_VREG) // Q_PER_KV
        vprev = self.vprev()
        vband = []
        for o in range(HEADS_PER_VREG):
            e = kv_head * HEADS_PER_VREG + o
            vband.append(vprev[e])
            vband.append(self.ve[e])
        out = jnp.dot(self.probs.pop(p), jnp.concatenate(vband, axis=0),
                      preferred_element_type=F32)
        zb = self.load(self.o_zb + p * LANES, self.o_zb + (p + 1) * LANES).astype(F32)
        self.store(self.d_a + p * LANES, (out * _silu(zb)).astype(BF16))


def _stage_weights(w_hbm, w_scr, ring, rows, sem):
    n_slots = ring.shape[0] // rows
    n_chunks = w_hbm.shape[0] // rows
    assert ring.shape[0] % rows == 0 and w_hbm.shape[0] % rows == 0

    def slot_ref(c):
        return ring.at[pl.ds((c % n_slots) * rows, rows)]

    def copy(c):
        return pltpu.make_async_copy(w_hbm.at[pl.ds(c * rows, rows)], slot_ref(c),
                                     sem.at[c % n_slots])

    for c in range(min(n_slots, n_chunks)):
        copy(c).start(priority=c % 2)
    for c in range(n_chunks):
        copy(c).wait()
        w_scr[c * rows:(c + 1) * rows, :] = slot_ref(c)[...].astype(BF16)
        if c + n_slots < n_chunks:
            copy(c + n_slots).start(priority=(c + n_slots) % 2)


def _layer_kernel(sink_ref, x_ref, moda_ref, modc_ref, modf_ref, ng_ref, fg_ref,
                  cos_ref, sin_ref, lng_ref, lnb_ref, ws_ref, bst_ref, win_hbm, wout_hbm,
                  o_ref, win_scr, wout_scr, proj0_scr, proj1_scr, y_scr, xprev_scr, h_scr,
                  kprev_scr, vprev_scr, stage_scr, sem_in, sem_out,
                  *, d_model, d_in, d_a, d_b, d_kv, blocks_per_batch, n_blk):
    step = pl.program_id(0)

    @pl.when(step == 0)
    def _():
        _stage_weights(win_hbm, win_scr, stage_scr, STAGE_ROWS_IN, sem_in)
        _stage_weights(wout_hbm, wout_scr, xprev_scr, STAGE_ROWS_OUT, sem_out)

    args = (sink_ref, x_ref, moda_ref, modc_ref, modf_ref, ng_ref, fg_ref, cos_ref, sin_ref,
            lng_ref, lnb_ref, ws_ref, bst_ref, win_scr, wout_scr, o_ref)
    scr = (y_scr, xprev_scr, h_scr, kprev_scr, vprev_scr)
    dims = dict(d_model=d_model, d_in=d_in, d_a=d_a, d_b=d_b, d_kv=d_kv,
                blocks_per_batch=blocks_per_batch, n_blk=n_blk)
    bufs = ((proj0_scr, proj1_scr), (proj1_scr, proj0_scr))
    inner = jnp.logical_and(step > 0, step < n_blk)

    @pl.when(step == 0)
    def _():
        kprev_scr[...] = jnp.zeros_like(kprev_scr)
        vprev_scr[...] = jnp.zeros_like(vprev_scr)
        _layer_step(step, *args, *bufs[0], *scr, **dims, do_bc=False)

    @pl.when(jnp.logical_and(inner, step % 2 == 0))
    def _():
        _layer_step(step, *args, *bufs[0], *scr, **dims)

    @pl.when(jnp.logical_and(inner, step % 2 == 1))
    def _():
        _layer_step(step, *args, *bufs[1], *scr, **dims)

    @pl.when(step == n_blk)
    def _():
        _layer_step(step, *args, *bufs[n_blk % 2], *scr, **dims, do_a=False)


def _layer_step(step, sink_ref, x_ref, moda_ref, modc_ref, modf_ref, ng_ref, fg_ref,
                cos_ref, sin_ref, lng_ref, lnb_ref, ws_ref, bst_ref, win_ref, wout_ref,
                o_ref, proj_w, proj_r, y_scr, xprev_scr, h_scr, kprev_scr, vprev_scr,
                *, d_model, d_in, d_a, d_b, d_kv, blocks_per_batch, n_blk,
                do_a=True, do_bc=True):
    blk_c = jnp.clip(step - 1, 0, n_blk - 1)
    first_in_batch = (blk_c % blocks_per_batch) == 0

    if do_a:
        x = x_ref[...]
        ms = jnp.mean(x * x, axis=-1, keepdims=True)
        xn = x * lax.rsqrt(ms + NORM_EPS) * ng_ref[...]
        shift = moda_ref[:, 0:d_model]
        scale1 = 1.0 + moda_ref[:, d_model:2 * d_model]
        h_scr[...] = (xn * scale1 + shift).astype(BF16)

    lng = lng_ref[...]
    lnb = lnb_ref[...]
    bst = bst_ref[...]
    n_sub = T_BLK // CHUNK
    blocks = []
    for sb in range(n_sub):
        r0 = sb * CHUNK

        def load(c0, c1, r0=r0):
            return proj_r[r0:r0 + CHUNK, c0:c1]

        def store(c0, val, r0=r0):
            y_scr[r0:r0 + CHUNK, c0:c0 + val.shape[1]] = val

        if sb == 0:
            kprev = lambda: [kprev_scr[e] for e in range(N_EXP)]
            vprev = lambda: [vprev_scr[e] for e in range(N_EXP)]
            has_prev = jnp.logical_not(first_in_batch)
        else:
            kprev = lambda b=blocks[sb - 1]: b.ke
            vprev = lambda b=blocks[sb - 1]: b.ve
            has_prev = True
        blocks.append(_MixerBlock(
            load, store, cos_ref[r0:r0 + CHUNK, :], sin_ref[r0:r0 + CHUNK, :], lng, lnb,
            ws_ref, bst, sink_ref, kprev, vprev, has_prev, d_a=d_a, d_b=d_b, d_kv=d_kv))

    def proj_chunk(n0):
        acc = jnp.dot(h_scr[...], win_ref[:, n0:n0 + TN_DOT], preferred_element_type=F32)
        proj_w[:, n0:n0 + TN_DOT] = acc.astype(BF16)

    chunks = list(range(0, d_in, TN_DOT))
    next_chunk = [0]

    def emit_chunks(n):
        for _ in range(n):
            if do_a and next_chunk[0] < len(chunks):
                proj_chunk(chunks[next_chunk[0]])
                next_chunk[0] += 1

    if not do_bc:
        emit_chunks(len(chunks))
        xprev_scr[...] = x
        return

    for b in blocks:
        b.prep()
    n_pairs = d_b // LANES
    slots = [(b, list(range(p0, min(p0 + PAIRS_PER_SLOT, n_pairs))))
             for b in blocks for p0 in range(0, n_pairs, PAIRS_PER_SLOT)]
    groups_per_slot = -(-A_GROUPS * n_sub // len(slots))
    group_list = [(b, g) for b in blocks for g in range(A_GROUPS)]
    for p in slots[0][1]:
        slots[0][0].scores(p)
    for j, (b, pairs) in enumerate(slots):
        emit_chunks(1)
        if j + 1 < len(slots):
            for p in slots[j + 1][1]:
                slots[j + 1][0].scores(p)
        for bg, g in group_list[j * groups_per_slot:(j + 1) * groups_per_slot]:
            bg.group(g)
        for p in pairs:
            b.values(p)
    for e in range(N_EXP):
        kprev_scr[e] = blocks[-1].ke[e]
        vprev_scr[e] = blocks[-1].ve[e]
    emit_chunks(len(chunks) - next_chunk[0] - 2)

    gate = modc_ref[:, 2 * d_model:3 * d_model]
    ssq = jnp.zeros((T_BLK, 1), F32)
    for n0 in range(0, d_model, TN_DOT):
        sl = slice(n0, n0 + TN_DOT)
        acc = jnp.dot(y_scr[...], wout_ref[:, sl], preferred_element_type=F32)
        xr = xprev_scr[:, sl] + gate[:, sl] * acc
        ssq = ssq + jnp.sum(xr * xr, axis=-1, keepdims=True)
        o_ref[:, sl] = xr
    emit_chunks(len(chunks))
    inv = lax.rsqrt(ssq * (1.0 / d_model) + NORM_EPS)
    shift_f = modf_ref[:, 0:d_model]
    scale_f = modf_ref[:, d_model:2 * d_model]
    o_ref[...] = (o_ref[...] * inv * fg_ref[...]) * (1.0 + scale_f) + shift_f

    if do_a:
        xprev_scr[...] = x


def _layer(x2, sinks, mod3, modf3, norm_g, final_g, cos_t, sin_t, ln_g, ln_b, w_sp, b_sp_t,
           w_in, w_out, seq, d_a, d_b, d_kv):
    rows, d_model = x2.shape
    d_in = w_in.shape[-1]
    d_mix = d_a + d_b
    n_blk = rows // T_BLK
    bpb = seq // T_BLK
    assert d_mix == d_model, "w_out is staged through the x-history scratch"

    def blk_a(i):
        return jnp.minimum(i, n_blk - 1)

    def blk_c(i):
        return jnp.clip(i - 1, 0, n_blk - 1)

    const2 = lambda i: (0, 0)
    kern = functools.partial(_layer_kernel, d_model=d_model, d_in=d_in, d_a=d_a, d_b=d_b,
                             d_kv=d_kv, blocks_per_batch=bpb, n_blk=n_blk)
    return pl.pallas_call(
        kern,
        grid=(n_blk + 1,),
        in_specs=[
            pl.BlockSpec(memory_space=pltpu.SMEM),
            pl.BlockSpec((T_BLK, d_model), lambda i: (blk_a(i), 0)),
            pl.BlockSpec((None, 1, 3 * d_model), lambda i: (blk_a(i) // bpb, 0, 0)),
            pl.BlockSpec((None, 1, 3 * d_model), lambda i: (blk_c(i) // bpb, 0, 0)),
            pl.BlockSpec((None, 1, 2 * d_model), lambda i: (blk_c(i) // bpb, 0, 0)),
            pl.BlockSpec((1, d_model), const2),
            pl.BlockSpec((1, d_model), const2),
            pl.BlockSpec((T_BLK, LANES), lambda i: (blk_c(i) % bpb, 0)),
            pl.BlockSpec((T_BLK, LANES), lambda i: (blk_c(i) % bpb, 0)),
            pl.BlockSpec((1, d_a), const2),
            pl.BlockSpec((1, d_a), const2),
            pl.BlockSpec((A_GROUPS, CHUNK, CHUNK), lambda i: (0, 0, 0)),
            pl.BlockSpec((CHUNK, A_GROUPS), const2),
            pl.BlockSpec(memory_space=pl.ANY),
            pl.BlockSpec(memory_space=pl.ANY),
        ],
        out_specs=pl.BlockSpec((T_BLK, d_model), lambda i: (blk_c(i), 0)),
        out_shape=jax.ShapeDtypeStruct((rows, d_model), F32),
        scratch_shapes=[
            pltpu.VMEM((d_model, d_in), BF16),
            pltpu.VMEM((d_mix, d_model), BF16),
            pltpu.VMEM((T_BLK, d_in), BF16),
            pltpu.VMEM((T_BLK, d_in), BF16),
            pltpu.VMEM((T_BLK, d_mix), BF16),
            pltpu.VMEM((T_BLK, d_model), F32),
            pltpu.VMEM((T_BLK, d_model), BF16),
            pltpu.VMEM((N_EXP, CHUNK, LANES), BF16),
            pltpu.VMEM((N_EXP, CHUNK, LANES), BF16),
            pltpu.VMEM((STAGE_SLOTS_IN * STAGE_ROWS_IN, d_in), F32),
            pltpu.SemaphoreType.DMA((STAGE_SLOTS_IN,)),
            pltpu.SemaphoreType.DMA((T_BLK // STAGE_ROWS_OUT,)),
        ],
        compiler_params=pltpu.CompilerParams(
            dimension_semantics=("arbitrary",), vmem_limit_bytes=VMEM_LIMIT_LAYER),
        name="layer",
    )(sinks, x2, mod3, mod3, modf3, norm_g.reshape(1, d_model), final_g.reshape(1, d_model),
      cos_t, sin_t, ln_g, ln_b, w_sp, b_sp_t, w_in, w_out)


def _rope_tables(seq):
    half = HEAD_DIM // 2
    inv_freq = ROPE_THETA ** (-jnp.arange(0, HEAD_DIM, 2, dtype=F32) / HEAD_DIM)
    ang = jnp.arange(seq, dtype=F32)[:, None] * inv_freq[None, :]
    cos = jnp.cos(ang)
    sin = jnp.sin(ang)
    cos_t = jnp.tile(cos, (1, LANES // half))
    sin_t = jnp.tile(jnp.concatenate([-sin, sin], axis=1), (1, HEADS_PER_VREG))
    return cos_t, sin_t


def kernel(x, c, w_ada, b_ada, norm_g, w_in, ln_v_g, ln_v_b, w_spatial, b_spatial, sinks,
           w_out, w_ada_final, b_ada_final, final_norm_g):
    bsz, seq, d_model = x.shape
    assert w_ada.shape[0] == 1, "single-layer stack"
    d_a = ln_v_g.shape[-1]
    d_mix = w_out.shape[-2]
    d_b = d_mix - d_a
    d_kv = N_KV_HEADS * HEAD_DIM
    d_in = w_in.shape[-1]
    assert d_in == 3 * d_a + 2 * d_b + 2 * d_kv
    assert d_a == A_GROUPS * A_GROUP_W and d_b == N_KV_HEADS * Q_PER_KV * HEAD_DIM
    assert seq % T_BLK == 0 and T_BLK % CHUNK == 0

    x2 = x.reshape(bsz * seq, d_model)
    mod, mod_f = _ada_mod(c, w_ada, b_ada, w_ada_final, b_ada_final)
    mod3 = mod.reshape(bsz, 1, 3 * d_model)
    modf3 = mod_f.reshape(bsz, 1, 2 * d_model)
    cos_t, sin_t = _rope_tables(seq)
    out = _layer(x2, sinks.reshape(-1), mod3, modf3, norm_g, final_norm_g, cos_t, sin_t,
                 ln_v_g.reshape(1, d_a), ln_v_b.reshape(1, d_a),
                 w_spatial.reshape(A_GROUPS, CHUNK, CHUNK),
                 b_spatial.reshape(A_GROUPS, CHUNK).T,
                 w_in.reshape(d_model, d_in), w_out.reshape(d_mix, d_model),
                 seq, d_a, d_b, d_kv)
    return out.reshape(bsz, seq, d_model)
```

```python
import functools

import jax
import jax.numpy as jnp
from jax import lax
from jax.experimental import pallas as pl
from jax.experimental.pallas import tpu as pltpu

F32 = jnp.float32
BF16 = jnp.bfloat16

CHUNK = 128
A_GROUPS = 8
A_GROUP_W = 128
HEAD_DIM = 64
N_KV_HEADS = 4
Q_PER_KV = 4
ROPE_THETA = 10000.0
NORM_EPS = 1e-5

LANES = 128
HEADS_PER_VREG = LANES // HEAD_DIM
N_EXP = N_KV_HEADS * HEADS_PER_VREG

T_BLK = 256
TN_DOT = 512
PAIRS_PER_SLOT = 2
W_TILES = 32
TN_ADA = 1024
VMEM_LIMIT_ADA = 40 * 1024 * 1024
VMEM_LIMIT_LAYER = 63 * 1024 * 1024


def _silu(z):
    hz = 0.5 * z
    return hz + hz * jnp.tanh(hz)


def _ada_kernel(c_ref, wa_top, wa_bot, wf_top, wf_bot, ba_ref, bf_ref, oa_ref, of_ref,
                *, n_a_tiles):
    j = pl.program_id(0)
    c = c_ref[...]
    ca = (c * (1.0 / (1.0 + jnp.exp(-c)))).astype(BF16)
    half = ca.shape[1] // 2

    def mod(w_top, w_bot, b_ref, o_ref):
        acc = jnp.dot(ca[:, :half], w_top[...].astype(BF16), preferred_element_type=F32)
        acc = acc + jnp.dot(ca[:, half:], w_bot[...].astype(BF16),
                            preferred_element_type=F32)
        o_ref[...] = acc + b_ref[...]

    @pl.when(j < n_a_tiles)
    def _():
        mod(wa_top, wa_bot, ba_ref, oa_ref)

    @pl.when(j >= n_a_tiles)
    def _():
        mod(wf_top, wf_bot, bf_ref, of_ref)


def _ada_mod(c, w_a, b_a, w_f, b_f):
    bsz, d = c.shape
    n_a, n_f = w_a.shape[-1], w_f.shape[-1]
    w_a = w_a.reshape(d, n_a)
    w_f = w_f.reshape(d, n_f)
    ta, tf = n_a // TN_ADA, n_f // TN_ADA
    half = d // 2
    a_tile = lambda j: jnp.minimum(j, ta - 1)
    f_tile = lambda j: jnp.maximum(j - ta, 0)
    return pl.pallas_call(
        functools.partial(_ada_kernel, n_a_tiles=ta),
        grid=(ta + tf,),
        in_specs=[
            pl.BlockSpec((bsz, d), lambda j: (0, 0)),
            pl.BlockSpec((half, TN_ADA), lambda j: (0, a_tile(j))),
            pl.BlockSpec((half, TN_ADA), lambda j: (1, a_tile(j))),
            pl.BlockSpec((half, TN_ADA), lambda j: (0, f_tile(j))),
            pl.BlockSpec((half, TN_ADA), lambda j: (1, f_tile(j))),
            pl.BlockSpec((1, TN_ADA), lambda j: (0, a_tile(j))),
            pl.BlockSpec((1, TN_ADA), lambda j: (0, f_tile(j))),
        ],
        out_specs=[
            pl.BlockSpec((bsz, TN_ADA), lambda j: (0, a_tile(j))),
            pl.BlockSpec((bsz, TN_ADA), lambda j: (0, f_tile(j))),
        ],
        out_shape=[jax.ShapeDtypeStruct((bsz, n_a), F32),
                   jax.ShapeDtypeStruct((bsz, n_f), F32)],
        compiler_params=pltpu.CompilerParams(
            dimension_semantics=("arbitrary",), vmem_limit_bytes=VMEM_LIMIT_ADA),
        name="ada_mod",
    )(c, w_a, w_a, w_f, w_f, b_a.reshape(1, n_a), b_f.reshape(1, n_f))


class _MixerBlock:
    def __init__(self, load, store, cos, sin, lng, lnb, ws_ref, bst, sink_ref, kprev, vprev,
                 has_prev, *, d_a, d_b, d_kv):
        self.load, self.store = load, store
        self.cos, self.sin, self.lng, self.lnb = cos, sin, lng, lnb
        self.ws_ref, self.bst, self.sink_ref = ws_ref, bst, sink_ref
        self.kprev, self.vprev, self.has_prev = kprev, vprev, has_prev
        self.d_a, self.d_b, self.d_kv = d_a, d_b, d_kv
        self.o_u, self.o_v, self.o_za = 0, d_a, 2 * d_a
        self.o_q = 3 * d_a
        self.o_k = self.o_q + d_b
        self.o_vv = self.o_k + d_kv
        self.o_zb = self.o_vv + d_kv
        row = lax.broadcasted_iota(jnp.int32, (CHUNK, CHUNK), 0)
        col = lax.broadcasted_iota(jnp.int32, (CHUNK, CHUNK), 1)
        self.col = col
        self.causal = col <= row
        self.first_half = (col & (HEAD_DIM - 1)) < (HEAD_DIM // 2)
        self.probs = {}

    def _rope(self, xv):
        rot = jnp.where(self.first_half,
                        pltpu.roll(xv, LANES - HEAD_DIM // 2, 1),
                        pltpu.roll(xv, HEAD_DIM // 2, 1))
        return xv * self.cos + rot * self.sin

    def prep(self):
        va = self.load(self.o_v, self.o_v + self.d_a).astype(F32)
        mu = jnp.mean(va, axis=-1, keepdims=True)
        vc = va - mu
        var = jnp.mean(vc * vc, axis=-1, keepdims=True)
        self.vn = (vc * lax.rsqrt(var + NORM_EPS) * self.lng + self.lnb).astype(BF16)
        self.tril = self.causal.astype(F32)
        low_half = self.col < HEAD_DIM
        self.ke = [None] * N_EXP
        self.ve = [None] * N_EXP
        for c in range(self.d_kv // LANES):
            kc = self._rope(self.load(self.o_k + c * LANES, self.o_k + (c + 1) * LANES)
                            .astype(F32))
            vcol = self.load(self.o_vv + c * LANES, self.o_vv + (c + 1) * LANES).astype(F32)
            kc_sw = pltpu.roll(kc, HEAD_DIM, 1)
            vcol_sw = pltpu.roll(vcol, HEAD_DIM, 1)
            for j in range(HEADS_PER_VREG):
                kv_head = c * HEADS_PER_VREG + j
                for o in range(HEADS_PER_VREG):
                    mask = low_half if o == 0 else jnp.logical_not(low_half)
                    e = kv_head * HEADS_PER_VREG + o
                    self.ke[e] = jnp.where(mask, kc if o == j else kc_sw, 0.0).astype(BF16)
                    self.ve[e] = jnp.where(mask, vcol if o == j else vcol_sw, 0.0).astype(BF16)

    def group(self, g):
        c0 = g * A_GROUP_W
        w = (self.ws_ref[g] * self.tril).astype(BF16)
        s = (jnp.dot(w, self.vn[:, c0:c0 + A_GROUP_W], preferred_element_type=F32)
             + self.bst[:, g:g + 1])
        u = self.load(self.o_u + c0, self.o_u + c0 + A_GROUP_W).astype(F32)
        za = self.load(self.o_za + c0, self.o_za + c0 + A_GROUP_W).astype(F32)
        self.store(c0, (u * s * _silu(za)).astype(BF16))

    def scores(self, p):
        kv_head = (p * HEADS_PER_VREG) // Q_PER_KV
        kprev = self.kprev()
        q2 = (self._rope(self.load(self.o_q + p * LANES, self.o_q + (p + 1) * LANES)
                         .astype(F32)) * (HEAD_DIM ** -0.5)).astype(BF16)
        probs = []
        for o in range(HEADS_PER_VREG):
            h = p * HEADS_PER_VREG + o
            e = kv_head * HEADS_PER_VREG + o
            kband = jnp.concatenate([kprev[e], self.ke[e]], axis=0)
            s = lax.dot_general(q2, kband, (((1,), (1,)), ((), ())),
                                preferred_element_type=F32)
            s_prev = s[:, :CHUNK]
            if self.has_prev is not True:
                s_prev = jnp.where(self.has_prev, s_prev, -jnp.inf)
            comb = jnp.where(self.causal, s[:, CHUNK:], s_prev)
            sink = self.sink_ref[h]
            m = jnp.maximum(jnp.max(comb, axis=-1, keepdims=True), sink)
            pexp = jnp.exp(comb - m)
            denom = jnp.sum(pexp, axis=-1, keepdims=True) + jnp.exp(sink - m)
            pn = pexp * (1.0 / denom)
            probs.append(jnp.where(self.causal, 0.0, pn).astype(BF16))
            probs.append(jnp.where(self.causal, pn, 0.0).astype(BF16))
        self.probs[p] = jnp.concatenate(probs, axis=1)

    def values(self, p):
        kv_head = (p * HEADS_PER_VREG) // Q_PER_KV
        vprev = self.vprev()
        vband = []
        for o in range(HEADS_PER_VREG):
            e = kv_head * HEADS_PER_VREG + o
            vband.append(vprev[e])
            vband.append(self.ve[e])
        out = jnp.dot(self.probs.pop(p), jnp.concatenate(vband, axis=0),
                      preferred_element_type=F32)
        zb = self.load(self.o_zb + p * LANES, self.o_zb + (p + 1) * LANES).astype(F32)
        self.store(self.d_a + p * LANES, (out * _silu(zb)).astype(BF16))


def _layer_kernel(sink_ref, x_ref, moda_ref, modc_ref, modf_ref, ng_ref, fg_ref,
                  cos_ref, sin_ref, lng_ref, lnb_ref, ws_ref, bst_ref, win32_ref, wout32_ref,
                  o_ref, win_scr, wout_scr, proj0_scr, proj1_scr, y_scr, xh0_scr, xh1_scr,
                  h0_scr, h1_scr, kprev_scr, vprev_scr,
                  *, d_model, d_in, d_a, d_b, d_kv, blocks_per_batch, n_blk):
    i = pl.program_id(0)

    @pl.when(i < W_TILES)
    def _():
        r_in = pl.multiple_of(i * win32_ref.shape[0], win32_ref.shape[0])
        win_scr[pl.ds(r_in, win32_ref.shape[0]), :] = win32_ref[...].astype(BF16)
        r_out = pl.multiple_of(i * wout32_ref.shape[0], wout32_ref.shape[0])
        wout_scr[pl.ds(r_out, wout32_ref.shape[0]), :] = wout32_ref[...].astype(BF16)

    step = i - W_TILES

    args = (sink_ref, x_ref, moda_ref, modc_ref, modf_ref, ng_ref, fg_ref, cos_ref, sin_ref,
            lng_ref, lnb_ref, ws_ref, bst_ref, win_scr, wout_scr, o_ref)
    dims = dict(d_model=d_model, d_in=d_in, d_a=d_a, d_b=d_b, d_kv=d_kv,
                blocks_per_batch=blocks_per_batch, n_blk=n_blk)
    proj = (proj0_scr, proj1_scr)
    hbuf = (h0_scr, h1_scr)
    xh = (xh0_scr, xh1_scr)

    def run(p, **flags):
        _layer_step(step, *args, proj[p], proj[1 - p], hbuf[p], hbuf[1 - p], xh[p],
                    y_scr, kprev_scr, vprev_scr, **dims, **flags)

    @pl.when(step == 0)
    def _():
        kprev_scr[...] = jnp.zeros_like(kprev_scr)
        vprev_scr[...] = jnp.zeros_like(vprev_scr)
        run(0, do_chunks=False, do_bc=False)

    @pl.when(step == 1)
    def _():
        run(1, do_bc=False)

    inner = jnp.logical_and(step > 1, step <= n_blk)

    @pl.when(jnp.logical_and(inner, step % 2 == 0))
    def _():
        run(0)

    @pl.when(jnp.logical_and(inner, step % 2 == 1))
    def _():
        run(1)

    @pl.when(step == n_blk + 1)
    def _():
        run((n_blk + 1) % 2, do_chunks=False, do_norm=False)


def _layer_step(step, sink_ref, x_ref, moda_ref, modc_ref, modf_ref, ng_ref, fg_ref,
                cos_ref, sin_ref, lng_ref, lnb_ref, ws_ref, bst_ref, win_ref, wout_ref,
                o_ref, proj_own, proj_oth, h_own, h_oth, xh, y_scr, kprev_scr, vprev_scr,
                *, d_model, d_in, d_a, d_b, d_kv, blocks_per_batch, n_blk,
                do_chunks=True, do_bc=True, do_norm=True):
    blk_c = jnp.clip(step - 2, 0, n_blk - 1)
    first_in_batch = (blk_c % blocks_per_batch) == 0
    proj_r = proj_own

    if do_norm:
        x = x_ref[...]
        ms = jnp.mean(x * x, axis=-1, keepdims=True)
        xn = x * lax.rsqrt(ms + NORM_EPS) * ng_ref[...]
        shift = moda_ref[:, 0:d_model]
        scale1 = 1.0 + moda_ref[:, d_model:2 * d_model]
        h_own[...] = (xn * scale1 + shift).astype(BF16)

    lng = lng_ref[...]
    lnb = lnb_ref[...]
    bst = bst_ref[...]
    n_sub = T_BLK // CHUNK
    blocks = []
    for sb in range(n_sub):
        r0 = sb * CHUNK

        def load(c0, c1, r0=r0):
            return proj_r[r0:r0 + CHUNK, c0:c1]

        def store(c0, val, r0=r0):
            y_scr[r0:r0 + CHUNK, c0:c0 + val.shape[1]] = val

        if sb == 0:
            kprev = lambda: [kprev_scr[e] for e in range(N_EXP)]
            vprev = lambda: [vprev_scr[e] for e in range(N_EXP)]
            has_prev = jnp.logical_not(first_in_batch)
        else:
            kprev = lambda b=blocks[sb - 1]: b.ke
            vprev = lambda b=blocks[sb - 1]: b.ve
            has_prev = True
        blocks.append(_MixerBlock(
            load, store, cos_ref[r0:r0 + CHUNK, :], sin_ref[r0:r0 + CHUNK, :], lng, lnb,
            ws_ref, bst, sink_ref, kprev, vprev, has_prev, d_a=d_a, d_b=d_b, d_kv=d_kv))

    def proj_chunk(n0, h_ref, proj_ref):
        acc = jnp.dot(h_ref[...], win_ref[:, n0:n0 + TN_DOT], preferred_element_type=F32)
        proj_ref[:, n0:n0 + TN_DOT] = acc.astype(BF16)

    chunks = list(range(TN_DOT, d_in, TN_DOT))
    next_chunk = [0]

    def emit_chunks(n):
        for _ in range(n):
            if do_chunks and next_chunk[0] < len(chunks):
                proj_chunk(chunks[next_chunk[0]], h_oth, proj_oth)
                next_chunk[0] += 1

    def norm_tail():
        if do_norm:
            proj_chunk(0, h_own, proj_own)
            xh[...] = x

    if not do_bc:
        emit_chunks(len(chunks))
        norm_tail()
        return

    emit_chunks(1)
    for b in blocks:
        b.prep()
    n_pairs = d_b // LANES
    slots = [(b, list(range(p0, min(p0 + PAIRS_PER_SLOT, n_pairs))))
             for b in blocks for p0 in range(0, n_pairs, PAIRS_PER_SLOT)]
    groups_per_slot = -(-A_GROUPS * n_sub // len(slots))
    group_list = [(b, g) for b in blocks for g in range(A_GROUPS)]
    for p in slots[0][1]:
        slots[0][0].scores(p)
    for j, (b, pairs) in enumerate(slots):
        emit_chunks(1)
        if j + 1 < len(slots):
            for p in slots[j + 1][1]:
                slots[j + 1][0].scores(p)
        for bg, g in group_list[j * groups_per_slot:(j + 1) * groups_per_slot]:
            bg.group(g)
        for p in pairs:
            b.values(p)
    for e in range(N_EXP):
        kprev_scr[e] = blocks[-1].ke[e]
        vprev_scr[e] = blocks[-1].ve[e]
    emit_chunks(len(chunks))

    gate = modc_ref[:, 2 * d_model:3 * d_model]
    ssq = jnp.zeros((T_BLK, 1), F32)
    for n0 in range(0, d_model, TN_DOT):
        sl = slice(n0, n0 + TN_DOT)
        acc = jnp.dot(y_scr[...], wout_ref[:, sl], preferred_element_type=F32)
        xr = xh[:, sl] + gate[:, sl] * acc
        ssq = ssq + jnp.sum(xr * xr, axis=-1, keepdims=True)
        o_ref[:, sl] = xr
    norm_tail()
    inv = lax.rsqrt(ssq * (1.0 / d_model) + NORM_EPS)
    shift_f = modf_ref[:, 0:d_model]
    scale_f = modf_ref[:, d_model:2 * d_model]
    o_ref[...] = (o_ref[...] * inv * fg_ref[...]) * (1.0 + scale_f) + shift_f


def _layer(x2, sinks, mod3, modf3, norm_g, final_g, cos_t, sin_t, ln_g, ln_b, w_sp, b_sp_t,
           w_in, w_out, seq, d_a, d_b, d_kv):
    rows, d_model = x2.shape
    d_in = w_in.shape[-1]
    d_mix = d_a + d_b
    n_blk = rows // T_BLK
    bpb = seq // T_BLK
    assert d_model % W_TILES == 0 and d_mix % W_TILES == 0

    def blk_a(i):
        return jnp.clip(i - W_TILES, 0, n_blk - 1)

    def blk_c(i):
        return jnp.clip(i - W_TILES - 2, 0, n_blk - 1)

    def w_tile(i):
        return (jnp.minimum(i, W_TILES - 1), 0)

    const2 = lambda i: (0, 0)
    kern = functools.partial(_layer_kernel, d_model=d_model, d_in=d_in, d_a=d_a, d_b=d_b,
                             d_kv=d_kv, blocks_per_batch=bpb, n_blk=n_blk)
    return pl.pallas_call(
        kern,
        grid=(W_TILES + n_blk + 2,),
        in_specs=[
            pl.BlockSpec(memory_space=pltpu.SMEM),
            pl.BlockSpec((T_BLK, d_model), lambda i: (blk_a(i), 0)),
            pl.BlockSpec((None, 1, 3 * d_model), lambda i: (blk_a(i) // bpb, 0, 0)),
            pl.BlockSpec((None, 1, 3 * d_model), lambda i: (blk_c(i) // bpb, 0, 0)),
            pl.BlockSpec((None, 1, 2 * d_model), lambda i: (blk_c(i) // bpb, 0, 0)),
            pl.BlockSpec((1, d_model), const2),
            pl.BlockSpec((1, d_model), const2),
            pl.BlockSpec((T_BLK, LANES), lambda i: (blk_c(i) % bpb, 0)),
            pl.BlockSpec((T_BLK, LANES), lambda i: (blk_c(i) % bpb, 0)),
            pl.BlockSpec((1, d_a), const2),
            pl.BlockSpec((1, d_a), const2),
            pl.BlockSpec((A_GROUPS, CHUNK, CHUNK), lambda i: (0, 0, 0)),
            pl.BlockSpec((CHUNK, A_GROUPS), const2),
            pl.BlockSpec((d_model // W_TILES, d_in), w_tile),
            pl.BlockSpec((d_mix // W_TILES, d_model), w_tile),
        ],
        out_specs=pl.BlockSpec((T_BLK, d_model), lambda i: (blk_c(i), 0)),
        out_shape=jax.ShapeDtypeStruct((rows, d_model), F32),
        scratch_shapes=[
            pltpu.VMEM((d_model, d_in), BF16),
            pltpu.VMEM((d_mix, d_model), BF16),
            pltpu.VMEM((T_BLK, d_in), BF16),
            pltpu.VMEM((T_BLK, d_in), BF16),
            pltpu.VMEM((T_BLK, d_mix), BF16),
            pltpu.VMEM((T_BLK, d_model), F32),
            pltpu.VMEM((T_BLK, d_model), F32),
            pltpu.VMEM((T_BLK, d_model), BF16),
            pltpu.VMEM((T_BLK, d_model), BF16),
            pltpu.VMEM((N_EXP, CHUNK, LANES), BF16),
            pltpu.VMEM((N_EXP, CHUNK, LANES), BF16),
        ],
        compiler_params=pltpu.CompilerParams(
            dimension_semantics=("arbitrary",), vmem_limit_bytes=VMEM_LIMIT_LAYER),
        name="layer",
    )(sinks, x2, mod3, mod3, modf3, norm_g.reshape(1, d_model), final_g.reshape(1, d_model),
      cos_t, sin_t, ln_g, ln_b, w_sp, b_sp_t, w_in, w_out)


def _rope_tables(seq):
    half = HEAD_DIM // 2
    inv_freq = ROPE_THETA ** (-jnp.arange(0, HEAD_DIM, 2, dtype=F32) / HEAD_DIM)
    ang = jnp.arange(seq, dtype=F32)[:, None] * inv_freq[None, :]
    cos = jnp.cos(ang)
    sin = jnp.sin(ang)
    cos_t = jnp.tile(cos, (1, LANES // half))
    sin_t = jnp.tile(jnp.concatenate([-sin, sin], axis=1), (1, HEADS_PER_VREG))
    return cos_t, sin_t


def kernel(x, c, w_ada, b_ada, norm_g, w_in, ln_v_g, ln_v_b, w_spatial, b_spatial, sinks,
           w_out, w_ada_final, b_ada_final, final_norm_g):
    bsz, seq, d_model = x.shape
    assert w_ada.shape[0] == 1, "single-layer stack"
    d_a = ln_v_g.shape[-1]
    d_mix = w_out.shape[-2]
    d_b = d_mix - d_a
    d_kv = N_KV_HEADS * HEAD_DIM
    d_in = w_in.shape[-1]
    assert d_in == 3 * d_a + 2 * d_b + 2 * d_kv
    assert d_a == A_GROUPS * A_GROUP_W and d_b == N_KV_HEADS * Q_PER_KV * HEAD_DIM
    assert seq % T_BLK == 0 and T_BLK % CHUNK == 0

    x2 = x.reshape(bsz * seq, d_model)
    mod, mod_f = _ada_mod(c, w_ada, b_ada, w_ada_final, b_ada_final)
    mod3 = mod.reshape(bsz, 1, 3 * d_model)
    modf3 = mod_f.reshape(bsz, 1, 2 * d_model)
    cos_t, sin_t = _rope_tables(seq)
    out = _layer(x2, sinks.reshape(-1), mod3, modf3, norm_g, final_norm_g, cos_t, sin_t,
                 ln_v_g.reshape(1, d_a), ln_v_b.reshape(1, d_a),
                 w_spatial.reshape(A_GROUPS, CHUNK, CHUNK),
                 b_spatial.reshape(A_GROUPS, CHUNK).T,
                 w_in.reshape(d_model, d_in), w_out.reshape(d_mix, d_model),
                 seq, d_a, d_b, d_kv)
    return out.reshape(bsz, seq, d_model)
```

```python
import functools

import jax
import jax.numpy as jnp
from jax import lax
from jax.experimental import pallas as pl
from jax.experimental.pallas import tpu as pltpu

F32 = jnp.float32
BF16 = jnp.bfloat16

CHUNK = 128
A_GROUPS = 8
A_GROUP_W = 128
HEAD_DIM = 64
N_KV_HEADS = 4
Q_PER_KV = 4
ROPE_THETA = 10000.0
NORM_EPS = 1e-5

LANES = 128
SUBLANES = 8
HEADS_PER_VREG = LANES // HEAD_DIM
N_EXP = N_KV_HEADS * HEADS_PER_VREG

T_BLK = 256
TN_DOT = 512
PAIRS_PER_SLOT = 2
W_TILES = 32
TN_ADA = 1024
VMEM_LIMIT_ADA = 40 * 1024 * 1024
VMEM_LIMIT_LAYER = 63 * 1024 * 1024


def _silu(z):
    hz = 0.5 * z
    return hz + hz * jnp.tanh(hz)


def _ada_kernel(c_ref, wa_top, wa_bot, wf_top, wf_bot, ba_ref, bf_ref, oa_ref, of_ref,
                *, n_a_tiles):
    j = pl.program_id(0)
    c = c_ref[...]
    ca = (c * (1.0 / (1.0 + jnp.exp(-c)))).astype(BF16)
    half = ca.shape[1] // 2

    def mod(w_top, w_bot, b_ref, o_ref):
        acc = jnp.dot(ca[:, :half], w_top[...].astype(BF16), preferred_element_type=F32)
        acc = acc + jnp.dot(ca[:, half:], w_bot[...].astype(BF16),
                            preferred_element_type=F32)
        o_ref[...] = acc + b_ref[...]

    @pl.when(j < n_a_tiles)
    def _():
        mod(wa_top, wa_bot, ba_ref, oa_ref)

    @pl.when(j >= n_a_tiles)
    def _():
        mod(wf_top, wf_bot, bf_ref, of_ref)


def _ada_mod(c, w_a, b_a, w_f, b_f):
    bsz, d = c.shape
    n_a, n_f = w_a.shape[-1], w_f.shape[-1]
    w_a = w_a.reshape(d, n_a)
    w_f = w_f.reshape(d, n_f)
    ta, tf = n_a // TN_ADA, n_f // TN_ADA
    half = d // 2
    a_tile = lambda j: jnp.minimum(j, ta - 1)
    f_tile = lambda j: jnp.maximum(j - ta, 0)
    return pl.pallas_call(
        functools.partial(_ada_kernel, n_a_tiles=ta),
        grid=(ta + tf,),
        in_specs=[
            pl.BlockSpec((bsz, d), lambda j: (0, 0)),
            pl.BlockSpec((half, TN_ADA), lambda j: (0, a_tile(j))),
            pl.BlockSpec((half, TN_ADA), lambda j: (1, a_tile(j))),
            pl.BlockSpec((half, TN_ADA), lambda j: (0, f_tile(j))),
            pl.BlockSpec((half, TN_ADA), lambda j: (1, f_tile(j))),
            pl.BlockSpec((1, TN_ADA), lambda j: (0, a_tile(j))),
            pl.BlockSpec((1, TN_ADA), lambda j: (0, f_tile(j))),
        ],
        out_specs=[
            pl.BlockSpec((bsz, TN_ADA), lambda j: (0, a_tile(j))),
            pl.BlockSpec((bsz, TN_ADA), lambda j: (0, f_tile(j))),
        ],
        out_shape=[jax.ShapeDtypeStruct((bsz, n_a), F32),
                   jax.ShapeDtypeStruct((bsz, n_f), F32)],
        compiler_params=pltpu.CompilerParams(
            dimension_semantics=("arbitrary",), vmem_limit_bytes=VMEM_LIMIT_ADA),
        name="ada_mod",
    )(c, w_a, w_a, w_f, w_f, b_a.reshape(1, n_a), b_f.reshape(1, n_f))


class _MixerBlock:
    def __init__(self, load, store, cos, sin, lng, lnb, ws_ref, bst, sink_ref, kprev, vprev,
                 has_prev, *, d_a, d_b, d_kv):
        self.load, self.store = load, store
        self.cos, self.sin, self.lng, self.lnb = cos, sin, lng, lnb
        self.ws_ref, self.bst, self.sink_ref = ws_ref, bst, sink_ref
        self.kprev, self.vprev, self.has_prev = kprev, vprev, has_prev
        self.d_a, self.d_b, self.d_kv = d_a, d_b, d_kv
        self.o_u, self.o_v, self.o_za = 0, d_a, 2 * d_a
        self.o_q = 3 * d_a
        self.o_k = self.o_q + d_b
        self.o_vv = self.o_k + d_kv
        self.o_zb = self.o_vv + d_kv
        row = lax.broadcasted_iota(jnp.int32, (CHUNK, CHUNK), 0)
        col = lax.broadcasted_iota(jnp.int32, (CHUNK, CHUNK), 1)
        self.col = col
        self.causal = col <= row
        self.first_half = (col & (HEAD_DIM - 1)) < (HEAD_DIM // 2)
        self.probs = {}

    def _rope(self, xv):
        rot = jnp.where(self.first_half,
                        pltpu.roll(xv, LANES - HEAD_DIM // 2, 1),
                        pltpu.roll(xv, HEAD_DIM // 2, 1))
        return xv * self.cos + rot * self.sin

    def prep(self):
        va = self.load(self.o_v, self.o_v + self.d_a).astype(F32)
        mu = jnp.mean(va, axis=-1, keepdims=True)
        vc = va - mu
        var = jnp.mean(vc * vc, axis=-1, keepdims=True)
        self.vn = (vc * lax.rsqrt(var + NORM_EPS) * self.lng + self.lnb).astype(BF16)
        self.tril = self.causal.astype(F32)
        low_half = self.col < HEAD_DIM
        self.ke = [None] * N_EXP
        self.ve = [None] * N_EXP
        for c in range(self.d_kv // LANES):
            kc = self._rope(self.load(self.o_k + c * LANES, self.o_k + (c + 1) * LANES)
                            .astype(F32))
            vcol = self.load(self.o_vv + c * LANES, self.o_vv + (c + 1) * LANES).astype(F32)
            kc_sw = pltpu.roll(kc, HEAD_DIM, 1)
            vcol_sw = pltpu.roll(vcol, HEAD_DIM, 1)
            for j in range(HEADS_PER_VREG):
                kv_head = c * HEADS_PER_VREG + j
                for o in range(HEADS_PER_VREG):
                    mask = low_half if o == 0 else jnp.logical_not(low_half)
                    e = kv_head * HEADS_PER_VREG + o
                    self.ke[e] = jnp.where(mask, kc if o == j else kc_sw, 0.0).astype(BF16)
                    self.ve[e] = jnp.where(mask, vcol if o == j else vcol_sw, 0.0).astype(BF16)

    def group(self, g):
        c0 = g * A_GROUP_W
        w = (self.ws_ref[g] * self.tril).astype(BF16)
        s = (jnp.dot(w, self.vn[:, c0:c0 + A_GROUP_W], preferred_element_type=F32)
             + self.bst[:, g:g + 1])
        u = self.load(self.o_u + c0, self.o_u + c0 + A_GROUP_W).astype(F32)
        za = self.load(self.o_za + c0, self.o_za + c0 + A_GROUP_W).astype(F32)
        self.store(c0, (u * s * _silu(za)).astype(BF16))

    def scores(self, p):
        kv_head = (p * HEADS_PER_VREG) // Q_PER_KV
        kprev = self.kprev()
        q2 = (self._rope(self.load(self.o_q + p * LANES, self.o_q + (p + 1) * LANES)
                         .astype(F32)) * (HEAD_DIM ** -0.5)).astype(BF16)
        probs = []
        for o in range(HEADS_PER_VREG):
            h = p * HEADS_PER_VREG + o
            e = kv_head * HEADS_PER_VREG + o
            kband = jnp.concatenate([kprev[e], self.ke[e]], axis=0)
            s = lax.dot_general(q2, kband, (((1,), (1,)), ((), ())),
                                preferred_element_type=F32)
            s_prev = s[:, :CHUNK]
            if self.has_prev is not True:
                s_prev = jnp.where(self.has_prev, s_prev, -jnp.inf)
            comb = jnp.where(self.causal, s[:, CHUNK:], s_prev)
            sink = self.sink_ref[h]
            m = jnp.maximum(jnp.max(comb, axis=-1, keepdims=True), sink)
            pexp = jnp.exp(comb - m)
            denom = jnp.sum(pexp, axis=-1, keepdims=True) + jnp.exp(sink - m)
            pn = pexp * (1.0 / denom)
            probs.append(jnp.where(self.causal, 0.0, pn).astype(BF16))
            probs.append(jnp.where(self.causal, pn, 0.0).astype(BF16))
        self.probs[p] = jnp.concatenate(probs, axis=1)

    def values(self, p):
        kv_head = (p * HEADS_PER_VREG) // Q_PER_KV
        vprev = self.vprev()
        vband = []
        for o in range(HEADS_PER_VREG):
            e = kv_head * HEADS_PER_VREG + o
            vband.append(vprev[e])
            vband.append(self.ve[e])
        out = jnp.dot(self.probs.pop(p), jnp.concatenate(vband, axis=0),
                      preferred_element_type=F32)
        zb = self.load(self.o_zb + p * LANES, self.o_zb + (p + 1) * LANES).astype(F32)
        self.store(self.d_a + p * LANES, (out * _silu(zb)).astype(BF16))


def _layer_kernel(sink_ref, x_ref, moda_ref, modc_ref, modf_ref, ng_ref, fg_ref,
                  cos_ref, sin_ref, lng_ref, lnb_ref, ws_ref, bst_ref, win32_ref, wout32_ref,
                  o_ref, win_scr, wout_scr, proj0_scr, proj1_scr, y_scr, xh0_scr, xh1_scr,
                  h0_scr, h1_scr, kprev_scr, vprev_scr,
                  *, d_model, d_in, d_a, d_b, d_kv, blocks_per_batch, n_blk):
    i = pl.program_id(0)

    @pl.when(i < W_TILES)
    def _():
        r_in = pl.multiple_of(i * win32_ref.shape[0], win32_ref.shape[0])
        win_scr[pl.ds(r_in, win32_ref.shape[0]), :] = win32_ref[...].astype(BF16)
        r_out = pl.multiple_of(i * wout32_ref.shape[0], wout32_ref.shape[0])
        wout_scr[pl.ds(r_out, wout32_ref.shape[0]), :] = wout32_ref[...].astype(BF16)

    step = i - W_TILES

    args = (sink_ref, x_ref, moda_ref, modc_ref, modf_ref, ng_ref, fg_ref, cos_ref, sin_ref,
            lng_ref, lnb_ref, ws_ref, bst_ref, win_scr, wout_scr, o_ref)
    dims = dict(d_model=d_model, d_in=d_in, d_a=d_a, d_b=d_b, d_kv=d_kv,
                blocks_per_batch=blocks_per_batch, n_blk=n_blk)
    proj = (proj0_scr, proj1_scr)
    hbuf = (h0_scr, h1_scr)
    xh = (xh0_scr, xh1_scr)

    def run(p, **flags):
        _layer_step(step, *args, proj[p], proj[1 - p], hbuf[p], hbuf[1 - p], xh[p],
                    y_scr, kprev_scr, vprev_scr, **dims, **flags)

    @pl.when(step == 0)
    def _():
        kprev_scr[...] = jnp.zeros_like(kprev_scr)
        vprev_scr[...] = jnp.zeros_like(vprev_scr)
        run(0, do_chunks=False, do_bc=False)

    @pl.when(step == 1)
    def _():
        run(1, do_bc=False)

    inner = jnp.logical_and(step > 1, step <= n_blk)

    @pl.when(jnp.logical_and(inner, step % 2 == 0))
    def _():
        run(0)

    @pl.when(jnp.logical_and(inner, step % 2 == 1))
    def _():
        run(1)

    @pl.when(step == n_blk + 1)
    def _():
        run((n_blk + 1) % 2, do_chunks=False, do_norm=False)


def _layer_step(step, sink_ref, x_ref, moda_ref, modc_ref, modf_ref, ng_ref, fg_ref,
                cos_ref, sin_ref, lng_ref, lnb_ref, ws_ref, bst_ref, win_ref, wout_ref,
                o_ref, proj_own, proj_oth, h_own, h_oth, xh, y_scr, kprev_scr, vprev_scr,
                *, d_model, d_in, d_a, d_b, d_kv, blocks_per_batch, n_blk,
                do_chunks=True, do_bc=True, do_norm=True):
    blk_c = jnp.clip(step - 2, 0, n_blk - 1)
    first_in_batch = (blk_c % blocks_per_batch) == 0
    proj_r = proj_own

    if do_norm:
        x = x_ref[...]
        ms = jnp.mean(x * x, axis=-1, keepdims=True)
        xn = x * lax.rsqrt(ms + NORM_EPS) * ng_ref[0:1, :]
        shift = moda_ref[:, 0:d_model]
        scale1 = 1.0 + moda_ref[:, d_model:2 * d_model]
        h_own[...] = (xn * scale1 + shift).astype(BF16)

    lng = lng_ref[0:1, :]
    lnb = lnb_ref[0:1, :]
    bst = bst_ref[...]
    n_sub = T_BLK // CHUNK
    blocks = []
    for sb in range(n_sub):
        r0 = sb * CHUNK

        def load(c0, c1, r0=r0):
            return proj_r[r0:r0 + CHUNK, c0:c1]

        def store(c0, val, r0=r0):
            y_scr[r0:r0 + CHUNK, c0:c0 + val.shape[1]] = val

        if sb == 0:
            kprev = lambda: [kprev_scr[e] for e in range(N_EXP)]
            vprev = lambda: [vprev_scr[e] for e in range(N_EXP)]
            has_prev = jnp.logical_not(first_in_batch)
        else:
            kprev = lambda b=blocks[sb - 1]: b.ke
            vprev = lambda b=blocks[sb - 1]: b.ve
            has_prev = True
        blocks.append(_MixerBlock(
            load, store, cos_ref[r0:r0 + CHUNK, :], sin_ref[r0:r0 + CHUNK, :], lng, lnb,
            ws_ref, bst, sink_ref, kprev, vprev, has_prev, d_a=d_a, d_b=d_b, d_kv=d_kv))

    def proj_chunk(n0, h_ref, proj_ref):
        acc = jnp.dot(h_ref[...], win_ref[:, n0:n0 + TN_DOT], preferred_element_type=F32)
        proj_ref[:, n0:n0 + TN_DOT] = acc.astype(BF16)

    chunks = list(range(TN_DOT, d_in, TN_DOT))
    next_chunk = [0]

    def emit_chunks(n):
        for _ in range(n):
            if do_chunks and next_chunk[0] < len(chunks):
                proj_chunk(chunks[next_chunk[0]], h_oth, proj_oth)
                next_chunk[0] += 1

    def norm_tail():
        if do_norm:
            proj_chunk(0, h_own, proj_own)
            xh[...] = x

    if not do_bc:
        emit_chunks(len(chunks))
        norm_tail()
        return

    emit_chunks(1)
    for b in blocks:
        b.prep()
    n_pairs = d_b // LANES
    slots = [(b, list(range(p0, min(p0 + PAIRS_PER_SLOT, n_pairs))))
             for b in blocks for p0 in range(0, n_pairs, PAIRS_PER_SLOT)]
    groups_per_slot = -(-A_GROUPS * n_sub // len(slots))
    group_list = [(b, g) for b in blocks for g in range(A_GROUPS)]
    for p in slots[0][1]:
        slots[0][0].scores(p)
    for j, (b, pairs) in enumerate(slots):
        emit_chunks(1)
        if j + 1 < len(slots):
            for p in slots[j + 1][1]:
                slots[j + 1][0].scores(p)
        for bg, g in group_list[j * groups_per_slot:(j + 1) * groups_per_slot]:
            bg.group(g)
        for p in pairs:
            b.values(p)
    for e in range(N_EXP):
        kprev_scr[e] = blocks[-1].ke[e]
        vprev_scr[e] = blocks[-1].ve[e]
    emit_chunks(len(chunks))

    gate = modc_ref[:, 2 * d_model:3 * d_model]
    ssq = jnp.zeros((T_BLK, 1), F32)
    for n0 in range(0, d_model, TN_DOT):
        sl = slice(n0, n0 + TN_DOT)
        acc = jnp.dot(y_scr[...], wout_ref[:, sl], preferred_element_type=F32)
        xr = xh[:, sl] + gate[:, sl] * acc
        ssq = ssq + jnp.sum(xr * xr, axis=-1, keepdims=True)
        o_ref[:, sl] = xr
    norm_tail()
    inv = lax.rsqrt(ssq * (1.0 / d_model) + NORM_EPS)
    shift_f = modf_ref[:, 0:d_model]
    scale_f = modf_ref[:, d_model:2 * d_model]
    o_ref[...] = (o_ref[...] * inv * fg_ref[0:1, :]) * (1.0 + scale_f) + shift_f


def _rows8(v):
    return jnp.broadcast_to(v.reshape(1, -1), (SUBLANES, v.size))


def _layer(x2, sinks, mod3, modf3, norm_g, final_g, cos_t, sin_t, ln_g, ln_b, w_sp, b_sp_t,
           w_in, w_out, seq, d_a, d_b, d_kv):
    rows, d_model = x2.shape
    d_in = w_in.shape[-1]
    d_mix = d_a + d_b
    n_blk = rows // T_BLK
    bpb = seq // T_BLK
    assert d_model % W_TILES == 0 and d_mix % W_TILES == 0

    def blk_a(i):
        return jnp.clip(i - W_TILES, 0, n_blk - 1)

    def blk_c(i):
        return jnp.clip(i - W_TILES - 2, 0, n_blk - 1)

    def w_tile(i):
        return (jnp.minimum(i, W_TILES - 1), 0)

    const2 = lambda i: (0, 0)
    kern = functools.partial(_layer_kernel, d_model=d_model, d_in=d_in, d_a=d_a, d_b=d_b,
                             d_kv=d_kv, blocks_per_batch=bpb, n_blk=n_blk)
    return pl.pallas_call(
        kern,
        grid=(W_TILES + n_blk + 2,),
        in_specs=[
            pl.BlockSpec(memory_space=pltpu.SMEM),
            pl.BlockSpec((T_BLK, d_model), lambda i: (blk_a(i), 0)),
            pl.BlockSpec((None, 1, 3 * d_model), lambda i: (blk_a(i) // bpb, 0, 0)),
            pl.BlockSpec((None, 1, 3 * d_model), lambda i: (blk_c(i) // bpb, 0, 0)),
            pl.BlockSpec((None, 1, 2 * d_model), lambda i: (blk_c(i) // bpb, 0, 0)),
            pl.BlockSpec((SUBLANES, d_model), const2),
            pl.BlockSpec((SUBLANES, d_model), const2),
            pl.BlockSpec((T_BLK, LANES), lambda i: (blk_c(i) % bpb, 0)),
            pl.BlockSpec((T_BLK, LANES), lambda i: (blk_c(i) % bpb, 0)),
            pl.BlockSpec((SUBLANES, d_a), const2),
            pl.BlockSpec((SUBLANES, d_a), const2),
            pl.BlockSpec((A_GROUPS, CHUNK, CHUNK), lambda i: (0, 0, 0)),
            pl.BlockSpec((CHUNK, A_GROUPS), const2),
            pl.BlockSpec((d_model // W_TILES, d_in), w_tile),
            pl.BlockSpec((d_mix // W_TILES, d_model), w_tile),
        ],
        out_specs=pl.BlockSpec((T_BLK, d_model), lambda i: (blk_c(i), 0)),
        out_shape=jax.ShapeDtypeStruct((rows, d_model), F32),
        scratch_shapes=[
            pltpu.VMEM((d_model, d_in), BF16),
            pltpu.VMEM((d_mix, d_model), BF16),
            pltpu.VMEM((T_BLK, d_in), BF16),
            pltpu.VMEM((T_BLK, d_in), BF16),
            pltpu.VMEM((T_BLK, d_mix), BF16),
            pltpu.VMEM((T_BLK, d_model), F32),
            pltpu.VMEM((T_BLK, d_model), F32),
            pltpu.VMEM((T_BLK, d_model), BF16),
            pltpu.VMEM((T_BLK, d_model), BF16),
            pltpu.VMEM((N_EXP, CHUNK, LANES), BF16),
            pltpu.VMEM((N_EXP, CHUNK, LANES), BF16),
        ],
        compiler_params=pltpu.CompilerParams(
            dimension_semantics=("arbitrary",), vmem_limit_bytes=VMEM_LIMIT_LAYER),
        name="layer",
    )(sinks, x2, mod3, mod3, modf3, _rows8(norm_g), _rows8(final_g),
      cos_t, sin_t, _rows8(ln_g), _rows8(ln_b), w_sp, b_sp_t, w_in, w_out)


def _rope_tables(seq):
    half = HEAD_DIM // 2
    inv_freq = ROPE_THETA ** (-jnp.arange(0, HEAD_DIM, 2, dtype=F32) / HEAD_DIM)
    ang = jnp.arange(seq, dtype=F32)[:, None] * inv_freq[None, :]
    cos = jnp.cos(ang)
    sin = jnp.sin(ang)
    cos_t = jnp.tile(cos, (1, LANES // half))
    sin_t = jnp.tile(jnp.concatenate([-sin, sin], axis=1), (1, HEADS_PER_VREG))
    return cos_t, sin_t


def kernel(x, c, w_ada, b_ada, norm_g, w_in, ln_v_g, ln_v_b, w_spatial, b_spatial, sinks,
           w_out, w_ada_final, b_ada_final, final_norm_g):
    bsz, seq, d_model = x.shape
    assert w_ada.shape[0] == 1, "single-layer stack"
    d_a = ln_v_g.shape[-1]
    d_mix = w_out.shape[-2]
    d_b = d_mix - d_a
    d_kv = N_KV_HEADS * HEAD_DIM
    d_in = w_in.shape[-1]
    assert d_in == 3 * d_a + 2 * d_b + 2 * d_kv
    assert d_a == A_GROUPS * A_GROUP_W and d_b == N_KV_HEADS * Q_PER_KV * HEAD_DIM
    assert seq % T_BLK == 0 and T_BLK % CHUNK == 0

    x2 = x.reshape(bsz * seq, d_model)
    mod, mod_f = _ada_mod(c, w_ada, b_ada, w_ada_final, b_ada_final)
    mod3 = mod.reshape(bsz, 1, 3 * d_model)
    modf3 = mod_f.reshape(bsz, 1, 2 * d_model)
    cos_t, sin_t = _rope_tables(seq)
    out = _layer(x2, sinks.reshape(-1), mod3, modf3, norm_g, final_norm_g, cos_t, sin_t,
                 ln_v_g.reshape(1, d_a), ln_v_b.reshape(1, d_a),
                 w_spatial.reshape(A_GROUPS, CHUNK, CHUNK),
                 b_spatial.reshape(A_GROUPS, CHUNK).T,
                 w_in.reshape(d_model, d_in), w_out.reshape(d_mix, d_model),
                 seq, d_a, d_b, d_kv)
    return out.reshape(bsz, seq, d_model)
```

```python
import functools

import jax
import jax.numpy as jnp
from jax import lax
from jax.experimental import pallas as pl
from jax.experimental.pallas import tpu as pltpu

F32 = jnp.float32
BF16 = jnp.bfloat16

CHUNK = 128
A_GROUPS = 8
A_GROUP_W = 128
HEAD_DIM = 64
N_KV_HEADS = 4
Q_PER_KV = 4
ROPE_THETA = 10000.0
NORM_EPS = 1e-5

LANES = 128
SUBLANES = 8
HEADS_PER_VREG = LANES // HEAD_DIM
N_EXP = N_KV_HEADS * HEADS_PER_VREG

T_BLK = 256
TN_DOT = 512
PAIRS_PER_SLOT = 2
W_TILES = 32
TN_ADA = 1024
VMEM_LIMIT_ADA = 40 * 1024 * 1024
VMEM_LIMIT_LAYER = 60 * 1024 * 1024


def _silu(z):
    hz = 0.5 * z
    return hz + hz * jnp.tanh(hz)


def _ada_kernel(c_ref, wa_top, wa_bot, wf_top, wf_bot, ba_ref, bf_ref, oa_ref, of_ref,
                *, n_a_tiles):
    j = pl.program_id(0)
    c = c_ref[...]
    ca = (c * (1.0 / (1.0 + jnp.exp(-c)))).astype(BF16)
    half = ca.shape[1] // 2

    def mod(w_top, w_bot, b_ref, o_ref):
        acc = jnp.dot(ca[:, :half], w_top[...].astype(BF16), preferred_element_type=F32)
        acc = acc + jnp.dot(ca[:, half:], w_bot[...].astype(BF16),
                            preferred_element_type=F32)
        o_ref[...] = acc + b_ref[...]

    @pl.when(j < n_a_tiles)
    def _():
        mod(wa_top, wa_bot, ba_ref, oa_ref)

    @pl.when(j >= n_a_tiles)
    def _():
        mod(wf_top, wf_bot, bf_ref, of_ref)


def _ada_mod(c, w_a, b_a, w_f, b_f):
    bsz, d = c.shape
    n_a, n_f = w_a.shape[-1], w_f.shape[-1]
    w_a = w_a.reshape(d, n_a)
    w_f = w_f.reshape(d, n_f)
    ta, tf = n_a // TN_ADA, n_f // TN_ADA
    half = d // 2
    a_tile = lambda j: jnp.minimum(j, ta - 1)
    f_tile = lambda j: jnp.maximum(j - ta, 0)
    return pl.pallas_call(
        functools.partial(_ada_kernel, n_a_tiles=ta),
        grid=(ta + tf,),
        in_specs=[
            pl.BlockSpec((bsz, d), lambda j: (0, 0)),
            pl.BlockSpec((half, TN_ADA), lambda j: (0, a_tile(j))),
            pl.BlockSpec((half, TN_ADA), lambda j: (1, a_tile(j))),
            pl.BlockSpec((half, TN_ADA), lambda j: (0, f_tile(j))),
            pl.BlockSpec((half, TN_ADA), lambda j: (1, f_tile(j))),
            pl.BlockSpec((1, TN_ADA), lambda j: (0, a_tile(j))),
            pl.BlockSpec((1, TN_ADA), lambda j: (0, f_tile(j))),
        ],
        out_specs=[
            pl.BlockSpec((bsz, TN_ADA), lambda j: (0, a_tile(j))),
            pl.BlockSpec((bsz, TN_ADA), lambda j: (0, f_tile(j))),
        ],
        out_shape=[jax.ShapeDtypeStruct((bsz, n_a), F32),
                   jax.ShapeDtypeStruct((bsz, n_f), F32)],
        compiler_params=pltpu.CompilerParams(
            dimension_semantics=("arbitrary",), vmem_limit_bytes=VMEM_LIMIT_ADA),
        name="ada_mod",
    )(c, w_a, w_a, w_f, w_f, b_a.reshape(1, n_a), b_f.reshape(1, n_f))


class _MixerBlock:
    def __init__(self, load, store, cos, sin, lng, lnb, ws_ref, bst, sink_ref, kprev, vprev,
                 has_prev, *, d_a, d_b, d_kv):
        self.load, self.store = load, store
        self.cos, self.sin, self.lng, self.lnb = cos, sin, lng, lnb
        self.ws_ref, self.bst, self.sink_ref = ws_ref, bst, sink_ref
        self.kprev, self.vprev, self.has_prev = kprev, vprev, has_prev
        self.d_a, self.d_b, self.d_kv = d_a, d_b, d_kv
        self.o_u, self.o_v, self.o_za = 0, d_a, 2 * d_a
        self.o_q = 3 * d_a
        self.o_k = self.o_q + d_b
        self.o_vv = self.o_k + d_kv
        self.o_zb = self.o_vv + d_kv
        row = lax.broadcasted_iota(jnp.int32, (CHUNK, CHUNK), 0)
        col = lax.broadcasted_iota(jnp.int32, (CHUNK, CHUNK), 1)
        self.col = col
        self.causal = col <= row
        self.first_half = (col & (HEAD_DIM - 1)) < (HEAD_DIM // 2)
        self.probs = {}

    def _rope(self, xv):
        rot = jnp.where(self.first_half,
                        pltpu.roll(xv, LANES - HEAD_DIM // 2, 1),
                        pltpu.roll(xv, HEAD_DIM // 2, 1))
        return xv * self.cos + rot * self.sin

    def prep(self):
        va = self.load(self.o_v, self.o_v + self.d_a).astype(F32)
        mu = jnp.mean(va, axis=-1, keepdims=True)
        vc = va - mu
        var = jnp.mean(vc * vc, axis=-1, keepdims=True)
        self.vn = (vc * lax.rsqrt(var + NORM_EPS) * self.lng + self.lnb).astype(BF16)
        self.tril = self.causal.astype(F32)
        low_half = self.col < HEAD_DIM
        self.ke = [None] * N_EXP
        self.ve = [None] * N_EXP
        for c in range(self.d_kv // LANES):
            kc = self._rope(self.load(self.o_k + c * LANES, self.o_k + (c + 1) * LANES)
                            .astype(F32))
            vcol = self.load(self.o_vv + c * LANES, self.o_vv + (c + 1) * LANES).astype(F32)
            kc_sw = pltpu.roll(kc, HEAD_DIM, 1)
            vcol_sw = pltpu.roll(vcol, HEAD_DIM, 1)
            for j in range(HEADS_PER_VREG):
                kv_head = c * HEADS_PER_VREG + j
                for o in range(HEADS_PER_VREG):
                    mask = low_half if o == 0 else jnp.logical_not(low_half)
                    e = kv_head * HEADS_PER_VREG + o
                    self.ke[e] = jnp.where(mask, kc if o == j else kc_sw, 0.0).astype(BF16)
                    self.ve[e] = jnp.where(mask, vcol if o == j else vcol_sw, 0.0).astype(BF16)

    def group(self, g):
        c0 = g * A_GROUP_W
        w = (self.ws_ref[g] * self.tril).astype(BF16)
        s = (jnp.dot(w, self.vn[:, c0:c0 + A_GROUP_W], preferred_element_type=F32)
             + self.bst[:, g:g + 1])
        u = self.load(self.o_u + c0, self.o_u + c0 + A_GROUP_W).astype(F32)
        za = self.load(self.o_za + c0, self.o_za + c0 + A_GROUP_W).astype(F32)
        self.store(c0, (u * s * _silu(za)).astype(BF16))

    def scores(self, p):
        kv_head = (p * HEADS_PER_VREG) // Q_PER_KV
        kprev = self.kprev()
        q2 = (self._rope(self.load(self.o_q + p * LANES, self.o_q + (p + 1) * LANES)
                         .astype(F32)) * (HEAD_DIM ** -0.5)).astype(BF16)
        probs = []
        for o in range(HEADS_PER_VREG):
            h = p * HEADS_PER_VREG + o
            e = kv_head * HEADS_PER_VREG + o
            kband = jnp.concatenate([kprev[e], self.ke[e]], axis=0)
            s = lax.dot_general(q2, kband, (((1,), (1,)), ((), ())),
                                preferred_element_type=F32)
            s_prev = s[:, :CHUNK]
            if self.has_prev is not True:
                s_prev = jnp.where(self.has_prev, s_prev, -jnp.inf)
            comb = jnp.where(self.causal, s[:, CHUNK:], s_prev)
            sink = self.sink_ref[h]
            m = jnp.maximum(jnp.max(comb, axis=-1, keepdims=True), sink)
            pexp = jnp.exp(comb - m)
            denom = jnp.sum(pexp, axis=-1, keepdims=True) + jnp.exp(sink - m)
            pn = pexp * (1.0 / denom)
            probs.append(jnp.where(self.causal, 0.0, pn).astype(BF16))
            probs.append(jnp.where(self.causal, pn, 0.0).astype(BF16))
        self.probs[p] = jnp.concatenate(probs, axis=1)

    def values(self, p):
        kv_head = (p * HEADS_PER_VREG) // Q_PER_KV
        vprev = self.vprev()
        vband = []
        for o in range(HEADS_PER_VREG):
            e = kv_head * HEADS_PER_VREG + o
            vband.append(vprev[e])
            vband.append(self.ve[e])
        out = jnp.dot(self.probs.pop(p), jnp.concatenate(vband, axis=0),
                      preferred_element_type=F32)
        zb = self.load(self.o_zb + p * LANES, self.o_zb + (p + 1) * LANES).astype(F32)
        self.store(self.d_a + p * LANES, (out * _silu(zb)).astype(BF16))


def _layer_kernel(sink_ref, x_ref, moda_ref, modc_ref, modf_ref, ng_ref, fg_ref,
                  cos_ref, sin_ref, lng_ref, lnb_ref, ws_ref, bst_ref, win32_ref, wout32_ref,
                  o_ref, win_scr, wout_scr, proj0_scr, proj1_scr, y_scr, xh0_scr, xh1_scr,
                  h0_scr, h1_scr, kprev_scr, vprev_scr,
                  *, d_model, d_in, d_a, d_b, d_kv, blocks_per_batch, n_blk):
    i = pl.program_id(0)

    @pl.when(i < W_TILES)
    def _():
        r_in = pl.multiple_of(i * win32_ref.shape[0], win32_ref.shape[0])
        win_scr[pl.ds(r_in, win32_ref.shape[0]), :] = win32_ref[...].astype(BF16)
        r_out = pl.multiple_of(i * wout32_ref.shape[0], wout32_ref.shape[0])
        wout_scr[pl.ds(r_out, wout32_ref.shape[0]), :] = wout32_ref[...].astype(BF16)

    step = i - W_TILES

    args = (sink_ref, x_ref, moda_ref, modc_ref, modf_ref, ng_ref, fg_ref, cos_ref, sin_ref,
            lng_ref, lnb_ref, ws_ref, bst_ref, win_scr, wout_scr, o_ref)
    dims = dict(d_model=d_model, d_in=d_in, d_a=d_a, d_b=d_b, d_kv=d_kv,
                blocks_per_batch=blocks_per_batch, n_blk=n_blk)
    proj = (proj0_scr, proj1_scr)
    hbuf = (h0_scr, h1_scr)
    xh = (xh0_scr, xh1_scr)

    def run(p, **flags):
        _layer_step(step, *args, proj[p], proj[1 - p], hbuf[p], hbuf[1 - p], xh[p],
                    y_scr, kprev_scr, vprev_scr, **dims, **flags)

    @pl.when(step == 0)
    def _():
        kprev_scr[...] = jnp.zeros_like(kprev_scr)
        vprev_scr[...] = jnp.zeros_like(vprev_scr)
        run(0, do_chunks=False, do_bc=False)

    @pl.when(step == 1)
    def _():
        run(1, do_bc=False)

    inner = jnp.logical_and(step > 1, step <= n_blk)

    @pl.when(jnp.logical_and(inner, step % 2 == 0))
    def _():
        run(0)

    @pl.when(jnp.logical_and(inner, step % 2 == 1))
    def _():
        run(1)

    @pl.when(step == n_blk + 1)
    def _():
        run((n_blk + 1) % 2, do_chunks=False, do_norm=False)


def _layer_step(step, sink_ref, x_ref, moda_ref, modc_ref, modf_ref, ng_ref, fg_ref,
                cos_ref, sin_ref, lng_ref, lnb_ref, ws_ref, bst_ref, win_ref, wout_ref,
                o_ref, proj_own, proj_oth, h_own, h_oth, xh, y_scr, kprev_scr, vprev_scr,
                *, d_model, d_in, d_a, d_b, d_kv, blocks_per_batch, n_blk,
                do_chunks=True, do_bc=True, do_norm=True):
    blk_c = jnp.clip(step - 2, 0, n_blk - 1)
    first_in_batch = (blk_c % blocks_per_batch) == 0
    proj_r = proj_own

    if do_norm:
        x = x_ref[...]
        ms = jnp.mean(x * x, axis=-1, keepdims=True)
        xn = x * lax.rsqrt(ms + NORM_EPS) * ng_ref[0:1, :]
        shift = moda_ref[:, 0:d_model]
        scale1 = 1.0 + moda_ref[:, d_model:2 * d_model]
        h_own[...] = (xn * scale1 + shift).astype(BF16)

    lng = lng_ref[0:1, :]
    lnb = lnb_ref[0:1, :]
    bst = bst_ref[...]
    n_sub = T_BLK // CHUNK
    blocks = []
    for sb in range(n_sub):
        r0 = sb * CHUNK

        def load(c0, c1, r0=r0):
            return proj_r[r0:r0 + CHUNK, c0:c1]

        def store(c0, val, r0=r0):
            y_scr[r0:r0 + CHUNK, c0:c0 + val.shape[1]] = val

        if sb == 0:
            kprev = lambda: [kprev_scr[e] for e in range(N_EXP)]
            vprev = lambda: [vprev_scr[e] for e in range(N_EXP)]
            has_prev = jnp.logical_not(first_in_batch)
        else:
            kprev = lambda b=blocks[sb - 1]: b.ke
            vprev = lambda b=blocks[sb - 1]: b.ve
            has_prev = True
        blocks.append(_MixerBlock(
            load, store, cos_ref[r0:r0 + CHUNK, :], sin_ref[r0:r0 + CHUNK, :], lng, lnb,
            ws_ref, bst, sink_ref, kprev, vprev, has_prev, d_a=d_a, d_b=d_b, d_kv=d_kv))

    def proj_chunk(n0, h_ref, proj_ref):
        acc = jnp.dot(h_ref[...], win_ref[:, n0:n0 + TN_DOT], preferred_element_type=F32)
        proj_ref[:, n0:n0 + TN_DOT] = acc.astype(BF16)

    chunks = list(range(TN_DOT, d_in, TN_DOT))
    next_chunk = [0]

    def emit_chunks(n):
        for _ in range(n):
            if do_chunks and next_chunk[0] < len(chunks):
                proj_chunk(chunks[next_chunk[0]], h_oth, proj_oth)
                next_chunk[0] += 1

    def norm_tail():
        if do_norm:
            proj_chunk(0, h_own, proj_own)
            xh[...] = x

    if not do_bc:
        emit_chunks(len(chunks))
        norm_tail()
        return

    emit_chunks(1)
    for b in blocks:
        b.prep()
    n_pairs = d_b // LANES
    slots = [(b, list(range(p0, min(p0 + PAIRS_PER_SLOT, n_pairs))))
             for b in blocks for p0 in range(0, n_pairs, PAIRS_PER_SLOT)]
    groups_per_slot = -(-A_GROUPS * n_sub // len(slots))
    group_list = [(b, g) for b in blocks for g in range(A_GROUPS)]
    for p in slots[0][1]:
        slots[0][0].scores(p)
    for j, (b, pairs) in enumerate(slots):
        emit_chunks(1)
        if j + 1 < len(slots):
            for p in slots[j + 1][1]:
                slots[j + 1][0].scores(p)
        for bg, g in group_list[j * groups_per_slot:(j + 1) * groups_per_slot]:
            bg.group(g)
        for p in pairs:
            b.values(p)
    for e in range(N_EXP):
        kprev_scr[e] = blocks[-1].ke[e]
        vprev_scr[e] = blocks[-1].ve[e]
    emit_chunks(len(chunks))

    gate = modc_ref[:, 2 * d_model:3 * d_model]
    ssq = jnp.zeros((T_BLK, 1), F32)
    for n0 in range(0, d_model, TN_DOT):
        sl = slice(n0, n0 + TN_DOT)
        acc = jnp.dot(y_scr[...], wout_ref[:, sl], preferred_element_type=F32)
        xr = xh[:, sl] + gate[:, sl] * acc
        ssq = ssq + jnp.sum(xr * xr, axis=-1, keepdims=True)
        o_ref[:, sl] = xr
    norm_tail()
    inv = lax.rsqrt(ssq * (1.0 / d_model) + NORM_EPS)
    shift_f = modf_ref[:, 0:d_model]
    scale_f = modf_ref[:, d_model:2 * d_model]
    o_ref[...] = (o_ref[...] * inv * fg_ref[0:1, :]) * (1.0 + scale_f) + shift_f


def _rows8(v):
    return jnp.broadcast_to(v.reshape(1, -1), (SUBLANES, v.size))


def _layer(x2, sinks, mod3, modf3, norm_g, final_g, cos_t, sin_t, ln_g, ln_b, w_sp, b_sp_t,
           w_in, w_out, seq, d_a, d_b, d_kv):
    rows, d_model = x2.shape
    d_in = w_in.shape[-1]
    d_mix = d_a + d_b
    n_blk = rows // T_BLK
    bpb = seq // T_BLK
    assert d_model % W_TILES == 0 and d_mix % W_TILES == 0

    def blk_a(i):
        return jnp.clip(i - W_TILES, 0, n_blk - 1)

    def blk_c(i):
        return jnp.clip(i - W_TILES - 2, 0, n_blk - 1)

    def w_tile(i):
        return (jnp.minimum(i, W_TILES - 1), 0)

    const2 = lambda i: (0, 0)
    kern = functools.partial(_layer_kernel, d_model=d_model, d_in=d_in, d_a=d_a, d_b=d_b,
                             d_kv=d_kv, blocks_per_batch=bpb, n_blk=n_blk)
    return pl.pallas_call(
        kern,
        grid=(W_TILES + n_blk + 2,),
        in_specs=[
            pl.BlockSpec(memory_space=pltpu.SMEM),
            pl.BlockSpec((T_BLK, d_model), lambda i: (blk_a(i), 0)),
            pl.BlockSpec((None, 1, 3 * d_model), lambda i: (blk_a(i) // bpb, 0, 0)),
            pl.BlockSpec((None, 1, 3 * d_model), lambda i: (blk_c(i) // bpb, 0, 0)),
            pl.BlockSpec((None, 1, 2 * d_model), lambda i: (blk_c(i) // bpb, 0, 0)),
            pl.BlockSpec((SUBLANES, d_model), const2),
            pl.BlockSpec((SUBLANES, d_model), const2),
            pl.BlockSpec((T_BLK, LANES), lambda i: (blk_c(i) % bpb, 0)),
            pl.BlockSpec((T_BLK, LANES), lambda i: (blk_c(i) % bpb, 0)),
            pl.BlockSpec((SUBLANES, d_a), const2),
            pl.BlockSpec((SUBLANES, d_a), const2),
            pl.BlockSpec((A_GROUPS, CHUNK, CHUNK), lambda i: (0, 0, 0)),
            pl.BlockSpec((CHUNK, A_GROUPS), const2),
            pl.BlockSpec((d_model // W_TILES, d_in), w_tile),
            pl.BlockSpec((d_mix // W_TILES, d_model), w_tile),
        ],
        out_specs=pl.BlockSpec((T_BLK, d_model), lambda i: (blk_c(i), 0)),
        out_shape=jax.ShapeDtypeStruct((rows, d_model), F32),
        scratch_shapes=[
            pltpu.VMEM((d_model, d_in), BF16),
            pltpu.VMEM((d_mix, d_model), BF16),
            pltpu.VMEM((T_BLK, d_in), BF16),
            pltpu.VMEM((T_BLK, d_in), BF16),
            pltpu.VMEM((T_BLK, d_mix), BF16),
            pltpu.VMEM((T_BLK, d_model), F32),
            pltpu.VMEM((T_BLK, d_model), F32),
            pltpu.VMEM((T_BLK, d_model), BF16),
            pltpu.VMEM((T_BLK, d_model), BF16),
            pltpu.VMEM((N_EXP, CHUNK, LANES), BF16),
            pltpu.VMEM((N_EXP, CHUNK, LANES), BF16),
        ],
        compiler_params=pltpu.CompilerParams(
            dimension_semantics=("arbitrary",), vmem_limit_bytes=VMEM_LIMIT_LAYER),
        name="layer",
    )(sinks, x2, mod3, mod3, modf3, _rows8(norm_g), _rows8(final_g),
      cos_t, sin_t, _rows8(ln_g), _rows8(ln_b), w_sp, b_sp_t, w_in, w_out)


def _rope_tables(seq):
    half = HEAD_DIM // 2
    inv_freq = ROPE_THETA ** (-jnp.arange(0, HEAD_DIM, 2, dtype=F32) / HEAD_DIM)
    ang = jnp.arange(seq, dtype=F32)[:, None] * inv_freq[None, :]
    cos = jnp.cos(ang)
    sin = jnp.sin(ang)
    cos_t = jnp.tile(cos, (1, LANES // half))
    sin_t = jnp.tile(jnp.concatenate([-sin, sin], axis=1), (1, HEADS_PER_VREG))
    return cos_t, sin_t


def kernel(x, c, w_ada, b_ada, norm_g, w_in, ln_v_g, ln_v_b, w_spatial, b_spatial, sinks,
           w_out, w_ada_final, b_ada_final, final_norm_g):
    bsz, seq, d_model = x.shape
    assert w_ada.shape[0] == 1, "single-layer stack"
    d_a = ln_v_g.shape[-1]
    d_mix = w_out.shape[-2]
    d_b = d_mix - d_a
    d_kv = N_KV_HEADS * HEAD_DIM
    d_in = w_in.shape[-1]
    assert d_in == 3 * d_a + 2 * d_b + 2 * d_kv
    assert d_a == A_GROUPS * A_GROUP_W and d_b == N_KV_HEADS * Q_PER_KV * HEAD_DIM
    assert seq % T_BLK == 0 and T_BLK % CHUNK == 0

    x2 = x.reshape(bsz * seq, d_model)
    mod, mod_f = _ada_mod(c, w_ada, b_ada, w_ada_final, b_ada_final)
    mod3 = mod.reshape(bsz, 1, 3 * d_model)
    modf3 = mod_f.reshape(bsz, 1, 2 * d_model)
    cos_t, sin_t = _rope_tables(seq)
    out = _layer(x2, sinks.reshape(-1), mod3, modf3, norm_g, final_norm_g, cos_t, sin_t,
                 ln_v_g.reshape(1, d_a), ln_v_b.reshape(1, d_a),
                 w_spatial.reshape(A_GROUPS, CHUNK, CHUNK),
                 b_spatial.reshape(A_GROUPS, CHUNK).T,
                 w_in.reshape(d_model, d_in), w_out.reshape(d_mix, d_model),
                 seq, d_a, d_b, d_kv)
    return out.reshape(bsz, seq, d_model)
```

```python
import functools

import jax
import jax.numpy as jnp
from jax import lax
from jax.experimental import pallas as pl
from jax.experimental.pallas import tpu as pltpu

F32 = jnp.float32
BF16 = jnp.bfloat16

CHUNK = 128
A_GROUPS = 8
A_GROUP_W = 128
HEAD_DIM = 64
N_KV_HEADS = 4
Q_PER_KV = 4
ROPE_THETA = 10000.0
NORM_EPS = 1e-5

LANES = 128
HEADS_PER_VREG = LANES // HEAD_DIM
N_EXP = N_KV_HEADS * HEADS_PER_VREG

T_BLK = 256
TN_DOT = 512
PAIRS_PER_SLOT = 2
W_TILES = 32
TN_ADA = 1024
VMEM_LIMIT_ADA = 40 * 1024 * 1024
VMEM_LIMIT_LAYER = 60 * 1024 * 1024


def _silu(z):
    hz = 0.5 * z
    return hz + hz * jnp.tanh(hz)


def _ada_kernel(c_ref, wa_top, wa_bot, wf_top, wf_bot, ba_ref, bf_ref, oa_ref, of_ref,
                *, n_a_tiles):
    j = pl.program_id(0)
    c = c_ref[...]
    ca = (c * (1.0 / (1.0 + jnp.exp(-c)))).astype(BF16)
    half = ca.shape[1] // 2

    def mod(w_top, w_bot, b_ref, o_ref):
        acc = jnp.dot(ca[:, :half], w_top[...].astype(BF16), preferred_element_type=F32)
        acc = acc + jnp.dot(ca[:, half:], w_bot[...].astype(BF16),
                            preferred_element_type=F32)
        o_ref[...] = acc + b_ref[...]

    @pl.when(j < n_a_tiles)
    def _():
        mod(wa_top, wa_bot, ba_ref, oa_ref)

    @pl.when(j >= n_a_tiles)
    def _():
        mod(wf_top, wf_bot, bf_ref, of_ref)


def _ada_mod(c, w_a, b_a, w_f, b_f):
    bsz, d = c.shape
    n_a, n_f = w_a.shape[-1], w_f.shape[-1]
    w_a = w_a.reshape(d, n_a)
    w_f = w_f.reshape(d, n_f)
    ta, tf = n_a // TN_ADA, n_f // TN_ADA
    half = d // 2
    a_tile = lambda j: jnp.minimum(j, ta - 1)
    f_tile = lambda j: jnp.maximum(j - ta, 0)
    return pl.pallas_call(
        functools.partial(_ada_kernel, n_a_tiles=ta),
        grid=(ta + tf,),
        in_specs=[
            pl.BlockSpec((bsz, d), lambda j: (0, 0)),
            pl.BlockSpec((half, TN_ADA), lambda j: (0, a_tile(j))),
            pl.BlockSpec((half, TN_ADA), lambda j: (1, a_tile(j))),
            pl.BlockSpec((half, TN_ADA), lambda j: (0, f_tile(j))),
            pl.BlockSpec((half, TN_ADA), lambda j: (1, f_tile(j))),
            pl.BlockSpec((1, TN_ADA), lambda j: (0, a_tile(j))),
            pl.BlockSpec((1, TN_ADA), lambda j: (0, f_tile(j))),
        ],
        out_specs=[
            pl.BlockSpec((bsz, TN_ADA), lambda j: (0, a_tile(j))),
            pl.BlockSpec((bsz, TN_ADA), lambda j: (0, f_tile(j))),
        ],
        out_shape=[jax.ShapeDtypeStruct((bsz, n_a), F32),
                   jax.ShapeDtypeStruct((bsz, n_f), F32)],
        compiler_params=pltpu.CompilerParams(
            dimension_semantics=("arbitrary",), vmem_limit_bytes=VMEM_LIMIT_ADA),
        name="ada_mod",
    )(c, w_a, w_a, w_f, w_f, b_a.reshape(1, n_a), b_f.reshape(1, n_f))


class _MixerBlock:
    def __init__(self, load, store, cos, sin, lng, lnb, ws_ref, bst, sink_ref, kprev, vprev,
                 has_prev, *, d_a, d_b, d_kv):
        self.load, self.store = load, store
        self.cos, self.sin, self.lng, self.lnb = cos, sin, lng, lnb
        self.ws_ref, self.bst, self.sink_ref = ws_ref, bst, sink_ref
        self.kprev, self.vprev, self.has_prev = kprev, vprev, has_prev
        self.d_a, self.d_b, self.d_kv = d_a, d_b, d_kv
        self.o_u, self.o_v, self.o_za = 0, d_a, 2 * d_a
        self.o_q = 3 * d_a
        self.o_k = self.o_q + d_b
        self.o_vv = self.o_k + d_kv
        self.o_zb = self.o_vv + d_kv
        row = lax.broadcasted_iota(jnp.int32, (CHUNK, CHUNK), 0)
        col = lax.broadcasted_iota(jnp.int32, (CHUNK, CHUNK), 1)
        self.col = col
        self.causal = col <= row
        self.first_half = (col & (HEAD_DIM - 1)) < (HEAD_DIM // 2)
        self.probs = {}

    def _rope(self, xv):
        rot = jnp.where(self.first_half,
                        pltpu.roll(xv, LANES - HEAD_DIM // 2, 1),
                        pltpu.roll(xv, HEAD_DIM // 2, 1))
        return xv * self.cos + rot * self.sin

    def prep(self):
        va = self.load(self.o_v, self.o_v + self.d_a).astype(F32)
        mu = jnp.mean(va, axis=-1, keepdims=True)
        vc = va - mu
        var = jnp.mean(vc * vc, axis=-1, keepdims=True)
        self.vn = (vc * lax.rsqrt(var + NORM_EPS) * self.lng + self.lnb).astype(BF16)
        self.tril = self.causal.astype(F32)
        low_half = self.col < HEAD_DIM
        self.ke = [None] * N_EXP
        self.ve = [None] * N_EXP
        for c in range(self.d_kv // LANES):
            kc = self._rope(self.load(self.o_k + c * LANES, self.o_k + (c + 1) * LANES)
                            .astype(F32))
            vcol = self.load(self.o_vv + c * LANES, self.o_vv + (c + 1) * LANES).astype(F32)
            kc_sw = pltpu.roll(kc, HEAD_DIM, 1)
            vcol_sw = pltpu.roll(vcol, HEAD_DIM, 1)
            for j in range(HEADS_PER_VREG):
                kv_head = c * HEADS_PER_VREG + j
                for o in range(HEADS_PER_VREG):
                    mask = low_half if o == 0 else jnp.logical_not(low_half)
                    e = kv_head * HEADS_PER_VREG + o
                    self.ke[e] = jnp.where(mask, kc if o == j else kc_sw, 0.0).astype(BF16)
                    self.ve[e] = jnp.where(mask, vcol if o == j else vcol_sw, 0.0).astype(BF16)

    def group_pair(self, g0):
        c0 = g0 * A_GROUP_W
        w = jnp.concatenate([(self.ws_ref[g0] * self.tril).astype(BF16),
                             (self.ws_ref[g0 + 1] * self.tril).astype(BF16)], axis=1)
        zero = jnp.zeros((CHUNK, A_GROUP_W), BF16)
        rhs = jnp.concatenate(
            [jnp.concatenate([self.vn[:, c0:c0 + A_GROUP_W], zero], axis=1),
             jnp.concatenate([zero, self.vn[:, c0 + A_GROUP_W:c0 + 2 * A_GROUP_W]], axis=1)],
            axis=0)
        s2 = jnp.dot(w, rhs, preferred_element_type=F32)
        for k in range(2):
            g = g0 + k
            cg = g * A_GROUP_W
            s = s2[:, k * A_GROUP_W:(k + 1) * A_GROUP_W] + self.bst[:, g:g + 1]
            u = self.load(self.o_u + cg, self.o_u + cg + A_GROUP_W).astype(F32)
            za = self.load(self.o_za + cg, self.o_za + cg + A_GROUP_W).astype(F32)
            self.store(cg, (u * s * _silu(za)).astype(BF16))

    def scores(self, p):
        kv_head = (p * HEADS_PER_VREG) // Q_PER_KV
        kprev = self.kprev()
        q2 = (self._rope(self.load(self.o_q + p * LANES, self.o_q + (p + 1) * LANES)
                         .astype(F32)) * (HEAD_DIM ** -0.5)).astype(BF16)
        probs = []
        for o in range(HEADS_PER_VREG):
            h = p * HEADS_PER_VREG + o
            e = kv_head * HEADS_PER_VREG + o
            kband = jnp.concatenate([kprev[e], self.ke[e]], axis=0)
            s = lax.dot_general(q2, kband, (((1,), (1,)), ((), ())),
                                preferred_element_type=F32)
            s_prev = s[:, :CHUNK]
            if self.has_prev is not True:
                s_prev = jnp.where(self.has_prev, s_prev, -jnp.inf)
            comb = jnp.where(self.causal, s[:, CHUNK:], s_prev)
            sink = self.sink_ref[h]
            m = jnp.maximum(jnp.max(comb, axis=-1, keepdims=True), sink)
            pexp = jnp.exp(comb - m)
            denom = jnp.sum(pexp, axis=-1, keepdims=True) + jnp.exp(sink - m)
            pn = pexp * (1.0 / denom)
            probs.append(jnp.where(self.causal, 0.0, pn).astype(BF16))
            probs.append(jnp.where(self.causal, pn, 0.0).astype(BF16))
        self.probs[p] = jnp.concatenate(probs, axis=1)

    def values(self, p):
        kv_head = (p * HEADS_PER_VREG) // Q_PER_KV
        vprev = self.vprev()
        vband = []
        for o in range(HEADS_PER_VREG):
            e = kv_head * HEADS_PER_VREG + o
            vband.append(vprev[e])
            vband.append(self.ve[e])
        out = jnp.dot(self.probs.pop(p), jnp.concatenate(vband, axis=0),
                      preferred_element_type=F32)
        zb = self.load(self.o_zb + p * LANES, self.o_zb + (p + 1) * LANES).astype(F32)
        self.store(self.d_a + p * LANES, (out * _silu(zb)).astype(BF16))


def _layer_kernel(sink_ref, x_ref, moda_ref, modc_ref, modf_ref, ng_ref, fg_ref,
                  cos_ref, sin_ref, lng_ref, lnb_ref, ws_ref, bst_ref, win32_ref, wout32_ref,
                  o_ref, win_scr, wout_scr, proj0_scr, proj1_scr, y_scr, xprev_scr, h_scr,
                  kprev_scr, vprev_scr,
                  *, d_model, d_in, d_a, d_b, d_kv, blocks_per_batch, n_blk):
    i = pl.program_id(0)

    @pl.when(i < W_TILES)
    def _():
        r_in = pl.multiple_of(i * win32_ref.shape[0], win32_ref.shape[0])
        win_scr[pl.ds(r_in, win32_ref.shape[0]), :] = win32_ref[...].astype(BF16)
        r_out = pl.multiple_of(i * wout32_ref.shape[0], wout32_ref.shape[0])
        wout_scr[pl.ds(r_out, wout32_ref.shape[0]), :] = wout32_ref[...].astype(BF16)

    step = i - W_TILES

    args = (sink_ref, x_ref, moda_ref, modc_ref, modf_ref, ng_ref, fg_ref, cos_ref, sin_ref,
            lng_ref, lnb_ref, ws_ref, bst_ref, win_scr, wout_scr, o_ref)
    scr = (y_scr, xprev_scr, h_scr, kprev_scr, vprev_scr)
    dims = dict(d_model=d_model, d_in=d_in, d_a=d_a, d_b=d_b, d_kv=d_kv,
                blocks_per_batch=blocks_per_batch, n_blk=n_blk)
    bufs = ((proj0_scr, proj1_scr), (proj1_scr, proj0_scr))
    inner = jnp.logical_and(step > 0, step < n_blk)

    @pl.when(step == 0)
    def _():
        kprev_scr[...] = jnp.zeros_like(kprev_scr)
        vprev_scr[...] = jnp.zeros_like(vprev_scr)
        _layer_step(step, *args, *bufs[0], *scr, **dims, do_bc=False)

    @pl.when(jnp.logical_and(inner, step % 2 == 0))
    def _():
        _layer_step(step, *args, *bufs[0], *scr, **dims)

    @pl.when(jnp.logical_and(inner, step % 2 == 1))
    def _():
        _layer_step(step, *args, *bufs[1], *scr, **dims)

    @pl.when(step == n_blk)
    def _():
        _layer_step(step, *args, *bufs[n_blk % 2], *scr, **dims, do_a=False)


def _layer_step(step, sink_ref, x_ref, moda_ref, modc_ref, modf_ref, ng_ref, fg_ref,
                cos_ref, sin_ref, lng_ref, lnb_ref, ws_ref, bst_ref, win_ref, wout_ref,
                o_ref, proj_w, proj_r, y_scr, xprev_scr, h_scr, kprev_scr, vprev_scr,
                *, d_model, d_in, d_a, d_b, d_kv, blocks_per_batch, n_blk,
                do_a=True, do_bc=True):
    blk_c = jnp.clip(step - 1, 0, n_blk - 1)
    first_in_batch = (blk_c % blocks_per_batch) == 0

    if do_a:
        x = x_ref[...]
        ms = jnp.mean(x * x, axis=-1, keepdims=True)
        shift = moda_ref[:, 0:d_model]
        gain = ng_ref[...] * (1.0 + moda_ref[:, d_model:2 * d_model])
        h_scr[...] = (x * lax.rsqrt(ms + NORM_EPS) * gain + shift).astype(BF16)

    lng = lng_ref[...]
    lnb = lnb_ref[...]
    bst = bst_ref[...]
    n_sub = T_BLK // CHUNK
    blocks = []
    for sb in range(n_sub):
        r0 = sb * CHUNK

        def load(c0, c1, r0=r0):
            return proj_r[r0:r0 + CHUNK, c0:c1]

        def store(c0, val, r0=r0):
            y_scr[r0:r0 + CHUNK, c0:c0 + val.shape[1]] = val

        if sb == 0:
            kprev = lambda: [kprev_scr[e] for e in range(N_EXP)]
            vprev = lambda: [vprev_scr[e] for e in range(N_EXP)]
            has_prev = jnp.logical_not(first_in_batch)
        else:
            kprev = lambda b=blocks[sb - 1]: b.ke
            vprev = lambda b=blocks[sb - 1]: b.ve
            has_prev = True
        blocks.append(_MixerBlock(
            load, store, cos_ref[r0:r0 + CHUNK, :], sin_ref[r0:r0 + CHUNK, :], lng, lnb,
            ws_ref, bst, sink_ref, kprev, vprev, has_prev, d_a=d_a, d_b=d_b, d_kv=d_kv))

    def proj_chunk(n0):
        acc = jnp.dot(h_scr[...], win_ref[:, n0:n0 + TN_DOT], preferred_element_type=F32)
        proj_w[:, n0:n0 + TN_DOT] = acc.astype(BF16)

    chunks = list(range(0, d_in, TN_DOT))
    next_chunk = [0]

    def emit_chunks(n):
        for _ in range(n):
            if do_a and next_chunk[0] < len(chunks):
                proj_chunk(chunks[next_chunk[0]])
                next_chunk[0] += 1

    if not do_bc:
        emit_chunks(len(chunks))
        xprev_scr[...] = x
        return

    for b in blocks:
        b.prep()
    n_pairs = d_b // LANES
    slots = [(b, list(range(p0, min(p0 + PAIRS_PER_SLOT, n_pairs))))
             for b in blocks for p0 in range(0, n_pairs, PAIRS_PER_SLOT)]
    group_pairs = [(b, g0) for b in blocks for g0 in range(0, A_GROUPS, 2)]
    pairs_per_slot = -(-len(group_pairs) // len(slots))
    for p in slots[0][1]:
        slots[0][0].scores(p)
    for j, (b, pairs) in enumerate(slots):
        emit_chunks(1)
        if j + 1 < len(slots):
            for p in slots[j + 1][1]:
                slots[j + 1][0].scores(p)
        for bg, g0 in group_pairs[j * pairs_per_slot:(j + 1) * pairs_per_slot]:
            bg.group_pair(g0)
        for p in pairs:
            b.values(p)
    for e in range(N_EXP):
        kprev_scr[e] = blocks[-1].ke[e]
        vprev_scr[e] = blocks[-1].ve[e]
    emit_chunks(len(chunks) - next_chunk[0] - 2)

    gate = modc_ref[:, 2 * d_model:3 * d_model]
    ssq = jnp.zeros((T_BLK, 1), F32)
    for n0 in range(0, d_model, TN_DOT):
        sl = slice(n0, n0 + TN_DOT)
        acc = jnp.dot(y_scr[...], wout_ref[:, sl], preferred_element_type=F32)
        xr = xprev_scr[:, sl] + gate[:, sl] * acc
        ssq = ssq + jnp.sum(xr * xr, axis=-1, keepdims=True)
        o_ref[:, sl] = xr
    emit_chunks(len(chunks))
    inv = lax.rsqrt(ssq * (1.0 / d_model) + NORM_EPS)
    shift_f = modf_ref[:, 0:d_model]
    scale_f = modf_ref[:, d_model:2 * d_model]
    o_ref[...] = o_ref[...] * inv * (fg_ref[...] * (1.0 + scale_f)) + shift_f

    if do_a:
        xprev_scr[...] = x


def _layer(x2, sinks, mod3, modf3, norm_g, final_g, cos_t, sin_t, ln_g, ln_b, w_sp, b_sp_t,
           w_in, w_out, seq, d_a, d_b, d_kv):
    rows, d_model = x2.shape
    d_in = w_in.shape[-1]
    d_mix = d_a + d_b
    n_blk = rows // T_BLK
    bpb = seq // T_BLK
    assert d_model % W_TILES == 0 and d_mix % W_TILES == 0

    def blk_a(i):
        return jnp.clip(i - W_TILES, 0, n_blk - 1)

    def blk_c(i):
        return jnp.clip(i - W_TILES - 1, 0, n_blk - 1)

    def w_tile(i):
        return (jnp.minimum(i, W_TILES - 1), 0)

    const2 = lambda i: (0, 0)
    kern = functools.partial(_layer_kernel, d_model=d_model, d_in=d_in, d_a=d_a, d_b=d_b,
                             d_kv=d_kv, blocks_per_batch=bpb, n_blk=n_blk)
    return pl.pallas_call(
        kern,
        grid=(W_TILES + n_blk + 1,),
        in_specs=[
            pl.BlockSpec(memory_space=pltpu.SMEM),
            pl.BlockSpec((T_BLK, d_model), lambda i: (blk_a(i), 0)),
            pl.BlockSpec((None, 1, 3 * d_model), lambda i: (blk_a(i) // bpb, 0, 0)),
            pl.BlockSpec((None, 1, 3 * d_model), lambda i: (blk_c(i) // bpb, 0, 0)),
            pl.BlockSpec((None, 1, 2 * d_model), lambda i: (blk_c(i) // bpb, 0, 0)),
            pl.BlockSpec((1, d_model), const2),
            pl.BlockSpec((1, d_model), const2),
            pl.BlockSpec((T_BLK, LANES), lambda i: (blk_c(i) % bpb, 0)),
            pl.BlockSpec((T_BLK, LANES), lambda i: (blk_c(i) % bpb, 0)),
            pl.BlockSpec((1, d_a), const2),
            pl.BlockSpec((1, d_a), const2),
            pl.BlockSpec((A_GROUPS, CHUNK, CHUNK), lambda i: (0, 0, 0)),
            pl.BlockSpec((CHUNK, A_GROUPS), const2),
            pl.BlockSpec((d_model // W_TILES, d_in), w_tile),
            pl.BlockSpec((d_mix // W_TILES, d_model), w_tile),
        ],
        out_specs=pl.BlockSpec((T_BLK, d_model), lambda i: (blk_c(i), 0)),
        out_shape=jax.ShapeDtypeStruct((rows, d_model), F32),
        scratch_shapes=[
            pltpu.VMEM((d_model, d_in), BF16),
            pltpu.VMEM((d_mix, d_model), BF16),
            pltpu.VMEM((T_BLK, d_in), BF16),
            pltpu.VMEM((T_BLK, d_in), BF16),
            pltpu.VMEM((T_BLK, d_mix), BF16),
            pltpu.VMEM((T_BLK, d_model), F32),
            pltpu.VMEM((T_BLK, d_model), BF16),
            pltpu.VMEM((N_EXP, CHUNK, LANES), BF16),
            pltpu.VMEM((N_EXP, CHUNK, LANES), BF16),
        ],
        compiler_params=pltpu.CompilerParams(
            dimension_semantics=("arbitrary",), vmem_limit_bytes=VMEM_LIMIT_LAYER),
        name="layer",
    )(sinks, x2, mod3, mod3, modf3, norm_g.reshape(1, d_model), final_g.reshape(1, d_model),
      cos_t, sin_t, ln_g, ln_b, w_sp, b_sp_t, w_in, w_out)


def _rope_tables(seq):
    half = HEAD_DIM // 2
    inv_freq = ROPE_THETA ** (-jnp.arange(0, HEAD_DIM, 2, dtype=F32) / HEAD_DIM)
    ang = jnp.arange(seq, dtype=F32)[:, None] * inv_freq[None, :]
    cos = jnp.cos(ang)
    sin = jnp.sin(ang)
    cos_t = jnp.tile(cos, (1, LANES // half))
    sin_t = jnp.tile(jnp.concatenate([-sin, sin], axis=1), (1, HEADS_PER_VREG))
    return cos_t, sin_t


def kernel(x, c, w_ada, b_ada, norm_g, w_in, ln_v_g, ln_v_b, w_spatial, b_spatial, sinks,
           w_out, w_ada_final, b_ada_final, final_norm_g):
    bsz, seq, d_model = x.shape
    assert w_ada.shape[0] == 1, "single-layer stack"
    d_a = ln_v_g.shape[-1]
    d_mix = w_out.shape[-2]
    d_b = d_mix - d_a
    d_kv = N_KV_HEADS * HEAD_DIM
    d_in = w_in.shape[-1]
    assert d_in == 3 * d_a + 2 * d_b + 2 * d_kv
    assert d_a == A_GROUPS * A_GROUP_W and d_b == N_KV_HEADS * Q_PER_KV * HEAD_DIM
    assert seq % T_BLK == 0 and T_BLK % CHUNK == 0

    x2 = x.reshape(bsz * seq, d_model)
    mod, mod_f = _ada_mod(c, w_ada, b_ada, w_ada_final, b_ada_final)
    mod3 = mod.reshape(bsz, 1, 3 * d_model)
    modf3 = mod_f.reshape(bsz, 1, 2 * d_model)
    cos_t, sin_t = _rope_tables(seq)
    out = _layer(x2, sinks.reshape(-1), mod3, modf3, norm_g, final_norm_g, cos_t, sin_t,
                 ln_v_g.reshape(1, d_a), ln_v_b.reshape(1, d_a),
                 w_spatial.reshape(A_GROUPS, CHUNK, CHUNK),
                 b_spatial.reshape(A_GROUPS, CHUNK).T,
                 w_in.reshape(d_model, d_in), w_out.reshape(d_mix, d_model),
                 seq, d_a, d_b, d_kv)
    return out.reshape(bsz, seq, d_model)
```

```python
import functools

import jax
import jax.numpy as jnp
from jax import lax
from jax.experimental import pallas as pl
from jax.experimental.pallas import tpu as pltpu

F32 = jnp.float32
BF16 = jnp.bfloat16

CHUNK = 128
A_GROUPS = 8
A_GROUP_W = 128
HEAD_DIM = 64
N_KV_HEADS = 4
Q_PER_KV = 4
ROPE_THETA = 10000.0
NORM_EPS = 1e-5

LANES = 128
HEADS_PER_VREG = LANES // HEAD_DIM
N_EXP = N_KV_HEADS * HEADS_PER_VREG
COLS_PER_KV = Q_PER_KV // HEADS_PER_VREG

T_BLK = 256
TN_DOT = 512
W_TILES = 32
TN_ADA = 1024
VMEM_LIMIT_ADA = 40 * 1024 * 1024
VMEM_LIMIT_LAYER = 60 * 1024 * 1024


def _silu(z):
    hz = 0.5 * z
    return hz + hz * jnp.tanh(hz)


def _ada_kernel(c_ref, wa_top, wa_bot, wf_top, wf_bot, ba_ref, bf_ref, oa_ref, of_ref,
                *, n_a_tiles):
    j = pl.program_id(0)
    c = c_ref[...]
    ca = (c * (1.0 / (1.0 + jnp.exp(-c)))).astype(BF16)
    half = ca.shape[1] // 2

    def mod(w_top, w_bot, b_ref, o_ref):
        acc = jnp.dot(ca[:, :half], w_top[...].astype(BF16), preferred_element_type=F32)
        acc = acc + jnp.dot(ca[:, half:], w_bot[...].astype(BF16),
                            preferred_element_type=F32)
        o_ref[...] = acc + b_ref[...]

    @pl.when(j < n_a_tiles)
    def _():
        mod(wa_top, wa_bot, ba_ref, oa_ref)

    @pl.when(j >= n_a_tiles)
    def _():
        mod(wf_top, wf_bot, bf_ref, of_ref)


def _ada_mod(c, w_a, b_a, w_f, b_f):
    bsz, d = c.shape
    n_a, n_f = w_a.shape[-1], w_f.shape[-1]
    w_a = w_a.reshape(d, n_a)
    w_f = w_f.reshape(d, n_f)
    ta, tf = n_a // TN_ADA, n_f // TN_ADA
    half = d // 2
    a_tile = lambda j: jnp.minimum(j, ta - 1)
    f_tile = lambda j: jnp.maximum(j - ta, 0)
    return pl.pallas_call(
        functools.partial(_ada_kernel, n_a_tiles=ta),
        grid=(ta + tf,),
        in_specs=[
            pl.BlockSpec((bsz, d), lambda j: (0, 0)),
            pl.BlockSpec((half, TN_ADA), lambda j: (0, a_tile(j))),
            pl.BlockSpec((half, TN_ADA), lambda j: (1, a_tile(j))),
            pl.BlockSpec((half, TN_ADA), lambda j: (0, f_tile(j))),
            pl.BlockSpec((half, TN_ADA), lambda j: (1, f_tile(j))),
            pl.BlockSpec((1, TN_ADA), lambda j: (0, a_tile(j))),
            pl.BlockSpec((1, TN_ADA), lambda j: (0, f_tile(j))),
        ],
        out_specs=[
            pl.BlockSpec((bsz, TN_ADA), lambda j: (0, a_tile(j))),
            pl.BlockSpec((bsz, TN_ADA), lambda j: (0, f_tile(j))),
        ],
        out_shape=[jax.ShapeDtypeStruct((bsz, n_a), F32),
                   jax.ShapeDtypeStruct((bsz, n_f), F32)],
        compiler_params=pltpu.CompilerParams(
            dimension_semantics=("arbitrary",), vmem_limit_bytes=VMEM_LIMIT_ADA),
        name="ada_mod",
    )(c, w_a, w_a, w_f, w_f, b_a.reshape(1, n_a), b_f.reshape(1, n_f))


class _MixerBlock:
    def __init__(self, load, store, cos, sin, lng, lnb, ws_ref, bst, sink_ref, kprev, vprev,
                 has_prev, *, d_a, d_b, d_kv):
        self.load, self.store = load, store
        self.cos, self.sin, self.lng, self.lnb = cos, sin, lng, lnb
        self.ws_ref, self.bst, self.sink_ref = ws_ref, bst, sink_ref
        self.kprev, self.vprev, self.has_prev = kprev, vprev, has_prev
        self.d_a, self.d_b, self.d_kv = d_a, d_b, d_kv
        self.o_u, self.o_v, self.o_za = 0, d_a, 2 * d_a
        self.o_q = 3 * d_a
        self.o_k = self.o_q + d_b
        self.o_vv = self.o_k + d_kv
        self.o_zb = self.o_vv + d_kv
        row = lax.broadcasted_iota(jnp.int32, (CHUNK, CHUNK), 0)
        col = lax.broadcasted_iota(jnp.int32, (CHUNK, CHUNK), 1)
        self.col = col
        self.causal = col <= row
        self.first_half = (col & (HEAD_DIM - 1)) < (HEAD_DIM // 2)
        self.probs = {}

    def _rope(self, xv):
        rot = jnp.where(self.first_half,
                        pltpu.roll(xv, LANES - HEAD_DIM // 2, 1),
                        pltpu.roll(xv, HEAD_DIM // 2, 1))
        return xv * self.cos + rot * self.sin

    def prep(self):
        va = self.load(self.o_v, self.o_v + self.d_a).astype(F32)
        mu = jnp.mean(va, axis=-1, keepdims=True)
        vc = va - mu
        var = jnp.mean(vc * vc, axis=-1, keepdims=True)
        self.vn = (vc * lax.rsqrt(var + NORM_EPS) * self.lng + self.lnb).astype(BF16)
        self.tril = self.causal.astype(F32)
        low_half = self.col < HEAD_DIM
        self.ke = [None] * N_EXP
        self.ve = [None] * N_EXP
        for c in range(self.d_kv // LANES):
            kc = self._rope(self.load(self.o_k + c * LANES, self.o_k + (c + 1) * LANES)
                            .astype(F32))
            vcol = self.load(self.o_vv + c * LANES, self.o_vv + (c + 1) * LANES).astype(F32)
            kc_sw = pltpu.roll(kc, HEAD_DIM, 1)
            vcol_sw = pltpu.roll(vcol, HEAD_DIM, 1)
            for j in range(HEADS_PER_VREG):
                kv_head = c * HEADS_PER_VREG + j
                for o in range(HEADS_PER_VREG):
                    mask = low_half if o == 0 else jnp.logical_not(low_half)
                    e = kv_head * HEADS_PER_VREG + o
                    self.ke[e] = jnp.where(mask, kc if o == j else kc_sw, 0.0).astype(BF16)
                    self.ve[e] = jnp.where(mask, vcol if o == j else vcol_sw, 0.0).astype(BF16)

    def group_pair(self, g0):
        c0 = g0 * A_GROUP_W
        w = jnp.concatenate([(self.ws_ref[g0] * self.tril).astype(BF16),
                             (self.ws_ref[g0 + 1] * self.tril).astype(BF16)], axis=1)
        zero = jnp.zeros((CHUNK, A_GROUP_W), BF16)
        rhs = jnp.concatenate(
            [jnp.concatenate([self.vn[:, c0:c0 + A_GROUP_W], zero], axis=1),
             jnp.concatenate([zero, self.vn[:, c0 + A_GROUP_W:c0 + 2 * A_GROUP_W]], axis=1)],
            axis=0)
        s2 = jnp.dot(w, rhs, preferred_element_type=F32)
        for k in range(2):
            g = g0 + k
            cg = g * A_GROUP_W
            s = s2[:, k * A_GROUP_W:(k + 1) * A_GROUP_W] + self.bst[:, g:g + 1]
            u = self.load(self.o_u + cg, self.o_u + cg + A_GROUP_W).astype(F32)
            za = self.load(self.o_za + cg, self.o_za + cg + A_GROUP_W).astype(F32)
            self.store(cg, (u * s * _silu(za)).astype(BF16))

    def scores(self, p0):
        kv_head = (p0 * HEADS_PER_VREG) // Q_PER_KV
        kprev = self.kprev()
        q4 = jnp.concatenate(
            [(self._rope(self.load(self.o_q + p * LANES, self.o_q + (p + 1) * LANES)
                         .astype(F32)) * (HEAD_DIM ** -0.5)).astype(BF16)
             for p in range(p0, p0 + COLS_PER_KV)], axis=0)
        probs = [[] for _ in range(COLS_PER_KV)]
        for o in range(HEADS_PER_VREG):
            e = kv_head * HEADS_PER_VREG + o
            kband = jnp.concatenate([kprev[e], self.ke[e]], axis=0)
            s4 = lax.dot_general(q4, kband, (((1,), (1,)), ((), ())),
                                 preferred_element_type=F32)
            for a in range(COLS_PER_KV):
                s = s4[a * CHUNK:(a + 1) * CHUNK]
                h = (p0 + a) * HEADS_PER_VREG + o
                s_prev = s[:, :CHUNK]
                if self.has_prev is not True:
                    s_prev = jnp.where(self.has_prev, s_prev, -jnp.inf)
                comb = jnp.where(self.causal, s[:, CHUNK:], s_prev)
                sink = self.sink_ref[h]
                m = jnp.maximum(jnp.max(comb, axis=-1, keepdims=True), sink)
                pexp = jnp.exp(comb - m)
                denom = jnp.sum(pexp, axis=-1, keepdims=True) + jnp.exp(sink - m)
                pn = pexp * (1.0 / denom)
                probs[a].append(jnp.where(self.causal, 0.0, pn).astype(BF16))
                probs[a].append(jnp.where(self.causal, pn, 0.0).astype(BF16))
        self.probs[p0] = jnp.concatenate(
            [jnp.concatenate(pa, axis=1) for pa in probs], axis=0)

    def values(self, p0):
        kv_head = (p0 * HEADS_PER_VREG) // Q_PER_KV
        vprev = self.vprev()
        vband = []
        for o in range(HEADS_PER_VREG):
            e = kv_head * HEADS_PER_VREG + o
            vband.append(vprev[e])
            vband.append(self.ve[e])
        out4 = jnp.dot(self.probs.pop(p0), jnp.concatenate(vband, axis=0),
                       preferred_element_type=F32)
        for a in range(COLS_PER_KV):
            p = p0 + a
            zb = self.load(self.o_zb + p * LANES, self.o_zb + (p + 1) * LANES).astype(F32)
            self.store(self.d_a + p * LANES,
                       (out4[a * CHUNK:(a + 1) * CHUNK] * _silu(zb)).astype(BF16))


def _layer_kernel(sink_ref, x_ref, moda_ref, modc_ref, modf_ref, ng_ref, fg_ref,
                  cos_ref, sin_ref, lng_ref, lnb_ref, ws_ref, bst_ref, win32_ref, wout32_ref,
                  o_ref, win_scr, wout_scr, proj0_scr, proj1_scr, y_scr, xprev_scr, h_scr,
                  kprev_scr, vprev_scr,
                  *, d_model, d_in, d_a, d_b, d_kv, blocks_per_batch, n_blk):
    i = pl.program_id(0)

    @pl.when(i < W_TILES)
    def _():
        r_in = pl.multiple_of(i * win32_ref.shape[0], win32_ref.shape[0])
        win_scr[pl.ds(r_in, win32_ref.shape[0]), :] = win32_ref[...].astype(BF16)
        r_out = pl.multiple_of(i * wout32_ref.shape[0], wout32_ref.shape[0])
        wout_scr[pl.ds(r_out, wout32_ref.shape[0]), :] = wout32_ref[...].astype(BF16)

    step = i - W_TILES

    args = (sink_ref, x_ref, moda_ref, modc_ref, modf_ref, ng_ref, fg_ref, cos_ref, sin_ref,
            lng_ref, lnb_ref, ws_ref, bst_ref, win_scr, wout_scr, o_ref)
    scr = (y_scr, xprev_scr, h_scr, kprev_scr, vprev_scr)
    dims = dict(d_model=d_model, d_in=d_in, d_a=d_a, d_b=d_b, d_kv=d_kv,
                blocks_per_batch=blocks_per_batch, n_blk=n_blk)
    bufs = ((proj0_scr, proj1_scr), (proj1_scr, proj0_scr))
    inner = jnp.logical_and(step > 0, step < n_blk)

    @pl.when(step == 0)
    def _():
        kprev_scr[...] = jnp.zeros_like(kprev_scr)
        vprev_scr[...] = jnp.zeros_like(vprev_scr)
        _layer_step(step, *args, *bufs[0], *scr, **dims, do_bc=False)

    @pl.when(jnp.logical_and(inner, step % 2 == 0))
    def _():
        _layer_step(step, *args, *bufs[0], *scr, **dims)

    @pl.when(jnp.logical_and(inner, step % 2 == 1))
    def _():
        _layer_step(step, *args, *bufs[1], *scr, **dims)

    @pl.when(step == n_blk)
    def _():
        _layer_step(step, *args, *bufs[n_blk % 2], *scr, **dims, do_a=False)


def _layer_step(step, sink_ref, x_ref, moda_ref, modc_ref, modf_ref, ng_ref, fg_ref,
                cos_ref, sin_ref, lng_ref, lnb_ref, ws_ref, bst_ref, win_ref, wout_ref,
                o_ref, proj_w, proj_r, y_scr, xprev_scr, h_scr, kprev_scr, vprev_scr,
                *, d_model, d_in, d_a, d_b, d_kv, blocks_per_batch, n_blk,
                do_a=True, do_bc=True):
    blk_c = jnp.clip(step - 1, 0, n_blk - 1)
    first_in_batch = (blk_c % blocks_per_batch) == 0

    if do_a:
        x = x_ref[...]
        ms = jnp.mean(x * x, axis=-1, keepdims=True)
        shift = moda_ref[:, 0:d_model]
        gain = ng_ref[...] * (1.0 + moda_ref[:, d_model:2 * d_model])
        h_scr[...] = (x * lax.rsqrt(ms + NORM_EPS) * gain + shift).astype(BF16)

    lng = lng_ref[...]
    lnb = lnb_ref[...]
    bst = bst_ref[...]
    n_sub = T_BLK // CHUNK
    blocks = []
    for sb in range(n_sub):
        r0 = sb * CHUNK

        def load(c0, c1, r0=r0):
            return proj_r[r0:r0 + CHUNK, c0:c1]

        def store(c0, val, r0=r0):
            y_scr[r0:r0 + CHUNK, c0:c0 + val.shape[1]] = val

        if sb == 0:
            kprev = lambda: [kprev_scr[e] for e in range(N_EXP)]
            vprev = lambda: [vprev_scr[e] for e in range(N_EXP)]
            has_prev = jnp.logical_not(first_in_batch)
        else:
            kprev = lambda b=blocks[sb - 1]: b.ke
            vprev = lambda b=blocks[sb - 1]: b.ve
            has_prev = True
        blocks.append(_MixerBlock(
            load, store, cos_ref[r0:r0 + CHUNK, :], sin_ref[r0:r0 + CHUNK, :], lng, lnb,
            ws_ref, bst, sink_ref, kprev, vprev, has_prev, d_a=d_a, d_b=d_b, d_kv=d_kv))

    def proj_chunk(n0):
        acc = jnp.dot(h_scr[...], win_ref[:, n0:n0 + TN_DOT], preferred_element_type=F32)
        proj_w[:, n0:n0 + TN_DOT] = acc.astype(BF16)

    chunks = list(range(0, d_in, TN_DOT))
    next_chunk = [0]

    def emit_chunks(n):
        for _ in range(n):
            if do_a and next_chunk[0] < len(chunks):
                proj_chunk(chunks[next_chunk[0]])
                next_chunk[0] += 1

    if not do_bc:
        emit_chunks(len(chunks))
        xprev_scr[...] = x
        return

    for b in blocks:
        b.prep()
    n_pairs = d_b // LANES
    slots = [(b, p0) for b in blocks for p0 in range(0, n_pairs, COLS_PER_KV)]
    group_pairs = [(b, g0) for b in blocks for g0 in range(0, A_GROUPS, 2)]
    pairs_per_slot = -(-len(group_pairs) // len(slots))
    slots[0][0].scores(slots[0][1])
    for j, (b, p0) in enumerate(slots):
        emit_chunks(1)
        if j + 1 < len(slots):
            slots[j + 1][0].scores(slots[j + 1][1])
        for bg, g0 in group_pairs[j * pairs_per_slot:(j + 1) * pairs_per_slot]:
            bg.group_pair(g0)
        b.values(p0)
    for e in range(N_EXP):
        kprev_scr[e] = blocks[-1].ke[e]
        vprev_scr[e] = blocks[-1].ve[e]
    emit_chunks(len(chunks) - next_chunk[0] - 2)

    gate = modc_ref[:, 2 * d_model:3 * d_model]
    ssq = jnp.zeros((T_BLK, 1), F32)
    for n0 in range(0, d_model, TN_DOT):
        sl = slice(n0, n0 + TN_DOT)
        acc = jnp.dot(y_scr[...], wout_ref[:, sl], preferred_element_type=F32)
        xr = xprev_scr[:, sl] + gate[:, sl] * acc
        ssq = ssq + jnp.sum(xr * xr, axis=-1, keepdims=True)
        o_ref[:, sl] = xr
    emit_chunks(len(chunks))
    inv = lax.rsqrt(ssq * (1.0 / d_model) + NORM_EPS)
    shift_f = modf_ref[:, 0:d_model]
    scale_f = modf_ref[:, d_model:2 * d_model]
    o_ref[...] = o_ref[...] * inv * (fg_ref[...] * (1.0 + scale_f)) + shift_f

    if do_a:
        xprev_scr[...] = x


def _layer(x2, sinks, mod3, modf3, norm_g, final_g, cos_t, sin_t, ln_g, ln_b, w_sp, b_sp_t,
           w_in, w_out, seq, d_a, d_b, d_kv):
    rows, d_model = x2.shape
    d_in = w_in.shape[-1]
    d_mix = d_a + d_b
    n_blk = rows // T_BLK
    bpb = seq // T_BLK
    assert d_model % W_TILES == 0 and d_mix % W_TILES == 0

    def blk_a(i):
        return jnp.clip(i - W_TILES, 0, n_blk - 1)

    def blk_c(i):
        return jnp.clip(i - W_TILES - 1, 0, n_blk - 1)

    def w_tile(i):
        return (jnp.minimum(i, W_TILES - 1), 0)

    const2 = lambda i: (0, 0)
    kern = functools.partial(_layer_kernel, d_model=d_model, d_in=d_in, d_a=d_a, d_b=d_b,
                             d_kv=d_kv, blocks_per_batch=bpb, n_blk=n_blk)
    return pl.pallas_call(
        kern,
        grid=(W_TILES + n_blk + 1,),
        in_specs=[
            pl.BlockSpec(memory_space=pltpu.SMEM),
            pl.BlockSpec((T_BLK, d_model), lambda i: (blk_a(i), 0)),
            pl.BlockSpec((None, 1, 3 * d_model), lambda i: (blk_a(i) // bpb, 0, 0)),
            pl.BlockSpec((None, 1, 3 * d_model), lambda i: (blk_c(i) // bpb, 0, 0)),
            pl.BlockSpec((None, 1, 2 * d_model), lambda i: (blk_c(i) // bpb, 0, 0)),
            pl.BlockSpec((1, d_model), const2),
            pl.BlockSpec((1, d_model), const2),
            pl.BlockSpec((T_BLK, LANES), lambda i: (blk_c(i) % bpb, 0)),
            pl.BlockSpec((T_BLK, LANES), lambda i: (blk_c(i) % bpb, 0)),
            pl.BlockSpec((1, d_a), const2),
            pl.BlockSpec((1, d_a), const2),
            pl.BlockSpec((A_GROUPS, CHUNK, CHUNK), lambda i: (0, 0, 0)),
            pl.BlockSpec((CHUNK, A_GROUPS), const2),
            pl.BlockSpec((d_model // W_TILES, d_in), w_tile),
            pl.BlockSpec((d_mix // W_TILES, d_model), w_tile),
        ],
        out_specs=pl.BlockSpec((T_BLK, d_model), lambda i: (blk_c(i), 0)),
        out_shape=jax.ShapeDtypeStruct((rows, d_model), F32),
        scratch_shapes=[
            pltpu.VMEM((d_model, d_in), BF16),
            pltpu.VMEM((d_mix, d_model), BF16),
            pltpu.VMEM((T_BLK, d_in), BF16),
            pltpu.VMEM((T_BLK, d_in), BF16),
            pltpu.VMEM((T_BLK, d_mix), BF16),
            pltpu.VMEM((T_BLK, d_model), F32),
            pltpu.VMEM((T_BLK, d_model), BF16),
            pltpu.VMEM((N_EXP, CHUNK, LANES), BF16),
            pltpu.VMEM((N_EXP, CHUNK, LANES), BF16),
        ],
        compiler_params=pltpu.CompilerParams(
            dimension_semantics=("arbitrary",), vmem_limit_bytes=VMEM_LIMIT_LAYER),
        name="layer",
    )(sinks, x2, mod3, mod3, modf3, norm_g.reshape(1, d_model), final_g.reshape(1, d_model),
      cos_t, sin_t, ln_g, ln_b, w_sp, b_sp_t, w_in, w_out)


def _rope_tables(seq):
    half = HEAD_DIM // 2
    inv_freq = ROPE_THETA ** (-jnp.arange(0, HEAD_DIM, 2, dtype=F32) / HEAD_DIM)
    ang = jnp.arange(seq, dtype=F32)[:, None] * inv_freq[None, :]
    cos = jnp.cos(ang)
    sin = jnp.sin(ang)
    cos_t = jnp.tile(cos, (1, LANES // half))
    sin_t = jnp.tile(jnp.concatenate([-sin, sin], axis=1), (1, HEADS_PER_VREG))
    return cos_t, sin_t


def kernel(x, c, w_ada, b_ada, norm_g, w_in, ln_v_g, ln_v_b, w_spatial, b_spatial, sinks,
           w_out, w_ada_final, b_ada_final, final_norm_g):
    bsz, seq, d_model = x.shape
    assert w_ada.shape[0] == 1, "single-layer stack"
    d_a = ln_v_g.shape[-1]
    d_mix = w_out.shape[-2]
    d_b = d_mix - d_a
    d_kv = N_KV_HEADS * HEAD_DIM
    d_in = w_in.shape[-1]
    assert d_in == 3 * d_a + 2 * d_b + 2 * d_kv
    assert d_a == A_GROUPS * A_GROUP_W and d_b == N_KV_HEADS * Q_PER_KV * HEAD_DIM
    assert seq % T_BLK == 0 and T_BLK % CHUNK == 0

    x2 = x.reshape(bsz * seq, d_model)
    mod, mod_f = _ada_mod(c, w_ada, b_ada, w_ada_final, b_ada_final)
    mod3 = mod.reshape(bsz, 1, 3 * d_model)
    modf3 = mod_f.reshape(bsz, 1, 2 * d_model)
    cos_t, sin_t = _rope_tables(seq)
    out = _layer(x2, sinks.reshape(-1), mod3, modf3, norm_g, final_norm_g, cos_t, sin_t,
                 ln_v_g.reshape(1, d_a), ln_v_b.reshape(1, d_a),
                 w_spatial.reshape(A_GROUPS, CHUNK, CHUNK),
                 b_spatial.reshape(A_GROUPS, CHUNK).T,
                 w_in.reshape(d_model, d_in), w_out.reshape(d_mix, d_model),
                 seq, d_a, d_b, d_kv)
    return out.reshape(bsz, seq, d_model)
```

```python
import functools

import jax
import jax.numpy as jnp
from jax import lax
from jax.experimental import pallas as pl
from jax.experimental.pallas import tpu as pltpu

F32 = jnp.float32
BF16 = jnp.bfloat16

CHUNK = 128
A_GROUPS = 8
A_GROUP_W = 128
HEAD_DIM = 64
N_KV_HEADS = 4
Q_PER_KV = 4
ROPE_THETA = 10000.0
NORM_EPS = 1e-5
LOG2E = 1.4426950408889634

LANES = 128
HEADS_PER_VREG = LANES // HEAD_DIM
N_EXP = N_KV_HEADS * HEADS_PER_VREG
COLS_PER_KV = Q_PER_KV // HEADS_PER_VREG

T_BLK = 256
TN_DOT = 512
W_TILES = 32
TN_ADA = 1024
VMEM_LIMIT_ADA = 40 * 1024 * 1024
VMEM_LIMIT_LAYER = 60 * 1024 * 1024


def _silu(z):
    hz = 0.5 * z
    return hz + hz * jnp.tanh(hz)


def _ada_kernel(c_ref, wa_top, wa_bot, wf_top, wf_bot, ba_ref, bf_ref, oa_ref, of_ref,
                *, n_a_tiles):
    j = pl.program_id(0)
    c = c_ref[...]
    ca = (c * (1.0 / (1.0 + jnp.exp(-c)))).astype(BF16)
    half = ca.shape[1] // 2

    def mod(w_top, w_bot, b_ref, o_ref):
        acc = jnp.dot(ca[:, :half], w_top[...].astype(BF16), preferred_element_type=F32)
        acc = acc + jnp.dot(ca[:, half:], w_bot[...].astype(BF16),
                            preferred_element_type=F32)
        o_ref[...] = acc + b_ref[...]

    @pl.when(j < n_a_tiles)
    def _():
        mod(wa_top, wa_bot, ba_ref, oa_ref)

    @pl.when(j >= n_a_tiles)
    def _():
        mod(wf_top, wf_bot, bf_ref, of_ref)


def _ada_mod(c, w_a, b_a, w_f, b_f):
    bsz, d = c.shape
    n_a, n_f = w_a.shape[-1], w_f.shape[-1]
    w_a = w_a.reshape(d, n_a)
    w_f = w_f.reshape(d, n_f)
    ta, tf = n_a // TN_ADA, n_f // TN_ADA
    half = d // 2
    a_tile = lambda j: jnp.minimum(j, ta - 1)
    f_tile = lambda j: jnp.maximum(j - ta, 0)
    return pl.pallas_call(
        functools.partial(_ada_kernel, n_a_tiles=ta),
        grid=(ta + tf,),
        in_specs=[
            pl.BlockSpec((bsz, d), lambda j: (0, 0)),
            pl.BlockSpec((half, TN_ADA), lambda j: (0, a_tile(j))),
            pl.BlockSpec((half, TN_ADA), lambda j: (1, a_tile(j))),
            pl.BlockSpec((half, TN_ADA), lambda j: (0, f_tile(j))),
            pl.BlockSpec((half, TN_ADA), lambda j: (1, f_tile(j))),
            pl.BlockSpec((1, TN_ADA), lambda j: (0, a_tile(j))),
            pl.BlockSpec((1, TN_ADA), lambda j: (0, f_tile(j))),
        ],
        out_specs=[
            pl.BlockSpec((bsz, TN_ADA), lambda j: (0, a_tile(j))),
            pl.BlockSpec((bsz, TN_ADA), lambda j: (0, f_tile(j))),
        ],
        out_shape=[jax.ShapeDtypeStruct((bsz, n_a), F32),
                   jax.ShapeDtypeStruct((bsz, n_f), F32)],
        compiler_params=pltpu.CompilerParams(
            dimension_semantics=("arbitrary",), vmem_limit_bytes=VMEM_LIMIT_ADA),
        name="ada_mod",
    )(c, w_a, w_a, w_f, w_f, b_a.reshape(1, n_a), b_f.reshape(1, n_f))


class _MixerBlock:
    def __init__(self, load, store, cos, sin, lng, lnb, ws_ref, bst, sink_ref, kprev, vprev,
                 has_prev, *, d_a, d_b, d_kv):
        self.load, self.store = load, store
        self.cos, self.sin, self.lng, self.lnb = cos, sin, lng, lnb
        self.ws_ref, self.bst, self.sink_ref = ws_ref, bst, sink_ref
        self.kprev, self.vprev, self.has_prev = kprev, vprev, has_prev
        self.d_a, self.d_b, self.d_kv = d_a, d_b, d_kv
        self.o_u, self.o_v, self.o_za = 0, d_a, 2 * d_a
        self.o_q = 3 * d_a
        self.o_k = self.o_q + d_b
        self.o_vv = self.o_k + d_kv
        self.o_zb = self.o_vv + d_kv
        row = lax.broadcasted_iota(jnp.int32, (CHUNK, CHUNK), 0)
        col = lax.broadcasted_iota(jnp.int32, (CHUNK, CHUNK), 1)
        self.col = col
        self.causal = col <= row
        self.first_half = (col & (HEAD_DIM - 1)) < (HEAD_DIM // 2)
        self.probs = {}

    def _rope(self, xv, query=False):
        t0 = LANES if query else 0
        rot = jnp.where(self.first_half,
                        pltpu.roll(xv, LANES - HEAD_DIM // 2, 1),
                        pltpu.roll(xv, HEAD_DIM // 2, 1))
        return xv * self.cos[:, t0:t0 + LANES] + rot * self.sin[:, t0:t0 + LANES]

    def prep(self):
        va = self.load(self.o_v, self.o_v + self.d_a).astype(F32)
        mu = jnp.mean(va, axis=-1, keepdims=True)
        vc = va - mu
        var = jnp.mean(vc * vc, axis=-1, keepdims=True)
        self.vn = (vc * lax.rsqrt(var + NORM_EPS) * self.lng + self.lnb).astype(BF16)
        self.tril = self.causal.astype(F32)
        low_half = self.col < HEAD_DIM
        self.ke = [None] * N_EXP
        self.ve = [None] * N_EXP
        for c in range(self.d_kv // LANES):
            kc = self._rope(self.load(self.o_k + c * LANES, self.o_k + (c + 1) * LANES)
                            .astype(F32))
            vcol = self.load(self.o_vv + c * LANES, self.o_vv + (c + 1) * LANES).astype(F32)
            kc_sw = pltpu.roll(kc, HEAD_DIM, 1)
            vcol_sw = pltpu.roll(vcol, HEAD_DIM, 1)
            for j in range(HEADS_PER_VREG):
                kv_head = c * HEADS_PER_VREG + j
                for o in range(HEADS_PER_VREG):
                    mask = low_half if o == 0 else jnp.logical_not(low_half)
                    e = kv_head * HEADS_PER_VREG + o
                    self.ke[e] = jnp.where(mask, kc if o == j else kc_sw, 0.0).astype(BF16)
                    self.ve[e] = jnp.where(mask, vcol if o == j else vcol_sw, 0.0).astype(BF16)

    def group_pair(self, g0):
        c0 = g0 * A_GROUP_W
        w = jnp.concatenate([(self.ws_ref[g0] * self.tril).astype(BF16),
                             (self.ws_ref[g0 + 1] * self.tril).astype(BF16)], axis=1)
        zero = jnp.zeros((CHUNK, A_GROUP_W), BF16)
        rhs = jnp.concatenate(
            [jnp.concatenate([self.vn[:, c0:c0 + A_GROUP_W], zero], axis=1),
             jnp.concatenate([zero, self.vn[:, c0 + A_GROUP_W:c0 + 2 * A_GROUP_W]], axis=1)],
            axis=0)
        s2 = jnp.dot(w, rhs, preferred_element_type=F32)
        for k in range(2):
            g = g0 + k
            cg = g * A_GROUP_W
            s = s2[:, k * A_GROUP_W:(k + 1) * A_GROUP_W] + self.bst[:, g:g + 1]
            u = self.load(self.o_u + cg, self.o_u + cg + A_GROUP_W).astype(F32)
            za = self.load(self.o_za + cg, self.o_za + cg + A_GROUP_W).astype(F32)
            self.store(cg, (u * s * _silu(za)).astype(BF16))

    def scores(self, p0):
        kv_head = (p0 * HEADS_PER_VREG) // Q_PER_KV
        kprev = self.kprev()
        q4 = jnp.concatenate(
            [self._rope(self.load(self.o_q + p * LANES, self.o_q + (p + 1) * LANES)
                        .astype(F32), query=True).astype(BF16)
             for p in range(p0, p0 + COLS_PER_KV)], axis=0)
        probs = [[] for _ in range(COLS_PER_KV)]
        for o in range(HEADS_PER_VREG):
            e = kv_head * HEADS_PER_VREG + o
            kband = jnp.concatenate([kprev[e], self.ke[e]], axis=0)
            s4 = lax.dot_general(q4, kband, (((1,), (1,)), ((), ())),
                                 preferred_element_type=F32)
            for a in range(COLS_PER_KV):
                s = s4[a * CHUNK:(a + 1) * CHUNK]
                h = (p0 + a) * HEADS_PER_VREG + o
                s_prev = s[:, :CHUNK]
                if self.has_prev is not True:
                    s_prev = jnp.where(self.has_prev, s_prev, -jnp.inf)
                comb = jnp.where(self.causal, s[:, CHUNK:], s_prev)
                sink = self.sink_ref[h] * LOG2E
                m = jnp.maximum(jnp.max(comb, axis=-1, keepdims=True), sink)
                pexp = jnp.exp2(comb - m)
                denom = jnp.sum(pexp, axis=-1, keepdims=True) + jnp.exp2(sink - m)
                pn = (pexp * (1.0 / denom)).astype(BF16)
                zero = jnp.zeros_like(pn)
                probs[a].append(jnp.where(self.causal, zero, pn))
                probs[a].append(jnp.where(self.causal, pn, zero))
        self.probs[p0] = jnp.concatenate(
            [jnp.concatenate(pa, axis=1) for pa in probs], axis=0)

    def values(self, p0):
        kv_head = (p0 * HEADS_PER_VREG) // Q_PER_KV
        vprev = self.vprev()
        vband = []
        for o in range(HEADS_PER_VREG):
            e = kv_head * HEADS_PER_VREG + o
            vband.append(vprev[e])
            vband.append(self.ve[e])
        out4 = jnp.dot(self.probs.pop(p0), jnp.concatenate(vband, axis=0),
                       preferred_element_type=F32)
        for a in range(COLS_PER_KV):
            p = p0 + a
            zb = self.load(self.o_zb + p * LANES, self.o_zb + (p + 1) * LANES).astype(F32)
            self.store(self.d_a + p * LANES,
                       (out4[a * CHUNK:(a + 1) * CHUNK] * _silu(zb)).astype(BF16))


def _layer_kernel(sink_ref, x_ref, moda_ref, modc_ref, modf_ref, ng_ref, fg_ref,
                  cos_ref, sin_ref, lng_ref, lnb_ref, ws_ref, bst_ref, win32_ref, wout32_ref,
                  o_ref, win_scr, wout_scr, proj0_scr, proj1_scr, y_scr, xprev_scr, h_scr,
                  kprev_scr, vprev_scr,
                  *, d_model, d_in, d_a, d_b, d_kv, blocks_per_batch, n_blk):
    i = pl.program_id(0)

    @pl.when(i < W_TILES)
    def _():
        r_in = pl.multiple_of(i * win32_ref.shape[0], win32_ref.shape[0])
        win_scr[pl.ds(r_in, win32_ref.shape[0]), :] = win32_ref[...].astype(BF16)
        r_out = pl.multiple_of(i * wout32_ref.shape[0], wout32_ref.shape[0])
        wout_scr[pl.ds(r_out, wout32_ref.shape[0]), :] = wout32_ref[...].astype(BF16)

    step = i - W_TILES

    args = (sink_ref, x_ref, moda_ref, modc_ref, modf_ref, ng_ref, fg_ref, cos_ref, sin_ref,
            lng_ref, lnb_ref, ws_ref, bst_ref, win_scr, wout_scr, o_ref)
    scr = (y_scr, xprev_scr, h_scr, kprev_scr, vprev_scr)
    dims = dict(d_model=d_model, d_in=d_in, d_a=d_a, d_b=d_b, d_kv=d_kv,
                blocks_per_batch=blocks_per_batch, n_blk=n_blk)
    bufs = ((proj0_scr, proj1_scr), (proj1_scr, proj0_scr))
    inner = jnp.logical_and(step > 0, step < n_blk)

    @pl.when(step == 0)
    def _():
        kprev_scr[...] = jnp.zeros_like(kprev_scr)
        vprev_scr[...] = jnp.zeros_like(vprev_scr)
        _layer_step(step, *args, *bufs[0], *scr, **dims, do_bc=False)

    @pl.when(jnp.logical_and(inner, step % 2 == 0))
    def _():
        _layer_step(step, *args, *bufs[0], *scr, **dims)

    @pl.when(jnp.logical_and(inner, step % 2 == 1))
    def _():
        _layer_step(step, *args, *bufs[1], *scr, **dims)

    @pl.when(step == n_blk)
    def _():
        _layer_step(step, *args, *bufs[n_blk % 2], *scr, **dims, do_a=False)


def _layer_step(step, sink_ref, x_ref, moda_ref, modc_ref, modf_ref, ng_ref, fg_ref,
                cos_ref, sin_ref, lng_ref, lnb_ref, ws_ref, bst_ref, win_ref, wout_ref,
                o_ref, proj_w, proj_r, y_scr, xprev_scr, h_scr, kprev_scr, vprev_scr,
                *, d_model, d_in, d_a, d_b, d_kv, blocks_per_batch, n_blk,
                do_a=True, do_bc=True):
    blk_c = jnp.clip(step - 1, 0, n_blk - 1)
    first_in_batch = (blk_c % blocks_per_batch) == 0

    if do_a:
        x = x_ref[...]
        ms = jnp.mean(x * x, axis=-1, keepdims=True)
        shift = moda_ref[:, 0:d_model]
        gain = ng_ref[...] * (1.0 + moda_ref[:, d_model:2 * d_model])
        h_scr[...] = (x * lax.rsqrt(ms + NORM_EPS) * gain + shift).astype(BF16)

    lng = lng_ref[...]
    lnb = lnb_ref[...]
    bst = bst_ref[...]
    n_sub = T_BLK // CHUNK
    blocks = []
    for sb in range(n_sub):
        r0 = sb * CHUNK

        def load(c0, c1, r0=r0):
            return proj_r[r0:r0 + CHUNK, c0:c1]

        def store(c0, val, r0=r0):
            y_scr[r0:r0 + CHUNK, c0:c0 + val.shape[1]] = val

        if sb == 0:
            kprev = lambda: [kprev_scr[e] for e in range(N_EXP)]
            vprev = lambda: [vprev_scr[e] for e in range(N_EXP)]
            has_prev = jnp.logical_not(first_in_batch)
        else:
            kprev = lambda b=blocks[sb - 1]: b.ke
            vprev = lambda b=blocks[sb - 1]: b.ve
            has_prev = True
        blocks.append(_MixerBlock(
            load, store, cos_ref[r0:r0 + CHUNK, :], sin_ref[r0:r0 + CHUNK, :], lng, lnb,
            ws_ref, bst, sink_ref, kprev, vprev, has_prev, d_a=d_a, d_b=d_b, d_kv=d_kv))

    def proj_chunk(n0):
        acc = jnp.dot(h_scr[...], win_ref[:, n0:n0 + TN_DOT], preferred_element_type=F32)
        proj_w[:, n0:n0 + TN_DOT] = acc.astype(BF16)

    chunks = list(range(0, d_in, TN_DOT))
    next_chunk = [0]

    def emit_chunks(n):
        for _ in range(n):
            if do_a and next_chunk[0] < len(chunks):
                proj_chunk(chunks[next_chunk[0]])
                next_chunk[0] += 1

    if not do_bc:
        emit_chunks(len(chunks))
        xprev_scr[...] = x
        return

    for b in blocks:
        b.prep()
    n_pairs = d_b // LANES
    slots = [(b, p0) for b in blocks for p0 in range(0, n_pairs, COLS_PER_KV)]
    group_pairs = [(b, g0) for b in blocks for g0 in range(0, A_GROUPS, 2)]
    pairs_per_slot = -(-len(group_pairs) // len(slots))
    slots[0][0].scores(slots[0][1])
    for j, (b, p0) in enumerate(slots):
        emit_chunks(1)
        if j + 1 < len(slots):
            slots[j + 1][0].scores(slots[j + 1][1])
        for bg, g0 in group_pairs[j * pairs_per_slot:(j + 1) * pairs_per_slot]:
            bg.group_pair(g0)
        b.values(p0)
    for e in range(N_EXP):
        kprev_scr[e] = blocks[-1].ke[e]
        vprev_scr[e] = blocks[-1].ve[e]
    emit_chunks(len(chunks) - next_chunk[0] - 2)

    gate = modc_ref[:, 2 * d_model:3 * d_model]
    ssq = jnp.zeros((T_BLK, 1), F32)
    for n0 in range(0, d_model, TN_DOT):
        sl = slice(n0, n0 + TN_DOT)
        acc = jnp.dot(y_scr[...], wout_ref[:, sl], preferred_element_type=F32)
        xr = xprev_scr[:, sl] + gate[:, sl] * acc
        ssq = ssq + jnp.sum(xr * xr, axis=-1, keepdims=True)
        o_ref[:, sl] = xr
    emit_chunks(len(chunks))
    inv = lax.rsqrt(ssq * (1.0 / d_model) + NORM_EPS)
    shift_f = modf_ref[:, 0:d_model]
    scale_f = modf_ref[:, d_model:2 * d_model]
    o_ref[...] = o_ref[...] * inv * (fg_ref[...] * (1.0 + scale_f)) + shift_f

    if do_a:
        xprev_scr[...] = x


def _layer(x2, sinks, mod3, modf3, norm_g, final_g, cos_t, sin_t, ln_g, ln_b, w_sp, b_sp_t,
           w_in, w_out, seq, d_a, d_b, d_kv):
    rows, d_model = x2.shape
    d_in = w_in.shape[-1]
    d_mix = d_a + d_b
    n_blk = rows // T_BLK
    bpb = seq // T_BLK
    assert d_model % W_TILES == 0 and d_mix % W_TILES == 0

    def blk_a(i):
        return jnp.clip(i - W_TILES, 0, n_blk - 1)

    def blk_c(i):
        return jnp.clip(i - W_TILES - 1, 0, n_blk - 1)

    def w_tile(i):
        return (jnp.minimum(i, W_TILES - 1), 0)

    const2 = lambda i: (0, 0)
    kern = functools.partial(_layer_kernel, d_model=d_model, d_in=d_in, d_a=d_a, d_b=d_b,
                             d_kv=d_kv, blocks_per_batch=bpb, n_blk=n_blk)
    return pl.pallas_call(
        kern,
        grid=(W_TILES + n_blk + 1,),
        in_specs=[
            pl.BlockSpec(memory_space=pltpu.SMEM),
            pl.BlockSpec((T_BLK, d_model), lambda i: (blk_a(i), 0)),
            pl.BlockSpec((None, 1, 3 * d_model), lambda i: (blk_a(i) // bpb, 0, 0)),
            pl.BlockSpec((None, 1, 3 * d_model), lambda i: (blk_c(i) // bpb, 0, 0)),
            pl.BlockSpec((None, 1, 2 * d_model), lambda i: (blk_c(i) // bpb, 0, 0)),
            pl.BlockSpec((1, d_model), const2),
            pl.BlockSpec((1, d_model), const2),
            pl.BlockSpec((T_BLK, 2 * LANES), lambda i: (blk_c(i) % bpb, 0)),
            pl.BlockSpec((T_BLK, 2 * LANES), lambda i: (blk_c(i) % bpb, 0)),
            pl.BlockSpec((1, d_a), const2),
            pl.BlockSpec((1, d_a), const2),
            pl.BlockSpec((A_GROUPS, CHUNK, CHUNK), lambda i: (0, 0, 0)),
            pl.BlockSpec((CHUNK, A_GROUPS), const2),
            pl.BlockSpec((d_model // W_TILES, d_in), w_tile),
            pl.BlockSpec((d_mix // W_TILES, d_model), w_tile),
        ],
        out_specs=pl.BlockSpec((T_BLK, d_model), lambda i: (blk_c(i), 0)),
        out_shape=jax.ShapeDtypeStruct((rows, d_model), F32),
        scratch_shapes=[
            pltpu.VMEM((d_model, d_in), BF16),
            pltpu.VMEM((d_mix, d_model), BF16),
            pltpu.VMEM((T_BLK, d_in), BF16),
            pltpu.VMEM((T_BLK, d_in), BF16),
            pltpu.VMEM((T_BLK, d_mix), BF16),
            pltpu.VMEM((T_BLK, d_model), F32),
            pltpu.VMEM((T_BLK, d_model), BF16),
            pltpu.VMEM((N_EXP, CHUNK, LANES), BF16),
            pltpu.VMEM((N_EXP, CHUNK, LANES), BF16),
        ],
        compiler_params=pltpu.CompilerParams(
            dimension_semantics=("arbitrary",), vmem_limit_bytes=VMEM_LIMIT_LAYER),
        name="layer",
    )(sinks, x2, mod3, mod3, modf3, norm_g.reshape(1, d_model), final_g.reshape(1, d_model),
      cos_t, sin_t, ln_g, ln_b, w_sp, b_sp_t, w_in, w_out)


def _rope_tables(seq):
    half = HEAD_DIM // 2
    inv_freq = ROPE_THETA ** (-jnp.arange(0, HEAD_DIM, 2, dtype=F32) / HEAD_DIM)
    ang = jnp.arange(seq, dtype=F32)[:, None] * inv_freq[None, :]
    cos = jnp.cos(ang)
    sin = jnp.sin(ang)
    cos_t = jnp.tile(cos, (1, LANES // half))
    sin_t = jnp.tile(jnp.concatenate([-sin, sin], axis=1), (1, HEADS_PER_VREG))
    q_scale = LOG2E * HEAD_DIM ** -0.5
    return (jnp.concatenate([cos_t, cos_t * q_scale], axis=1),
            jnp.concatenate([sin_t, sin_t * q_scale], axis=1))


def kernel(x, c, w_ada, b_ada, norm_g, w_in, ln_v_g, ln_v_b, w_spatial, b_spatial, sinks,
           w_out, w_ada_final, b_ada_final, final_norm_g):
    bsz, seq, d_model = x.shape
    assert w_ada.shape[0] == 1, "single-layer stack"
    d_a = ln_v_g.shape[-1]
    d_mix = w_out.shape[-2]
    d_b = d_mix - d_a
    d_kv = N_KV_HEADS * HEAD_DIM
    d_in = w_in.shape[-1]
    assert d_in == 3 * d_a + 2 * d_b + 2 * d_kv
    assert d_a == A_GROUPS * A_GROUP_W and d_b == N_KV_HEADS * Q_PER_KV * HEAD_DIM
    assert seq % T_BLK == 0 and T_BLK % CHUNK == 0

    x2 = x.reshape(bsz * seq, d_model)
    mod, mod_f = _ada_mod(c, w_ada, b_ada, w_ada_final, b_ada_final)
    mod3 = mod.reshape(bsz, 1, 3 * d_model)
    modf3 = mod_f.reshape(bsz, 1, 2 * d_model)
    cos_t, sin_t = _rope_tables(seq)
    out = _layer(x2, sinks.reshape(-1), mod3, modf3, norm_g, final_norm_g, cos_t, sin_t,
                 ln_v_g.reshape(1, d_a), ln_v_b.reshape(1, d_a),
                 w_spatial.reshape(A_GROUPS, CHUNK, CHUNK),
                 b_spatial.reshape(A_GROUPS, CHUNK).T,
                 w_in.reshape(d_model, d_in), w_out.reshape(d_mix, d_model),
                 seq, d_a, d_b, d_kv)
    return out.reshape(bsz, seq, d_model)
```

```python
import functools

import jax
import jax.numpy as jnp
from jax import lax
from jax.experimental import pallas as pl
from jax.experimental.pallas import tpu as pltpu

F32 = jnp.float32
BF16 = jnp.bfloat16

CHUNK = 128
A_GROUPS = 8
A_GROUP_W = 128
HEAD_DIM = 64
N_KV_HEADS = 4
Q_PER_KV = 4
ROPE_THETA = 10000.0
NORM_EPS = 1e-5
LOG2E = 1.4426950408889634

LANES = 128
HEADS_PER_VREG = LANES // HEAD_DIM
N_EXP = N_KV_HEADS * HEADS_PER_VREG
COLS_PER_KV = Q_PER_KV // HEADS_PER_VREG

T_BLK = 256
TN_DOT = 512
W_TILES = 32
TN_ADA = 1024
VMEM_LIMIT_ADA = 40 * 1024 * 1024
VMEM_LIMIT_LAYER = 60 * 1024 * 1024


def _silu(z):
    hz = 0.5 * z
    return hz + hz * jnp.tanh(hz)


def _ada_kernel(c_ref, wa_top, wa_bot, wf_top, wf_bot, ba_ref, bf_ref, oa_ref, of_ref,
                *, n_a_tiles):
    j = pl.program_id(0)
    c = c_ref[...]
    ca = (c * (1.0 / (1.0 + jnp.exp(-c)))).astype(BF16)
    half = ca.shape[1] // 2

    def mod(w_top, w_bot, b_ref, o_ref):
        acc = jnp.dot(ca[:, :half], w_top[...].astype(BF16), preferred_element_type=F32)
        acc = acc + jnp.dot(ca[:, half:], w_bot[...].astype(BF16),
                            preferred_element_type=F32)
        o_ref[...] = acc + b_ref[...]

    @pl.when(j < n_a_tiles)
    def _():
        mod(wa_top, wa_bot, ba_ref, oa_ref)

    @pl.when(j >= n_a_tiles)
    def _():
        mod(wf_top, wf_bot, bf_ref, of_ref)


def _ada_mod(c, w_a, b_a, w_f, b_f):
    bsz, d = c.shape
    n_a, n_f = w_a.shape[-1], w_f.shape[-1]
    w_a = w_a.reshape(d, n_a)
    w_f = w_f.reshape(d, n_f)
    ta, tf = n_a // TN_ADA, n_f // TN_ADA
    half = d // 2
    a_tile = lambda j: jnp.minimum(j, ta - 1)
    f_tile = lambda j: jnp.maximum(j - ta, 0)
    return pl.pallas_call(
        functools.partial(_ada_kernel, n_a_tiles=ta),
        grid=(ta + tf,),
        in_specs=[
            pl.BlockSpec((bsz, d), lambda j: (0, 0)),
            pl.BlockSpec((half, TN_ADA), lambda j: (0, a_tile(j))),
            pl.BlockSpec((half, TN_ADA), lambda j: (1, a_tile(j))),
            pl.BlockSpec((half, TN_ADA), lambda j: (0, f_tile(j))),
            pl.BlockSpec((half, TN_ADA), lambda j: (1, f_tile(j))),
            pl.BlockSpec((1, TN_ADA), lambda j: (0, a_tile(j))),
            pl.BlockSpec((1, TN_ADA), lambda j: (0, f_tile(j))),
        ],
        out_specs=[
            pl.BlockSpec((bsz, TN_ADA), lambda j: (0, a_tile(j))),
            pl.BlockSpec((bsz, TN_ADA), lambda j: (0, f_tile(j))),
        ],
        out_shape=[jax.ShapeDtypeStruct((bsz, n_a), F32),
                   jax.ShapeDtypeStruct((bsz, n_f), F32)],
        compiler_params=pltpu.CompilerParams(
            dimension_semantics=("arbitrary",), vmem_limit_bytes=VMEM_LIMIT_ADA),
        name="ada_mod",
    )(c, w_a, w_a, w_f, w_f, b_a.reshape(1, n_a), b_f.reshape(1, n_f))


class _MixerBlock:
    def __init__(self, load, store, cos, sin, lng, lnb, ws_ref, bst, sink_ref, kprev, vprev,
                 has_prev, *, d_a, d_b, d_kv):
        self.load, self.store = load, store
        self.cos, self.sin, self.lng, self.lnb = cos, sin, lng, lnb
        q_scale = LOG2E * HEAD_DIM ** -0.5
        self.cos_q, self.sin_q = cos * q_scale, sin * q_scale
        self.ws_ref, self.bst, self.sink_ref = ws_ref, bst, sink_ref
        self.kprev, self.vprev, self.has_prev = kprev, vprev, has_prev
        self.d_a, self.d_b, self.d_kv = d_a, d_b, d_kv
        self.o_u, self.o_v, self.o_za = 0, d_a, 2 * d_a
        self.o_q = 3 * d_a
        self.o_k = self.o_q + d_b
        self.o_vv = self.o_k + d_kv
        self.o_zb = self.o_vv + d_kv
        row = lax.broadcasted_iota(jnp.int32, (CHUNK, CHUNK), 0)
        col = lax.broadcasted_iota(jnp.int32, (CHUNK, CHUNK), 1)
        self.col = col
        self.causal = col <= row
        self.first_half = (col & (HEAD_DIM - 1)) < (HEAD_DIM // 2)
        self.probs = {}

    def _rope(self, xv, query=False):
        cos, sin = (self.cos_q, self.sin_q) if query else (self.cos, self.sin)
        rot = jnp.where(self.first_half,
                        pltpu.roll(xv, LANES - HEAD_DIM // 2, 1),
                        pltpu.roll(xv, HEAD_DIM // 2, 1))
        return xv * cos + rot * sin

    def prep(self):
        va = self.load(self.o_v, self.o_v + self.d_a).astype(F32)
        mu = jnp.mean(va, axis=-1, keepdims=True)
        vc = va - mu
        var = jnp.mean(vc * vc, axis=-1, keepdims=True)
        self.vn = (vc * lax.rsqrt(var + NORM_EPS) * self.lng + self.lnb).astype(BF16)
        self.tril = self.causal.astype(F32)
        low_half = self.col < HEAD_DIM
        self.ke = [None] * N_EXP
        self.ve = [None] * N_EXP
        for c in range(self.d_kv // LANES):
            kc = self._rope(self.load(self.o_k + c * LANES, self.o_k + (c + 1) * LANES)
                            .astype(F32))
            vcol = self.load(self.o_vv + c * LANES, self.o_vv + (c + 1) * LANES).astype(F32)
            kc_sw = pltpu.roll(kc, HEAD_DIM, 1)
            vcol_sw = pltpu.roll(vcol, HEAD_DIM, 1)
            for j in range(HEADS_PER_VREG):
                kv_head = c * HEADS_PER_VREG + j
                for o in range(HEADS_PER_VREG):
                    mask = low_half if o == 0 else jnp.logical_not(low_half)
                    e = kv_head * HEADS_PER_VREG + o
                    self.ke[e] = jnp.where(mask, kc if o == j else kc_sw, 0.0).astype(BF16)
                    self.ve[e] = jnp.where(mask, vcol if o == j else vcol_sw, 0.0).astype(BF16)

    def group_pair(self, g0):
        c0 = g0 * A_GROUP_W
        w = jnp.concatenate([(self.ws_ref[g0] * self.tril).astype(BF16),
                             (self.ws_ref[g0 + 1] * self.tril).astype(BF16)], axis=1)
        zero = jnp.zeros((CHUNK, A_GROUP_W), BF16)
        rhs = jnp.concatenate(
            [jnp.concatenate([self.vn[:, c0:c0 + A_GROUP_W], zero], axis=1),
             jnp.concatenate([zero, self.vn[:, c0 + A_GROUP_W:c0 + 2 * A_GROUP_W]], axis=1)],
            axis=0)
        s2 = jnp.dot(w, rhs, preferred_element_type=F32)
        for k in range(2):
            g = g0 + k
            cg = g * A_GROUP_W
            s = s2[:, k * A_GROUP_W:(k + 1) * A_GROUP_W] + self.bst[:, g:g + 1]
            u = self.load(self.o_u + cg, self.o_u + cg + A_GROUP_W).astype(F32)
            za = self.load(self.o_za + cg, self.o_za + cg + A_GROUP_W).astype(F32)
            self.store(cg, (u * s * _silu(za)).astype(BF16))

    def scores(self, p0):
        kv_head = (p0 * HEADS_PER_VREG) // Q_PER_KV
        kprev = self.kprev()
        q4 = jnp.concatenate(
            [self._rope(self.load(self.o_q + p * LANES, self.o_q + (p + 1) * LANES)
                        .astype(F32), query=True).astype(BF16)
             for p in range(p0, p0 + COLS_PER_KV)], axis=0)
        probs = [[] for _ in range(COLS_PER_KV)]
        for o in range(HEADS_PER_VREG):
            e = kv_head * HEADS_PER_VREG + o
            kband = jnp.concatenate([kprev[e], self.ke[e]], axis=0)
            s4 = lax.dot_general(q4, kband, (((1,), (1,)), ((), ())),
                                 preferred_element_type=F32)
            for a in range(COLS_PER_KV):
                s = s4[a * CHUNK:(a + 1) * CHUNK]
                h = (p0 + a) * HEADS_PER_VREG + o
                s_prev = s[:, :CHUNK]
                if self.has_prev is not True:
                    s_prev = jnp.where(self.has_prev, s_prev, -jnp.inf)
                comb = jnp.where(self.causal, s[:, CHUNK:], s_prev)
                sink = self.sink_ref[h] * LOG2E
                m = jnp.maximum(jnp.max(comb, axis=-1, keepdims=True), sink)
                pexp = jnp.exp2(comb - m)
                denom = jnp.sum(pexp, axis=-1, keepdims=True) + jnp.exp2(sink - m)
                pn = (pexp * (1.0 / denom)).astype(BF16)
                zero = jnp.zeros_like(pn)
                probs[a].append(jnp.where(self.causal, zero, pn))
                probs[a].append(jnp.where(self.causal, pn, zero))
        self.probs[p0] = jnp.concatenate(
            [jnp.concatenate(pa, axis=1) for pa in probs], axis=0)

    def values(self, p0):
        kv_head = (p0 * HEADS_PER_VREG) // Q_PER_KV
        vprev = self.vprev()
        vband = []
        for o in range(HEADS_PER_VREG):
            e = kv_head * HEADS_PER_VREG + o
            vband.append(vprev[e])
            vband.append(self.ve[e])
        out4 = jnp.dot(self.probs.pop(p0), jnp.concatenate(vband, axis=0),
                       preferred_element_type=F32)
        for a in range(COLS_PER_KV):
            p = p0 + a
            zb = self.load(self.o_zb + p * LANES, self.o_zb + (p + 1) * LANES).astype(F32)
            self.store(self.d_a + p * LANES,
                       (out4[a * CHUNK:(a + 1) * CHUNK] * _silu(zb)).astype(BF16))


def _layer_kernel(sink_ref, x_ref, moda_ref, modc_ref, modf_ref, ng_ref, fg_ref,
                  cos_ref, sin_ref, lng_ref, lnb_ref, ws_ref, bst_ref, win32_ref, wout32_ref,
                  o_ref, win_scr, wout_scr, proj0_scr, proj1_scr, y_scr, xprev_scr, h_scr,
                  kprev_scr, vprev_scr,
                  *, d_model, d_in, d_a, d_b, d_kv, blocks_per_batch, n_blk):
    i = pl.program_id(0)

    @pl.when(i < W_TILES)
    def _():
        r_in = pl.multiple_of(i * win32_ref.shape[0], win32_ref.shape[0])
        win_scr[pl.ds(r_in, win32_ref.shape[0]), :] = win32_ref[...].astype(BF16)
        r_out = pl.multiple_of(i * wout32_ref.shape[0], wout32_ref.shape[0])
        wout_scr[pl.ds(r_out, wout32_ref.shape[0]), :] = wout32_ref[...].astype(BF16)

    step = i - W_TILES

    args = (sink_ref, x_ref, moda_ref, modc_ref, modf_ref, ng_ref, fg_ref, cos_ref, sin_ref,
            lng_ref, lnb_ref, ws_ref, bst_ref, win_scr, wout_scr, o_ref)
    scr = (y_scr, xprev_scr, h_scr, kprev_scr, vprev_scr)
    dims = dict(d_model=d_model, d_in=d_in, d_a=d_a, d_b=d_b, d_kv=d_kv,
                blocks_per_batch=blocks_per_batch, n_blk=n_blk)
    bufs = ((proj0_scr, proj1_scr), (proj1_scr, proj0_scr))
    inner = jnp.logical_and(step > 0, step < n_blk)

    @pl.when(step == 0)
    def _():
        kprev_scr[...] = jnp.zeros_like(kprev_scr)
        vprev_scr[...] = jnp.zeros_like(vprev_scr)
        _layer_step(step, *args, *bufs[0], *scr, **dims, do_bc=False)

    @pl.when(jnp.logical_and(inner, step % 2 == 0))
    def _():
        _layer_step(step, *args, *bufs[0], *scr, **dims)

    @pl.when(jnp.logical_and(inner, step % 2 == 1))
    def _():
        _layer_step(step, *args, *bufs[1], *scr, **dims)

    @pl.when(step == n_blk)
    def _():
        _layer_step(step, *args, *bufs[n_blk % 2], *scr, **dims, do_a=False)


def _layer_step(step, sink_ref, x_ref, moda_ref, modc_ref, modf_ref, ng_ref, fg_ref,
                cos_ref, sin_ref, lng_ref, lnb_ref, ws_ref, bst_ref, win_ref, wout_ref,
                o_ref, proj_w, proj_r, y_scr, xprev_scr, h_scr, kprev_scr, vprev_scr,
                *, d_model, d_in, d_a, d_b, d_kv, blocks_per_batch, n_blk,
                do_a=True, do_bc=True):
    blk_c = jnp.clip(step - 1, 0, n_blk - 1)
    first_in_batch = (blk_c % blocks_per_batch) == 0

    if do_a:
        x = x_ref[...]
        ms = jnp.mean(x * x, axis=-1, keepdims=True)
        row_a = pl.ds(jnp.minimum(step, n_blk - 1) // blocks_per_batch, 1)
        shift = moda_ref[row_a, 0:d_model]
        gain = ng_ref[...] * (1.0 + moda_ref[row_a, d_model:2 * d_model])
        h_scr[...] = (x * lax.rsqrt(ms + NORM_EPS) * gain + shift).astype(BF16)

    lng = lng_ref[...]
    lnb = lnb_ref[...]
    bst = bst_ref[...]
    n_sub = T_BLK // CHUNK
    blocks = []
    for sb in range(n_sub):
        r0 = sb * CHUNK

        def load(c0, c1, r0=r0):
            return proj_r[r0:r0 + CHUNK, c0:c1]

        def store(c0, val, r0=r0):
            y_scr[r0:r0 + CHUNK, c0:c0 + val.shape[1]] = val

        if sb == 0:
            kprev = lambda: [kprev_scr[e] for e in range(N_EXP)]
            vprev = lambda: [vprev_scr[e] for e in range(N_EXP)]
            has_prev = jnp.logical_not(first_in_batch)
        else:
            kprev = lambda b=blocks[sb - 1]: b.ke
            vprev = lambda b=blocks[sb - 1]: b.ve
            has_prev = True
        blocks.append(_MixerBlock(
            load, store, cos_ref[r0:r0 + CHUNK, :], sin_ref[r0:r0 + CHUNK, :], lng, lnb,
            ws_ref, bst, sink_ref, kprev, vprev, has_prev, d_a=d_a, d_b=d_b, d_kv=d_kv))

    def proj_chunk(n0):
        acc = jnp.dot(h_scr[...], win_ref[:, n0:n0 + TN_DOT], preferred_element_type=F32)
        proj_w[:, n0:n0 + TN_DOT] = acc.astype(BF16)

    chunks = list(range(0, d_in, TN_DOT))
    next_chunk = [0]

    def emit_chunks(n):
        for _ in range(n):
            if do_a and next_chunk[0] < len(chunks):
                proj_chunk(chunks[next_chunk[0]])
                next_chunk[0] += 1

    if not do_bc:
        emit_chunks(len(chunks))
        xprev_scr[...] = x
        return

    for b in blocks:
        b.prep()
    n_pairs = d_b // LANES
    slots = [(b, p0) for b in blocks for p0 in range(0, n_pairs, COLS_PER_KV)]
    group_pairs = [(b, g0) for b in blocks for g0 in range(0, A_GROUPS, 2)]
    pairs_per_slot = -(-len(group_pairs) // len(slots))
    slots[0][0].scores(slots[0][1])
    for j, (b, p0) in enumerate(slots):
        emit_chunks(1)
        if j + 1 < len(slots):
            slots[j + 1][0].scores(slots[j + 1][1])
        for bg, g0 in group_pairs[j * pairs_per_slot:(j + 1) * pairs_per_slot]:
            bg.group_pair(g0)
        b.values(p0)
    for e in range(N_EXP):
        kprev_scr[e] = blocks[-1].ke[e]
        vprev_scr[e] = blocks[-1].ve[e]
    emit_chunks(len(chunks) - next_chunk[0] - 2)

    row_c = pl.ds(blk_c // blocks_per_batch, 1)
    gate = modc_ref[row_c, 2 * d_model:3 * d_model]
    ssq = jnp.zeros((T_BLK, 1), F32)
    for n0 in range(0, d_model, TN_DOT):
        sl = slice(n0, n0 + TN_DOT)
        acc = jnp.dot(y_scr[...], wout_ref[:, sl], preferred_element_type=F32)
        xr = xprev_scr[:, sl] + gate[:, sl] * acc
        ssq = ssq + jnp.sum(xr * xr, axis=-1, keepdims=True)
        o_ref[:, sl] = xr
    emit_chunks(len(chunks))
    inv = lax.rsqrt(ssq * (1.0 / d_model) + NORM_EPS)
    shift_f = modf_ref[row_c, 0:d_model]
    scale_f = modf_ref[row_c, d_model:2 * d_model]
    o_ref[...] = o_ref[...] * inv * (fg_ref[...] * (1.0 + scale_f)) + shift_f

    if do_a:
        xprev_scr[...] = x


def _layer(x2, sinks, mod, mod_f, norm_g, final_g, cos_t, sin_t, ln_g, ln_b, w_sp, b_sp_t,
           w_in, w_out, seq, d_a, d_b, d_kv):
    rows, d_model = x2.shape
    d_in = w_in.shape[-1]
    d_mix = d_a + d_b
    n_blk = rows // T_BLK
    bpb = seq // T_BLK
    assert d_model % W_TILES == 0 and d_mix % W_TILES == 0

    def blk_a(i):
        return jnp.clip(i - W_TILES, 0, n_blk - 1)

    def blk_c(i):
        return jnp.clip(i - W_TILES - 1, 0, n_blk - 1)

    def w_tile(i):
        return (jnp.minimum(i, W_TILES - 1), 0)

    const2 = lambda i: (0, 0)
    kern = functools.partial(_layer_kernel, d_model=d_model, d_in=d_in, d_a=d_a, d_b=d_b,
                             d_kv=d_kv, blocks_per_batch=bpb, n_blk=n_blk)
    return pl.pallas_call(
        kern,
        grid=(W_TILES + n_blk + 1,),
        in_specs=[
            pl.BlockSpec(memory_space=pltpu.SMEM),
            pl.BlockSpec((T_BLK, d_model), lambda i: (blk_a(i), 0)),
            pl.BlockSpec(mod.shape, const2),
            pl.BlockSpec(mod.shape, const2),
            pl.BlockSpec(mod_f.shape, const2),
            pl.BlockSpec((1, d_model), const2),
            pl.BlockSpec((1, d_model), const2),
            pl.BlockSpec((T_BLK, LANES), lambda i: (blk_c(i) % bpb, 0)),
            pl.BlockSpec((T_BLK, LANES), lambda i: (blk_c(i) % bpb, 0)),
            pl.BlockSpec((1, d_a), const2),
            pl.BlockSpec((1, d_a), const2),
            pl.BlockSpec((A_GROUPS, CHUNK, CHUNK), lambda i: (0, 0, 0)),
            pl.BlockSpec((CHUNK, A_GROUPS), const2),
            pl.BlockSpec((d_model // W_TILES, d_in), w_tile),
            pl.BlockSpec((d_mix // W_TILES, d_model), w_tile),
        ],
        out_specs=pl.BlockSpec((T_BLK, d_model), lambda i: (blk_c(i), 0)),
        out_shape=jax.ShapeDtypeStruct((rows, d_model), F32),
        scratch_shapes=[
            pltpu.VMEM((d_model, d_in), BF16),
            pltpu.VMEM((d_mix, d_model), BF16),
            pltpu.VMEM((T_BLK, d_in), BF16),
            pltpu.VMEM((T_BLK, d_in), BF16),
            pltpu.VMEM((T_BLK, d_mix), BF16),
            pltpu.VMEM((T_BLK, d_model), F32),
            pltpu.VMEM((T_BLK, d_model), BF16),
            pltpu.VMEM((N_EXP, CHUNK, LANES), BF16),
            pltpu.VMEM((N_EXP, CHUNK, LANES), BF16),
        ],
        compiler_params=pltpu.CompilerParams(
            dimension_semantics=("arbitrary",), vmem_limit_bytes=VMEM_LIMIT_LAYER),
        name="layer",
    )(sinks, x2, mod, mod, mod_f, norm_g.reshape(1, d_model), final_g.reshape(1, d_model),
      cos_t, sin_t, ln_g, ln_b, w_sp, b_sp_t, w_in, w_out)


def _rope_tables(seq):
    half = HEAD_DIM // 2
    inv_freq = ROPE_THETA ** (-jnp.arange(0, HEAD_DIM, 2, dtype=F32) / HEAD_DIM)
    ang = jnp.arange(seq, dtype=F32)[:, None] * inv_freq[None, :]
    cos = jnp.cos(ang)
    sin = jnp.sin(ang)
    cos_t = jnp.tile(cos, (1, LANES // half))
    sin_t = jnp.tile(jnp.concatenate([-sin, sin], axis=1), (1, HEADS_PER_VREG))
    return cos_t, sin_t


def kernel(x, c, w_ada, b_ada, norm_g, w_in, ln_v_g, ln_v_b, w_spatial, b_spatial, sinks,
           w_out, w_ada_final, b_ada_final, final_norm_g):
    bsz, seq, d_model = x.shape
    assert w_ada.shape[0] == 1, "single-layer stack"
    d_a = ln_v_g.shape[-1]
    d_mix = w_out.shape[-2]
    d_b = d_mix - d_a
    d_kv = N_KV_HEADS * HEAD_DIM
    d_in = w_in.shape[-1]
    assert d_in == 3 * d_a + 2 * d_b + 2 * d_kv
    assert d_a == A_GROUPS * A_GROUP_W and d_b == N_KV_HEADS * Q_PER_KV * HEAD_DIM
    assert seq % T_BLK == 0 and T_BLK % CHUNK == 0

    x2 = x.reshape(bsz * seq, d_model)
    mod, mod_f = _ada_mod(c, w_ada, b_ada, w_ada_final, b_ada_final)
    cos_t, sin_t = _rope_tables(seq)
    out = _layer(x2, sinks.reshape(-1), mod, mod_f, norm_g, final_norm_g, cos_t, sin_t,
                 ln_v_g.reshape(1, d_a), ln_v_b.reshape(1, d_a),
                 w_spatial.reshape(A_GROUPS, CHUNK, CHUNK),
                 b_spatial.reshape(A_GROUPS, CHUNK).T,
                 w_in.reshape(d_model, d_in), w_out.reshape(d_mix, d_model),
                 seq, d_a, d_b, d_kv)
    return out.reshape(bsz, seq, d_model)
```

```python
import functools

import jax
import jax.numpy as jnp
from jax import lax
from jax.experimental import pallas as pl
from jax.experimental.pallas import tpu as pltpu

F32 = jnp.float32
BF16 = jnp.bfloat16

CHUNK = 128
A_GROUPS = 8
A_GROUP_W = 128
HEAD_DIM = 64
N_KV_HEADS = 4
Q_PER_KV = 4
ROPE_THETA = 10000.0
NORM_EPS = 1e-5
LOG2E = 1.4426950408889634

LANES = 128
HEADS_PER_VREG = LANES // HEAD_DIM
N_EXP = N_KV_HEADS * HEADS_PER_VREG
COLS_PER_KV = Q_PER_KV // HEADS_PER_VREG

T_BLK = 256
TN_DOT = 512
W_TILES = 32
TN_ADA = 1024
VMEM_LIMIT_ADA = 40 * 1024 * 1024
VMEM_LIMIT_LAYER = 60 * 1024 * 1024


def _silu(z):
    hz = 0.5 * z
    return hz + hz * jnp.tanh(hz)


def _ada_kernel(c_ref, wa_top, wa_bot, wf_top, wf_bot, ba_ref, bf_ref, oa_ref, of_ref,
                *, n_a_tiles):
    j = pl.program_id(0)
    c = c_ref[...]
    ca = (c * (1.0 / (1.0 + jnp.exp(-c)))).astype(BF16)
    half = ca.shape[1] // 2

    def mod(w_top, w_bot, b_ref, o_ref):
        acc = jnp.dot(ca[:, :half], w_top[...].astype(BF16), preferred_element_type=F32)
        acc = acc + jnp.dot(ca[:, half:], w_bot[...].astype(BF16),
                            preferred_element_type=F32)
        o_ref[...] = acc + b_ref[...]

    @pl.when(j < n_a_tiles)
    def _():
        mod(wa_top, wa_bot, ba_ref, oa_ref)

    @pl.when(j >= n_a_tiles)
    def _():
        mod(wf_top, wf_bot, bf_ref, of_ref)


def _ada_mod(c, w_a, b_a, w_f, b_f):
    bsz, d = c.shape
    n_a, n_f = w_a.shape[-1], w_f.shape[-1]
    w_a = w_a.reshape(d, n_a)
    w_f = w_f.reshape(d, n_f)
    ta, tf = n_a // TN_ADA, n_f // TN_ADA
    half = d // 2
    a_tile = lambda j: jnp.minimum(j, ta - 1)
    f_tile = lambda j: jnp.maximum(j - ta, 0)
    return pl.pallas_call(
        functools.partial(_ada_kernel, n_a_tiles=ta),
        grid=(ta + tf,),
        in_specs=[
            pl.BlockSpec((bsz, d), lambda j: (0, 0)),
            pl.BlockSpec((half, TN_ADA), lambda j: (0, a_tile(j))),
            pl.BlockSpec((half, TN_ADA), lambda j: (1, a_tile(j))),
            pl.BlockSpec((half, TN_ADA), lambda j: (0, f_tile(j))),
            pl.BlockSpec((half, TN_ADA), lambda j: (1, f_tile(j))),
            pl.BlockSpec((1, TN_ADA), lambda j: (0, a_tile(j))),
            pl.BlockSpec((1, TN_ADA), lambda j: (0, f_tile(j))),
        ],
        out_specs=[
            pl.BlockSpec((bsz, TN_ADA), lambda j: (0, a_tile(j))),
            pl.BlockSpec((bsz, TN_ADA), lambda j: (0, f_tile(j))),
        ],
        out_shape=[jax.ShapeDtypeStruct((bsz, n_a), F32),
                   jax.ShapeDtypeStruct((bsz, n_f), F32)],
        compiler_params=pltpu.CompilerParams(
            dimension_semantics=("arbitrary",), vmem_limit_bytes=VMEM_LIMIT_ADA),
        name="ada_mod",
    )(c, w_a, w_a, w_f, w_f, b_a.reshape(1, n_a), b_f.reshape(1, n_f))


class _MixerBlock:
    def __init__(self, load, store, cos, sin, lng, lnb, ws_ref, bst, sink_ref, kprev, vprev,
                 has_prev, *, d_a, d_b, d_kv):
        self.load, self.store = load, store
        self.cos, self.sin, self.lng, self.lnb = cos, sin, lng, lnb
        q_scale = LOG2E * HEAD_DIM ** -0.5
        self.cos_q, self.sin_q = cos * q_scale, sin * q_scale
        self.ws_ref, self.bst, self.sink_ref = ws_ref, bst, sink_ref
        self.kprev, self.vprev, self.has_prev = kprev, vprev, has_prev
        self.d_a, self.d_b, self.d_kv = d_a, d_b, d_kv
        self.o_u, self.o_v, self.o_za = 0, d_a, 2 * d_a
        self.o_q = 3 * d_a
        self.o_k = self.o_q + d_b
        self.o_vv = self.o_k + d_kv
        self.o_zb = self.o_vv + d_kv
        row = lax.broadcasted_iota(jnp.int32, (CHUNK, CHUNK), 0)
        col = lax.broadcasted_iota(jnp.int32, (CHUNK, CHUNK), 1)
        self.col = col
        self.causal = col <= row
        self.first_half = (col & (HEAD_DIM - 1)) < (HEAD_DIM // 2)
        self.probs = {}

    def _rope(self, xv, query=False):
        cos, sin = (self.cos_q, self.sin_q) if query else (self.cos, self.sin)
        rot = jnp.where(self.first_half,
                        pltpu.roll(xv, LANES - HEAD_DIM // 2, 1),
                        pltpu.roll(xv, HEAD_DIM // 2, 1))
        return xv * cos + rot * sin

    def prep(self):
        va = self.load(self.o_v, self.o_v + self.d_a).astype(F32)
        mu = jnp.mean(va, axis=-1, keepdims=True)
        vc = va - mu
        var = jnp.mean(vc * vc, axis=-1, keepdims=True)
        self.vn = (vc * lax.rsqrt(var + NORM_EPS) * self.lng + self.lnb).astype(BF16)
        self.tril = self.causal.astype(F32)
        low_half = self.col < HEAD_DIM
        self.ke = [None] * N_EXP
        self.ve = [None] * N_EXP
        for c in range(self.d_kv // LANES):
            kc = self._rope(self.load(self.o_k + c * LANES, self.o_k + (c + 1) * LANES)
                            .astype(F32))
            vcol = self.load(self.o_vv + c * LANES, self.o_vv + (c + 1) * LANES).astype(F32)
            kc_sw = pltpu.roll(kc, HEAD_DIM, 1)
            vcol_sw = pltpu.roll(vcol, HEAD_DIM, 1)
            for j in range(HEADS_PER_VREG):
                kv_head = c * HEADS_PER_VREG + j
                for o in range(HEADS_PER_VREG):
                    mask = low_half if o == 0 else jnp.logical_not(low_half)
                    e = kv_head * HEADS_PER_VREG + o
                    self.ke[e] = jnp.where(mask, kc if o == j else kc_sw, 0.0).astype(BF16)
                    self.ve[e] = jnp.where(mask, vcol if o == j else vcol_sw, 0.0).astype(BF16)

    def group_pair(self, g0):
        c0 = g0 * A_GROUP_W
        w = jnp.concatenate([(self.ws_ref[g0] * self.tril).astype(BF16),
                             (self.ws_ref[g0 + 1] * self.tril).astype(BF16)], axis=1)
        zero = jnp.zeros((CHUNK, A_GROUP_W), BF16)
        rhs = jnp.concatenate(
            [jnp.concatenate([self.vn[:, c0:c0 + A_GROUP_W], zero], axis=1),
             jnp.concatenate([zero, self.vn[:, c0 + A_GROUP_W:c0 + 2 * A_GROUP_W]], axis=1)],
            axis=0)
        s2 = jnp.dot(w, rhs, preferred_element_type=F32)
        for k in range(2):
            g = g0 + k
            cg = g * A_GROUP_W
            s = s2[:, k * A_GROUP_W:(k + 1) * A_GROUP_W] + self.bst[:, g:g + 1]
            u = self.load(self.o_u + cg, self.o_u + cg + A_GROUP_W)
            za = self.load(self.o_za + cg, self.o_za + cg + A_GROUP_W)
            self.store(cg, u * s.astype(BF16) * _silu(za))

    def scores(self, p0):
        kv_head = (p0 * HEADS_PER_VREG) // Q_PER_KV
        kprev = self.kprev()
        q4 = jnp.concatenate(
            [self._rope(self.load(self.o_q + p * LANES, self.o_q + (p + 1) * LANES)
                        .astype(F32), query=True).astype(BF16)
             for p in range(p0, p0 + COLS_PER_KV)], axis=0)
        probs = [[] for _ in range(COLS_PER_KV)]
        for o in range(HEADS_PER_VREG):
            e = kv_head * HEADS_PER_VREG + o
            kband = jnp.concatenate([kprev[e], self.ke[e]], axis=0)
            s4 = lax.dot_general(q4, kband, (((1,), (1,)), ((), ())),
                                 preferred_element_type=F32)
            for a in range(COLS_PER_KV):
                s = s4[a * CHUNK:(a + 1) * CHUNK]
                h = (p0 + a) * HEADS_PER_VREG + o
                s_prev = s[:, :CHUNK]
                if self.has_prev is not True:
                    s_prev = jnp.where(self.has_prev, s_prev, -jnp.inf)
                comb = jnp.where(self.causal, s[:, CHUNK:], s_prev)
                sink = self.sink_ref[h] * LOG2E
                m = jnp.maximum(jnp.max(comb, axis=-1, keepdims=True), sink)
                pexp = jnp.exp2(comb - m)
                denom = jnp.sum(pexp, axis=-1, keepdims=True) + jnp.exp2(sink - m)
                pn = (pexp * (1.0 / denom)).astype(BF16)
                zero = jnp.zeros_like(pn)
                probs[a].append(jnp.where(self.causal, zero, pn))
                probs[a].append(jnp.where(self.causal, pn, zero))
        self.probs[p0] = jnp.concatenate(
            [jnp.concatenate(pa, axis=1) for pa in probs], axis=0)

    def values(self, p0):
        kv_head = (p0 * HEADS_PER_VREG) // Q_PER_KV
        vprev = self.vprev()
        vband = []
        for o in range(HEADS_PER_VREG):
            e = kv_head * HEADS_PER_VREG + o
            vband.append(vprev[e])
            vband.append(self.ve[e])
        out4 = jnp.dot(self.probs.pop(p0), jnp.concatenate(vband, axis=0),
                       preferred_element_type=F32)
        for a in range(COLS_PER_KV):
            p = p0 + a
            zb = self.load(self.o_zb + p * LANES, self.o_zb + (p + 1) * LANES)
            self.store(self.d_a + p * LANES,
                       out4[a * CHUNK:(a + 1) * CHUNK].astype(BF16) * _silu(zb))


def _layer_kernel(sink_ref, x_ref, moda_ref, modc_ref, modf_ref, ng_ref, fg_ref,
                  cos_ref, sin_ref, lng_ref, lnb_ref, ws_ref, bst_ref, win32_ref, wout32_ref,
                  o_ref, win_scr, wout_scr, proj0_scr, proj1_scr, y_scr, xprev_scr, h_scr,
                  kprev_scr, vprev_scr,
                  *, d_model, d_in, d_a, d_b, d_kv, blocks_per_batch, n_blk):
    i = pl.program_id(0)

    @pl.when(i < W_TILES)
    def _():
        r_in = pl.multiple_of(i * win32_ref.shape[0], win32_ref.shape[0])
        win_scr[pl.ds(r_in, win32_ref.shape[0]), :] = win32_ref[...].astype(BF16)
        r_out = pl.multiple_of(i * wout32_ref.shape[0], wout32_ref.shape[0])
        wout_scr[pl.ds(r_out, wout32_ref.shape[0]), :] = wout32_ref[...].astype(BF16)

    step = i - W_TILES

    args = (sink_ref, x_ref, moda_ref, modc_ref, modf_ref, ng_ref, fg_ref, cos_ref, sin_ref,
            lng_ref, lnb_ref, ws_ref, bst_ref, win_scr, wout_scr, o_ref)
    scr = (y_scr, xprev_scr, h_scr, kprev_scr, vprev_scr)
    dims = dict(d_model=d_model, d_in=d_in, d_a=d_a, d_b=d_b, d_kv=d_kv,
                blocks_per_batch=blocks_per_batch, n_blk=n_blk)
    bufs = ((proj0_scr, proj1_scr), (proj1_scr, proj0_scr))
    inner = jnp.logical_and(step > 0, step < n_blk)

    @pl.when(step == 0)
    def _():
        kprev_scr[...] = jnp.zeros_like(kprev_scr)
        vprev_scr[...] = jnp.zeros_like(vprev_scr)
        _layer_step(step, *args, *bufs[0], *scr, **dims, do_bc=False)

    @pl.when(jnp.logical_and(inner, step % 2 == 0))
    def _():
        _layer_step(step, *args, *bufs[0], *scr, **dims)

    @pl.when(jnp.logical_and(inner, step % 2 == 1))
    def _():
        _layer_step(step, *args, *bufs[1], *scr, **dims)

    @pl.when(step == n_blk)
    def _():
        _layer_step(step, *args, *bufs[n_blk % 2], *scr, **dims, do_a=False)


def _layer_step(step, sink_ref, x_ref, moda_ref, modc_ref, modf_ref, ng_ref, fg_ref,
                cos_ref, sin_ref, lng_ref, lnb_ref, ws_ref, bst_ref, win_ref, wout_ref,
                o_ref, proj_w, proj_r, y_scr, xprev_scr, h_scr, kprev_scr, vprev_scr,
                *, d_model, d_in, d_a, d_b, d_kv, blocks_per_batch, n_blk,
                do_a=True, do_bc=True):
    blk_c = jnp.clip(step - 1, 0, n_blk - 1)
    first_in_batch = (blk_c % blocks_per_batch) == 0

    if do_a:
        x = x_ref[...]
        ms = jnp.mean(x * x, axis=-1, keepdims=True)
        row_a = pl.ds(jnp.minimum(step, n_blk - 1) // blocks_per_batch, 1)
        shift = moda_ref[row_a, 0:d_model]
        gain = ng_ref[...] * (1.0 + moda_ref[row_a, d_model:2 * d_model])
        h_scr[...] = (x * lax.rsqrt(ms + NORM_EPS) * gain + shift).astype(BF16)

    lng = lng_ref[...]
    lnb = lnb_ref[...]
    bst = bst_ref[...]
    n_sub = T_BLK // CHUNK
    blocks = []
    for sb in range(n_sub):
        r0 = sb * CHUNK

        def load(c0, c1, r0=r0):
            return proj_r[r0:r0 + CHUNK, c0:c1]

        def store(c0, val, r0=r0):
            y_scr[r0:r0 + CHUNK, c0:c0 + val.shape[1]] = val

        if sb == 0:
            kprev = lambda: [kprev_scr[e] for e in range(N_EXP)]
            vprev = lambda: [vprev_scr[e] for e in range(N_EXP)]
            has_prev = jnp.logical_not(first_in_batch)
        else:
            kprev = lambda b=blocks[sb - 1]: b.ke
            vprev = lambda b=blocks[sb - 1]: b.ve
            has_prev = True
        blocks.append(_MixerBlock(
            load, store, cos_ref[r0:r0 + CHUNK, :], sin_ref[r0:r0 + CHUNK, :], lng, lnb,
            ws_ref, bst, sink_ref, kprev, vprev, has_prev, d_a=d_a, d_b=d_b, d_kv=d_kv))

    def proj_chunk(n0):
        acc = jnp.dot(h_scr[...], win_ref[:, n0:n0 + TN_DOT], preferred_element_type=F32)
        proj_w[:, n0:n0 + TN_DOT] = acc.astype(BF16)

    chunks = list(range(0, d_in, TN_DOT))
    next_chunk = [0]

    def emit_chunks(n):
        for _ in range(n):
            if do_a and next_chunk[0] < len(chunks):
                proj_chunk(chunks[next_chunk[0]])
                next_chunk[0] += 1

    if not do_bc:
        emit_chunks(len(chunks))
        xprev_scr[...] = x
        return

    for b in blocks:
        b.prep()
    n_pairs = d_b // LANES
    slots = [(b, p0) for b in blocks for p0 in range(0, n_pairs, COLS_PER_KV)]
    group_pairs = [(b, g0) for b in blocks for g0 in range(0, A_GROUPS, 2)]
    pairs_per_slot = -(-len(group_pairs) // len(slots))
    slots[0][0].scores(slots[0][1])
    for j, (b, p0) in enumerate(slots):
        emit_chunks(1)
        if j + 1 < len(slots):
            slots[j + 1][0].scores(slots[j + 1][1])
        for bg, g0 in group_pairs[j * pairs_per_slot:(j + 1) * pairs_per_slot]:
            bg.group_pair(g0)
        b.values(p0)
    for e in range(N_EXP):
        kprev_scr[e] = blocks[-1].ke[e]
        vprev_scr[e] = blocks[-1].ve[e]
    emit_chunks(len(chunks) - next_chunk[0] - 2)

    row_c = pl.ds(blk_c // blocks_per_batch, 1)
    gate = modc_ref[row_c, 2 * d_model:3 * d_model]
    ssq = jnp.zeros((T_BLK, 1), F32)
    for n0 in range(0, d_model, TN_DOT):
        sl = slice(n0, n0 + TN_DOT)
        acc = jnp.dot(y_scr[...], wout_ref[:, sl], preferred_element_type=F32)
        xr = xprev_scr[:, sl] + gate[:, sl] * acc
        ssq = ssq + jnp.sum(xr * xr, axis=-1, keepdims=True)
        o_ref[:, sl] = xr
    emit_chunks(len(chunks))
    inv = lax.rsqrt(ssq * (1.0 / d_model) + NORM_EPS)
    shift_f = modf_ref[row_c, 0:d_model]
    scale_f = modf_ref[row_c, d_model:2 * d_model]
    o_ref[...] = o_ref[...] * inv * (fg_ref[...] * (1.0 + scale_f)) + shift_f

    if do_a:
        xprev_scr[...] = x


def _layer(x2, sinks, mod, mod_f, norm_g, final_g, cos_t, sin_t, ln_g, ln_b, w_sp, b_sp_t,
           w_in, w_out, seq, d_a, d_b, d_kv):
    rows, d_model = x2.shape
    d_in = w_in.shape[-1]
    d_mix = d_a + d_b
    n_blk = rows // T_BLK
    bpb = seq // T_BLK
    assert d_model % W_TILES == 0 and d_mix % W_TILES == 0

    def blk_a(i):
        return jnp.clip(i - W_TILES, 0, n_blk - 1)

    def blk_c(i):
        return jnp.clip(i - W_TILES - 1, 0, n_blk - 1)

    def w_tile(i):
        return (jnp.minimum(i, W_TILES - 1), 0)

    const2 = lambda i: (0, 0)
    kern = functools.partial(_layer_kernel, d_model=d_model, d_in=d_in, d_a=d_a, d_b=d_b,
                             d_kv=d_kv, blocks_per_batch=bpb, n_blk=n_blk)
    return pl.pallas_call(
        kern,
        grid=(W_TILES + n_blk + 1,),
        in_specs=[
            pl.BlockSpec(memory_space=pltpu.SMEM),
            pl.BlockSpec((T_BLK, d_model), lambda i: (blk_a(i), 0)),
            pl.BlockSpec(mod.shape, const2),
            pl.BlockSpec(mod.shape, const2),
            pl.BlockSpec(mod_f.shape, const2),
            pl.BlockSpec((1, d_model), const2),
            pl.BlockSpec((1, d_model), const2),
            pl.BlockSpec((T_BLK, LANES), lambda i: (blk_c(i) % bpb, 0)),
            pl.BlockSpec((T_BLK, LANES), lambda i: (blk_c(i) % bpb, 0)),
            pl.BlockSpec((1, d_a), const2),
            pl.BlockSpec((1, d_a), const2),
            pl.BlockSpec((A_GROUPS, CHUNK, CHUNK), lambda i: (0, 0, 0)),
            pl.BlockSpec((CHUNK, A_GROUPS), const2),
            pl.BlockSpec((d_model // W_TILES, d_in), w_tile),
            pl.BlockSpec((d_mix // W_TILES, d_model), w_tile),
        ],
        out_specs=pl.BlockSpec((T_BLK, d_model), lambda i: (blk_c(i), 0)),
        out_shape=jax.ShapeDtypeStruct((rows, d_model), F32),
        scratch_shapes=[
            pltpu.VMEM((d_model, d_in), BF16),
            pltpu.VMEM((d_mix, d_model), BF16),
            pltpu.VMEM((T_BLK, d_in), BF16),
            pltpu.VMEM((T_BLK, d_in), BF16),
            pltpu.VMEM((T_BLK, d_mix), BF16),
            pltpu.VMEM((T_BLK, d_model), F32),
            pltpu.VMEM((T_BLK, d_model), BF16),
            pltpu.VMEM((N_EXP, CHUNK, LANES), BF16),
            pltpu.VMEM((N_EXP, CHUNK, LANES), BF16),
        ],
        compiler_params=pltpu.CompilerParams(
            dimension_semantics=("arbitrary",), vmem_limit_bytes=VMEM_LIMIT_LAYER),
        name="layer",
    )(sinks, x2, mod, mod, mod_f, norm_g.reshape(1, d_model), final_g.reshape(1, d_model),
      cos_t, sin_t, ln_g, ln_b, w_sp, b_sp_t, w_in, w_out)


def _rope_tables(seq):
    half = HEAD_DIM // 2
    inv_freq = ROPE_THETA ** (-jnp.arange(0, HEAD_DIM, 2, dtype=F32) / HEAD_DIM)
    ang = jnp.arange(seq, dtype=F32)[:, None] * inv_freq[None, :]
    cos = jnp.cos(ang)
    sin = jnp.sin(ang)
    cos_t = jnp.tile(cos, (1, LANES // half))
    sin_t = jnp.tile(jnp.concatenate([-sin, sin], axis=1), (1, HEADS_PER_VREG))
    return cos_t, sin_t


def kernel(x, c, w_ada, b_ada, norm_g, w_in, ln_v_g, ln_v_b, w_spatial, b_spatial, sinks,
           w_out, w_ada_final, b_ada_final, final_norm_g):
    bsz, seq, d_model = x.shape
    assert w_ada.shape[0] == 1, "single-layer stack"
    d_a = ln_v_g.shape[-1]
    d_mix = w_out.shape[-2]
    d_b = d_mix - d_a
    d_kv = N_KV_HEADS * HEAD_DIM
    d_in = w_in.shape[-1]
    assert d_in == 3 * d_a + 2 * d_b + 2 * d_kv
    assert d_a == A_GROUPS * A_GROUP_W and d_b == N_KV_HEADS * Q_PER_KV * HEAD_DIM
    assert seq % T_BLK == 0 and T_BLK % CHUNK == 0

    x2 = x.reshape(bsz * seq, d_model)
    mod, mod_f = _ada_mod(c, w_ada, b_ada, w_ada_final, b_ada_final)
    cos_t, sin_t = _rope_tables(seq)
    out = _layer(x2, sinks.reshape(-1), mod, mod_f, norm_g, final_norm_g, cos_t, sin_t,
                 ln_v_g.reshape(1, d_a), ln_v_b.reshape(1, d_a),
                 w_spatial.reshape(A_GROUPS, CHUNK, CHUNK),
                 b_spatial.reshape(A_GROUPS, CHUNK).T,
                 w_in.reshape(d_model, d_in), w_out.reshape(d_mix, d_model),
                 seq, d_a, d_b, d_kv)
    return out.reshape(bsz, seq, d_model)
```

```python
import functools

import jax
import jax.numpy as jnp
from jax import lax
from jax.experimental import pallas as pl
from jax.experimental.pallas import tpu as pltpu

F32 = jnp.float32
BF16 = jnp.bfloat16

CHUNK = 128
A_GROUPS = 8
A_GROUP_W = 128
HEAD_DIM = 64
N_KV_HEADS = 4
Q_PER_KV = 4
ROPE_THETA = 10000.0
NORM_EPS = 1e-5
LOG2E = 1.4426950408889634

LANES = 128
HEADS_PER_VREG = LANES // HEAD_DIM
N_EXP = N_KV_HEADS * HEADS_PER_VREG
COLS_PER_KV = Q_PER_KV // HEADS_PER_VREG

T_BLK = 256
TN_DOT = 512
W_TILES = 16
TN_ADA = 1024
VMEM_LIMIT_ADA = 40 * 1024 * 1024
VMEM_LIMIT_LAYER = 63 * 1024 * 1024


def _silu(z):
    hz = 0.5 * z
    return hz + hz * jnp.tanh(hz)


def _ada_kernel(c_ref, wa_top, wa_bot, wf_top, wf_bot, ba_ref, bf_ref, oa_ref, of_ref,
                *, n_a_tiles):
    j = pl.program_id(0)
    c = c_ref[...]
    ca = (c * (1.0 / (1.0 + jnp.exp(-c)))).astype(BF16)
    half = ca.shape[1] // 2

    def mod(w_top, w_bot, b_ref, o_ref):
        acc = jnp.dot(ca[:, :half], w_top[...].astype(BF16), preferred_element_type=F32)
        acc = acc + jnp.dot(ca[:, half:], w_bot[...].astype(BF16),
                            preferred_element_type=F32)
        o_ref[...] = acc + b_ref[...]

    @pl.when(j < n_a_tiles)
    def _():
        mod(wa_top, wa_bot, ba_ref, oa_ref)

    @pl.when(j >= n_a_tiles)
    def _():
        mod(wf_top, wf_bot, bf_ref, of_ref)


def _ada_mod(c, w_a, b_a, w_f, b_f):
    bsz, d = c.shape
    n_a, n_f = w_a.shape[-1], w_f.shape[-1]
    w_a = w_a.reshape(d, n_a)
    w_f = w_f.reshape(d, n_f)
    ta, tf = n_a // TN_ADA, n_f // TN_ADA
    half = d // 2
    a_tile = lambda j: jnp.minimum(j, ta - 1)
    f_tile = lambda j: jnp.maximum(j - ta, 0)
    return pl.pallas_call(
        functools.partial(_ada_kernel, n_a_tiles=ta),
        grid=(ta + tf,),
        in_specs=[
            pl.BlockSpec((bsz, d), lambda j: (0, 0)),
            pl.BlockSpec((half, TN_ADA), lambda j: (0, a_tile(j))),
            pl.BlockSpec((half, TN_ADA), lambda j: (1, a_tile(j))),
            pl.BlockSpec((half, TN_ADA), lambda j: (0, f_tile(j))),
            pl.BlockSpec((half, TN_ADA), lambda j: (1, f_tile(j))),
            pl.BlockSpec((1, TN_ADA), lambda j: (0, a_tile(j))),
            pl.BlockSpec((1, TN_ADA), lambda j: (0, f_tile(j))),
        ],
        out_specs=[
            pl.BlockSpec((bsz, TN_ADA), lambda j: (0, a_tile(j))),
            pl.BlockSpec((bsz, TN_ADA), lambda j: (0, f_tile(j))),
        ],
        out_shape=[jax.ShapeDtypeStruct((bsz, n_a), F32),
                   jax.ShapeDtypeStruct((bsz, n_f), F32)],
        compiler_params=pltpu.CompilerParams(
            dimension_semantics=("arbitrary",), vmem_limit_bytes=VMEM_LIMIT_ADA),
        name="ada_mod",
    )(c, w_a, w_a, w_f, w_f, b_a.reshape(1, n_a), b_f.reshape(1, n_f))


class _MixerBlock:
    def __init__(self, load, store, cos, sin, lng, lnb, ws_ref, bst, sink_ref, kprev, vprev,
                 has_prev, *, d_a, d_b, d_kv):
        self.load, self.store = load, store
        self.cos, self.sin, self.lng, self.lnb = cos, sin, lng, lnb
        q_scale = LOG2E * HEAD_DIM ** -0.5
        self.cos_q, self.sin_q = cos * q_scale, sin * q_scale
        self.ws_ref, self.bst, self.sink_ref = ws_ref, bst, sink_ref
        self.kprev, self.vprev, self.has_prev = kprev, vprev, has_prev
        self.d_a, self.d_b, self.d_kv = d_a, d_b, d_kv
        self.o_u, self.o_v, self.o_za = 0, d_a, 2 * d_a
        self.o_q = 3 * d_a
        self.o_k = self.o_q + d_b
        self.o_vv = self.o_k + d_kv
        self.o_zb = self.o_vv + d_kv
        row = lax.broadcasted_iota(jnp.int32, (CHUNK, CHUNK), 0)
        col = lax.broadcasted_iota(jnp.int32, (CHUNK, CHUNK), 1)
        self.col = col
        self.causal = col <= row
        self.first_half = (col & (HEAD_DIM - 1)) < (HEAD_DIM // 2)
        self.probs = {}

    def _rope(self, xv, query=False):
        cos, sin = (self.cos_q, self.sin_q) if query else (self.cos, self.sin)
        rot = jnp.where(self.first_half,
                        pltpu.roll(xv, LANES - HEAD_DIM // 2, 1),
                        pltpu.roll(xv, HEAD_DIM // 2, 1))
        return xv * cos + rot * sin

    def prep(self):
        va = self.load(self.o_v, self.o_v + self.d_a).astype(F32)
        mu = jnp.mean(va, axis=-1, keepdims=True)
        vc = va - mu
        var = jnp.mean(vc * vc, axis=-1, keepdims=True)
        self.vn = (vc * lax.rsqrt(var + NORM_EPS) * self.lng + self.lnb).astype(BF16)
        self.tril = self.causal.astype(F32)
        low_half = self.col < HEAD_DIM
        self.ke = [None] * N_EXP
        self.ve = [None] * N_EXP
        for c in range(self.d_kv // LANES):
            kc = self._rope(self.load(self.o_k + c * LANES, self.o_k + (c + 1) * LANES)
                            .astype(F32))
            vcol = self.load(self.o_vv + c * LANES, self.o_vv + (c + 1) * LANES).astype(F32)
            kc_sw = pltpu.roll(kc, HEAD_DIM, 1)
            vcol_sw = pltpu.roll(vcol, HEAD_DIM, 1)
            for j in range(HEADS_PER_VREG):
                kv_head = c * HEADS_PER_VREG + j
                for o in range(HEADS_PER_VREG):
                    mask = low_half if o == 0 else jnp.logical_not(low_half)
                    e = kv_head * HEADS_PER_VREG + o
                    self.ke[e] = jnp.where(mask, kc if o == j else kc_sw, 0.0).astype(BF16)
                    self.ve[e] = jnp.where(mask, vcol if o == j else vcol_sw, 0.0).astype(BF16)

    def group_pair(self, g0):
        c0 = g0 * A_GROUP_W
        w = jnp.concatenate([(self.ws_ref[g0] * self.tril).astype(BF16),
                             (self.ws_ref[g0 + 1] * self.tril).astype(BF16)], axis=1)
        zero = jnp.zeros((CHUNK, A_GROUP_W), BF16)
        rhs = jnp.concatenate(
            [jnp.concatenate([self.vn[:, c0:c0 + A_GROUP_W], zero], axis=1),
             jnp.concatenate([zero, self.vn[:, c0 + A_GROUP_W:c0 + 2 * A_GROUP_W]], axis=1)],
            axis=0)
        s2 = jnp.dot(w, rhs, preferred_element_type=F32)
        for k in range(2):
            g = g0 + k
            cg = g * A_GROUP_W
            s = s2[:, k * A_GROUP_W:(k + 1) * A_GROUP_W] + self.bst[:, g:g + 1]
            u = self.load(self.o_u + cg, self.o_u + cg + A_GROUP_W).astype(F32)
            za = self.load(self.o_za + cg, self.o_za + cg + A_GROUP_W).astype(F32)
            self.store(cg, (u * s * _silu(za)).astype(BF16))

    def scores(self, p0):
        kv_head = (p0 * HEADS_PER_VREG) // Q_PER_KV
        kprev = self.kprev()
        q4 = jnp.concatenate(
            [self._rope(self.load(self.o_q + p * LANES, self.o_q + (p + 1) * LANES)
                        .astype(F32), query=True).astype(BF16)
             for p in range(p0, p0 + COLS_PER_KV)], axis=0)
        probs = [[] for _ in range(COLS_PER_KV)]
        for o in range(HEADS_PER_VREG):
            e = kv_head * HEADS_PER_VREG + o
            kband = jnp.concatenate([kprev[e], self.ke[e]], axis=0)
            s4 = lax.dot_general(q4, kband, (((1,), (1,)), ((), ())),
                                 preferred_element_type=F32)
            for a in range(COLS_PER_KV):
                s = s4[a * CHUNK:(a + 1) * CHUNK]
                h = (p0 + a) * HEADS_PER_VREG + o
                s_prev = s[:, :CHUNK]
                if self.has_prev is not True:
                    s_prev = jnp.where(self.has_prev, s_prev, -jnp.inf)
                comb = jnp.where(self.causal, s[:, CHUNK:], s_prev)
                sink = self.sink_ref[h] * LOG2E
                m = jnp.maximum(jnp.max(comb, axis=-1, keepdims=True), sink)
                pexp = jnp.exp2(comb - m)
                denom = jnp.sum(pexp, axis=-1, keepdims=True) + jnp.exp2(sink - m)
                pn = (pexp * (1.0 / denom)).astype(BF16)
                zero = jnp.zeros_like(pn)
                probs[a].append(jnp.where(self.causal, zero, pn))
                probs[a].append(jnp.where(self.causal, pn, zero))
        self.probs[p0] = jnp.concatenate(
            [jnp.concatenate(pa, axis=1) for pa in probs], axis=0)

    def values(self, p0):
        kv_head = (p0 * HEADS_PER_VREG) // Q_PER_KV
        vprev = self.vprev()
        vband = []
        for o in range(HEADS_PER_VREG):
            e = kv_head * HEADS_PER_VREG + o
            vband.append(vprev[e])
            vband.append(self.ve[e])
        out4 = jnp.dot(self.probs.pop(p0), jnp.concatenate(vband, axis=0),
                       preferred_element_type=F32)
        for a in range(COLS_PER_KV):
            p = p0 + a
            zb = self.load(self.o_zb + p * LANES, self.o_zb + (p + 1) * LANES).astype(F32)
            self.store(self.d_a + p * LANES,
                       (out4[a * CHUNK:(a + 1) * CHUNK] * _silu(zb)).astype(BF16))


def _layer_kernel(sink_ref, x_ref, moda_ref, modc_ref, modf_ref, ng_ref, fg_ref,
                  cos_ref, sin_ref, lng_ref, lnb_ref, ws_ref, bst_ref, win32_ref, wout32_ref,
                  o_ref, win_scr, wout_scr, proj0_scr, proj1_scr, y_scr, xprev_scr, h_scr,
                  kprev_scr, vprev_scr,
                  *, d_model, d_in, d_a, d_b, d_kv, blocks_per_batch, n_blk):
    i = pl.program_id(0)

    @pl.when(i < W_TILES)
    def _():
        r_in = pl.multiple_of(i * win32_ref.shape[0], win32_ref.shape[0])
        win_scr[pl.ds(r_in, win32_ref.shape[0]), :] = win32_ref[...].astype(BF16)
        r_out = pl.multiple_of(i * wout32_ref.shape[0], wout32_ref.shape[0])
        wout_scr[pl.ds(r_out, wout32_ref.shape[0]), :] = wout32_ref[...].astype(BF16)

    step = i - W_TILES

    args = (sink_ref, x_ref, moda_ref, modc_ref, modf_ref, ng_ref, fg_ref, cos_ref, sin_ref,
            lng_ref, lnb_ref, ws_ref, bst_ref, win_scr, wout_scr, o_ref)
    scr = (y_scr, xprev_scr, h_scr, kprev_scr, vprev_scr)
    dims = dict(d_model=d_model, d_in=d_in, d_a=d_a, d_b=d_b, d_kv=d_kv,
                blocks_per_batch=blocks_per_batch, n_blk=n_blk)
    bufs = ((proj0_scr, proj1_scr), (proj1_scr, proj0_scr))
    inner = jnp.logical_and(step > 0, step < n_blk)

    @pl.when(step == 0)
    def _():
        kprev_scr[...] = jnp.zeros_like(kprev_scr)
        vprev_scr[...] = jnp.zeros_like(vprev_scr)
        _layer_step(step, *args, *bufs[0], *scr, **dims, do_bc=False)

    @pl.when(jnp.logical_and(inner, step % 2 == 0))
    def _():
        _layer_step(step, *args, *bufs[0], *scr, **dims)

    @pl.when(jnp.logical_and(inner, step % 2 == 1))
    def _():
        _layer_step(step, *args, *bufs[1], *scr, **dims)

    @pl.when(step == n_blk)
    def _():
        _layer_step(step, *args, *bufs[n_blk % 2], *scr, **dims, do_a=False)


def _layer_step(step, sink_ref, x_ref, moda_ref, modc_ref, modf_ref, ng_ref, fg_ref,
                cos_ref, sin_ref, lng_ref, lnb_ref, ws_ref, bst_ref, win_ref, wout_ref,
                o_ref, proj_w, proj_r, y_scr, xprev_scr, h_scr, kprev_scr, vprev_scr,
                *, d_model, d_in, d_a, d_b, d_kv, blocks_per_batch, n_blk,
                do_a=True, do_bc=True):
    blk_c = jnp.clip(step - 1, 0, n_blk - 1)
    first_in_batch = (blk_c % blocks_per_batch) == 0

    if do_a:
        x = x_ref[...]
        ms = jnp.mean(x * x, axis=-1, keepdims=True)
        row_a = pl.ds(jnp.minimum(step, n_blk - 1) // blocks_per_batch, 1)
        shift = moda_ref[row_a, 0:d_model]
        gain = ng_ref[...] * (1.0 + moda_ref[row_a, d_model:2 * d_model])
        h_scr[...] = (x * lax.rsqrt(ms + NORM_EPS) * gain + shift).astype(BF16)

    lng = lng_ref[...]
    lnb = lnb_ref[...]
    bst = bst_ref[...]
    n_sub = T_BLK // CHUNK
    blocks = []
    for sb in range(n_sub):
        r0 = sb * CHUNK

        def load(c0, c1, r0=r0):
            return proj_r[r0:r0 + CHUNK, c0:c1]

        def store(c0, val, r0=r0):
            y_scr[r0:r0 + CHUNK, c0:c0 + val.shape[1]] = val

        if sb == 0:
            kprev = lambda: [kprev_scr[e] for e in range(N_EXP)]
            vprev = lambda: [vprev_scr[e] for e in range(N_EXP)]
            has_prev = jnp.logical_not(first_in_batch)
        else:
            kprev = lambda b=blocks[sb - 1]: b.ke
            vprev = lambda b=blocks[sb - 1]: b.ve
            has_prev = True
        blocks.append(_MixerBlock(
            load, store, cos_ref[r0:r0 + CHUNK, :], sin_ref[r0:r0 + CHUNK, :], lng, lnb,
            ws_ref, bst, sink_ref, kprev, vprev, has_prev, d_a=d_a, d_b=d_b, d_kv=d_kv))

    def proj_chunk(n0):
        acc = jnp.dot(h_scr[...], win_ref[:, n0:n0 + TN_DOT], preferred_element_type=F32)
        proj_w[:, n0:n0 + TN_DOT] = acc.astype(BF16)

    chunks = list(range(0, d_in, TN_DOT))
    next_chunk = [0]

    def emit_chunks(n):
        for _ in range(n):
            if do_a and next_chunk[0] < len(chunks):
                proj_chunk(chunks[next_chunk[0]])
                next_chunk[0] += 1

    if not do_bc:
        emit_chunks(len(chunks))
        xprev_scr[...] = x
        return

    for b in blocks:
        b.prep()
    n_pairs = d_b // LANES
    slots = [(b, p0) for b in blocks for p0 in range(0, n_pairs, COLS_PER_KV)]
    group_pairs = [(b, g0) for b in blocks for g0 in range(0, A_GROUPS, 2)]
    pairs_per_slot = -(-len(group_pairs) // len(slots))
    slots[0][0].scores(slots[0][1])
    for j, (b, p0) in enumerate(slots):
        emit_chunks(1)
        if j + 1 < len(slots):
            slots[j + 1][0].scores(slots[j + 1][1])
        for bg, g0 in group_pairs[j * pairs_per_slot:(j + 1) * pairs_per_slot]:
            bg.group_pair(g0)
        b.values(p0)
    for e in range(N_EXP):
        kprev_scr[e] = blocks[-1].ke[e]
        vprev_scr[e] = blocks[-1].ve[e]
    emit_chunks(len(chunks) - next_chunk[0] - 2)

    row_c = pl.ds(blk_c // blocks_per_batch, 1)
    gate = modc_ref[row_c, 2 * d_model:3 * d_model]
    ssq = jnp.zeros((T_BLK, 1), F32)
    for n0 in range(0, d_model, TN_DOT):
        sl = slice(n0, n0 + TN_DOT)
        acc = jnp.dot(y_scr[...], wout_ref[:, sl], preferred_element_type=F32)
        xr = xprev_scr[:, sl] + gate[:, sl] * acc
        ssq = ssq + jnp.sum(xr * xr, axis=-1, keepdims=True)
        o_ref[:, sl] = xr
    emit_chunks(len(chunks))
    inv = lax.rsqrt(ssq * (1.0 / d_model) + NORM_EPS)
    shift_f = modf_ref[row_c, 0:d_model]
    scale_f = modf_ref[row_c, d_model:2 * d_model]
    o_ref[...] = o_ref[...] * inv * (fg_ref[...] * (1.0 + scale_f)) + shift_f

    if do_a:
        xprev_scr[...] = x


def _layer(x2, sinks, mod, mod_f, norm_g, final_g, cos_t, sin_t, ln_g, ln_b, w_sp, b_sp_t,
           w_in, w_out, seq, d_a, d_b, d_kv):
    rows, d_model = x2.shape
    d_in = w_in.shape[-1]
    d_mix = d_a + d_b
    n_blk = rows // T_BLK
    bpb = seq // T_BLK
    assert d_model % W_TILES == 0 and d_mix % W_TILES == 0

    def blk_a(i):
        return jnp.clip(i - W_TILES, 0, n_blk - 1)

    def blk_c(i):
        return jnp.clip(i - W_TILES - 1, 0, n_blk - 1)

    def w_tile(i):
        return (jnp.minimum(i, W_TILES - 1), 0)

    const2 = lambda i: (0, 0)
    kern = functools.partial(_layer_kernel, d_model=d_model, d_in=d_in, d_a=d_a, d_b=d_b,
                             d_kv=d_kv, blocks_per_batch=bpb, n_blk=n_blk)
    return pl.pallas_call(
        kern,
        grid=(W_TILES + n_blk + 1,),
        in_specs=[
            pl.BlockSpec(memory_space=pltpu.SMEM),
            pl.BlockSpec((T_BLK, d_model), lambda i: (blk_a(i), 0)),
            pl.BlockSpec(mod.shape, const2),
            pl.BlockSpec(mod.shape, const2),
            pl.BlockSpec(mod_f.shape, const2),
            pl.BlockSpec((1, d_model), const2),
            pl.BlockSpec((1, d_model), const2),
            pl.BlockSpec((T_BLK, LANES), lambda i: (blk_c(i) % bpb, 0)),
            pl.BlockSpec((T_BLK, LANES), lambda i: (blk_c(i) % bpb, 0)),
            pl.BlockSpec((1, d_a), const2),
            pl.BlockSpec((1, d_a), const2),
            pl.BlockSpec((A_GROUPS, CHUNK, CHUNK), lambda i: (0, 0, 0)),
            pl.BlockSpec((CHUNK, A_GROUPS), const2),
            pl.BlockSpec((d_model // W_TILES, d_in), w_tile),
            pl.BlockSpec((d_mix // W_TILES, d_model), w_tile),
        ],
        out_specs=pl.BlockSpec((T_BLK, d_model), lambda i: (blk_c(i), 0)),
        out_shape=jax.ShapeDtypeStruct((rows, d_model), F32),
        scratch_shapes=[
            pltpu.VMEM((d_model, d_in), BF16),
            pltpu.VMEM((d_mix, d_model), BF16),
            pltpu.VMEM((T_BLK, d_in), BF16),
            pltpu.VMEM((T_BLK, d_in), BF16),
            pltpu.VMEM((T_BLK, d_mix), BF16),
            pltpu.VMEM((T_BLK, d_model), F32),
            pltpu.VMEM((T_BLK, d_model), BF16),
            pltpu.VMEM((N_EXP, CHUNK, LANES), BF16),
            pltpu.VMEM((N_EXP, CHUNK, LANES), BF16),
        ],
        compiler_params=pltpu.CompilerParams(
            dimension_semantics=("arbitrary",), vmem_limit_bytes=VMEM_LIMIT_LAYER),
        name="layer",
    )(sinks, x2, mod, mod, mod_f, norm_g.reshape(1, d_model), final_g.reshape(1, d_model),
      cos_t, sin_t, ln_g, ln_b, w_sp, b_sp_t, w_in, w_out)


def _rope_tables(seq):
    half = HEAD_DIM // 2
    inv_freq = ROPE_THETA ** (-jnp.arange(0, HEAD_DIM, 2, dtype=F32) / HEAD_DIM)
    ang = jnp.arange(seq, dtype=F32)[:, None] * inv_freq[None, :]
    cos = jnp.cos(ang)
    sin = jnp.sin(ang)
    cos_t = jnp.tile(cos, (1, LANES // half))
    sin_t = jnp.tile(jnp.concatenate([-sin, sin], axis=1), (1, HEADS_PER_VREG))
    return cos_t, sin_t


def kernel(x, c, w_ada, b_ada, norm_g, w_in, ln_v_g, ln_v_b, w_spatial, b_spatial, sinks,
           w_out, w_ada_final, b_ada_final, final_norm_g):
    bsz, seq, d_model = x.shape
    assert w_ada.shape[0] == 1, "single-layer stack"
    d_a = ln_v_g.shape[-1]
    d_mix = w_out.shape[-2]
    d_b = d_mix - d_a
    d_kv = N_KV_HEADS * HEAD_DIM
    d_in = w_in.shape[-1]
    assert d_in == 3 * d_a + 2 * d_b + 2 * d_kv
    assert d_a == A_GROUPS * A_GROUP_W and d_b == N_KV_HEADS * Q_PER_KV * HEAD_DIM
    assert seq % T_BLK == 0 and T_BLK % CHUNK == 0

    x2 = x.reshape(bsz * seq, d_model)
    mod, mod_f = _ada_mod(c, w_ada, b_ada, w_ada_final, b_ada_final)
    cos_t, sin_t = _rope_tables(seq)
    out = _layer(x2, sinks.reshape(-1), mod, mod_f, norm_g, final_norm_g, cos_t, sin_t,
                 ln_v_g.reshape(1, d_a), ln_v_b.reshape(1, d_a),
                 w_spatial.reshape(A_GROUPS, CHUNK, CHUNK),
                 b_spatial.reshape(A_GROUPS, CHUNK).T,
                 w_in.reshape(d_model, d_in), w_out.reshape(d_mix, d_model),
                 seq, d_a, d_b, d_kv)
    return out.reshape(bsz, seq, d_model)
```

```python
import functools

import jax
import jax.numpy as jnp
from jax import lax
from jax.experimental import pallas as pl
from jax.experimental.pallas import tpu as pltpu

F32 = jnp.float32
BF16 = jnp.bfloat16

CHUNK = 128
A_GROUPS = 8
A_GROUP_W = 128
HEAD_DIM = 64
N_KV_HEADS = 4
Q_PER_KV = 4
ROPE_THETA = 10000.0
NORM_EPS = 1e-5
LOG2E = 1.4426950408889634

LANES = 128
HEADS_PER_VREG = LANES // HEAD_DIM
N_EXP = N_KV_HEADS * HEADS_PER_VREG
COLS_PER_KV = Q_PER_KV // HEADS_PER_VREG

T_BLK = 256
TN_DOT = 512
W_TILES = 16
TN_ADA = 1024
VMEM_LIMIT_ADA = 40 * 1024 * 1024
VMEM_LIMIT_LAYER = 63 * 1024 * 1024


def _silu(z):
    hz = 0.5 * z
    return hz + hz * jnp.tanh(hz)


def _ada_kernel(c_ref, wa_top, wa_bot, wf_top, wf_bot, ba_ref, bf_ref, oa_ref, of_ref,
                *, n_a_tiles):
    j = pl.program_id(0)
    c = c_ref[...]
    ca = (c * (1.0 / (1.0 + jnp.exp(-c)))).astype(BF16)
    half = ca.shape[1] // 2

    def mod(w_top, w_bot, b_ref, o_ref):
        acc = jnp.dot(ca[:, :half], w_top[...].astype(BF16), preferred_element_type=F32)
        acc = acc + jnp.dot(ca[:, half:], w_bot[...].astype(BF16),
                            preferred_element_type=F32)
        o_ref[...] = acc + b_ref[...]

    @pl.when(j < n_a_tiles)
    def _():
        mod(wa_top, wa_bot, ba_ref, oa_ref)

    @pl.when(j >= n_a_tiles)
    def _():
        mod(wf_top, wf_bot, bf_ref, of_ref)


def _ada_mod(c, w_a, b_a, w_f, b_f):
    bsz, d = c.shape
    n_a, n_f = w_a.shape[-1], w_f.shape[-1]
    w_a = w_a.reshape(d, n_a)
    w_f = w_f.reshape(d, n_f)
    ta, tf = n_a // TN_ADA, n_f // TN_ADA
    half = d // 2
    a_tile = lambda j: jnp.minimum(j, ta - 1)
    f_tile = lambda j: jnp.maximum(j - ta, 0)
    return pl.pallas_call(
        functools.partial(_ada_kernel, n_a_tiles=ta),
        grid=(ta + tf,),
        in_specs=[
            pl.BlockSpec((bsz, d), lambda j: (0, 0)),
            pl.BlockSpec((half, TN_ADA), lambda j: (0, a_tile(j))),
            pl.BlockSpec((half, TN_ADA), lambda j: (1, a_tile(j))),
            pl.BlockSpec((half, TN_ADA), lambda j: (0, f_tile(j))),
            pl.BlockSpec((half, TN_ADA), lambda j: (1, f_tile(j))),
            pl.BlockSpec((1, TN_ADA), lambda j: (0, a_tile(j))),
            pl.BlockSpec((1, TN_ADA), lambda j: (0, f_tile(j))),
        ],
        out_specs=[
            pl.BlockSpec((bsz, TN_ADA), lambda j: (0, a_tile(j))),
            pl.BlockSpec((bsz, TN_ADA), lambda j: (0, f_tile(j))),
        ],
        out_shape=[jax.ShapeDtypeStruct((bsz, n_a), F32),
                   jax.ShapeDtypeStruct((bsz, n_f), F32)],
        compiler_params=pltpu.CompilerParams(
            dimension_semantics=("arbitrary",), vmem_limit_bytes=VMEM_LIMIT_ADA),
        name="ada_mod",
    )(c, w_a, w_a, w_f, w_f, b_a.reshape(1, n_a), b_f.reshape(1, n_f))


class _MixerBlock:
    def __init__(self, load, store, cos, sin, lng, lnb, ws_ref, bst, sink_ref, kprev, vprev,
                 has_prev, *, d_a, d_b, d_kv):
        self.load, self.store = load, store
        self.cos, self.sin, self.lng, self.lnb = cos, sin, lng, lnb
        q_scale = LOG2E * HEAD_DIM ** -0.5
        self.cos_q, self.sin_q = cos * q_scale, sin * q_scale
        self.ws_ref, self.bst, self.sink_ref = ws_ref, bst, sink_ref
        self.kprev, self.vprev, self.has_prev = kprev, vprev, has_prev
        self.d_a, self.d_b, self.d_kv = d_a, d_b, d_kv
        self.o_u, self.o_v, self.o_za = 0, d_a, 2 * d_a
        self.o_q = 3 * d_a
        self.o_k = self.o_q + d_b
        self.o_vv = self.o_k + d_kv
        self.o_zb = self.o_vv + d_kv
        row = lax.broadcasted_iota(jnp.int32, (CHUNK, CHUNK), 0)
        col = lax.broadcasted_iota(jnp.int32, (CHUNK, CHUNK), 1)
        self.col = col
        self.causal = col <= row
        self.first_half = (col & (HEAD_DIM - 1)) < (HEAD_DIM // 2)
        self.probs = {}

    def _rope(self, xv, query=False):
        cos, sin = (self.cos_q, self.sin_q) if query else (self.cos, self.sin)
        rot = jnp.where(self.first_half,
                        pltpu.roll(xv, LANES - HEAD_DIM // 2, 1),
                        pltpu.roll(xv, HEAD_DIM // 2, 1))
        return xv * cos + rot * sin

    def prep(self):
        va = self.load(self.o_v, self.o_v + self.d_a).astype(F32)
        mu = jnp.mean(va, axis=-1, keepdims=True)
        vc = va - mu
        var = jnp.mean(vc * vc, axis=-1, keepdims=True)
        self.vn = (vc * lax.rsqrt(var + NORM_EPS) * self.lng + self.lnb).astype(BF16)
        self.tril = self.causal.astype(F32)
        low_half = self.col < HEAD_DIM
        self.ke = [None] * N_EXP
        self.ve = [None] * N_EXP
        for c in range(self.d_kv // LANES):
            kc = self._rope(self.load(self.o_k + c * LANES, self.o_k + (c + 1) * LANES)
                            .astype(F32))
            vcol = self.load(self.o_vv + c * LANES, self.o_vv + (c + 1) * LANES).astype(F32)
            kc_sw = pltpu.roll(kc, HEAD_DIM, 1)
            vcol_sw = pltpu.roll(vcol, HEAD_DIM, 1)
            for j in range(HEADS_PER_VREG):
                kv_head = c * HEADS_PER_VREG + j
                for o in range(HEADS_PER_VREG):
                    mask = low_half if o == 0 else jnp.logical_not(low_half)
                    e = kv_head * HEADS_PER_VREG + o
                    self.ke[e] = jnp.where(mask, kc if o == j else kc_sw, 0.0).astype(BF16)
                    self.ve[e] = jnp.where(mask, vcol if o == j else vcol_sw, 0.0).astype(BF16)

    def group_pair(self, g0):
        c0 = g0 * A_GROUP_W
        w = jnp.concatenate([(self.ws_ref[g0] * self.tril).astype(BF16),
                             (self.ws_ref[g0 + 1] * self.tril).astype(BF16)], axis=1)
        zero = jnp.zeros((CHUNK, A_GROUP_W), BF16)
        rhs = jnp.concatenate(
            [jnp.concatenate([self.vn[:, c0:c0 + A_GROUP_W], zero], axis=1),
             jnp.concatenate([zero, self.vn[:, c0 + A_GROUP_W:c0 + 2 * A_GROUP_W]], axis=1)],
            axis=0)
        s2 = jnp.dot(w, rhs, preferred_element_type=F32)
        for k in range(2):
            g = g0 + k
            cg = g * A_GROUP_W
            s = s2[:, k * A_GROUP_W:(k + 1) * A_GROUP_W] + self.bst[:, g:g + 1]
            u = self.load(self.o_u + cg, self.o_u + cg + A_GROUP_W).astype(F32)
            za = self.load(self.o_za + cg, self.o_za + cg + A_GROUP_W).astype(F32)
            self.store(cg, (u * s * _silu(za)).astype(BF16))

    def scores(self, p0):
        kv_head = (p0 * HEADS_PER_VREG) // Q_PER_KV
        kprev = self.kprev()
        q4 = jnp.concatenate(
            [self._rope(self.load(self.o_q + p * LANES, self.o_q + (p + 1) * LANES)
                        .astype(F32), query=True).astype(BF16)
             for p in range(p0, p0 + COLS_PER_KV)], axis=0)
        probs = [[] for _ in range(COLS_PER_KV)]
        for o in range(HEADS_PER_VREG):
            e = kv_head * HEADS_PER_VREG + o
            kband = jnp.concatenate([kprev[e], self.ke[e]], axis=0)
            s4 = lax.dot_general(q4, kband, (((1,), (1,)), ((), ())),
                                 preferred_element_type=F32)
            for a in range(COLS_PER_KV):
                s = s4[a * CHUNK:(a + 1) * CHUNK]
                h = (p0 + a) * HEADS_PER_VREG + o
                s_prev = s[:, :CHUNK]
                if self.has_prev is not True:
                    s_prev = jnp.where(self.has_prev, s_prev, -jnp.inf)
                comb = jnp.where(self.causal, s[:, CHUNK:], s_prev)
                sink = self.sink_ref[h] * LOG2E
                m = jnp.maximum(jnp.max(comb, axis=-1, keepdims=True), sink)
                pexp = jnp.exp2(comb - m)
                denom = jnp.sum(pexp, axis=-1, keepdims=True) + jnp.exp2(sink - m)
                pn = (pexp * (1.0 / denom)).astype(BF16)
                zero = jnp.zeros_like(pn)
                probs[a].append(jnp.where(self.causal, zero, pn))
                probs[a].append(jnp.where(self.causal, pn, zero))
        self.probs[p0] = jnp.concatenate(
            [jnp.concatenate(pa, axis=1) for pa in probs], axis=0)

    def values(self, p0):
        kv_head = (p0 * HEADS_PER_VREG) // Q_PER_KV
        vprev = self.vprev()
        vband = []
        for o in range(HEADS_PER_VREG):
            e = kv_head * HEADS_PER_VREG + o
            vband.append(vprev[e])
            vband.append(self.ve[e])
        out4 = jnp.dot(self.probs.pop(p0), jnp.concatenate(vband, axis=0),
                       preferred_element_type=F32)
        for a in range(COLS_PER_KV):
            p = p0 + a
            zb = self.load(self.o_zb + p * LANES, self.o_zb + (p + 1) * LANES).astype(F32)
            self.store(self.d_a + p * LANES,
                       (out4[a * CHUNK:(a + 1) * CHUNK] * _silu(zb)).astype(BF16))


def _layer_kernel(sink_ref, x_ref, moda_ref, modc_ref, modf_ref, ng_ref, fg_ref,
                  cos_ref, sin_ref, lng_ref, lnb_ref, ws_ref, bst_ref, win32_ref, wout32_ref,
                  o_ref, win_scr, wout_scr, proj0_scr, proj1_scr, y_scr, xprev_scr, h_scr,
                  kprev_scr, vprev_scr,
                  *, d_model, d_in, d_a, d_b, d_kv, blocks_per_batch, n_blk):
    i = pl.program_id(0)

    @pl.when(i < W_TILES)
    def _():
        r_in = pl.multiple_of(i * win32_ref.shape[0], win32_ref.shape[0])
        win_scr[pl.ds(r_in, win32_ref.shape[0]), :] = win32_ref[...].astype(BF16)
        r_out = pl.multiple_of(i * wout32_ref.shape[0], wout32_ref.shape[0])
        wout_scr[pl.ds(r_out, wout32_ref.shape[0]), :] = wout32_ref[...].astype(BF16)

    step = i - W_TILES

    args = (sink_ref, x_ref, moda_ref, modc_ref, modf_ref, ng_ref, fg_ref, cos_ref, sin_ref,
            lng_ref, lnb_ref, ws_ref, bst_ref, win_scr, wout_scr, o_ref)
    scr = (y_scr, xprev_scr, h_scr, kprev_scr, vprev_scr)
    dims = dict(d_model=d_model, d_in=d_in, d_a=d_a, d_b=d_b, d_kv=d_kv,
                blocks_per_batch=blocks_per_batch, n_blk=n_blk)
    bufs = ((proj0_scr, proj1_scr), (proj1_scr, proj0_scr))
    inner = jnp.logical_and(step > 0, step < n_blk)

    @pl.when(step == 0)
    def _():
        kprev_scr[...] = jnp.zeros_like(kprev_scr)
        vprev_scr[...] = jnp.zeros_like(vprev_scr)
        _layer_step(step, *args, *bufs[0], *scr, **dims, do_bc=False)

    @pl.when(jnp.logical_and(inner, step % 2 == 0))
    def _():
        _layer_step(step, *args, *bufs[0], *scr, **dims)

    @pl.when(jnp.logical_and(inner, step % 2 == 1))
    def _():
        _layer_step(step, *args, *bufs[1], *scr, **dims)

    @pl.when(step == n_blk)
    def _():
        _layer_step(step, *args, *bufs[n_blk % 2], *scr, **dims, do_a=False)


def _layer_step(step, sink_ref, x_ref, moda_ref, modc_ref, modf_ref, ng_ref, fg_ref,
                cos_ref, sin_ref, lng_ref, lnb_ref, ws_ref, bst_ref, win_ref, wout_ref,
                o_ref, proj_w, proj_r, y_scr, xprev_scr, h_scr, kprev_scr, vprev_scr,
                *, d_model, d_in, d_a, d_b, d_kv, blocks_per_batch, n_blk,
                do_a=True, do_bc=True):
    blk_c = jnp.clip(step - 1, 0, n_blk - 1)
    first_in_batch = (blk_c % blocks_per_batch) == 0

    if do_a:
        x = x_ref[...]
        ms = jnp.mean(x * x, axis=-1, keepdims=True)
        row_a = pl.ds(jnp.minimum(step, n_blk - 1) // blocks_per_batch, 1)
        shift = moda_ref[row_a, 0:d_model]
        gain = ng_ref[...] * (1.0 + moda_ref[row_a, d_model:2 * d_model])
        h_scr[...] = (x * lax.rsqrt(ms + NORM_EPS) * gain + shift).astype(BF16)

    lng = lng_ref[...]
    lnb = lnb_ref[...]
    bst = bst_ref[...]
    n_sub = T_BLK // CHUNK
    blocks = []
    for sb in range(n_sub):
        r0 = sb * CHUNK

        def load(c0, c1, r0=r0):
            return proj_r[r0:r0 + CHUNK, c0:c1]

        def store(c0, val, r0=r0):
            y_scr[r0:r0 + CHUNK, c0:c0 + val.shape[1]] = val

        if sb == 0:
            kprev = lambda: [kprev_scr[e] for e in range(N_EXP)]
            vprev = lambda: [vprev_scr[e] for e in range(N_EXP)]
            has_prev = jnp.logical_not(first_in_batch)
        else:
            kprev = lambda b=blocks[sb - 1]: b.ke
            vprev = lambda b=blocks[sb - 1]: b.ve
            has_prev = True
        blocks.append(_MixerBlock(
            load, store, cos_ref[r0:r0 + CHUNK, :], sin_ref[r0:r0 + CHUNK, :], lng, lnb,
            ws_ref, bst, sink_ref, kprev, vprev, has_prev, d_a=d_a, d_b=d_b, d_kv=d_kv))

    def proj_chunk(n0):
        acc = jnp.dot(h_scr[...], win_ref[:, n0:n0 + TN_DOT], preferred_element_type=F32)
        proj_w[:, n0:n0 + TN_DOT] = acc.astype(BF16)

    chunks = list(range(0, d_in, TN_DOT))
    next_chunk = [0]

    def emit_chunks(n):
        for _ in range(n):
            if do_a and next_chunk[0] < len(chunks):
                proj_chunk(chunks[next_chunk[0]])
                next_chunk[0] += 1

    if not do_bc:
        emit_chunks(len(chunks))
        xprev_scr[...] = x
        return

    n_pairs = d_b // LANES
    slots = [(b, p0) for b in blocks for p0 in range(0, n_pairs, COLS_PER_KV)]
    blocks[0].prep()
    for g0 in range(0, A_GROUPS, 2):
        blocks[0].group_pair(g0)
    slots[0][0].scores(slots[0][1])
    for b in blocks[1:]:
        b.prep()
    late_groups = [(b, g0) for b in blocks[1:] for g0 in range(0, A_GROUPS, 2)]
    first_late = len(slots) - len(late_groups)
    for j, (b, p0) in enumerate(slots):
        emit_chunks(1)
        if j + 1 < len(slots):
            slots[j + 1][0].scores(slots[j + 1][1])
        if j >= first_late:
            bg, g0 = late_groups[j - first_late]
            bg.group_pair(g0)
        b.values(p0)
    for e in range(N_EXP):
        kprev_scr[e] = blocks[-1].ke[e]
        vprev_scr[e] = blocks[-1].ve[e]
    emit_chunks(len(chunks) - next_chunk[0] - 2)

    row_c = pl.ds(blk_c // blocks_per_batch, 1)
    gate = modc_ref[row_c, 2 * d_model:3 * d_model]
    ssq = jnp.zeros((T_BLK, 1), F32)
    for n0 in range(0, d_model, TN_DOT):
        sl = slice(n0, n0 + TN_DOT)
        acc = jnp.dot(y_scr[...], wout_ref[:, sl], preferred_element_type=F32)
        xr = xprev_scr[:, sl] + gate[:, sl] * acc
        ssq = ssq + jnp.sum(xr * xr, axis=-1, keepdims=True)
        o_ref[:, sl] = xr
    emit_chunks(len(chunks))
    inv = lax.rsqrt(ssq * (1.0 / d_model) + NORM_EPS)
    shift_f = modf_ref[row_c, 0:d_model]
    scale_f = modf_ref[row_c, d_model:2 * d_model]
    o_ref[...] = o_ref[...] * inv * (fg_ref[...] * (1.0 + scale_f)) + shift_f

    if do_a:
        xprev_scr[...] = x


def _layer(x2, sinks, mod, mod_f, norm_g, final_g, cos_t, sin_t, ln_g, ln_b, w_sp, b_sp_t,
           w_in, w_out, seq, d_a, d_b, d_kv):
    rows, d_model = x2.shape
    d_in = w_in.shape[-1]
    d_mix = d_a + d_b
    n_blk = rows // T_BLK
    bpb = seq // T_BLK
    assert d_model % W_TILES == 0 and d_mix % W_TILES == 0

    def blk_a(i):
        return jnp.clip(i - W_TILES, 0, n_blk - 1)

    def blk_c(i):
        return jnp.clip(i - W_TILES - 1, 0, n_blk - 1)

    def w_tile(i):
        return (jnp.minimum(i, W_TILES - 1), 0)

    const2 = lambda i: (0, 0)
    kern = functools.partial(_layer_kernel, d_model=d_model, d_in=d_in, d_a=d_a, d_b=d_b,
                             d_kv=d_kv, blocks_per_batch=bpb, n_blk=n_blk)
    return pl.pallas_call(
        kern,
        grid=(W_TILES + n_blk + 1,),
        in_specs=[
            pl.BlockSpec(memory_space=pltpu.SMEM),
            pl.BlockSpec((T_BLK, d_model), lambda i: (blk_a(i), 0)),
            pl.BlockSpec(mod.shape, const2),
            pl.BlockSpec(mod.shape, const2),
            pl.BlockSpec(mod_f.shape, const2),
            pl.BlockSpec((1, d_model), const2),
            pl.BlockSpec((1, d_model), const2),
            pl.BlockSpec((T_BLK, LANES), lambda i: (blk_c(i) % bpb, 0)),
            pl.BlockSpec((T_BLK, LANES), lambda i: (blk_c(i) % bpb, 0)),
            pl.BlockSpec((1, d_a), const2),
            pl.BlockSpec((1, d_a), const2),
            pl.BlockSpec((A_GROUPS, CHUNK, CHUNK), lambda i: (0, 0, 0)),
            pl.BlockSpec((CHUNK, A_GROUPS), const2),
            pl.BlockSpec((d_model // W_TILES, d_in), w_tile),
            pl.BlockSpec((d_mix // W_TILES, d_model), w_tile),
        ],
        out_specs=pl.BlockSpec((T_BLK, d_model), lambda i: (blk_c(i), 0)),
        out_shape=jax.ShapeDtypeStruct((rows, d_model), F32),
        scratch_shapes=[
            pltpu.VMEM((d_model, d_in), BF16),
            pltpu.VMEM((d_mix, d_model), BF16),
            pltpu.VMEM((T_BLK, d_in), BF16),
            pltpu.VMEM((T_BLK, d_in), BF16),
            pltpu.VMEM((T_BLK, d_mix), BF16),
            pltpu.VMEM((T_BLK, d_model), F32),
            pltpu.VMEM((T_BLK, d_model), BF16),
            pltpu.VMEM((N_EXP, CHUNK, LANES), BF16),
            pltpu.VMEM((N_EXP, CHUNK, LANES), BF16),
        ],
        compiler_params=pltpu.CompilerParams(
            dimension_semantics=("arbitrary",), vmem_limit_bytes=VMEM_LIMIT_LAYER),
        name="layer",
    )(sinks, x2, mod, mod, mod_f, norm_g.reshape(1, d_model), final_g.reshape(1, d_model),
      cos_t, sin_t, ln_g, ln_b, w_sp, b_sp_t, w_in, w_out)


def _rope_tables(seq):
    half = HEAD_DIM // 2
    inv_freq = ROPE_THETA ** (-jnp.arange(0, HEAD_DIM, 2, dtype=F32) / HEAD_DIM)
    ang = jnp.arange(seq, dtype=F32)[:, None] * inv_freq[None, :]
    cos = jnp.cos(ang)
    sin = jnp.sin(ang)
    cos_t = jnp.tile(cos, (1, LANES // half))
    sin_t = jnp.tile(jnp.concatenate([-sin, sin], axis=1), (1, HEADS_PER_VREG))
    return cos_t, sin_t


def kernel(x, c, w_ada, b_ada, norm_g, w_in, ln_v_g, ln_v_b, w_spatial, b_spatial, sinks,
           w_out, w_ada_final, b_ada_final, final_norm_g):
    bsz, seq, d_model = x.shape
    assert w_ada.shape[0] == 1, "single-layer stack"
    d_a = ln_v_g.shape[-1]
    d_mix = w_out.shape[-2]
    d_b = d_mix - d_a
    d_kv = N_KV_HEADS * HEAD_DIM
    d_in = w_in.shape[-1]
    assert d_in == 3 * d_a + 2 * d_b + 2 * d_kv
    assert d_a == A_GROUPS * A_GROUP_W and d_b == N_KV_HEADS * Q_PER_KV * HEAD_DIM
    assert seq % T_BLK == 0 and T_BLK % CHUNK == 0

    x2 = x.reshape(bsz * seq, d_model)
    mod, mod_f = _ada_mod(c, w_ada, b_ada, w_ada_final, b_ada_final)
    cos_t, sin_t = _rope_tables(seq)
    out = _layer(x2, sinks.reshape(-1), mod, mod_f, norm_g, final_norm_g, cos_t, sin_t,
                 ln_v_g.reshape(1, d_a), ln_v_b.reshape(1, d_a),
                 w_spatial.reshape(A_GROUPS, CHUNK, CHUNK),
                 b_spatial.reshape(A_GROUPS, CHUNK).T,
                 w_in.reshape(d_model, d_in), w_out.reshape(d_mix, d_model),
                 seq, d_a, d_b, d_kv)
    return out.reshape(bsz, seq, d_model)
```

```python
import functools

import jax
import jax.numpy as jnp
from jax import lax
from jax.experimental import pallas as pl
from jax.experimental.pallas import tpu as pltpu

F32 = jnp.float32
BF16 = jnp.bfloat16

CHUNK = 128
A_GROUPS = 8
A_GROUP_W = 128
HEAD_DIM = 64
N_KV_HEADS = 4
Q_PER_KV = 4
ROPE_THETA = 10000.0
NORM_EPS = 1e-5
LOG2E = 1.4426950408889634

LANES = 128
HEADS_PER_VREG = LANES // HEAD_DIM
N_EXP = N_KV_HEADS * HEADS_PER_VREG
COLS_PER_KV = Q_PER_KV // HEADS_PER_VREG

T_BLK = 256
TN_DOT = 512
W_TILES = 16
TN_ADA = 1024
VMEM_LIMIT_ADA = 40 * 1024 * 1024
VMEM_LIMIT_LAYER = 63 * 1024 * 1024


def _silu(z):
    hz = 0.5 * z
    return hz + hz * jnp.tanh(hz)


def _ada_kernel(c_ref, wa_top, wa_bot, wf_top, wf_bot, ba_ref, bf_ref, oa_ref, of_ref,
                *, n_a_tiles):
    j = pl.program_id(0)
    c = c_ref[...]
    ca = (c * (1.0 / (1.0 + jnp.exp(-c)))).astype(BF16)
    half = ca.shape[1] // 2

    def mod(w_top, w_bot, b_ref, o_ref):
        acc = jnp.dot(ca[:, :half], w_top[...].astype(BF16), preferred_element_type=F32)
        acc = acc + jnp.dot(ca[:, half:], w_bot[...].astype(BF16),
                            preferred_element_type=F32)
        o_ref[...] = acc + b_ref[...]

    @pl.when(j < n_a_tiles)
    def _():
        mod(wa_top, wa_bot, ba_ref, oa_ref)

    @pl.when(j >= n_a_tiles)
    def _():
        mod(wf_top, wf_bot, bf_ref, of_ref)


def _ada_mod(c, w_a, b_a, w_f, b_f):
    bsz, d = c.shape
    n_a, n_f = w_a.shape[-1], w_f.shape[-1]
    w_a = w_a.reshape(d, n_a)
    w_f = w_f.reshape(d, n_f)
    ta, tf = n_a // TN_ADA, n_f // TN_ADA
    half = d // 2
    a_tile = lambda j: jnp.minimum(j, ta - 1)
    f_tile = lambda j: jnp.maximum(j - ta, 0)
    return pl.pallas_call(
        functools.partial(_ada_kernel, n_a_tiles=ta),
        grid=(ta + tf,),
        in_specs=[
            pl.BlockSpec((bsz, d), lambda j: (0, 0)),
            pl.BlockSpec((half, TN_ADA), lambda j: (0, a_tile(j))),
            pl.BlockSpec((half, TN_ADA), lambda j: (1, a_tile(j))),
            pl.BlockSpec((half, TN_ADA), lambda j: (0, f_tile(j))),
            pl.BlockSpec((half, TN_ADA), lambda j: (1, f_tile(j))),
            pl.BlockSpec((1, TN_ADA), lambda j: (0, a_tile(j))),
            pl.BlockSpec((1, TN_ADA), lambda j: (0, f_tile(j))),
        ],
        out_specs=[
            pl.BlockSpec((bsz, TN_ADA), lambda j: (0, a_tile(j))),
            pl.BlockSpec((bsz, TN_ADA), lambda j: (0, f_tile(j))),
        ],
        out_shape=[jax.ShapeDtypeStruct((bsz, n_a), F32),
                   jax.ShapeDtypeStruct((bsz, n_f), F32)],
        compiler_params=pltpu.CompilerParams(
            dimension_semantics=("arbitrary",), vmem_limit_bytes=VMEM_LIMIT_ADA),
        name="ada_mod",
    )(c, w_a, w_a, w_f, w_f, b_a.reshape(1, n_a), b_f.reshape(1, n_f))


class _MixerBlock:
    def __init__(self, load, store, cos, sin, lng, lnb, ws_ref, bst, sink_ref, kprev, vprev,
                 has_prev, *, d_a, d_b, d_kv):
        self.load, self.store = load, store
        self.cos, self.sin, self.lng, self.lnb = cos, sin, lng, lnb
        q_scale = LOG2E * HEAD_DIM ** -0.5
        self.cos_q, self.sin_q = cos * q_scale, sin * q_scale
        self.ws_ref, self.bst, self.sink_ref = ws_ref, bst, sink_ref
        self.kprev, self.vprev, self.has_prev = kprev, vprev, has_prev
        self.d_a, self.d_b, self.d_kv = d_a, d_b, d_kv
        self.o_u, self.o_v, self.o_za = 0, d_a, 2 * d_a
        self.o_q = 3 * d_a
        self.o_k = self.o_q + d_b
        self.o_vv = self.o_k + d_kv
        self.o_zb = self.o_vv + d_kv
        row = lax.broadcasted_iota(jnp.int32, (CHUNK, CHUNK), 0)
        col = lax.broadcasted_iota(jnp.int32, (CHUNK, CHUNK), 1)
        self.col = col
        self.causal = col <= row
        self.first_half = (col & (HEAD_DIM - 1)) < (HEAD_DIM // 2)
        self.probs = {}

    def _rope(self, xv, query=False):
        cos, sin = (self.cos_q, self.sin_q) if query else (self.cos, self.sin)
        rot = jnp.where(self.first_half,
                        pltpu.roll(xv, LANES - HEAD_DIM // 2, 1),
                        pltpu.roll(xv, HEAD_DIM // 2, 1))
        return xv * cos + rot * sin

    def prep(self):
        va = self.load(self.o_v, self.o_v + self.d_a).astype(F32)
        mu = jnp.mean(va, axis=-1, keepdims=True)
        vc = va - mu
        var = jnp.mean(vc * vc, axis=-1, keepdims=True)
        self.vn = (vc * lax.rsqrt(var + NORM_EPS) * self.lng + self.lnb).astype(BF16)
        self.tril = self.causal.astype(F32)
        low_half = self.col < HEAD_DIM
        self.ke = [None] * N_EXP
        self.ve = [None] * N_EXP
        for c in range(self.d_kv // LANES):
            kc = self._rope(self.load(self.o_k + c * LANES, self.o_k + (c + 1) * LANES)
                            .astype(F32))
            vcol = self.load(self.o_vv + c * LANES, self.o_vv + (c + 1) * LANES).astype(F32)
            kc_sw = pltpu.roll(kc, HEAD_DIM, 1)
            vcol_sw = pltpu.roll(vcol, HEAD_DIM, 1)
            for j in range(HEADS_PER_VREG):
                kv_head = c * HEADS_PER_VREG + j
                for o in range(HEADS_PER_VREG):
                    mask = low_half if o == 0 else jnp.logical_not(low_half)
                    e = kv_head * HEADS_PER_VREG + o
                    self.ke[e] = jnp.where(mask, kc if o == j else kc_sw, 0.0).astype(BF16)
                    self.ve[e] = jnp.where(mask, vcol if o == j else vcol_sw, 0.0).astype(BF16)

    def group_pair(self, g0):
        c0 = g0 * A_GROUP_W
        w = jnp.concatenate([(self.ws_ref[g0] * self.tril).astype(BF16),
                             (self.ws_ref[g0 + 1] * self.tril).astype(BF16)], axis=1)
        zero = jnp.zeros((CHUNK, A_GROUP_W), BF16)
        rhs = jnp.concatenate(
            [jnp.concatenate([self.vn[:, c0:c0 + A_GROUP_W], zero], axis=1),
             jnp.concatenate([zero, self.vn[:, c0 + A_GROUP_W:c0 + 2 * A_GROUP_W]], axis=1)],
            axis=0)
        s2 = jnp.dot(w, rhs, preferred_element_type=F32)
        for k in range(2):
            g = g0 + k
            cg = g * A_GROUP_W
            s = s2[:, k * A_GROUP_W:(k + 1) * A_GROUP_W] + self.bst[:, g:g + 1]
            u = self.load(self.o_u + cg, self.o_u + cg + A_GROUP_W).astype(F32)
            za = self.load(self.o_za + cg, self.o_za + cg + A_GROUP_W).astype(F32)
            self.store(cg, (u * s * _silu(za)).astype(BF16))

    def scores(self, p0):
        kv_head = (p0 * HEADS_PER_VREG) // Q_PER_KV
        kprev = self.kprev()
        q4 = jnp.concatenate(
            [self._rope(self.load(self.o_q + p * LANES, self.o_q + (p + 1) * LANES)
                        .astype(F32), query=True).astype(BF16)
             for p in range(p0, p0 + COLS_PER_KV)], axis=0)
        probs = [[] for _ in range(COLS_PER_KV)]
        for o in range(HEADS_PER_VREG):
            e = kv_head * HEADS_PER_VREG + o
            kband = jnp.concatenate([kprev[e], self.ke[e]], axis=0)
            s4 = lax.dot_general(q4, kband, (((1,), (1,)), ((), ())),
                                 preferred_element_type=F32)
            for a in range(COLS_PER_KV):
                s = s4[a * CHUNK:(a + 1) * CHUNK]
                h = (p0 + a) * HEADS_PER_VREG + o
                s_prev = s[:, :CHUNK]
                if self.has_prev is not True:
                    s_prev = jnp.where(self.has_prev, s_prev, -jnp.inf)
                comb = jnp.where(self.causal, s[:, CHUNK:], s_prev)
                sink = self.sink_ref[h] * LOG2E
                m = jnp.maximum(jnp.max(comb, axis=-1, keepdims=True), sink)
                pexp = jnp.exp2(comb - m)
                denom = jnp.sum(pexp, axis=-1, keepdims=True) + jnp.exp2(sink - m)
                pn = (pexp * (1.0 / denom)).astype(BF16)
                zero = jnp.zeros_like(pn)
                probs[a].append(jnp.where(self.causal, zero, pn))
                probs[a].append(jnp.where(self.causal, pn, zero))
        self.probs[p0] = jnp.concatenate(
            [jnp.concatenate(pa, axis=1) for pa in probs], axis=0)

    def values(self, p0):
        kv_head = (p0 * HEADS_PER_VREG) // Q_PER_KV
        vprev = self.vprev()
        vband = []
        for o in range(HEADS_PER_VREG):
            e = kv_head * HEADS_PER_VREG + o
            vband.append(vprev[e])
            vband.append(self.ve[e])
        out4 = jnp.dot(self.probs.pop(p0), jnp.concatenate(vband, axis=0),
                       preferred_element_type=F32)
        for a in range(COLS_PER_KV):
            p = p0 + a
            zb = self.load(self.o_zb + p * LANES, self.o_zb + (p + 1) * LANES).astype(F32)
            self.store(self.d_a + p * LANES,
                       (out4[a * CHUNK:(a + 1) * CHUNK] * _silu(zb)).astype(BF16))


def _layer_kernel(sink_ref, x_ref, moda_ref, modc_ref, modf_ref, ng_ref, fg_ref,
                  cos_ref, sin_ref, lng_ref, lnb_ref, ws_ref, bst_ref, win32_ref, wout32_ref,
                  o_ref, win_scr, wout_scr, proj0_scr, proj1_scr, y_scr, xprev_scr, h_scr,
                  kprev_scr, vprev_scr,
                  *, d_model, d_in, d_a, d_b, d_kv, blocks_per_batch, n_blk):
    i = pl.program_id(0)

    @pl.when(i < W_TILES)
    def _():
        r_in = pl.multiple_of(i * win32_ref.shape[0], win32_ref.shape[0])
        win_scr[pl.ds(r_in, win32_ref.shape[0]), :] = win32_ref[...].astype(BF16)
        r_out = pl.multiple_of(i * wout32_ref.shape[0], wout32_ref.shape[0])
        wout_scr[pl.ds(r_out, wout32_ref.shape[0]), :] = wout32_ref[...].astype(BF16)

    step = i - W_TILES

    args = (sink_ref, x_ref, moda_ref, modc_ref, modf_ref, ng_ref, fg_ref, cos_ref, sin_ref,
            lng_ref, lnb_ref, ws_ref, bst_ref, win_scr, wout_scr, o_ref)
    scr = (y_scr, xprev_scr, h_scr, kprev_scr, vprev_scr)
    dims = dict(d_model=d_model, d_in=d_in, d_a=d_a, d_b=d_b, d_kv=d_kv,
                blocks_per_batch=blocks_per_batch, n_blk=n_blk)
    bufs = ((proj0_scr, proj1_scr), (proj1_scr, proj0_scr))
    inner = jnp.logical_and(step > 0, step < n_blk)

    @pl.when(step == 0)
    def _():
        kprev_scr[...] = jnp.zeros_like(kprev_scr)
        vprev_scr[...] = jnp.zeros_like(vprev_scr)
        _layer_step(step, *args, *bufs[0], *scr, **dims, do_bc=False)

    @pl.when(jnp.logical_and(inner, step % 2 == 0))
    def _():
        _layer_step(step, *args, *bufs[0], *scr, **dims)

    @pl.when(jnp.logical_and(inner, step % 2 == 1))
    def _():
        _layer_step(step, *args, *bufs[1], *scr, **dims)

    @pl.when(step == n_blk)
    def _():
        _layer_step(step, *args, *bufs[n_blk % 2], *scr, **dims, do_a=False)


def _layer_step(step, sink_ref, x_ref, moda_ref, modc_ref, modf_ref, ng_ref, fg_ref,
                cos_ref, sin_ref, lng_ref, lnb_ref, ws_ref, bst_ref, win_ref, wout_ref,
                o_ref, proj_w, proj_r, y_scr, xprev_scr, h_scr, kprev_scr, vprev_scr,
                *, d_model, d_in, d_a, d_b, d_kv, blocks_per_batch, n_blk,
                do_a=True, do_bc=True):
    blk_c = jnp.clip(step - 1, 0, n_blk - 1)
    first_in_batch = (blk_c % blocks_per_batch) == 0

    if do_a:
        x = x_ref[...]
        ms = jnp.mean(x * x, axis=-1, keepdims=True)
        row_a = pl.ds(jnp.minimum(step, n_blk - 1) // blocks_per_batch, 1)
        shift = moda_ref[row_a, 0:d_model]
        gain = ng_ref[...] * (1.0 + moda_ref[row_a, d_model:2 * d_model])
        h_scr[...] = (x * lax.rsqrt(ms + NORM_EPS) * gain + shift).astype(BF16)

    lng = lng_ref[...]
    lnb = lnb_ref[...]
    bst = bst_ref[...]
    n_sub = T_BLK // CHUNK
    blocks = []
    for sb in range(n_sub):
        r0 = sb * CHUNK

        def load(c0, c1, r0=r0):
            return proj_r[r0:r0 + CHUNK, c0:c1]

        def store(c0, val, r0=r0):
            y_scr[r0:r0 + CHUNK, c0:c0 + val.shape[1]] = val

        if sb == 0:
            kprev = lambda: [kprev_scr[e] for e in range(N_EXP)]
            vprev = lambda: [vprev_scr[e] for e in range(N_EXP)]
            has_prev = jnp.logical_not(first_in_batch)
        else:
            kprev = lambda b=blocks[sb - 1]: b.ke
            vprev = lambda b=blocks[sb - 1]: b.ve
            has_prev = True
        blocks.append(_MixerBlock(
            load, store, cos_ref[r0:r0 + CHUNK, :], sin_ref[r0:r0 + CHUNK, :], lng, lnb,
            ws_ref, bst, sink_ref, kprev, vprev, has_prev, d_a=d_a, d_b=d_b, d_kv=d_kv))

    def proj_chunk(n0):
        acc = jnp.dot(h_scr[...], win_ref[:, n0:n0 + TN_DOT], preferred_element_type=F32)
        proj_w[:, n0:n0 + TN_DOT] = acc.astype(BF16)

    chunks = list(range(0, d_in, TN_DOT))
    next_chunk = [0]

    def emit_chunks(n):
        for _ in range(n):
            if do_a and next_chunk[0] < len(chunks):
                proj_chunk(chunks[next_chunk[0]])
                next_chunk[0] += 1

    if not do_bc:
        emit_chunks(len(chunks))
        xprev_scr[...] = x
        return

    for b in blocks:
        b.prep()
    n_pairs = d_b // LANES
    slots = [(b, p0) for b in blocks for p0 in range(0, n_pairs, COLS_PER_KV)]
    group_pairs = [(b, g0) for b in blocks for g0 in range(0, A_GROUPS, 2)]
    pairs_per_slot = -(-len(group_pairs) // len(slots))
    slots[0][0].scores(slots[0][1])
    for j, (b, p0) in enumerate(slots):
        emit_chunks(1)
        b.values(p0)
        if j + 1 < len(slots):
            slots[j + 1][0].scores(slots[j + 1][1])
        for bg, g0 in group_pairs[j * pairs_per_slot:(j + 1) * pairs_per_slot]:
            bg.group_pair(g0)
    for e in range(N_EXP):
        kprev_scr[e] = blocks[-1].ke[e]
        vprev_scr[e] = blocks[-1].ve[e]
    emit_chunks(len(chunks) - next_chunk[0] - 2)

    row_c = pl.ds(blk_c // blocks_per_batch, 1)
    gate = modc_ref[row_c, 2 * d_model:3 * d_model]
    ssq = jnp.zeros((T_BLK, 1), F32)
    for n0 in range(0, d_model, TN_DOT):
        sl = slice(n0, n0 + TN_DOT)
        acc = jnp.dot(y_scr[...], wout_ref[:, sl], preferred_element_type=F32)
        xr = xprev_scr[:, sl] + gate[:, sl] * acc
        ssq = ssq + jnp.sum(xr * xr, axis=-1, keepdims=True)
        o_ref[:, sl] = xr
    emit_chunks(len(chunks))
    inv = lax.rsqrt(ssq * (1.0 / d_model) + NORM_EPS)
    shift_f = modf_ref[row_c, 0:d_model]
    scale_f = modf_ref[row_c, d_model:2 * d_model]
    o_ref[...] = o_ref[...] * inv * (fg_ref[...] * (1.0 + scale_f)) + shift_f

    if do_a:
        xprev_scr[...] = x


def _layer(x2, sinks, mod, mod_f, norm_g, final_g, cos_t, sin_t, ln_g, ln_b, w_sp, b_sp_t,
           w_in, w_out, seq, d_a, d_b, d_kv):
    rows, d_model = x2.shape
    d_in = w_in.shape[-1]
    d_mix = d_a + d_b
    n_blk = rows // T_BLK
    bpb = seq // T_BLK
    assert d_model % W_TILES == 0 and d_mix % W_TILES == 0

    def blk_a(i):
        return jnp.clip(i - W_TILES, 0, n_blk - 1)

    def blk_c(i):
        return jnp.clip(i - W_TILES - 1, 0, n_blk - 1)

    def w_tile(i):
        return (jnp.minimum(i, W_TILES - 1), 0)

    const2 = lambda i: (0, 0)
    kern = functools.partial(_layer_kernel, d_model=d_model, d_in=d_in, d_a=d_a, d_b=d_b,
                             d_kv=d_kv, blocks_per_batch=bpb, n_blk=n_blk)
    return pl.pallas_call(
        kern,
        grid=(W_TILES + n_blk + 1,),
        in_specs=[
            pl.BlockSpec(memory_space=pltpu.SMEM),
            pl.BlockSpec((T_BLK, d_model), lambda i: (blk_a(i), 0)),
            pl.BlockSpec(mod.shape, const2),
            pl.BlockSpec(mod.shape, const2),
            pl.BlockSpec(mod_f.shape, const2),
            pl.BlockSpec((1, d_model), const2),
            pl.BlockSpec((1, d_model), const2),
            pl.BlockSpec((T_BLK, LANES), lambda i: (blk_c(i) % bpb, 0)),
            pl.BlockSpec((T_BLK, LANES), lambda i: (blk_c(i) % bpb, 0)),
            pl.BlockSpec((1, d_a), const2),
            pl.BlockSpec((1, d_a), const2),
            pl.BlockSpec((A_GROUPS, CHUNK, CHUNK), lambda i: (0, 0, 0)),
            pl.BlockSpec((CHUNK, A_GROUPS), const2),
            pl.BlockSpec((d_model // W_TILES, d_in), w_tile),
            pl.BlockSpec((d_mix // W_TILES, d_model), w_tile),
        ],
        out_specs=pl.BlockSpec((T_BLK, d_model), lambda i: (blk_c(i), 0)),
        out_shape=jax.ShapeDtypeStruct((rows, d_model), F32),
        scratch_shapes=[
            pltpu.VMEM((d_model, d_in), BF16),
            pltpu.VMEM((d_mix, d_model), BF16),
            pltpu.VMEM((T_BLK, d_in), BF16),
            pltpu.VMEM((T_BLK, d_in), BF16),
            pltpu.VMEM((T_BLK, d_mix), BF16),
            pltpu.VMEM((T_BLK, d_model), F32),
            pltpu.VMEM((T_BLK, d_model), BF16),
            pltpu.VMEM((N_EXP, CHUNK, LANES), BF16),
            pltpu.VMEM((N_EXP, CHUNK, LANES), BF16),
        ],
        compiler_params=pltpu.CompilerParams(
            dimension_semantics=("arbitrary",), vmem_limit_bytes=VMEM_LIMIT_LAYER),
        name="layer",
    )(sinks, x2, mod, mod, mod_f, norm_g.reshape(1, d_model), final_g.reshape(1, d_model),
      cos_t, sin_t, ln_g, ln_b, w_sp, b_sp_t, w_in, w_out)


def _rope_tables(seq):
    half = HEAD_DIM // 2
    inv_freq = ROPE_THETA ** (-jnp.arange(0, HEAD_DIM, 2, dtype=F32) / HEAD_DIM)
    ang = jnp.arange(seq, dtype=F32)[:, None] * inv_freq[None, :]
    cos = jnp.cos(ang)
    sin = jnp.sin(ang)
    cos_t = jnp.tile(cos, (1, LANES // half))
    sin_t = jnp.tile(jnp.concatenate([-sin, sin], axis=1), (1, HEADS_PER_VREG))
    return cos_t, sin_t


def kernel(x, c, w_ada, b_ada, norm_g, w_in, ln_v_g, ln_v_b, w_spatial, b_spatial, sinks,
           w_out, w_ada_final, b_ada_final, final_norm_g):
    bsz, seq, d_model = x.shape
    assert w_ada.shape[0] == 1, "single-layer stack"
    d_a = ln_v_g.shape[-1]
    d_mix = w_out.shape[-2]
    d_b = d_mix - d_a
    d_kv = N_KV_HEADS * HEAD_DIM
    d_in = w_in.shape[-1]
    assert d_in == 3 * d_a + 2 * d_b + 2 * d_kv
    assert d_a == A_GROUPS * A_GROUP_W and d_b == N_KV_HEADS * Q_PER_KV * HEAD_DIM
    assert seq % T_BLK == 0 and T_BLK % CHUNK == 0

    x2 = x.reshape(bsz * seq, d_model)
    mod, mod_f = _ada_mod(c, w_ada, b_ada, w_ada_final, b_ada_final)
    cos_t, sin_t = _rope_tables(seq)
    out = _layer(x2, sinks.reshape(-1), mod, mod_f, norm_g, final_norm_g, cos_t, sin_t,
                 ln_v_g.reshape(1, d_a), ln_v_b.reshape(1, d_a),
                 w_spatial.reshape(A_GROUPS, CHUNK, CHUNK),
                 b_spatial.reshape(A_GROUPS, CHUNK).T,
                 w_in.reshape(d_model, d_in), w_out.reshape(d_mix, d_model),
                 seq, d_a, d_b, d_kv)
    return out.reshape(bsz, seq, d_model)
```

```python
import functools

import jax
import jax.numpy as jnp
from jax import lax
from jax.experimental import pallas as pl
from jax.experimental.pallas import tpu as pltpu

F32 = jnp.float32
BF16 = jnp.bfloat16

CHUNK = 128
A_GROUPS = 8
A_GROUP_W = 128
HEAD_DIM = 64
N_KV_HEADS = 4
Q_PER_KV = 4
ROPE_THETA = 10000.0
NORM_EPS = 1e-5
LOG2E = 1.4426950408889634

LANES = 128
HEADS_PER_VREG = LANES // HEAD_DIM
N_EXP = N_KV_HEADS * HEADS_PER_VREG
COLS_PER_KV = Q_PER_KV // HEADS_PER_VREG

T_BLK = 256
TN_DOT = 512
W_TILES = 16
TN_ADA = 1024

MIB = 1024 * 1024
V7X_VMEM_BYTES = 64 * MIB
VMEM_LIMIT_ADA = 40 * MIB
VMEM_LIMIT_LAYER = V7X_VMEM_BYTES - MIB


def _silu(z):
    hz = 0.5 * z
    return hz + hz * jnp.tanh(hz)


def _ada_kernel(c_ref, wa_top, wa_bot, wf_top, wf_bot, ba_ref, bf_ref, oa_ref, of_ref,
                *, n_a_tiles):
    j = pl.program_id(0)
    c = c_ref[...]
    ca = (c * (1.0 / (1.0 + jnp.exp(-c)))).astype(BF16)
    half = ca.shape[1] // 2

    def mod(w_top, w_bot, b_ref, o_ref):
        acc = jnp.dot(ca[:, :half], w_top[...].astype(BF16), preferred_element_type=F32)
        acc = acc + jnp.dot(ca[:, half:], w_bot[...].astype(BF16),
                            preferred_element_type=F32)
        o_ref[...] = acc + b_ref[...]

    @pl.when(j < n_a_tiles)
    def _():
        mod(wa_top, wa_bot, ba_ref, oa_ref)

    @pl.when(j >= n_a_tiles)
    def _():
        mod(wf_top, wf_bot, bf_ref, of_ref)


def _ada_mod(c, w_a, b_a, w_f, b_f):
    bsz, d = c.shape
    n_a, n_f = w_a.shape[-1], w_f.shape[-1]
    w_a = w_a.reshape(d, n_a)
    w_f = w_f.reshape(d, n_f)
    ta, tf = n_a // TN_ADA, n_f // TN_ADA
    half = d // 2
    a_tile = lambda j: jnp.minimum(j, ta - 1)
    f_tile = lambda j: jnp.maximum(j - ta, 0)
    return pl.pallas_call(
        functools.partial(_ada_kernel, n_a_tiles=ta),
        grid=(ta + tf,),
        in_specs=[
            pl.BlockSpec((bsz, d), lambda j: (0, 0)),
            pl.BlockSpec((half, TN_ADA), lambda j: (0, a_tile(j))),
            pl.BlockSpec((half, TN_ADA), lambda j: (1, a_tile(j))),
            pl.BlockSpec((half, TN_ADA), lambda j: (0, f_tile(j))),
            pl.BlockSpec((half, TN_ADA), lambda j: (1, f_tile(j))),
            pl.BlockSpec((1, TN_ADA), lambda j: (0, a_tile(j))),
            pl.BlockSpec((1, TN_ADA), lambda j: (0, f_tile(j))),
        ],
        out_specs=[
            pl.BlockSpec((bsz, TN_ADA), lambda j: (0, a_tile(j))),
            pl.BlockSpec((bsz, TN_ADA), lambda j: (0, f_tile(j))),
        ],
        out_shape=[jax.ShapeDtypeStruct((bsz, n_a), F32),
                   jax.ShapeDtypeStruct((bsz, n_f), F32)],
        compiler_params=pltpu.CompilerParams(
            dimension_semantics=("arbitrary",), vmem_limit_bytes=VMEM_LIMIT_ADA),
        name="ada_mod",
    )(c, w_a, w_a, w_f, w_f, b_a.reshape(1, n_a), b_f.reshape(1, n_f))


class _MixerBlock:
    def __init__(self, load, store, cos, sin, lng, lnb, ws_ref, bst, sink_ref, kprev, vprev,
                 has_prev, *, d_a, d_b, d_kv):
        self.load, self.store = load, store
        self.cos, self.sin, self.lng, self.lnb = cos, sin, lng, lnb
        q_scale = LOG2E * HEAD_DIM ** -0.5
        self.cos_q, self.sin_q = cos * q_scale, sin * q_scale
        self.ws_ref, self.bst, self.sink_ref = ws_ref, bst, sink_ref
        self.kprev, self.vprev, self.has_prev = kprev, vprev, has_prev
        self.d_a, self.d_b, self.d_kv = d_a, d_b, d_kv
        self.o_u, self.o_v, self.o_za = 0, d_a, 2 * d_a
        self.o_q = 3 * d_a
        self.o_k = self.o_q + d_b
        self.o_vv = self.o_k + d_kv
        self.o_zb = self.o_vv + d_kv
        row = lax.broadcasted_iota(jnp.int32, (CHUNK, CHUNK), 0)
        col = lax.broadcasted_iota(jnp.int32, (CHUNK, CHUNK), 1)
        self.col = col
        self.causal = col <= row
        self.first_half = (col & (HEAD_DIM - 1)) < (HEAD_DIM // 2)
        self.probs = {}

    def _rope(self, xv, query=False):
        cos, sin = (self.cos_q, self.sin_q) if query else (self.cos, self.sin)
        rot = jnp.where(self.first_half,
                        pltpu.roll(xv, LANES - HEAD_DIM // 2, 1),
                        pltpu.roll(xv, HEAD_DIM // 2, 1))
        return xv * cos + rot * sin

    def prep(self):
        va = self.load(self.o_v, self.o_v + self.d_a).astype(F32)
        mu = jnp.mean(va, axis=-1, keepdims=True)
        vc = va - mu
        var = jnp.mean(vc * vc, axis=-1, keepdims=True)
        self.vn = (vc * lax.rsqrt(var + NORM_EPS) * self.lng + self.lnb).astype(BF16)
        self.tril = self.causal.astype(F32)
        low_half = self.col < HEAD_DIM
        self.ke = [None] * N_EXP
        self.ve = [None] * N_EXP
        for c in range(self.d_kv // LANES):
            kc = self._rope(self.load(self.o_k + c * LANES, self.o_k + (c + 1) * LANES)
                            .astype(F32))
            vcol = self.load(self.o_vv + c * LANES, self.o_vv + (c + 1) * LANES).astype(F32)
            kc_sw = pltpu.roll(kc, HEAD_DIM, 1)
            vcol_sw = pltpu.roll(vcol, HEAD_DIM, 1)
            for j in range(HEADS_PER_VREG):
                kv_head = c * HEADS_PER_VREG + j
                for o in range(HEADS_PER_VREG):
                    mask = low_half if o == 0 else jnp.logical_not(low_half)
                    e = kv_head * HEADS_PER_VREG + o
                    self.ke[e] = jnp.where(mask, kc if o == j else kc_sw, 0.0).astype(BF16)
                    self.ve[e] = jnp.where(mask, vcol if o == j else vcol_sw, 0.0).astype(BF16)

    def group_pair(self, g0):
        c0 = g0 * A_GROUP_W
        w = jnp.concatenate([(self.ws_ref[g0] * self.tril).astype(BF16),
                             (self.ws_ref[g0 + 1] * self.tril).astype(BF16)], axis=1)
        zero = jnp.zeros((CHUNK, A_GROUP_W), BF16)
        rhs = jnp.concatenate(
            [jnp.concatenate([self.vn[:, c0:c0 + A_GROUP_W], zero], axis=1),
             jnp.concatenate([zero, self.vn[:, c0 + A_GROUP_W:c0 + 2 * A_GROUP_W]], axis=1)],
            axis=0)
        s2 = jnp.dot(w, rhs, preferred_element_type=F32)
        for k in range(2):
            g = g0 + k
            cg = g * A_GROUP_W
            s = s2[:, k * A_GROUP_W:(k + 1) * A_GROUP_W] + self.bst[:, g:g + 1]
            u = self.load(self.o_u + cg, self.o_u + cg + A_GROUP_W).astype(F32)
            za = self.load(self.o_za + cg, self.o_za + cg + A_GROUP_W).astype(F32)
            self.store(cg, (u * s * _silu(za)).astype(BF16))

    def scores(self, p0):
        kv_head = (p0 * HEADS_PER_VREG) // Q_PER_KV
        kprev = self.kprev()
        q4 = jnp.concatenate(
            [self._rope(self.load(self.o_q + p * LANES, self.o_q + (p + 1) * LANES)
                        .astype(F32), query=True).astype(BF16)
             for p in range(p0, p0 + COLS_PER_KV)], axis=0)
        probs = [[] for _ in range(COLS_PER_KV)]
        for o in range(HEADS_PER_VREG):
            e = kv_head * HEADS_PER_VREG + o
            kband = jnp.concatenate([kprev[e], self.ke[e]], axis=0)
            s4 = lax.dot_general(q4, kband, (((1,), (1,)), ((), ())),
                                 preferred_element_type=F32)
            for a in range(COLS_PER_KV):
                s = s4[a * CHUNK:(a + 1) * CHUNK]
                h = (p0 + a) * HEADS_PER_VREG + o
                s_prev = s[:, :CHUNK]
                if self.has_prev is not True:
                    s_prev = jnp.where(self.has_prev, s_prev, -jnp.inf)
                comb = jnp.where(self.causal, s[:, CHUNK:], s_prev)
                sink = self.sink_ref[h] * LOG2E
                m = jnp.maximum(jnp.max(comb, axis=-1, keepdims=True), sink)
                pexp = jnp.exp2(comb - m)
                denom = jnp.sum(pexp, axis=-1, keepdims=True) + jnp.exp2(sink - m)
                pn = (pexp * (1.0 / denom)).astype(BF16)
                zero = jnp.zeros_like(pn)
                probs[a].append(jnp.where(self.causal, zero, pn))
                probs[a].append(jnp.where(self.causal, pn, zero))
        self.probs[p0] = jnp.concatenate(
            [jnp.concatenate(pa, axis=1) for pa in probs], axis=0)

    def values(self, p0):
        kv_head = (p0 * HEADS_PER_VREG) // Q_PER_KV
        vprev = self.vprev()
        vband = []
        for o in range(HEADS_PER_VREG):
            e = kv_head * HEADS_PER_VREG + o
            vband.append(vprev[e])
            vband.append(self.ve[e])
        out4 = jnp.dot(self.probs.pop(p0), jnp.concatenate(vband, axis=0),
                       preferred_element_type=F32)
        for a in range(COLS_PER_KV):
            p = p0 + a
            zb = self.load(self.o_zb + p * LANES, self.o_zb + (p + 1) * LANES).astype(F32)
            self.store(self.d_a + p * LANES,
                       (out4[a * CHUNK:(a + 1) * CHUNK] * _silu(zb)).astype(BF16))


def _layer_kernel(sink_ref, x_ref, moda_ref, modc_ref, modf_ref, ng_ref, fg_ref,
                  cos_ref, sin_ref, lng_ref, lnb_ref, ws_ref, bst_ref, win32_ref, wout32_ref,
                  o_ref, win_scr, wout_scr, proj0_scr, proj1_scr, y_scr, xprev_scr, h_scr,
                  kprev_scr, vprev_scr,
                  *, d_model, d_in, d_a, d_b, d_kv, blocks_per_batch, n_blk):
    i = pl.program_id(0)

    @pl.when(i < W_TILES)
    def _():
        r_in = pl.multiple_of(i * win32_ref.shape[0], win32_ref.shape[0])
        win_scr[pl.ds(r_in, win32_ref.shape[0]), :] = win32_ref[...].astype(BF16)
        r_out = pl.multiple_of(i * wout32_ref.shape[0], wout32_ref.shape[0])
        wout_scr[pl.ds(r_out, wout32_ref.shape[0]), :] = wout32_ref[...].astype(BF16)

    step = i - W_TILES

    args = (sink_ref, x_ref, moda_ref, modc_ref, modf_ref, ng_ref, fg_ref, cos_ref, sin_ref,
            lng_ref, lnb_ref, ws_ref, bst_ref, win_scr, wout_scr, o_ref)
    scr = (y_scr, xprev_scr, h_scr, kprev_scr, vprev_scr)
    dims = dict(d_model=d_model, d_in=d_in, d_a=d_a, d_b=d_b, d_kv=d_kv,
                blocks_per_batch=blocks_per_batch, n_blk=n_blk)
    bufs = ((proj0_scr, proj1_scr), (proj1_scr, proj0_scr))
    inner = jnp.logical_and(step > 0, step < n_blk)

    @pl.when(step == 0)
    def _():
        kprev_scr[...] = jnp.zeros_like(kprev_scr)
        vprev_scr[...] = jnp.zeros_like(vprev_scr)
        _layer_step(step, *args, *bufs[0], *scr, **dims, do_bc=False)

    @pl.when(jnp.logical_and(inner, step % 2 == 0))
    def _():
        _layer_step(step, *args, *bufs[0], *scr, **dims)

    @pl.when(jnp.logical_and(inner, step % 2 == 1))
    def _():
        _layer_step(step, *args, *bufs[1], *scr, **dims)

    @pl.when(step == n_blk)
    def _():
        _layer_step(step, *args, *bufs[n_blk % 2], *scr, **dims, do_a=False)


def _layer_step(step, sink_ref, x_ref, moda_ref, modc_ref, modf_ref, ng_ref, fg_ref,
                cos_ref, sin_ref, lng_ref, lnb_ref, ws_ref, bst_ref, win_ref, wout_ref,
                o_ref, proj_w, proj_r, y_scr, xprev_scr, h_scr, kprev_scr, vprev_scr,
                *, d_model, d_in, d_a, d_b, d_kv, blocks_per_batch, n_blk,
                do_a=True, do_bc=True):
    blk_c = jnp.clip(step - 1, 0, n_blk - 1)
    first_in_batch = (blk_c % blocks_per_batch) == 0

    if do_a:
        x = x_ref[...]
        ms = jnp.mean(x * x, axis=-1, keepdims=True)
        row_a = pl.ds(jnp.minimum(step, n_blk - 1) // blocks_per_batch, 1)
        shift = moda_ref[row_a, 0:d_model]
        gain = ng_ref[...] * (1.0 + moda_ref[row_a, d_model:2 * d_model])
        h_scr[...] = (x * lax.rsqrt(ms + NORM_EPS) * gain + shift).astype(BF16)

    lng = lng_ref[...]
    lnb = lnb_ref[...]
    bst = bst_ref[...]
    n_sub = T_BLK // CHUNK
    blocks = []
    for sb in range(n_sub):
        r0 = sb * CHUNK

        def load(c0, c1, r0=r0):
            return proj_r[r0:r0 + CHUNK, c0:c1]

        def store(c0, val, r0=r0):
            y_scr[r0:r0 + CHUNK, c0:c0 + val.shape[1]] = val

        if sb == 0:
            kprev = lambda: [kprev_scr[e] for e in range(N_EXP)]
            vprev = lambda: [vprev_scr[e] for e in range(N_EXP)]
            has_prev = jnp.logical_not(first_in_batch)
        else:
            kprev = lambda b=blocks[sb - 1]: b.ke
            vprev = lambda b=blocks[sb - 1]: b.ve
            has_prev = True
        blocks.append(_MixerBlock(
            load, store, cos_ref[r0:r0 + CHUNK, :], sin_ref[r0:r0 + CHUNK, :], lng, lnb,
            ws_ref, bst, sink_ref, kprev, vprev, has_prev, d_a=d_a, d_b=d_b, d_kv=d_kv))

    def proj_chunk(n0):
        acc = jnp.dot(h_scr[...], win_ref[:, n0:n0 + TN_DOT], preferred_element_type=F32)
        proj_w[:, n0:n0 + TN_DOT] = acc.astype(BF16)

    chunks = list(range(0, d_in, TN_DOT))
    next_chunk = [0]

    def emit_chunks(n):
        for _ in range(n):
            if do_a and next_chunk[0] < len(chunks):
                proj_chunk(chunks[next_chunk[0]])
                next_chunk[0] += 1

    if not do_bc:
        emit_chunks(len(chunks))
        xprev_scr[...] = x
        return

    for b in blocks:
        b.prep()
    n_pairs = d_b // LANES
    slots = [(b, p0) for b in blocks for p0 in range(0, n_pairs, COLS_PER_KV)]
    group_pairs = [(b, g0) for b in blocks for g0 in range(0, A_GROUPS, 2)]
    pairs_per_slot = -(-len(group_pairs) // len(slots))
    slots[0][0].scores(slots[0][1])
    for j, (b, p0) in enumerate(slots):
        emit_chunks(1)
        b.values(p0)
        if j + 1 < len(slots):
            slots[j + 1][0].scores(slots[j + 1][1])
        for bg, g0 in group_pairs[j * pairs_per_slot:(j + 1) * pairs_per_slot]:
            bg.group_pair(g0)
    for e in range(N_EXP):
        kprev_scr[e] = blocks[-1].ke[e]
        vprev_scr[e] = blocks[-1].ve[e]
    emit_chunks(len(chunks) - next_chunk[0] - 2)

    row_c = pl.ds(blk_c // blocks_per_batch, 1)
    gate = modc_ref[row_c, 2 * d_model:3 * d_model]
    ssq = jnp.zeros((T_BLK, 1), F32)
    for n0 in range(0, d_model, TN_DOT):
        sl = slice(n0, n0 + TN_DOT)
        acc = jnp.dot(y_scr[...], wout_ref[:, sl], preferred_element_type=F32)
        xr = xprev_scr[:, sl] + gate[:, sl] * acc
        ssq = ssq + jnp.sum(xr * xr, axis=-1, keepdims=True)
        o_ref[:, sl] = xr
    emit_chunks(len(chunks))
    inv = lax.rsqrt(ssq * (1.0 / d_model) + NORM_EPS)
    shift_f = modf_ref[row_c, 0:d_model]
    scale_f = modf_ref[row_c, d_model:2 * d_model]
    o_ref[...] = o_ref[...] * inv * (fg_ref[...] * (1.0 + scale_f)) + shift_f

    if do_a:
        xprev_scr[...] = x


def _layer(x2, sinks, mod, mod_f, norm_g, final_g, cos_t, sin_t, ln_g, ln_b, w_sp, b_sp_t,
           w_in, w_out, seq, d_a, d_b, d_kv):
    rows, d_model = x2.shape
    d_in = w_in.shape[-1]
    d_mix = d_a + d_b
    n_blk = rows // T_BLK
    bpb = seq // T_BLK
    assert d_model % W_TILES == 0 and d_mix % W_TILES == 0

    def blk_a(i):
        return jnp.clip(i - W_TILES, 0, n_blk - 1)

    def blk_c(i):
        return jnp.clip(i - W_TILES - 1, 0, n_blk - 1)

    def w_tile(i):
        return (jnp.minimum(i, W_TILES - 1), 0)

    const2 = lambda i: (0, 0)
    kern = functools.partial(_layer_kernel, d_model=d_model, d_in=d_in, d_a=d_a, d_b=d_b,
                             d_kv=d_kv, blocks_per_batch=bpb, n_blk=n_blk)
    return pl.pallas_call(
        kern,
        grid=(W_TILES + n_blk + 1,),
        in_specs=[
            pl.BlockSpec(memory_space=pltpu.SMEM),
            pl.BlockSpec((T_BLK, d_model), lambda i: (blk_a(i), 0)),
            pl.BlockSpec(mod.shape, const2),
            pl.BlockSpec(mod.shape, const2),
            pl.BlockSpec(mod_f.shape, const2),
            pl.BlockSpec((1, d_model), const2),
            pl.BlockSpec((1, d_model), const2),
            pl.BlockSpec((T_BLK, LANES), lambda i: (blk_c(i) % bpb, 0)),
            pl.BlockSpec((T_BLK, LANES), lambda i: (blk_c(i) % bpb, 0)),
            pl.BlockSpec((1, d_a), const2),
            pl.BlockSpec((1, d_a), const2),
            pl.BlockSpec((A_GROUPS, CHUNK, CHUNK), lambda i: (0, 0, 0)),
            pl.BlockSpec((CHUNK, A_GROUPS), const2),
            pl.BlockSpec((d_model // W_TILES, d_in), w_tile),
            pl.BlockSpec((d_mix // W_TILES, d_model), w_tile),
        ],
        out_specs=pl.BlockSpec((T_BLK, d_model), lambda i: (blk_c(i), 0)),
        out_shape=jax.ShapeDtypeStruct((rows, d_model), F32),
        scratch_shapes=[
            pltpu.VMEM((d_model, d_in), BF16),
            pltpu.VMEM((d_mix, d_model), BF16),
            pltpu.VMEM((T_BLK, d_in), BF16),
            pltpu.VMEM((T_BLK, d_in), BF16),
            pltpu.VMEM((T_BLK, d_mix), BF16),
            pltpu.VMEM((T_BLK, d_model), F32),
            pltpu.VMEM((T_BLK, d_model), BF16),
            pltpu.VMEM((N_EXP, CHUNK, LANES), BF16),
            pltpu.VMEM((N_EXP, CHUNK, LANES), BF16),
        ],
        compiler_params=pltpu.CompilerParams(
            dimension_semantics=("arbitrary",), vmem_limit_bytes=VMEM_LIMIT_LAYER),
        name="layer",
    )(sinks, x2, mod, mod, mod_f, norm_g.reshape(1, d_model), final_g.reshape(1, d_model),
      cos_t, sin_t, ln_g, ln_b, w_sp, b_sp_t, w_in, w_out)


def _rope_tables(seq):
    half = HEAD_DIM // 2
    inv_freq = ROPE_THETA ** (-jnp.arange(0, HEAD_DIM, 2, dtype=F32) / HEAD_DIM)
    ang = jnp.arange(seq, dtype=F32)[:, None] * inv_freq[None, :]
    cos = jnp.cos(ang)
    sin = jnp.sin(ang)
    cos_t = jnp.tile(cos, (1, LANES // half))
    sin_t = jnp.tile(jnp.concatenate([-sin, sin], axis=1), (1, HEADS_PER_VREG))
    return cos_t, sin_t


def kernel(x, c, w_ada, b_ada, norm_g, w_in, ln_v_g, ln_v_b, w_spatial, b_spatial, sinks,
           w_out, w_ada_final, b_ada_final, final_norm_g):
    bsz, seq, d_model = x.shape
    assert w_ada.shape[0] == 1, "single-layer stack"
    d_a = ln_v_g.shape[-1]
    d_mix = w_out.shape[-2]
    d_b = d_mix - d_a
    d_kv = N_KV_HEADS * HEAD_DIM
    d_in = w_in.shape[-1]
    assert d_in == 3 * d_a + 2 * d_b + 2 * d_kv
    assert d_a == A_GROUPS * A_GROUP_W and d_b == N_KV_HEADS * Q_PER_KV * HEAD_DIM
    assert seq % T_BLK == 0 and T_BLK % CHUNK == 0

    x2 = x.reshape(bsz * seq, d_model)
    mod, mod_f = _ada_mod(c, w_ada, b_ada, w_ada_final, b_ada_final)
    cos_t, sin_t = _rope_tables(seq)
    out = _layer(x2, sinks.reshape(-1), mod, mod_f, norm_g, final_norm_g, cos_t, sin_t,
                 ln_v_g.reshape(1, d_a), ln_v_b.reshape(1, d_a),
                 w_spatial.reshape(A_GROUPS, CHUNK, CHUNK),
                 b_spatial.reshape(A_GROUPS, CHUNK).T,
                 w_in.reshape(d_model, d_in), w_out.reshape(d_mix, d_model),
                 seq, d_a, d_b, d_kv)
    return out.reshape(bsz, seq, d_model)
```

```python
import functools

import jax
import jax.numpy as jnp
from jax import lax
from jax.experimental import pallas as pl
from jax.experimental.pallas import tpu as pltpu

F32 = jnp.float32
BF16 = jnp.bfloat16

CHUNK = 128
A_GROUPS = 8
A_GROUP_W = 128
HEAD_DIM = 64
N_KV_HEADS = 4
Q_PER_KV = 4
ROPE_THETA = 10000.0
NORM_EPS = 1e-5
LOG2E = 1.4426950408889634

LANES = 128
HEADS_PER_VREG = LANES // HEAD_DIM
N_EXP = N_KV_HEADS * HEADS_PER_VREG
COLS_PER_KV = Q_PER_KV // HEADS_PER_VREG

T_BLK = 256
TN_DOT = 512
W_TILES = 16
TN_ADA = 1024

MIB = 1024 * 1024
V7X_VMEM_BYTES = 64 * MIB
VMEM_LIMIT_ADA = 40 * MIB
VMEM_LIMIT_LAYER = V7X_VMEM_BYTES - MIB


def _silu(z):
    hz = 0.5 * z
    return hz + hz * jnp.tanh(hz)


def _ada_kernel(c_ref, wa_top, wa_bot, wf_top, wf_bot, ba_ref, bf_ref, oa_ref, of_ref,
                *, n_a_tiles):
    j = pl.program_id(0)
    c = c_ref[...]
    ca = (c * (1.0 / (1.0 + jnp.exp(-c)))).astype(BF16)
    half = ca.shape[1] // 2

    def mod(w_top, w_bot, b_ref, o_ref):
        acc = jnp.dot(ca[:, :half], w_top[...].astype(BF16), preferred_element_type=F32)
        acc = acc + jnp.dot(ca[:, half:], w_bot[...].astype(BF16),
                            preferred_element_type=F32)
        o_ref[...] = acc + b_ref[...]

    @pl.when(j < n_a_tiles)
    def _():
        mod(wa_top, wa_bot, ba_ref, oa_ref)

    @pl.when(j >= n_a_tiles)
    def _():
        mod(wf_top, wf_bot, bf_ref, of_ref)


def _ada_mod(c, w_a, b_a, w_f, b_f):
    bsz, d = c.shape
    n_a, n_f = w_a.shape[-1], w_f.shape[-1]
    w_a = w_a.reshape(d, n_a)
    w_f = w_f.reshape(d, n_f)
    ta, tf = n_a // TN_ADA, n_f // TN_ADA
    half = d // 2
    a_tile = lambda j: jnp.minimum(j, ta - 1)
    f_tile = lambda j: jnp.maximum(j - ta, 0)
    return pl.pallas_call(
        functools.partial(_ada_kernel, n_a_tiles=ta),
        grid=(ta + tf,),
        in_specs=[
            pl.BlockSpec((bsz, d), lambda j: (0, 0)),
            pl.BlockSpec((half, TN_ADA), lambda j: (0, a_tile(j))),
            pl.BlockSpec((half, TN_ADA), lambda j: (1, a_tile(j))),
            pl.BlockSpec((half, TN_ADA), lambda j: (0, f_tile(j))),
            pl.BlockSpec((half, TN_ADA), lambda j: (1, f_tile(j))),
            pl.BlockSpec((1, TN_ADA), lambda j: (0, a_tile(j))),
            pl.BlockSpec((1, TN_ADA), lambda j: (0, f_tile(j))),
        ],
        out_specs=[
            pl.BlockSpec((bsz, TN_ADA), lambda j: (0, a_tile(j))),
            pl.BlockSpec((bsz, TN_ADA), lambda j: (0, f_tile(j))),
        ],
        out_shape=[jax.ShapeDtypeStruct((bsz, n_a), F32),
                   jax.ShapeDtypeStruct((bsz, n_f), F32)],
        compiler_params=pltpu.CompilerParams(
            dimension_semantics=("arbitrary",), vmem_limit_bytes=VMEM_LIMIT_ADA),
        name="ada_mod",
    )(c, w_a, w_a, w_f, w_f, b_a.reshape(1, n_a), b_f.reshape(1, n_f))


class _MixerBlock:
    def __init__(self, load, store, cos, sin, lng, lnb, ws_ref, bst, sink_ref, kprev, vprev,
                 has_prev, *, d_a, d_b, d_kv):
        self.load, self.store = load, store
        self.cos, self.sin, self.lng, self.lnb = cos, sin, lng, lnb
        q_scale = LOG2E * HEAD_DIM ** -0.5
        self.cos_q, self.sin_q = cos * q_scale, sin * q_scale
        self.ws_ref, self.bst, self.sink_ref = ws_ref, bst, sink_ref
        self.kprev, self.vprev, self.has_prev = kprev, vprev, has_prev
        self.d_a, self.d_b, self.d_kv = d_a, d_b, d_kv
        self.o_u, self.o_v, self.o_za = 0, d_a, 2 * d_a
        self.o_q = 3 * d_a
        self.o_k = self.o_q + d_b
        self.o_vv = self.o_k + d_kv
        self.o_zb = self.o_vv + d_kv
        row = lax.broadcasted_iota(jnp.int32, (CHUNK, CHUNK), 0)
        col = lax.broadcasted_iota(jnp.int32, (CHUNK, CHUNK), 1)
        self.col = col
        self.causal = col <= row
        self.first_half = (col & (HEAD_DIM - 1)) < (HEAD_DIM // 2)
        self.probs = {}

    def _rope(self, xv, query=False):
        cos, sin = (self.cos_q, self.sin_q) if query else (self.cos, self.sin)
        rot = jnp.where(self.first_half,
                        pltpu.roll(xv, LANES - HEAD_DIM // 2, 1),
                        pltpu.roll(xv, HEAD_DIM // 2, 1))
        return xv * cos + rot * sin

    def prep(self):
        va = self.load(self.o_v, self.o_v + self.d_a).astype(F32)
        mu = jnp.mean(va, axis=-1, keepdims=True)
        vc = va - mu
        var = jnp.mean(vc * vc, axis=-1, keepdims=True)
        self.vn = (vc * lax.rsqrt(var + NORM_EPS) * self.lng + self.lnb).astype(BF16)
        self.tril = self.causal.astype(F32)
        low_half = self.col < HEAD_DIM
        self.ke = [None] * N_EXP
        self.ve = [None] * N_EXP
        for c in range(self.d_kv // LANES):
            kc = self._rope(self.load(self.o_k + c * LANES, self.o_k + (c + 1) * LANES)
                            .astype(F32))
            vcol = self.load(self.o_vv + c * LANES, self.o_vv + (c + 1) * LANES).astype(F32)
            kc_sw = pltpu.roll(kc, HEAD_DIM, 1)
            vcol_sw = pltpu.roll(vcol, HEAD_DIM, 1)
            for j in range(HEADS_PER_VREG):
                kv_head = c * HEADS_PER_VREG + j
                for o in range(HEADS_PER_VREG):
                    mask = low_half if o == 0 else jnp.logical_not(low_half)
                    e = kv_head * HEADS_PER_VREG + o
                    self.ke[e] = jnp.where(mask, kc if o == j else kc_sw, 0.0).astype(BF16)
                    self.ve[e] = jnp.where(mask, vcol if o == j else vcol_sw, 0.0).astype(BF16)

    def group_pair(self, g0):
        c0 = g0 * A_GROUP_W
        w = jnp.concatenate([(self.ws_ref[g0] * self.tril).astype(BF16),
                             (self.ws_ref[g0 + 1] * self.tril).astype(BF16)], axis=1)
        zero = jnp.zeros((CHUNK, A_GROUP_W), BF16)
        rhs = jnp.concatenate(
            [jnp.concatenate([self.vn[:, c0:c0 + A_GROUP_W], zero], axis=1),
             jnp.concatenate([zero, self.vn[:, c0 + A_GROUP_W:c0 + 2 * A_GROUP_W]], axis=1)],
            axis=0)
        s2 = jnp.dot(w, rhs, preferred_element_type=F32)
        for k in range(2):
            g = g0 + k
            cg = g * A_GROUP_W
            s = s2[:, k * A_GROUP_W:(k + 1) * A_GROUP_W] + self.bst[:, g:g + 1]
            u = self.load(self.o_u + cg, self.o_u + cg + A_GROUP_W).astype(F32)
            za = self.load(self.o_za + cg, self.o_za + cg + A_GROUP_W).astype(F32)
            self.store(cg, (u * s * _silu(za)).astype(BF16))

    def scores(self, p0):
        kv_head = (p0 * HEADS_PER_VREG) // Q_PER_KV
        kprev = self.kprev()
        q4 = jnp.concatenate(
            [self._rope(self.load(self.o_q + p * LANES, self.o_q + (p + 1) * LANES)
                        .astype(F32), query=True).astype(BF16)
             for p in range(p0, p0 + COLS_PER_KV)], axis=0)
        probs = [[] for _ in range(COLS_PER_KV)]
        for o in range(HEADS_PER_VREG):
            e = kv_head * HEADS_PER_VREG + o
            kband = jnp.concatenate([kprev[e], self.ke[e]], axis=0)
            s4 = lax.dot_general(q4, kband, (((1,), (1,)), ((), ())),
                                 preferred_element_type=F32)
            for a in range(COLS_PER_KV):
                s = s4[a * CHUNK:(a + 1) * CHUNK]
                h = (p0 + a) * HEADS_PER_VREG + o
                s_prev = s[:, :CHUNK]
                if self.has_prev is not True:
                    s_prev = jnp.where(self.has_prev, s_prev, -jnp.inf)
                comb = jnp.where(self.causal, s[:, CHUNK:], s_prev)
                sink = self.sink_ref[h] * LOG2E
                m = jnp.maximum(jnp.max(comb, axis=-1, keepdims=True), sink)
                pexp = jnp.exp2(comb - m)
                denom = jnp.sum(pexp, axis=-1, keepdims=True) + jnp.exp2(sink - m)
                pn = (pexp * (1.0 / denom)).astype(BF16)
                zero = jnp.zeros_like(pn)
                probs[a].append(jnp.where(self.causal, zero, pn))
                probs[a].append(jnp.where(self.causal, pn, zero))
        self.probs[p0] = jnp.concatenate(
            [jnp.concatenate(pa, axis=1) for pa in probs], axis=0)

    def values(self, p0):
        kv_head = (p0 * HEADS_PER_VREG) // Q_PER_KV
        vprev = self.vprev()
        vband = []
        for o in range(HEADS_PER_VREG):
            e = kv_head * HEADS_PER_VREG + o
            vband.append(vprev[e])
            vband.append(self.ve[e])
        out4 = jnp.dot(self.probs.pop(p0), jnp.concatenate(vband, axis=0),
                       preferred_element_type=F32)
        for a in range(COLS_PER_KV):
            p = p0 + a
            zb = self.load(self.o_zb + p * LANES, self.o_zb + (p + 1) * LANES).astype(F32)
            self.store(self.d_a + p * LANES,
                       (out4[a * CHUNK:(a + 1) * CHUNK] * _silu(zb)).astype(BF16))


def _layer_kernel(sink_ref, x_ref, moda_ref, modc_ref, modf_ref, ng_ref, fg_ref,
                  cos_ref, sin_ref, lng_ref, lnb_ref, ws_ref, bst_ref, win32_ref, wout32_ref,
                  o_ref, win_scr, wout_scr, proj0_scr, proj1_scr, y_scr, xprev_scr, h_scr,
                  kprev_scr, vprev_scr,
                  *, d_model, d_in, d_a, d_b, d_kv, blocks_per_batch, n_blk):
    i = pl.program_id(0)

    @pl.when(i < W_TILES)
    def _():
        r_in = pl.multiple_of(i * win32_ref.shape[0], win32_ref.shape[0])
        win_scr[pl.ds(r_in, win32_ref.shape[0]), :] = win32_ref[...].astype(BF16)
        r_out = pl.multiple_of(i * wout32_ref.shape[0], wout32_ref.shape[0])
        wout_scr[pl.ds(r_out, wout32_ref.shape[0]), :] = wout32_ref[...].astype(BF16)

    step = i - W_TILES

    args = (sink_ref, x_ref, moda_ref, modc_ref, modf_ref, ng_ref, fg_ref, cos_ref, sin_ref,
            lng_ref, lnb_ref, ws_ref, bst_ref, win_scr, wout_scr, o_ref)
    scr = (y_scr, xprev_scr, h_scr, kprev_scr, vprev_scr)
    dims = dict(d_model=d_model, d_in=d_in, d_a=d_a, d_b=d_b, d_kv=d_kv,
                blocks_per_batch=blocks_per_batch, n_blk=n_blk)
    bufs = ((proj0_scr, proj1_scr), (proj1_scr, proj0_scr))
    inner = jnp.logical_and(step > 0, step < n_blk)

    @pl.when(step == 0)
    def _():
        kprev_scr[...] = jnp.zeros_like(kprev_scr)
        vprev_scr[...] = jnp.zeros_like(vprev_scr)
        _layer_step(step, *args, *bufs[0], *scr, **dims, do_bc=False)

    @pl.when(jnp.logical_and(inner, step % 2 == 0))
    def _():
        _layer_step(step, *args, *bufs[0], *scr, **dims)

    @pl.when(jnp.logical_and(inner, step % 2 == 1))
    def _():
        _layer_step(step, *args, *bufs[1], *scr, **dims)

    @pl.when(step == n_blk)
    def _():
        _layer_step(step, *args, *bufs[n_blk % 2], *scr, **dims, do_a=False)


def _layer_step(step, sink_ref, x_ref, moda_ref, modc_ref, modf_ref, ng_ref, fg_ref,
                cos_ref, sin_ref, lng_ref, lnb_ref, ws_ref, bst_ref, win_ref, wout_ref,
                o_ref, proj_w, proj_r, y_scr, xprev_scr, h_scr, kprev_scr, vprev_scr,
                *, d_model, d_in, d_a, d_b, d_kv, blocks_per_batch, n_blk,
                do_a=True, do_bc=True):
    blk_c = jnp.clip(step - 1, 0, n_blk - 1)
    first_in_batch = (blk_c % blocks_per_batch) == 0

    if do_a:
        x = x_ref[...]
        ms = jnp.mean(x * x, axis=-1, keepdims=True)
        row_a = pl.ds(jnp.minimum(step, n_blk - 1) // blocks_per_batch, 1)
        shift = moda_ref[row_a, 0:d_model]
        gain = ng_ref[...] * (1.0 + moda_ref[row_a, d_model:2 * d_model])
        h_scr[...] = (x * lax.rsqrt(ms + NORM_EPS) * gain + shift).astype(BF16)

    lng = lng_ref[...]
    lnb = lnb_ref[...]
    bst = bst_ref[...]
    n_sub = T_BLK // CHUNK
    blocks = []
    for sb in range(n_sub):
        r0 = sb * CHUNK

        def load(c0, c1, r0=r0):
            return proj_r[r0:r0 + CHUNK, c0:c1]

        def store(c0, val, r0=r0):
            y_scr[r0:r0 + CHUNK, c0:c0 + val.shape[1]] = val

        if sb == 0:
            kprev = lambda: [kprev_scr[e] for e in range(N_EXP)]
            vprev = lambda: [vprev_scr[e] for e in range(N_EXP)]
            has_prev = jnp.logical_not(first_in_batch)
        else:
            kprev = lambda b=blocks[sb - 1]: b.ke
            vprev = lambda b=blocks[sb - 1]: b.ve
            has_prev = True
        blocks.append(_MixerBlock(
            load, store, cos_ref[r0:r0 + CHUNK, :], sin_ref[r0:r0 + CHUNK, :], lng, lnb,
            ws_ref, bst, sink_ref, kprev, vprev, has_prev, d_a=d_a, d_b=d_b, d_kv=d_kv))

    def proj_chunk(n0):
        acc = jnp.dot(h_scr[...], win_ref[:, n0:n0 + TN_DOT], preferred_element_type=F32)
        proj_w[:, n0:n0 + TN_DOT] = acc.astype(BF16)

    chunks = list(range(0, d_in, TN_DOT))
    next_chunk = [0]

    def emit_chunks(n):
        for _ in range(n):
            if do_a and next_chunk[0] < len(chunks):
                proj_chunk(chunks[next_chunk[0]])
                next_chunk[0] += 1

    if not do_bc:
        emit_chunks(len(chunks))
        xprev_scr[...] = x
        return

    for b in blocks:
        b.prep()
    n_pairs = d_b // LANES
    slots = [(b, p0) for b in blocks for p0 in range(0, n_pairs, COLS_PER_KV)]
    group_pairs = [(b, g0) for b in blocks for g0 in range(0, A_GROUPS, 2)]
    pairs_per_slot = -(-len(group_pairs) // len(slots))
    emit_chunks(1)
    slots[0][0].scores(slots[0][1])
    for j, (b, p0) in enumerate(slots):
        emit_chunks(1)
        b.values(p0)
        if j + 1 < len(slots):
            slots[j + 1][0].scores(slots[j + 1][1])
        for bg, g0 in group_pairs[j * pairs_per_slot:(j + 1) * pairs_per_slot]:
            bg.group_pair(g0)
    for e in range(N_EXP):
        kprev_scr[e] = blocks[-1].ke[e]
        vprev_scr[e] = blocks[-1].ve[e]
    emit_chunks(len(chunks) - next_chunk[0] - 2)

    row_c = pl.ds(blk_c // blocks_per_batch, 1)
    gate = modc_ref[row_c, 2 * d_model:3 * d_model]
    ssq = jnp.zeros((T_BLK, 1), F32)
    for n0 in range(0, d_model, TN_DOT):
        sl = slice(n0, n0 + TN_DOT)
        acc = jnp.dot(y_scr[...], wout_ref[:, sl], preferred_element_type=F32)
        xr = xprev_scr[:, sl] + gate[:, sl] * acc
        ssq = ssq + jnp.sum(xr * xr, axis=-1, keepdims=True)
        o_ref[:, sl] = xr
    emit_chunks(len(chunks))
    inv = lax.rsqrt(ssq * (1.0 / d_model) + NORM_EPS)
    shift_f = modf_ref[row_c, 0:d_model]
    scale_f = modf_ref[row_c, d_model:2 * d_model]
    o_ref[...] = o_ref[...] * inv * (fg_ref[...] * (1.0 + scale_f)) + shift_f

    if do_a:
        xprev_scr[...] = x


def _layer(x2, sinks, mod, mod_f, norm_g, final_g, cos_t, sin_t, ln_g, ln_b, w_sp, b_sp_t,
           w_in, w_out, seq, d_a, d_b, d_kv):
    rows, d_model = x2.shape
    d_in = w_in.shape[-1]
    d_mix = d_a + d_b
    n_blk = rows // T_BLK
    bpb = seq // T_BLK
    assert d_model % W_TILES == 0 and d_mix % W_TILES == 0

    def blk_a(i):
        return jnp.clip(i - W_TILES, 0, n_blk - 1)

    def blk_c(i):
        return jnp.clip(i - W_TILES - 1, 0, n_blk - 1)

    def w_tile(i):
        return (jnp.minimum(i, W_TILES - 1), 0)

    const2 = lambda i: (0, 0)
    kern = functools.partial(_layer_kernel, d_model=d_model, d_in=d_in, d_a=d_a, d_b=d_b,
                             d_kv=d_kv, blocks_per_batch=bpb, n_blk=n_blk)
    return pl.pallas_call(
        kern,
        grid=(W_TILES + n_blk + 1,),
        in_specs=[
            pl.BlockSpec(memory_space=pltpu.SMEM),
            pl.BlockSpec((T_BLK, d_model), lambda i: (blk_a(i), 0)),
            pl.BlockSpec(mod.shape, const2),
            pl.BlockSpec(mod.shape, const2),
            pl.BlockSpec(mod_f.shape, const2),
            pl.BlockSpec((1, d_model), const2),
            pl.BlockSpec((1, d_model), const2),
            pl.BlockSpec((T_BLK, LANES), lambda i: (blk_c(i) % bpb, 0)),
            pl.BlockSpec((T_BLK, LANES), lambda i: (blk_c(i) % bpb, 0)),
            pl.BlockSpec((1, d_a), const2),
            pl.BlockSpec((1, d_a), const2),
            pl.BlockSpec((A_GROUPS, CHUNK, CHUNK), lambda i: (0, 0, 0)),
            pl.BlockSpec((CHUNK, A_GROUPS), const2),
            pl.BlockSpec((d_model // W_TILES, d_in), w_tile),
            pl.BlockSpec((d_mix // W_TILES, d_model), w_tile),
        ],
        out_specs=pl.BlockSpec((T_BLK, d_model), lambda i: (blk_c(i), 0)),
        out_shape=jax.ShapeDtypeStruct((rows, d_model), F32),
        scratch_shapes=[
            pltpu.VMEM((d_model, d_in), BF16),
            pltpu.VMEM((d_mix, d_model), BF16),
            pltpu.VMEM((T_BLK, d_in), BF16),
            pltpu.VMEM((T_BLK, d_in), BF16),
            pltpu.VMEM((T_BLK, d_mix), BF16),
            pltpu.VMEM((T_BLK, d_model), F32),
            pltpu.VMEM((T_BLK, d_model), BF16),
            pltpu.VMEM((N_EXP, CHUNK, LANES), BF16),
            pltpu.VMEM((N_EXP, CHUNK, LANES), BF16),
        ],
        compiler_params=pltpu.CompilerParams(
            dimension_semantics=("arbitrary",), vmem_limit_bytes=VMEM_LIMIT_LAYER),
        name="layer",
    )(sinks, x2, mod, mod, mod_f, norm_g.reshape(1, d_model), final_g.reshape(1, d_model),
      cos_t, sin_t, ln_g, ln_b, w_sp, b_sp_t, w_in, w_out)


def _rope_tables(seq):
    half = HEAD_DIM // 2
    inv_freq = ROPE_THETA ** (-jnp.arange(0, HEAD_DIM, 2, dtype=F32) / HEAD_DIM)
    ang = jnp.arange(seq, dtype=F32)[:, None] * inv_freq[None, :]
    cos = jnp.cos(ang)
    sin = jnp.sin(ang)
    cos_t = jnp.tile(cos, (1, LANES // half))
    sin_t = jnp.tile(jnp.concatenate([-sin, sin], axis=1), (1, HEADS_PER_VREG))
    return cos_t, sin_t


def kernel(x, c, w_ada, b_ada, norm_g, w_in, ln_v_g, ln_v_b, w_spatial, b_spatial, sinks,
           w_out, w_ada_final, b_ada_final, final_norm_g):
    bsz, seq, d_model = x.shape
    assert w_ada.shape[0] == 1, "single-layer stack"
    d_a = ln_v_g.shape[-1]
    d_mix = w_out.shape[-2]
    d_b = d_mix - d_a
    d_kv = N_KV_HEADS * HEAD_DIM
    d_in = w_in.shape[-1]
    assert d_in == 3 * d_a + 2 * d_b + 2 * d_kv
    assert d_a == A_GROUPS * A_GROUP_W and d_b == N_KV_HEADS * Q_PER_KV * HEAD_DIM
    assert seq % T_BLK == 0 and T_BLK % CHUNK == 0

    x2 = x.reshape(bsz * seq, d_model)
    mod, mod_f = _ada_mod(c, w_ada, b_ada, w_ada_final, b_ada_final)
    cos_t, sin_t = _rope_tables(seq)
    out = _layer(x2, sinks.reshape(-1), mod, mod_f, norm_g, final_norm_g, cos_t, sin_t,
                 ln_v_g.reshape(1, d_a), ln_v_b.reshape(1, d_a),
                 w_spatial.reshape(A_GROUPS, CHUNK, CHUNK),
                 b_spatial.reshape(A_GROUPS, CHUNK).T,
                 w_in.reshape(d_model, d_in), w_out.reshape(d_mix, d_model),
                 seq, d_a, d_b, d_kv)
    return out.reshape(bsz, seq, d_model)
```

```python
import functools

import jax
import jax.numpy as jnp
from jax import lax
from jax.experimental import pallas as pl
from jax.experimental.pallas import tpu as pltpu

F32 = jnp.float32
BF16 = jnp.bfloat16

CHUNK = 128
A_GROUPS = 8
A_GROUP_W = 128
HEAD_DIM = 64
N_KV_HEADS = 4
Q_PER_KV = 4
ROPE_THETA = 10000.0
NORM_EPS = 1e-5
LOG2E = 1.4426950408889634

LANES = 128
HEADS_PER_VREG = LANES // HEAD_DIM
N_EXP = N_KV_HEADS * HEADS_PER_VREG
COLS_PER_KV = Q_PER_KV // HEADS_PER_VREG

T_BLK = 256
TN_DOT = 512
N_CARRY = 2
W_TILES = 16
TN_ADA = 1024

MIB = 1024 * 1024
V7X_VMEM_BYTES = 64 * MIB
VMEM_LIMIT_ADA = 40 * MIB
VMEM_LIMIT_LAYER = V7X_VMEM_BYTES - MIB


def _silu(z):
    hz = 0.5 * z
    return hz + hz * jnp.tanh(hz)


def _ada_kernel(c_ref, wa_top, wa_bot, wf_top, wf_bot, ba_ref, bf_ref, oa_ref, of_ref,
                *, n_a_tiles):
    j = pl.program_id(0)
    c = c_ref[...]
    ca = (c * (1.0 / (1.0 + jnp.exp(-c)))).astype(BF16)
    half = ca.shape[1] // 2

    def mod(w_top, w_bot, b_ref, o_ref):
        acc = jnp.dot(ca[:, :half], w_top[...].astype(BF16), preferred_element_type=F32)
        acc = acc + jnp.dot(ca[:, half:], w_bot[...].astype(BF16),
                            preferred_element_type=F32)
        o_ref[...] = acc + b_ref[...]

    @pl.when(j < n_a_tiles)
    def _():
        mod(wa_top, wa_bot, ba_ref, oa_ref)

    @pl.when(j >= n_a_tiles)
    def _():
        mod(wf_top, wf_bot, bf_ref, of_ref)


def _ada_mod(c, w_a, b_a, w_f, b_f):
    bsz, d = c.shape
    n_a, n_f = w_a.shape[-1], w_f.shape[-1]
    w_a = w_a.reshape(d, n_a)
    w_f = w_f.reshape(d, n_f)
    ta, tf = n_a // TN_ADA, n_f // TN_ADA
    half = d // 2
    a_tile = lambda j: jnp.minimum(j, ta - 1)
    f_tile = lambda j: jnp.maximum(j - ta, 0)
    return pl.pallas_call(
        functools.partial(_ada_kernel, n_a_tiles=ta),
        grid=(ta + tf,),
        in_specs=[
            pl.BlockSpec((bsz, d), lambda j: (0, 0)),
            pl.BlockSpec((half, TN_ADA), lambda j: (0, a_tile(j))),
            pl.BlockSpec((half, TN_ADA), lambda j: (1, a_tile(j))),
            pl.BlockSpec((half, TN_ADA), lambda j: (0, f_tile(j))),
            pl.BlockSpec((half, TN_ADA), lambda j: (1, f_tile(j))),
            pl.BlockSpec((1, TN_ADA), lambda j: (0, a_tile(j))),
            pl.BlockSpec((1, TN_ADA), lambda j: (0, f_tile(j))),
        ],
        out_specs=[
            pl.BlockSpec((bsz, TN_ADA), lambda j: (0, a_tile(j))),
            pl.BlockSpec((bsz, TN_ADA), lambda j: (0, f_tile(j))),
        ],
        out_shape=[jax.ShapeDtypeStruct((bsz, n_a), F32),
                   jax.ShapeDtypeStruct((bsz, n_f), F32)],
        compiler_params=pltpu.CompilerParams(
            dimension_semantics=("arbitrary",), vmem_limit_bytes=VMEM_LIMIT_ADA),
        name="ada_mod",
    )(c, w_a, w_a, w_f, w_f, b_a.reshape(1, n_a), b_f.reshape(1, n_f))


class _MixerBlock:
    def __init__(self, load, store, cos, sin, lng, lnb, ws_ref, bst, sink_ref, kprev, vprev,
                 has_prev, *, d_a, d_b, d_kv):
        self.load, self.store = load, store
        self.cos, self.sin, self.lng, self.lnb = cos, sin, lng, lnb
        q_scale = LOG2E * HEAD_DIM ** -0.5
        self.cos_q, self.sin_q = cos * q_scale, sin * q_scale
        self.ws_ref, self.bst, self.sink_ref = ws_ref, bst, sink_ref
        self.kprev, self.vprev, self.has_prev = kprev, vprev, has_prev
        self.d_a, self.d_b, self.d_kv = d_a, d_b, d_kv
        self.o_u, self.o_v, self.o_za = 0, d_a, 2 * d_a
        self.o_q = 3 * d_a
        self.o_k = self.o_q + d_b
        self.o_vv = self.o_k + d_kv
        self.o_zb = self.o_vv + d_kv
        row = lax.broadcasted_iota(jnp.int32, (CHUNK, CHUNK), 0)
        col = lax.broadcasted_iota(jnp.int32, (CHUNK, CHUNK), 1)
        self.col = col
        self.causal = col <= row
        self.first_half = (col & (HEAD_DIM - 1)) < (HEAD_DIM // 2)
        self.probs = {}

    def _rope(self, xv, query=False):
        cos, sin = (self.cos_q, self.sin_q) if query else (self.cos, self.sin)
        rot = jnp.where(self.first_half,
                        pltpu.roll(xv, LANES - HEAD_DIM // 2, 1),
                        pltpu.roll(xv, HEAD_DIM // 2, 1))
        return xv * cos + rot * sin

    def prep(self):
        va = self.load(self.o_v, self.o_v + self.d_a).astype(F32)
        mu = jnp.mean(va, axis=-1, keepdims=True)
        vc = va - mu
        var = jnp.mean(vc * vc, axis=-1, keepdims=True)
        self.vn = (vc * lax.rsqrt(var + NORM_EPS) * self.lng + self.lnb).astype(BF16)
        self.tril = self.causal.astype(F32)
        low_half = self.col < HEAD_DIM
        self.ke = [None] * N_EXP
        self.ve = [None] * N_EXP
        for c in range(self.d_kv // LANES):
            kc = self._rope(self.load(self.o_k + c * LANES, self.o_k + (c + 1) * LANES)
                            .astype(F32))
            vcol = self.load(self.o_vv + c * LANES, self.o_vv + (c + 1) * LANES).astype(F32)
            kc_sw = pltpu.roll(kc, HEAD_DIM, 1)
            vcol_sw = pltpu.roll(vcol, HEAD_DIM, 1)
            for j in range(HEADS_PER_VREG):
                kv_head = c * HEADS_PER_VREG + j
                for o in range(HEADS_PER_VREG):
                    mask = low_half if o == 0 else jnp.logical_not(low_half)
                    e = kv_head * HEADS_PER_VREG + o
                    self.ke[e] = jnp.where(mask, kc if o == j else kc_sw, 0.0).astype(BF16)
                    self.ve[e] = jnp.where(mask, vcol if o == j else vcol_sw, 0.0).astype(BF16)

    def group_pair(self, g0):
        c0 = g0 * A_GROUP_W
        w = jnp.concatenate([(self.ws_ref[g0] * self.tril).astype(BF16),
                             (self.ws_ref[g0 + 1] * self.tril).astype(BF16)], axis=1)
        zero = jnp.zeros((CHUNK, A_GROUP_W), BF16)
        rhs = jnp.concatenate(
            [jnp.concatenate([self.vn[:, c0:c0 + A_GROUP_W], zero], axis=1),
             jnp.concatenate([zero, self.vn[:, c0 + A_GROUP_W:c0 + 2 * A_GROUP_W]], axis=1)],
            axis=0)
        s2 = jnp.dot(w, rhs, preferred_element_type=F32)
        for k in range(2):
            g = g0 + k
            cg = g * A_GROUP_W
            s = s2[:, k * A_GROUP_W:(k + 1) * A_GROUP_W] + self.bst[:, g:g + 1]
            u = self.load(self.o_u + cg, self.o_u + cg + A_GROUP_W).astype(F32)
            za = self.load(self.o_za + cg, self.o_za + cg + A_GROUP_W).astype(F32)
            self.store(cg, (u * s * _silu(za)).astype(BF16))

    def scores(self, p0):
        kv_head = (p0 * HEADS_PER_VREG) // Q_PER_KV
        kprev = self.kprev()
        q4 = jnp.concatenate(
            [self._rope(self.load(self.o_q + p * LANES, self.o_q + (p + 1) * LANES)
                        .astype(F32), query=True).astype(BF16)
             for p in range(p0, p0 + COLS_PER_KV)], axis=0)
        probs = [[] for _ in range(COLS_PER_KV)]
        for o in range(HEADS_PER_VREG):
            e = kv_head * HEADS_PER_VREG + o
            kband = jnp.concatenate([kprev[e], self.ke[e]], axis=0)
            s4 = lax.dot_general(q4, kband, (((1,), (1,)), ((), ())),
                                 preferred_element_type=F32)
            for a in range(COLS_PER_KV):
                s = s4[a * CHUNK:(a + 1) * CHUNK]
                h = (p0 + a) * HEADS_PER_VREG + o
                s_prev = s[:, :CHUNK]
                if self.has_prev is not True:
                    s_prev = jnp.where(self.has_prev, s_prev, -jnp.inf)
                comb = jnp.where(self.causal, s[:, CHUNK:], s_prev)
                sink = self.sink_ref[h] * LOG2E
                m = jnp.maximum(jnp.max(comb, axis=-1, keepdims=True), sink)
                pexp = jnp.exp2(comb - m)
                denom = jnp.sum(pexp, axis=-1, keepdims=True) + jnp.exp2(sink - m)
                pn = (pexp * (1.0 / denom)).astype(BF16)
                zero = jnp.zeros_like(pn)
                probs[a].append(jnp.where(self.causal, zero, pn))
                probs[a].append(jnp.where(self.causal, pn, zero))
        self.probs[p0] = jnp.concatenate(
            [jnp.concatenate(pa, axis=1) for pa in probs], axis=0)

    def values(self, p0):
        kv_head = (p0 * HEADS_PER_VREG) // Q_PER_KV
        vprev = self.vprev()
        vband = []
        for o in range(HEADS_PER_VREG):
            e = kv_head * HEADS_PER_VREG + o
            vband.append(vprev[e])
            vband.append(self.ve[e])
        out4 = jnp.dot(self.probs.pop(p0), jnp.concatenate(vband, axis=0),
                       preferred_element_type=F32)
        for a in range(COLS_PER_KV):
            p = p0 + a
            zb = self.load(self.o_zb + p * LANES, self.o_zb + (p + 1) * LANES).astype(F32)
            self.store(self.d_a + p * LANES,
                       (out4[a * CHUNK:(a + 1) * CHUNK] * _silu(zb)).astype(BF16))


def _layer_kernel(sink_ref, x_ref, moda_ref, modc_ref, modf_ref, ng_ref, fg_ref,
                  cos_ref, sin_ref, lng_ref, lnb_ref, ws_ref, bst_ref, win32_ref, wout32_ref,
                  o_ref, win_scr, wout_scr, proj0_scr, proj1_scr, y_scr, xprev_scr, h_scr,
                  kprev_scr, vprev_scr,
                  *, d_model, d_in, d_a, d_b, d_kv, blocks_per_batch, n_blk):
    i = pl.program_id(0)

    @pl.when(i < W_TILES)
    def _():
        r_in = pl.multiple_of(i * win32_ref.shape[0], win32_ref.shape[0])
        win_scr[pl.ds(r_in, win32_ref.shape[0]), :] = win32_ref[...].astype(BF16)
        r_out = pl.multiple_of(i * wout32_ref.shape[0], wout32_ref.shape[0])
        wout_scr[pl.ds(r_out, wout32_ref.shape[0]), :] = wout32_ref[...].astype(BF16)

    step = i - W_TILES

    args = (sink_ref, x_ref, moda_ref, modc_ref, modf_ref, ng_ref, fg_ref, cos_ref, sin_ref,
            lng_ref, lnb_ref, ws_ref, bst_ref, win_scr, wout_scr, o_ref)
    scr = (y_scr, xprev_scr, h_scr, kprev_scr, vprev_scr)
    dims = dict(d_model=d_model, d_in=d_in, d_a=d_a, d_b=d_b, d_kv=d_kv,
                blocks_per_batch=blocks_per_batch, n_blk=n_blk)
    bufs = ((proj0_scr, proj1_scr), (proj1_scr, proj0_scr))
    inner = jnp.logical_and(step > 0, step < n_blk)

    @pl.when(step == 0)
    def _():
        kprev_scr[...] = jnp.zeros_like(kprev_scr)
        vprev_scr[...] = jnp.zeros_like(vprev_scr)
        _layer_step(step, *args, *bufs[0], *scr, **dims, do_bc=False)

    @pl.when(jnp.logical_and(inner, step % 2 == 0))
    def _():
        _layer_step(step, *args, *bufs[0], *scr, **dims)

    @pl.when(jnp.logical_and(inner, step % 2 == 1))
    def _():
        _layer_step(step, *args, *bufs[1], *scr, **dims)

    @pl.when(step == n_blk)
    def _():
        _layer_step(step, *args, *bufs[n_blk % 2], *scr, **dims, do_a=False)


def _layer_step(step, sink_ref, x_ref, moda_ref, modc_ref, modf_ref, ng_ref, fg_ref,
                cos_ref, sin_ref, lng_ref, lnb_ref, ws_ref, bst_ref, win_ref, wout_ref,
                o_ref, proj_w, proj_r, y_scr, xprev_scr, h_scr, kprev_scr, vprev_scr,
                *, d_model, d_in, d_a, d_b, d_kv, blocks_per_batch, n_blk,
                do_a=True, do_bc=True):
    blk_c = jnp.clip(step - 1, 0, n_blk - 1)
    first_in_batch = (blk_c % blocks_per_batch) == 0

    chunks = list(range(0, d_in, TN_DOT))
    carried = chunks[len(chunks) - N_CARRY:]
    if do_bc:
        for n0 in carried:
            acc = jnp.dot(h_scr[...], win_ref[:, n0:n0 + TN_DOT], preferred_element_type=F32)
            proj_r[:, n0:n0 + TN_DOT] = acc.astype(BF16)

    if do_a:
        x = x_ref[...]
        ms = jnp.mean(x * x, axis=-1, keepdims=True)
        row_a = pl.ds(jnp.minimum(step, n_blk - 1) // blocks_per_batch, 1)
        shift = moda_ref[row_a, 0:d_model]
        gain = ng_ref[...] * (1.0 + moda_ref[row_a, d_model:2 * d_model])
        h_scr[...] = (x * lax.rsqrt(ms + NORM_EPS) * gain + shift).astype(BF16)

    lng = lng_ref[...]
    lnb = lnb_ref[...]
    bst = bst_ref[...]
    n_sub = T_BLK // CHUNK
    blocks = []
    for sb in range(n_sub):
        r0 = sb * CHUNK

        def load(c0, c1, r0=r0):
            return proj_r[r0:r0 + CHUNK, c0:c1]

        def store(c0, val, r0=r0):
            y_scr[r0:r0 + CHUNK, c0:c0 + val.shape[1]] = val

        if sb == 0:
            kprev = lambda: [kprev_scr[e] for e in range(N_EXP)]
            vprev = lambda: [vprev_scr[e] for e in range(N_EXP)]
            has_prev = jnp.logical_not(first_in_batch)
        else:
            kprev = lambda b=blocks[sb - 1]: b.ke
            vprev = lambda b=blocks[sb - 1]: b.ve
            has_prev = True
        blocks.append(_MixerBlock(
            load, store, cos_ref[r0:r0 + CHUNK, :], sin_ref[r0:r0 + CHUNK, :], lng, lnb,
            ws_ref, bst, sink_ref, kprev, vprev, has_prev, d_a=d_a, d_b=d_b, d_kv=d_kv))

    def proj_chunk(n0):
        acc = jnp.dot(h_scr[...], win_ref[:, n0:n0 + TN_DOT], preferred_element_type=F32)
        proj_w[:, n0:n0 + TN_DOT] = acc.astype(BF16)

    chunks = chunks[:len(chunks) - N_CARRY]
    next_chunk = [0]

    def emit_chunks(n):
        for _ in range(n):
            if do_a and next_chunk[0] < len(chunks):
                proj_chunk(chunks[next_chunk[0]])
                next_chunk[0] += 1

    if not do_bc:
        emit_chunks(len(chunks))
        xprev_scr[...] = x
        return

    for b in blocks:
        b.prep()
    n_pairs = d_b // LANES
    slots = [(b, p0) for b in blocks for p0 in range(0, n_pairs, COLS_PER_KV)]
    group_pairs = [(b, g0) for b in blocks for g0 in range(0, A_GROUPS, 2)]
    pairs_per_slot = -(-len(group_pairs) // len(slots))
    slots[0][0].scores(slots[0][1])
    for j, (b, p0) in enumerate(slots):
        emit_chunks(1)
        b.values(p0)
        if j + 1 < len(slots):
            slots[j + 1][0].scores(slots[j + 1][1])
        for bg, g0 in group_pairs[j * pairs_per_slot:(j + 1) * pairs_per_slot]:
            bg.group_pair(g0)
    for e in range(N_EXP):
        kprev_scr[e] = blocks[-1].ke[e]
        vprev_scr[e] = blocks[-1].ve[e]
    emit_chunks(len(chunks) - next_chunk[0] - 1)

    row_c = pl.ds(blk_c // blocks_per_batch, 1)
    gate = modc_ref[row_c, 2 * d_model:3 * d_model]
    ssq = jnp.zeros((T_BLK, 1), F32)
    for n0 in range(0, d_model, TN_DOT):
        sl = slice(n0, n0 + TN_DOT)
        acc = jnp.dot(y_scr[...], wout_ref[:, sl], preferred_element_type=F32)
        xr = xprev_scr[:, sl] + gate[:, sl] * acc
        ssq = ssq + jnp.sum(xr * xr, axis=-1, keepdims=True)
        o_ref[:, sl] = xr
    emit_chunks(len(chunks))
    inv = lax.rsqrt(ssq * (1.0 / d_model) + NORM_EPS)
    shift_f = modf_ref[row_c, 0:d_model]
    scale_f = modf_ref[row_c, d_model:2 * d_model]
    o_ref[...] = o_ref[...] * inv * (fg_ref[...] * (1.0 + scale_f)) + shift_f

    if do_a:
        xprev_scr[...] = x


def _layer(x2, sinks, mod, mod_f, norm_g, final_g, cos_t, sin_t, ln_g, ln_b, w_sp, b_sp_t,
           w_in, w_out, seq, d_a, d_b, d_kv):
    rows, d_model = x2.shape
    d_in = w_in.shape[-1]
    d_mix = d_a + d_b
    n_blk = rows // T_BLK
    bpb = seq // T_BLK
    assert d_model % W_TILES == 0 and d_mix % W_TILES == 0

    def blk_a(i):
        return jnp.clip(i - W_TILES, 0, n_blk - 1)

    def blk_c(i):
        return jnp.clip(i - W_TILES - 1, 0, n_blk - 1)

    def w_tile(i):
        return (jnp.minimum(i, W_TILES - 1), 0)

    const2 = lambda i: (0, 0)
    kern = functools.partial(_layer_kernel, d_model=d_model, d_in=d_in, d_a=d_a, d_b=d_b,
                             d_kv=d_kv, blocks_per_batch=bpb, n_blk=n_blk)
    return pl.pallas_call(
        kern,
        grid=(W_TILES + n_blk + 1,),
        in_specs=[
            pl.BlockSpec(memory_space=pltpu.SMEM),
            pl.BlockSpec((T_BLK, d_model), lambda i: (blk_a(i), 0)),
            pl.BlockSpec(mod.shape, const2),
            pl.BlockSpec(mod.shape, const2),
            pl.BlockSpec(mod_f.shape, const2),
            pl.BlockSpec((1, d_model), const2),
            pl.BlockSpec((1, d_model), const2),
            pl.BlockSpec((T_BLK, LANES), lambda i: (blk_c(i) % bpb, 0)),
            pl.BlockSpec((T_BLK, LANES), lambda i: (blk_c(i) % bpb, 0)),
            pl.BlockSpec((1, d_a), const2),
            pl.BlockSpec((1, d_a), const2),
            pl.BlockSpec((A_GROUPS, CHUNK, CHUNK), lambda i: (0, 0, 0)),
            pl.BlockSpec((CHUNK, A_GROUPS), const2),
            pl.BlockSpec((d_model // W_TILES, d_in), w_tile),
            pl.BlockSpec((d_mix // W_TILES, d_model), w_tile),
        ],
        out_specs=pl.BlockSpec((T_BLK, d_model), lambda i: (blk_c(i), 0)),
        out_shape=jax.ShapeDtypeStruct((rows, d_model), F32),
        scratch_shapes=[
            pltpu.VMEM((d_model, d_in), BF16),
            pltpu.VMEM((d_mix, d_model), BF16),
            pltpu.VMEM((T_BLK, d_in), BF16),
            pltpu.VMEM((T_BLK, d_in), BF16),
            pltpu.VMEM((T_BLK, d_mix), BF16),
            pltpu.VMEM((T_BLK, d_model), F32),
            pltpu.VMEM((T_BLK, d_model), BF16),
            pltpu.VMEM((N_EXP, CHUNK, LANES), BF16),
            pltpu.VMEM((N_EXP, CHUNK, LANES), BF16),
        ],
        compiler_params=pltpu.CompilerParams(
            dimension_semantics=("arbitrary",), vmem_limit_bytes=VMEM_LIMIT_LAYER),
        name="layer",
    )(sinks, x2, mod, mod, mod_f, norm_g.reshape(1, d_model), final_g.reshape(1, d_model),
      cos_t, sin_t, ln_g, ln_b, w_sp, b_sp_t, w_in, w_out)


def _rope_tables(seq):
    half = HEAD_DIM // 2
    inv_freq = ROPE_THETA ** (-jnp.arange(0, HEAD_DIM, 2, dtype=F32) / HEAD_DIM)
    ang = jnp.arange(seq, dtype=F32)[:, None] * inv_freq[None, :]
    cos = jnp.cos(ang)
    sin = jnp.sin(ang)
    cos_t = jnp.tile(cos, (1, LANES // half))
    sin_t = jnp.tile(jnp.concatenate([-sin, sin], axis=1), (1, HEADS_PER_VREG))
    return cos_t, sin_t


def kernel(x, c, w_ada, b_ada, norm_g, w_in, ln_v_g, ln_v_b, w_spatial, b_spatial, sinks,
           w_out, w_ada_final, b_ada_final, final_norm_g):
    bsz, seq, d_model = x.shape
    assert w_ada.shape[0] == 1, "single-layer stack"
    d_a = ln_v_g.shape[-1]
    d_mix = w_out.shape[-2]
    d_b = d_mix - d_a
    d_kv = N_KV_HEADS * HEAD_DIM
    d_in = w_in.shape[-1]
    assert d_in == 3 * d_a + 2 * d_b + 2 * d_kv
    assert d_a == A_GROUPS * A_GROUP_W and d_b == N_KV_HEADS * Q_PER_KV * HEAD_DIM
    assert seq % T_BLK == 0 and T_BLK % CHUNK == 0

    x2 = x.reshape(bsz * seq, d_model)
    mod, mod_f = _ada_mod(c, w_ada, b_ada, w_ada_final, b_ada_final)
    cos_t, sin_t = _rope_tables(seq)
    out = _layer(x2, sinks.reshape(-1), mod, mod_f, norm_g, final_norm_g, cos_t, sin_t,
                 ln_v_g.reshape(1, d_a), ln_v_b.reshape(1, d_a),
                 w_spatial.reshape(A_GROUPS, CHUNK, CHUNK),
                 b_spatial.reshape(A_GROUPS, CHUNK).T,
                 w_in.reshape(d_model, d_in), w_out.reshape(d_mix, d_model),
                 seq, d_a, d_b, d_kv)
    return out.reshape(bsz, seq, d_model)
```

```python
import functools

import jax
import jax.numpy as jnp
from jax import lax
from jax.experimental import pallas as pl
from jax.experimental.pallas import tpu as pltpu

F32 = jnp.float32
BF16 = jnp.bfloat16

CHUNK = 128
A_GROUPS = 8
A_GROUP_W = 128
HEAD_DIM = 64
N_KV_HEADS = 4
Q_PER_KV = 4
ROPE_THETA = 10000.0
NORM_EPS = 1e-5
LOG2E = 1.4426950408889634

LANES = 128
HEADS_PER_VREG = LANES // HEAD_DIM
N_EXP = N_KV_HEADS * HEADS_PER_VREG
COLS_PER_KV = Q_PER_KV // HEADS_PER_VREG

T_BLK = 256
TN_DOT = 512
N_CARRY = 2
W_TILES = 16
TN_ADA_LAYER = 2048
TN_ADA_FINAL = 1024

MIB = 1024 * 1024
V7X_VMEM_BYTES = 64 * MIB
VMEM_LIMIT_ADA = 56 * MIB
VMEM_LIMIT_LAYER = V7X_VMEM_BYTES - MIB


def _silu(z):
    hz = 0.5 * z
    return hz + hz * jnp.tanh(hz)


def _ada_kernel(c_ref, wa_top, wa_bot, wf_top, wf_bot, ba_ref, bf_ref, oa_ref, of_ref,
                *, n_a_tiles):
    j = pl.program_id(0)
    c = c_ref[...]
    ca = (c * (1.0 / (1.0 + jnp.exp(-c)))).astype(BF16)
    half = ca.shape[1] // 2

    def mod(w_top, w_bot, b_ref, o_ref):
        acc = jnp.dot(ca[:, :half], w_top[...].astype(BF16), preferred_element_type=F32)
        acc = acc + jnp.dot(ca[:, half:], w_bot[...].astype(BF16),
                            preferred_element_type=F32)
        o_ref[...] = acc + b_ref[...]

    @pl.when(j < n_a_tiles)
    def _():
        mod(wa_top, wa_bot, ba_ref, oa_ref)

    @pl.when(j >= n_a_tiles)
    def _():
        mod(wf_top, wf_bot, bf_ref, of_ref)


def _ada_mod(c, w_a, b_a, w_f, b_f):
    bsz, d = c.shape
    n_a, n_f = w_a.shape[-1], w_f.shape[-1]
    w_a = w_a.reshape(d, n_a)
    w_f = w_f.reshape(d, n_f)
    ta, tf = n_a // TN_ADA_LAYER, n_f // TN_ADA_FINAL
    half = d // 2
    a_tile = lambda j: jnp.minimum(j, ta - 1)
    f_tile = lambda j: jnp.maximum(j - ta, 0)
    return pl.pallas_call(
        functools.partial(_ada_kernel, n_a_tiles=ta),
        grid=(ta + tf,),
        in_specs=[
            pl.BlockSpec((bsz, d), lambda j: (0, 0)),
            pl.BlockSpec((half, TN_ADA_LAYER), lambda j: (0, a_tile(j))),
            pl.BlockSpec((half, TN_ADA_LAYER), lambda j: (1, a_tile(j))),
            pl.BlockSpec((half, TN_ADA_FINAL), lambda j: (0, f_tile(j))),
            pl.BlockSpec((half, TN_ADA_FINAL), lambda j: (1, f_tile(j))),
            pl.BlockSpec((1, TN_ADA_LAYER), lambda j: (0, a_tile(j))),
            pl.BlockSpec((1, TN_ADA_FINAL), lambda j: (0, f_tile(j))),
        ],
        out_specs=[
            pl.BlockSpec((bsz, TN_ADA_LAYER), lambda j: (0, a_tile(j))),
            pl.BlockSpec((bsz, TN_ADA_FINAL), lambda j: (0, f_tile(j))),
        ],
        out_shape=[jax.ShapeDtypeStruct((bsz, n_a), F32),
                   jax.ShapeDtypeStruct((bsz, n_f), F32)],
        compiler_params=pltpu.CompilerParams(
            dimension_semantics=("arbitrary",), vmem_limit_bytes=VMEM_LIMIT_ADA),
        name="ada_mod",
    )(c, w_a, w_a, w_f, w_f, b_a.reshape(1, n_a), b_f.reshape(1, n_f))


class _MixerBlock:
    def __init__(self, load, store, cos, sin, lng, lnb, ws_ref, bst, sink_ref, kprev, vprev,
                 has_prev, *, d_a, d_b, d_kv):
        self.load, self.store = load, store
        self.cos, self.sin, self.lng, self.lnb = cos, sin, lng, lnb
        q_scale = LOG2E * HEAD_DIM ** -0.5
        self.cos_q, self.sin_q = cos * q_scale, sin * q_scale
        self.ws_ref, self.bst, self.sink_ref = ws_ref, bst, sink_ref
        self.kprev, self.vprev, self.has_prev = kprev, vprev, has_prev
        self.d_a, self.d_b, self.d_kv = d_a, d_b, d_kv
        self.o_u, self.o_v, self.o_za = 0, d_a, 2 * d_a
        self.o_q = 3 * d_a
        self.o_k = self.o_q + d_b
        self.o_vv = self.o_k + d_kv
        self.o_zb = self.o_vv + d_kv
        row = lax.broadcasted_iota(jnp.int32, (CHUNK, CHUNK), 0)
        col = lax.broadcasted_iota(jnp.int32, (CHUNK, CHUNK), 1)
        self.col = col
        self.causal = col <= row
        self.first_half = (col & (HEAD_DIM - 1)) < (HEAD_DIM // 2)
        self.probs = {}

    def _rope(self, xv, query=False):
        cos, sin = (self.cos_q, self.sin_q) if query else (self.cos, self.sin)
        rot = jnp.where(self.first_half,
                        pltpu.roll(xv, LANES - HEAD_DIM // 2, 1),
                        pltpu.roll(xv, HEAD_DIM // 2, 1))
        return xv * cos + rot * sin

    def prep(self):
        va = self.load(self.o_v, self.o_v + self.d_a).astype(F32)
        mu = jnp.mean(va, axis=-1, keepdims=True)
        vc = va - mu
        var = jnp.mean(vc * vc, axis=-1, keepdims=True)
        self.vn = (vc * lax.rsqrt(var + NORM_EPS) * self.lng + self.lnb).astype(BF16)
        self.tril = self.causal.astype(F32)
        low_half = self.col < HEAD_DIM
        self.ke = [None] * N_EXP
        self.ve = [None] * N_EXP
        for c in range(self.d_kv // LANES):
            kc = self._rope(self.load(self.o_k + c * LANES, self.o_k + (c + 1) * LANES)
                            .astype(F32))
            vcol = self.load(self.o_vv + c * LANES, self.o_vv + (c + 1) * LANES).astype(F32)
            kc_sw = pltpu.roll(kc, HEAD_DIM, 1)
            vcol_sw = pltpu.roll(vcol, HEAD_DIM, 1)
            for j in range(HEADS_PER_VREG):
                kv_head = c * HEADS_PER_VREG + j
                for o in range(HEADS_PER_VREG):
                    mask = low_half if o == 0 else jnp.logical_not(low_half)
                    e = kv_head * HEADS_PER_VREG + o
                    self.ke[e] = jnp.where(mask, kc if o == j else kc_sw, 0.0).astype(BF16)
                    self.ve[e] = jnp.where(mask, vcol if o == j else vcol_sw, 0.0).astype(BF16)

    def group_pair(self, g0):
        c0 = g0 * A_GROUP_W
        w = jnp.concatenate([(self.ws_ref[g0] * self.tril).astype(BF16),
                             (self.ws_ref[g0 + 1] * self.tril).astype(BF16)], axis=1)
        zero = jnp.zeros((CHUNK, A_GROUP_W), BF16)
        rhs = jnp.concatenate(
            [jnp.concatenate([self.vn[:, c0:c0 + A_GROUP_W], zero], axis=1),
             jnp.concatenate([zero, self.vn[:, c0 + A_GROUP_W:c0 + 2 * A_GROUP_W]], axis=1)],
            axis=0)
        s2 = jnp.dot(w, rhs, preferred_element_type=F32)
        for k in range(2):
            g = g0 + k
            cg = g * A_GROUP_W
            s = s2[:, k * A_GROUP_W:(k + 1) * A_GROUP_W] + self.bst[:, g:g + 1]
            u = self.load(self.o_u + cg, self.o_u + cg + A_GROUP_W).astype(F32)
            za = self.load(self.o_za + cg, self.o_za + cg + A_GROUP_W).astype(F32)
            self.store(cg, (u * s * _silu(za)).astype(BF16))

    def scores(self, p0):
        kv_head = (p0 * HEADS_PER_VREG) // Q_PER_KV
        kprev = self.kprev()
        q4 = jnp.concatenate(
            [self._rope(self.load(self.o_q + p * LANES, self.o_q + (p + 1) * LANES)
                        .astype(F32), query=True).astype(BF16)
             for p in range(p0, p0 + COLS_PER_KV)], axis=0)
        probs = [[] for _ in range(COLS_PER_KV)]
        for o in range(HEADS_PER_VREG):
            e = kv_head * HEADS_PER_VREG + o
            kband = jnp.concatenate([kprev[e], self.ke[e]], axis=0)
            s4 = lax.dot_general(q4, kband, (((1,), (1,)), ((), ())),
                                 preferred_element_type=F32)
            for a in range(COLS_PER_KV):
                s = s4[a * CHUNK:(a + 1) * CHUNK]
                h = (p0 + a) * HEADS_PER_VREG + o
                s_prev = s[:, :CHUNK]
                if self.has_prev is not True:
                    s_prev = jnp.where(self.has_prev, s_prev, -jnp.inf)
                comb = jnp.where(self.causal, s[:, CHUNK:], s_prev)
                sink = self.sink_ref[h] * LOG2E
                m = jnp.maximum(jnp.max(comb, axis=-1, keepdims=True), sink)
                pexp = jnp.exp2(comb - m)
                denom = jnp.sum(pexp, axis=-1, keepdims=True) + jnp.exp2(sink - m)
                pn = (pexp * (1.0 / denom)).astype(BF16)
                zero = jnp.zeros_like(pn)
                probs[a].append(jnp.where(self.causal, zero, pn))
                probs[a].append(jnp.where(self.causal, pn, zero))
        self.probs[p0] = jnp.concatenate(
            [jnp.concatenate(pa, axis=1) for pa in probs], axis=0)

    def values(self, p0):
        kv_head = (p0 * HEADS_PER_VREG) // Q_PER_KV
        vprev = self.vprev()
        vband = []
        for o in range(HEADS_PER_VREG):
            e = kv_head * HEADS_PER_VREG + o
            vband.append(vprev[e])
            vband.append(self.ve[e])
        out4 = jnp.dot(self.probs.pop(p0), jnp.concatenate(vband, axis=0),
                       preferred_element_type=F32)
        for a in range(COLS_PER_KV):
            p = p0 + a
            zb = self.load(self.o_zb + p * LANES, self.o_zb + (p + 1) * LANES).astype(F32)
            self.store(self.d_a + p * LANES,
                       (out4[a * CHUNK:(a + 1) * CHUNK] * _silu(zb)).astype(BF16))


def _layer_kernel(sink_ref, x_ref, moda_ref, modc_ref, modf_ref, ng_ref, fg_ref,
                  cos_ref, sin_ref, lng_ref, lnb_ref, ws_ref, bst_ref, win32_ref, wout32_ref,
                  o_ref, win_scr, wout_scr, proj0_scr, proj1_scr, y_scr, xprev_scr, h_scr,
                  kprev_scr, vprev_scr,
                  *, d_model, d_in, d_a, d_b, d_kv, blocks_per_batch, n_blk):
    i = pl.program_id(0)

    @pl.when(i < W_TILES)
    def _():
        r_in = pl.multiple_of(i * win32_ref.shape[0], win32_ref.shape[0])
        win_scr[pl.ds(r_in, win32_ref.shape[0]), :] = win32_ref[...].astype(BF16)
        r_out = pl.multiple_of(i * wout32_ref.shape[0], wout32_ref.shape[0])
        wout_scr[pl.ds(r_out, wout32_ref.shape[0]), :] = wout32_ref[...].astype(BF16)

    step = i - W_TILES

    args = (sink_ref, x_ref, moda_ref, modc_ref, modf_ref, ng_ref, fg_ref, cos_ref, sin_ref,
            lng_ref, lnb_ref, ws_ref, bst_ref, win_scr, wout_scr, o_ref)
    scr = (y_scr, xprev_scr, h_scr, kprev_scr, vprev_scr)
    dims = dict(d_model=d_model, d_in=d_in, d_a=d_a, d_b=d_b, d_kv=d_kv,
                blocks_per_batch=blocks_per_batch, n_blk=n_blk)
    bufs = ((proj0_scr, proj1_scr), (proj1_scr, proj0_scr))
    inner = jnp.logical_and(step > 0, step < n_blk)

    @pl.when(step == 0)
    def _():
        kprev_scr[...] = jnp.zeros_like(kprev_scr)
        vprev_scr[...] = jnp.zeros_like(vprev_scr)
        _layer_step(step, *args, *bufs[0], *scr, **dims, do_bc=False)

    @pl.when(jnp.logical_and(inner, step % 2 == 0))
    def _():
        _layer_step(step, *args, *bufs[0], *scr, **dims)

    @pl.when(jnp.logical_and(inner, step % 2 == 1))
    def _():
        _layer_step(step, *args, *bufs[1], *scr, **dims)

    @pl.when(step == n_blk)
    def _():
        _layer_step(step, *args, *bufs[n_blk % 2], *scr, **dims, do_a=False)


def _layer_step(step, sink_ref, x_ref, moda_ref, modc_ref, modf_ref, ng_ref, fg_ref,
                cos_ref, sin_ref, lng_ref, lnb_ref, ws_ref, bst_ref, win_ref, wout_ref,
                o_ref, proj_w, proj_r, y_scr, xprev_scr, h_scr, kprev_scr, vprev_scr,
                *, d_model, d_in, d_a, d_b, d_kv, blocks_per_batch, n_blk,
                do_a=True, do_bc=True):
    blk_c = jnp.clip(step - 1, 0, n_blk - 1)
    first_in_batch = (blk_c % blocks_per_batch) == 0

    chunks = list(range(0, d_in, TN_DOT))
    carried = chunks[len(chunks) - N_CARRY:]
    if do_bc:
        for n0 in carried:
            acc = jnp.dot(h_scr[...], win_ref[:, n0:n0 + TN_DOT], preferred_element_type=F32)
            proj_r[:, n0:n0 + TN_DOT] = acc.astype(BF16)

    if do_a:
        x = x_ref[...]
        ms = jnp.mean(x * x, axis=-1, keepdims=True)
        row_a = pl.ds(jnp.minimum(step, n_blk - 1) // blocks_per_batch, 1)
        shift = moda_ref[row_a, 0:d_model]
        gain = ng_ref[...] * (1.0 + moda_ref[row_a, d_model:2 * d_model])
        h_scr[...] = (x * lax.rsqrt(ms + NORM_EPS) * gain + shift).astype(BF16)

    lng = lng_ref[...]
    lnb = lnb_ref[...]
    bst = bst_ref[...]
    n_sub = T_BLK // CHUNK
    blocks = []
    for sb in range(n_sub):
        r0 = sb * CHUNK

        def load(c0, c1, r0=r0):
            return proj_r[r0:r0 + CHUNK, c0:c1]

        def store(c0, val, r0=r0):
            y_scr[r0:r0 + CHUNK, c0:c0 + val.shape[1]] = val

        if sb == 0:
            kprev = lambda: [kprev_scr[e] for e in range(N_EXP)]
            vprev = lambda: [vprev_scr[e] for e in range(N_EXP)]
            has_prev = jnp.logical_not(first_in_batch)
        else:
            kprev = lambda b=blocks[sb - 1]: b.ke
            vprev = lambda b=blocks[sb - 1]: b.ve
            has_prev = True
        blocks.append(_MixerBlock(
            load, store, cos_ref[r0:r0 + CHUNK, :], sin_ref[r0:r0 + CHUNK, :], lng, lnb,
            ws_ref, bst, sink_ref, kprev, vprev, has_prev, d_a=d_a, d_b=d_b, d_kv=d_kv))

    def proj_chunk(n0):
        acc = jnp.dot(h_scr[...], win_ref[:, n0:n0 + TN_DOT], preferred_element_type=F32)
        proj_w[:, n0:n0 + TN_DOT] = acc.astype(BF16)

    chunks = chunks[:len(chunks) - N_CARRY]
    next_chunk = [0]

    def emit_chunks(n):
        for _ in range(n):
            if do_a and next_chunk[0] < len(chunks):
                proj_chunk(chunks[next_chunk[0]])
                next_chunk[0] += 1

    if not do_bc:
        emit_chunks(len(chunks))
        xprev_scr[...] = x
        return

    for b in blocks:
        b.prep()
    n_pairs = d_b // LANES
    slots = [(b, p0) for b in blocks for p0 in range(0, n_pairs, COLS_PER_KV)]
    group_pairs = [(b, g0) for b in blocks for g0 in range(0, A_GROUPS, 2)]
    pairs_per_slot = -(-len(group_pairs) // len(slots))
    slots[0][0].scores(slots[0][1])
    for j, (b, p0) in enumerate(slots):
        emit_chunks(1)
        b.values(p0)
        if j + 1 < len(slots):
            slots[j + 1][0].scores(slots[j + 1][1])
        for bg, g0 in group_pairs[j * pairs_per_slot:(j + 1) * pairs_per_slot]:
            bg.group_pair(g0)
    for e in range(N_EXP):
        kprev_scr[e] = blocks[-1].ke[e]
        vprev_scr[e] = blocks[-1].ve[e]
    emit_chunks(len(chunks) - next_chunk[0] - 1)

    row_c = pl.ds(blk_c // blocks_per_batch, 1)
    gate = modc_ref[row_c, 2 * d_model:3 * d_model]
    ssq = jnp.zeros((T_BLK, 1), F32)
    for n0 in range(0, d_model, TN_DOT):
        sl = slice(n0, n0 + TN_DOT)
        acc = jnp.dot(y_scr[...], wout_ref[:, sl], preferred_element_type=F32)
        xr = xprev_scr[:, sl] + gate[:, sl] * acc
        ssq = ssq + jnp.sum(xr * xr, axis=-1, keepdims=True)
        o_ref[:, sl] = xr
    emit_chunks(len(chunks))
    inv = lax.rsqrt(ssq * (1.0 / d_model) + NORM_EPS)
    shift_f = modf_ref[row_c, 0:d_model]
    scale_f = modf_ref[row_c, d_model:2 * d_model]
    o_ref[...] = o_ref[...] * inv * (fg_ref[...] * (1.0 + scale_f)) + shift_f

    if do_a:
        xprev_scr[...] = x


def _layer(x2, sinks, mod, mod_f, norm_g, final_g, cos_t, sin_t, ln_g, ln_b, w_sp, b_sp_t,
           w_in, w_out, seq, d_a, d_b, d_kv):
    rows, d_model = x2.shape
    d_in = w_in.shape[-1]
    d_mix = d_a + d_b
    n_blk = rows // T_BLK
    bpb = seq // T_BLK
    assert d_model % W_TILES == 0 and d_mix % W_TILES == 0

    def blk_a(i):
        return jnp.clip(i - W_TILES, 0, n_blk - 1)

    def blk_c(i):
        return jnp.clip(i - W_TILES - 1, 0, n_blk - 1)

    def w_tile(i):
        return (jnp.minimum(i, W_TILES - 1), 0)

    const2 = lambda i: (0, 0)
    kern = functools.partial(_layer_kernel, d_model=d_model, d_in=d_in, d_a=d_a, d_b=d_b,
                             d_kv=d_kv, blocks_per_batch=bpb, n_blk=n_blk)
    return pl.pallas_call(
        kern,
        grid=(W_TILES + n_blk + 1,),
        in_specs=[
            pl.BlockSpec(memory_space=pltpu.SMEM),
            pl.BlockSpec((T_BLK, d_model), lambda i: (blk_a(i), 0)),
            pl.BlockSpec(mod.shape, const2),
            pl.BlockSpec(mod.shape, const2),
            pl.BlockSpec(mod_f.shape, const2),
            pl.BlockSpec((1, d_model), const2),
            pl.BlockSpec((1, d_model), const2),
            pl.BlockSpec((T_BLK, LANES), lambda i: (blk_c(i) % bpb, 0)),
            pl.BlockSpec((T_BLK, LANES), lambda i: (blk_c(i) % bpb, 0)),
            pl.BlockSpec((1, d_a), const2),
            pl.BlockSpec((1, d_a), const2),
            pl.BlockSpec((A_GROUPS, CHUNK, CHUNK), lambda i: (0, 0, 0)),
            pl.BlockSpec((CHUNK, A_GROUPS), const2),
            pl.BlockSpec((d_model // W_TILES, d_in), w_tile),
            pl.BlockSpec((d_mix // W_TILES, d_model), w_tile),
        ],
        out_specs=pl.BlockSpec((T_BLK, d_model), lambda i: (blk_c(i), 0)),
        out_shape=jax.ShapeDtypeStruct((rows, d_model), F32),
        scratch_shapes=[
            pltpu.VMEM((d_model, d_in), BF16),
            pltpu.VMEM((d_mix, d_model), BF16),
            pltpu.VMEM((T_BLK, d_in), BF16),
            pltpu.VMEM((T_BLK, d_in), BF16),
            pltpu.VMEM((T_BLK, d_mix), BF16),
            pltpu.VMEM((T_BLK, d_model), F32),
            pltpu.VMEM((T_BLK, d_model), BF16),
            pltpu.VMEM((N_EXP, CHUNK, LANES), BF16),
            pltpu.VMEM((N_EXP, CHUNK, LANES), BF16),
        ],
        compiler_params=pltpu.CompilerParams(
            dimension_semantics=("arbitrary",), vmem_limit_bytes=VMEM_LIMIT_LAYER),
        name="layer",
    )(sinks, x2, mod, mod, mod_f, norm_g.reshape(1, d_model), final_g.reshape(1, d_model),
      cos_t, sin_t, ln_g, ln_b, w_sp, b_sp_t, w_in, w_out)


def _rope_tables(seq):
    half = HEAD_DIM // 2
    inv_freq = ROPE_THETA ** (-jnp.arange(0, HEAD_DIM, 2, dtype=F32) / HEAD_DIM)
    ang = jnp.arange(seq, dtype=F32)[:, None] * inv_freq[None, :]
    cos = jnp.cos(ang)
    sin = jnp.sin(ang)
    cos_t = jnp.tile(cos, (1, LANES // half))
    sin_t = jnp.tile(jnp.concatenate([-sin, sin], axis=1), (1, HEADS_PER_VREG))
    return cos_t, sin_t


def kernel(x, c, w_ada, b_ada, norm_g, w_in, ln_v_g, ln_v_b, w_spatial, b_spatial, sinks,
           w_out, w_ada_final, b_ada_final, final_norm_g):
    bsz, seq, d_model = x.shape
    assert w_ada.shape[0] == 1, "single-layer stack"
    d_a = ln_v_g.shape[-1]
    d_mix = w_out.shape[-2]
    d_b = d_mix - d_a
    d_kv = N_KV_HEADS * HEAD_DIM
    d_in = w_in.shape[-1]
    assert d_in == 3 * d_a + 2 * d_b + 2 * d_kv
    assert d_a == A_GROUPS * A_GROUP_W and d_b == N_KV_HEADS * Q_PER_KV * HEAD_DIM
    assert seq % T_BLK == 0 and T_BLK % CHUNK == 0

    x2 = x.reshape(bsz * seq, d_model)
    mod, mod_f = _ada_mod(c, w_ada, b_ada, w_ada_final, b_ada_final)
    cos_t, sin_t = _rope_tables(seq)
    out = _layer(x2, sinks.reshape(-1), mod, mod_f, norm_g, final_norm_g, cos_t, sin_t,
                 ln_v_g.reshape(1, d_a), ln_v_b.reshape(1, d_a),
                 w_spatial.reshape(A_GROUPS, CHUNK, CHUNK),
                 b_spatial.reshape(A_GROUPS, CHUNK).T,
                 w_in.reshape(d_model, d_in), w_out.reshape(d_mix, d_model),
                 seq, d_a, d_b, d_kv)
    return out.reshape(bsz, seq, d_model)
```

```python
import functools

import jax
import jax.numpy as jnp
from jax import lax
from jax.experimental import pallas as pl
from jax.experimental.pallas import tpu as pltpu

F32 = jnp.float32
BF16 = jnp.bfloat16

CHUNK = 128
A_GROUPS = 8
A_GROUP_W = 128
HEAD_DIM = 64
N_KV_HEADS = 4
Q_PER_KV = 4
ROPE_THETA = 10000.0
NORM_EPS = 1e-5
LOG2E = 1.4426950408889634

LANES = 128
HEADS_PER_VREG = LANES // HEAD_DIM
N_EXP = N_KV_HEADS * HEADS_PER_VREG
COLS_PER_KV = Q_PER_KV // HEADS_PER_VREG

T_BLK = 256
TN_DOT = 512
N_CARRY = 2
W_TILES = 16
TN_ADA = 1024

MIB = 1024 * 1024
V7X_VMEM_BYTES = 64 * MIB
VMEM_LIMIT_ADA = 40 * MIB
VMEM_LIMIT_LAYER = V7X_VMEM_BYTES - MIB


def _silu(z):
    hz = 0.5 * z
    return hz + hz * jnp.tanh(hz)


def _ada_kernel(c_ref, wa_top, wa_bot, wf_top, wf_bot, ba_ref, bf_ref, oa_ref, of_ref,
                *, n_a_tiles):
    j = pl.program_id(0)
    c = c_ref[...]
    ca = (c * (1.0 / (1.0 + jnp.exp(-c)))).astype(BF16)
    half = ca.shape[1] // 2

    def mod(w_top, w_bot, b_ref, o_ref):
        acc = jnp.dot(ca[:, :half], w_top[...].astype(BF16), preferred_element_type=F32)
        acc = acc + jnp.dot(ca[:, half:], w_bot[...].astype(BF16),
                            preferred_element_type=F32)
        o_ref[...] = acc + b_ref[...]

    @pl.when(j < n_a_tiles)
    def _():
        mod(wa_top, wa_bot, ba_ref, oa_ref)

    @pl.when(j >= n_a_tiles)
    def _():
        mod(wf_top, wf_bot, bf_ref, of_ref)


def _ada_mod(c, w_a, b_a, w_f, b_f):
    bsz, d = c.shape
    n_a, n_f = w_a.shape[-1], w_f.shape[-1]
    w_a = w_a.reshape(d, n_a)
    w_f = w_f.reshape(d, n_f)
    ta, tf = n_a // TN_ADA, n_f // TN_ADA
    half = d // 2
    a_tile = lambda j: jnp.minimum(j, ta - 1)
    f_tile = lambda j: jnp.maximum(j - ta, 0)
    return pl.pallas_call(
        functools.partial(_ada_kernel, n_a_tiles=ta),
        grid=(ta + tf,),
        in_specs=[
            pl.BlockSpec((bsz, d), lambda j: (0, 0)),
            pl.BlockSpec((half, TN_ADA), lambda j: (0, a_tile(j))),
            pl.BlockSpec((half, TN_ADA), lambda j: (1, a_tile(j))),
            pl.BlockSpec((half, TN_ADA), lambda j: (0, f_tile(j))),
            pl.BlockSpec((half, TN_ADA), lambda j: (1, f_tile(j))),
            pl.BlockSpec((1, TN_ADA), lambda j: (0, a_tile(j))),
            pl.BlockSpec((1, TN_ADA), lambda j: (0, f_tile(j))),
        ],
        out_specs=[
            pl.BlockSpec((bsz, TN_ADA), lambda j: (0, a_tile(j))),
            pl.BlockSpec((bsz, TN_ADA), lambda j: (0, f_tile(j))),
        ],
        out_shape=[jax.ShapeDtypeStruct((bsz, n_a), F32),
                   jax.ShapeDtypeStruct((bsz, n_f), F32)],
        compiler_params=pltpu.CompilerParams(
            dimension_semantics=("arbitrary",), vmem_limit_bytes=VMEM_LIMIT_ADA),
        name="ada_mod",
    )(c, w_a, w_a, w_f, w_f, b_a.reshape(1, n_a), b_f.reshape(1, n_f))


class _MixerBlock:
    def __init__(self, load, store, cos, sin, lng, lnb, ws_ref, bst, sink_ref, kprev, vprev,
                 has_prev, *, d_a, d_b, d_kv):
        self.load, self.store = load, store
        self.cos, self.sin, self.lng, self.lnb = cos, sin, lng, lnb
        q_scale = LOG2E * HEAD_DIM ** -0.5
        self.cos_q, self.sin_q = cos * q_scale, sin * q_scale
        self.ws_ref, self.bst, self.sink_ref = ws_ref, bst, sink_ref
        self.kprev, self.vprev, self.has_prev = kprev, vprev, has_prev
        self.d_a, self.d_b, self.d_kv = d_a, d_b, d_kv
        self.o_u, self.o_v, self.o_za = 0, d_a, 2 * d_a
        self.o_q = 3 * d_a
        self.o_k = self.o_q + d_b
        self.o_vv = self.o_k + d_kv
        self.o_zb = self.o_vv + d_kv
        row = lax.broadcasted_iota(jnp.int32, (CHUNK, CHUNK), 0)
        col = lax.broadcasted_iota(jnp.int32, (CHUNK, CHUNK), 1)
        self.col = col
        self.causal = col <= row
        self.first_half = (col & (HEAD_DIM - 1)) < (HEAD_DIM // 2)
        self.probs = {}

    def _rope(self, xv, query=False):
        cos, sin = (self.cos_q, self.sin_q) if query else (self.cos, self.sin)
        rot = jnp.where(self.first_half,
                        pltpu.roll(xv, LANES - HEAD_DIM // 2, 1),
                        pltpu.roll(xv, HEAD_DIM // 2, 1))
        return xv * cos + rot * sin

    def prep(self):
        va = self.load(self.o_v, self.o_v + self.d_a).astype(F32)
        mu = jnp.mean(va, axis=-1, keepdims=True)
        vc = va - mu
        var = jnp.mean(vc * vc, axis=-1, keepdims=True)
        self.vn = (vc * lax.rsqrt(var + NORM_EPS) * self.lng + self.lnb).astype(BF16)
        self.tril = self.causal.astype(F32)
        low_half = self.col < HEAD_DIM
        self.ke = [None] * N_EXP
        self.ve = [None] * N_EXP
        for c in range(self.d_kv // LANES):
            kc = self._rope(self.load(self.o_k + c * LANES, self.o_k + (c + 1) * LANES)
                            .astype(F32))
            vcol = self.load(self.o_vv + c * LANES, self.o_vv + (c + 1) * LANES).astype(F32)
            kc_sw = pltpu.roll(kc, HEAD_DIM, 1)
            vcol_sw = pltpu.roll(vcol, HEAD_DIM, 1)
            for j in range(HEADS_PER_VREG):
                kv_head = c * HEADS_PER_VREG + j
                for o in range(HEADS_PER_VREG):
                    mask = low_half if o == 0 else jnp.logical_not(low_half)
                    e = kv_head * HEADS_PER_VREG + o
                    self.ke[e] = jnp.where(mask, kc if o == j else kc_sw, 0.0).astype(BF16)
                    self.ve[e] = jnp.where(mask, vcol if o == j else vcol_sw, 0.0).astype(BF16)

    def group_pair(self, g0):
        c0 = g0 * A_GROUP_W
        w = jnp.concatenate([(self.ws_ref[g0] * self.tril).astype(BF16),
                             (self.ws_ref[g0 + 1] * self.tril).astype(BF16)], axis=1)
        zero = jnp.zeros((CHUNK, A_GROUP_W), BF16)
        rhs = jnp.concatenate(
            [jnp.concatenate([self.vn[:, c0:c0 + A_GROUP_W], zero], axis=1),
             jnp.concatenate([zero, self.vn[:, c0 + A_GROUP_W:c0 + 2 * A_GROUP_W]], axis=1)],
            axis=0)
        s2 = jnp.dot(w, rhs, preferred_element_type=F32)
        for k in range(2):
            g = g0 + k
            cg = g * A_GROUP_W
            s = s2[:, k * A_GROUP_W:(k + 1) * A_GROUP_W] + self.bst[:, g:g + 1]
            u = self.load(self.o_u + cg, self.o_u + cg + A_GROUP_W).astype(F32)
            za = self.load(self.o_za + cg, self.o_za + cg + A_GROUP_W).astype(F32)
            self.store(cg, (u * s * _silu(za)).astype(BF16))

    def scores(self, p0):
        kv_head = (p0 * HEADS_PER_VREG) // Q_PER_KV
        kprev = self.kprev()
        q4 = jnp.concatenate(
            [self._rope(self.load(self.o_q + p * LANES, self.o_q + (p + 1) * LANES)
                        .astype(F32), query=True).astype(BF16)
             for p in range(p0, p0 + COLS_PER_KV)], axis=0)
        probs = [[] for _ in range(COLS_PER_KV)]
        for o in range(HEADS_PER_VREG):
            e = kv_head * HEADS_PER_VREG + o
            kband = jnp.concatenate([kprev[e], self.ke[e]], axis=0)
            s4 = lax.dot_general(q4, kband, (((1,), (1,)), ((), ())),
                                 preferred_element_type=F32)
            for a in range(COLS_PER_KV):
                s = s4[a * CHUNK:(a + 1) * CHUNK]
                h = (p0 + a) * HEADS_PER_VREG + o
                s_prev = s[:, :CHUNK]
                if self.has_prev is not True:
                    s_prev = jnp.where(self.has_prev, s_prev, -jnp.inf)
                comb = jnp.where(self.causal, s[:, CHUNK:], s_prev)
                sink = self.sink_ref[h] * LOG2E
                m = jnp.maximum(jnp.max(comb, axis=-1, keepdims=True), sink)
                pexp = jnp.exp2(comb - m)
                denom = jnp.sum(pexp, axis=-1, keepdims=True) + jnp.exp2(sink - m)
                pn = (pexp * (1.0 / denom)).astype(BF16)
                zero = jnp.zeros_like(pn)
                probs[a].append(jnp.where(self.causal, zero, pn))
                probs[a].append(jnp.where(self.causal, pn, zero))
        self.probs[p0] = jnp.concatenate(
            [jnp.concatenate(pa, axis=1) for pa in probs], axis=0)

    def values(self, p0):
        kv_head = (p0 * HEADS_PER_VREG) // Q_PER_KV
        vprev = self.vprev()
        vband = []
        for o in range(HEADS_PER_VREG):
            e = kv_head * HEADS_PER_VREG + o
            vband.append(vprev[e])
            vband.append(self.ve[e])
        out4 = jnp.dot(self.probs.pop(p0), jnp.concatenate(vband, axis=0),
                       preferred_element_type=F32)
        for a in range(COLS_PER_KV):
            p = p0 + a
            zb = self.load(self.o_zb + p * LANES, self.o_zb + (p + 1) * LANES).astype(F32)
            self.store(self.d_a + p * LANES,
                       (out4[a * CHUNK:(a + 1) * CHUNK] * _silu(zb)).astype(BF16))


def _layer_kernel(sink_ref, x_ref, moda_ref, modc_ref, modf_ref, ng_ref, fg_ref,
                  cos_ref, sin_ref, lng_ref, lnb_ref, ws_ref, bst_ref, win32_ref, wout32_ref,
                  o_ref, win_scr, wout_scr, proj0_scr, proj1_scr, y_scr, xprev_scr, h_scr,
                  kprev_scr, vprev_scr,
                  *, d_model, d_in, d_a, d_b, d_kv, blocks_per_batch, n_blk):
    i = pl.program_id(0)

    @pl.when(i < W_TILES)
    def _():
        r_in = pl.multiple_of(i * win32_ref.shape[0], win32_ref.shape[0])
        win_scr[pl.ds(r_in, win32_ref.shape[0]), :] = win32_ref[...].astype(BF16)
        r_out = pl.multiple_of(i * wout32_ref.shape[0], wout32_ref.shape[0])
        wout_scr[pl.ds(r_out, wout32_ref.shape[0]), :] = wout32_ref[...].astype(BF16)

    step = i - W_TILES

    args = (sink_ref, x_ref, moda_ref, modc_ref, modf_ref, ng_ref, fg_ref, cos_ref, sin_ref,
            lng_ref, lnb_ref, ws_ref, bst_ref, win_scr, wout_scr, o_ref)
    scr = (y_scr, xprev_scr, h_scr, kprev_scr, vprev_scr)
    dims = dict(d_model=d_model, d_in=d_in, d_a=d_a, d_b=d_b, d_kv=d_kv,
                blocks_per_batch=blocks_per_batch, n_blk=n_blk)
    bufs = ((proj0_scr, proj1_scr), (proj1_scr, proj0_scr))
    inner = jnp.logical_and(step > 0, step < n_blk)

    @pl.when(step == 0)
    def _():
        kprev_scr[...] = jnp.zeros_like(kprev_scr)
        vprev_scr[...] = jnp.zeros_like(vprev_scr)
        _layer_step(step, *args, *bufs[0], *scr, **dims, do_bc=False)

    @pl.when(jnp.logical_and(inner, step % 2 == 0))
    def _():
        _layer_step(step, *args, *bufs[0], *scr, **dims)

    @pl.when(jnp.logical_and(inner, step % 2 == 1))
    def _():
        _layer_step(step, *args, *bufs[1], *scr, **dims)

    @pl.when(step == n_blk)
    def _():
        _layer_step(step, *args, *bufs[n_blk % 2], *scr, **dims, do_a=False)


def _layer_step(step, sink_ref, x_ref, moda_ref, modc_ref, modf_ref, ng_ref, fg_ref,
                cos_ref, sin_ref, lng_ref, lnb_ref, ws_ref, bst_ref, win_ref, wout_ref,
                o_ref, proj_w, proj_r, y_scr, xprev_scr, h_scr, kprev_scr, vprev_scr,
                *, d_model, d_in, d_a, d_b, d_kv, blocks_per_batch, n_blk,
                do_a=True, do_bc=True):
    blk_c = jnp.clip(step - 1, 0, n_blk - 1)
    first_in_batch = (blk_c % blocks_per_batch) == 0

    chunks = list(range(0, d_in, TN_DOT))
    carried = chunks[len(chunks) - N_CARRY:]
    if do_bc:
        for n0 in carried:
            acc = jnp.dot(h_scr[...], win_ref[:, n0:n0 + TN_DOT], preferred_element_type=F32)
            proj_r[:, n0:n0 + TN_DOT] = acc.astype(BF16)

    if do_a:
        x = x_ref[...]
        ms = jnp.mean(x * x, axis=-1, keepdims=True)
        row_a = pl.ds(jnp.minimum(step, n_blk - 1) // blocks_per_batch, 1)
        shift = moda_ref[row_a, 0:d_model]
        gain = ng_ref[...] * (1.0 + moda_ref[row_a, d_model:2 * d_model])
        h_scr[...] = (x * lax.rsqrt(ms + NORM_EPS) * gain + shift).astype(BF16)

    lng = lng_ref[...]
    lnb = lnb_ref[...]
    bst = bst_ref[...]
    n_sub = T_BLK // CHUNK
    blocks = []
    for sb in range(n_sub):
        r0 = sb * CHUNK

        def load(c0, c1, r0=r0):
            return proj_r[r0:r0 + CHUNK, c0:c1]

        def store(c0, val, r0=r0):
            y_scr[r0:r0 + CHUNK, c0:c0 + val.shape[1]] = val

        if sb == 0:
            kprev = lambda: [kprev_scr[e] for e in range(N_EXP)]
            vprev = lambda: [vprev_scr[e] for e in range(N_EXP)]
            has_prev = jnp.logical_not(first_in_batch)
        else:
            kprev = lambda b=blocks[sb - 1]: b.ke
            vprev = lambda b=blocks[sb - 1]: b.ve
            has_prev = True
        blocks.append(_MixerBlock(
            load, store, cos_ref[r0:r0 + CHUNK, :], sin_ref[r0:r0 + CHUNK, :], lng, lnb,
            ws_ref, bst, sink_ref, kprev, vprev, has_prev, d_a=d_a, d_b=d_b, d_kv=d_kv))

    def proj_chunk(n0):
        acc = jnp.dot(h_scr[...], win_ref[:, n0:n0 + TN_DOT], preferred_element_type=F32)
        proj_w[:, n0:n0 + TN_DOT] = acc.astype(BF16)

    chunks = chunks[:len(chunks) - N_CARRY]
    next_chunk = [0]

    def emit_chunks(n):
        for _ in range(n):
            if do_a and next_chunk[0] < len(chunks):
                proj_chunk(chunks[next_chunk[0]])
                next_chunk[0] += 1

    if not do_bc:
        emit_chunks(len(chunks))
        xprev_scr[...] = x
        return

    for b in blocks:
        b.prep()
    n_pairs = d_b // LANES
    slots = [(b, p0) for b in blocks for p0 in range(0, n_pairs, COLS_PER_KV)]
    group_pairs = [(b, g0) for b in blocks for g0 in range(0, A_GROUPS, 2)]
    pairs_per_slot = -(-len(group_pairs) // len(slots))
    slots[0][0].scores(slots[0][1])
    for j, (b, p0) in enumerate(slots):
        emit_chunks(1)
        b.values(p0)
        if j + 1 < len(slots):
            slots[j + 1][0].scores(slots[j + 1][1])
        for bg, g0 in group_pairs[j * pairs_per_slot:(j + 1) * pairs_per_slot]:
            bg.group_pair(g0)
    for e in range(N_EXP):
        kprev_scr[e] = blocks[-1].ke[e]
        vprev_scr[e] = blocks[-1].ve[e]
    emit_chunks(len(chunks) - next_chunk[0] - 1)

    row_c = pl.ds(blk_c // blocks_per_batch, 1)
    gate = modc_ref[row_c, 2 * d_model:3 * d_model]
    shift_f = modf_ref[row_c, 0:d_model]
    gain_f = fg_ref[...] * (1.0 + modf_ref[row_c, d_model:2 * d_model])
    ssq = jnp.zeros((T_BLK, 1), F32)
    for n0 in range(0, d_model, TN_DOT):
        sl = slice(n0, n0 + TN_DOT)
        acc = jnp.dot(y_scr[...], wout_ref[:, sl], preferred_element_type=F32)
        xr = xprev_scr[:, sl] + gate[:, sl] * acc
        ssq = ssq + jnp.sum(xr * xr, axis=-1, keepdims=True)
        o_ref[:, sl] = xr * gain_f[:, sl]
    emit_chunks(len(chunks))
    inv = lax.rsqrt(ssq * (1.0 / d_model) + NORM_EPS)
    o_ref[...] = o_ref[...] * inv + shift_f

    if do_a:
        xprev_scr[...] = x


def _layer(x2, sinks, mod, mod_f, norm_g, final_g, cos_t, sin_t, ln_g, ln_b, w_sp, b_sp_t,
           w_in, w_out, seq, d_a, d_b, d_kv):
    rows, d_model = x2.shape
    d_in = w_in.shape[-1]
    d_mix = d_a + d_b
    n_blk = rows // T_BLK
    bpb = seq // T_BLK
    assert d_model % W_TILES == 0 and d_mix % W_TILES == 0

    def blk_a(i):
        return jnp.clip(i - W_TILES, 0, n_blk - 1)

    def blk_c(i):
        return jnp.clip(i - W_TILES - 1, 0, n_blk - 1)

    def w_tile(i):
        return (jnp.minimum(i, W_TILES - 1), 0)

    const2 = lambda i: (0, 0)
    kern = functools.partial(_layer_kernel, d_model=d_model, d_in=d_in, d_a=d_a, d_b=d_b,
                             d_kv=d_kv, blocks_per_batch=bpb, n_blk=n_blk)
    return pl.pallas_call(
        kern,
        grid=(W_TILES + n_blk + 1,),
        in_specs=[
            pl.BlockSpec(memory_space=pltpu.SMEM),
            pl.BlockSpec((T_BLK, d_model), lambda i: (blk_a(i), 0)),
            pl.BlockSpec(mod.shape, const2),
            pl.BlockSpec(mod.shape, const2),
            pl.BlockSpec(mod_f.shape, const2),
            pl.BlockSpec((1, d_model), const2),
            pl.BlockSpec((1, d_model), const2),
            pl.BlockSpec((T_BLK, LANES), lambda i: (blk_c(i) % bpb, 0)),
            pl.BlockSpec((T_BLK, LANES), lambda i: (blk_c(i) % bpb, 0)),
            pl.BlockSpec((1, d_a), const2),
            pl.BlockSpec((1, d_a), const2),
            pl.BlockSpec((A_GROUPS, CHUNK, CHUNK), lambda i: (0, 0, 0)),
            pl.BlockSpec((CHUNK, A_GROUPS), const2),
            pl.BlockSpec((d_model // W_TILES, d_in), w_tile),
            pl.BlockSpec((d_mix // W_TILES, d_model), w_tile),
        ],
        out_specs=pl.BlockSpec((T_BLK, d_model), lambda i: (blk_c(i), 0)),
        out_shape=jax.ShapeDtypeStruct((rows, d_model), F32),
        scratch_shapes=[
            pltpu.VMEM((d_model, d_in), BF16),
            pltpu.VMEM((d_mix, d_model), BF16),
            pltpu.VMEM((T_BLK, d_in), BF16),
            pltpu.VMEM((T_BLK, d_in), BF16),
            pltpu.VMEM((T_BLK, d_mix), BF16),
            pltpu.VMEM((T_BLK, d_model), F32),
            pltpu.VMEM((T_BLK, d_model), BF16),
            pltpu.VMEM((N_EXP, CHUNK, LANES), BF16),
            pltpu.VMEM((N_EXP, CHUNK, LANES), BF16),
        ],
        compiler_params=pltpu.CompilerParams(
            dimension_semantics=("arbitrary",), vmem_limit_bytes=VMEM_LIMIT_LAYER),
        name="layer",
    )(sinks, x2, mod, mod, mod_f, norm_g.reshape(1, d_model), final_g.reshape(1, d_model),
      cos_t, sin_t, ln_g, ln_b, w_sp, b_sp_t, w_in, w_out)


def _rope_tables(seq):
    half = HEAD_DIM // 2
    inv_freq = ROPE_THETA ** (-jnp.arange(0, HEAD_DIM, 2, dtype=F32) / HEAD_DIM)
    ang = jnp.arange(seq, dtype=F32)[:, None] * inv_freq[None, :]
    cos = jnp.cos(ang)
    sin = jnp.sin(ang)
    cos_t = jnp.tile(cos, (1, LANES // half))
    sin_t = jnp.tile(jnp.concatenate([-sin, sin], axis=1), (1, HEADS_PER_VREG))
    return cos_t, sin_t


def kernel(x, c, w_ada, b_ada, norm_g, w_in, ln_v_g, ln_v_b, w_spatial, b_spatial, sinks,
           w_out, w_ada_final, b_ada_final, final_norm_g):
    bsz, seq, d_model = x.shape
    assert w_ada.shape[0] == 1, "single-layer stack"
    d_a = ln_v_g.shape[-1]
    d_mix = w_out.shape[-2]
    d_b = d_mix - d_a
    d_kv = N_KV_HEADS * HEAD_DIM
    d_in = w_in.shape[-1]
    assert d_in == 3 * d_a + 2 * d_b + 2 * d_kv
    assert d_a == A_GROUPS * A_GROUP_W and d_b == N_KV_HEADS * Q_PER_KV * HEAD_DIM
    assert seq % T_BLK == 0 and T_BLK % CHUNK == 0

    x2 = x.reshape(bsz * seq, d_model)
    mod, mod_f = _ada_mod(c, w_ada, b_ada, w_ada_final, b_ada_final)
    cos_t, sin_t = _rope_tables(seq)
    out = _layer(x2, sinks.reshape(-1), mod, mod_f, norm_g, final_norm_g, cos_t, sin_t,
                 ln_v_g.reshape(1, d_a), ln_v_b.reshape(1, d_a),
                 w_spatial.reshape(A_GROUPS, CHUNK, CHUNK),
                 b_spatial.reshape(A_GROUPS, CHUNK).T,
                 w_in.reshape(d_model, d_in), w_out.reshape(d_mix, d_model),
                 seq, d_a, d_b, d_kv)
    return out.reshape(bsz, seq, d_model)
```

```python
import functools

import jax
import jax.numpy as jnp
import numpy as np
from jax import lax
from jax.experimental import pallas as pl
from jax.experimental.pallas import tpu as pltpu

F32 = jnp.float32
BF16 = jnp.bfloat16

CHUNK = 128
A_GROUPS = 8
A_GROUP_W = 128
HEAD_DIM = 64
N_KV_HEADS = 4
Q_PER_KV = 4
ROPE_THETA = 10000.0
NORM_EPS = 1e-5
LOG2E = 1.4426950408889634

LANES = 128
HEADS_PER_VREG = LANES // HEAD_DIM
N_EXP = N_KV_HEADS * HEADS_PER_VREG
COLS_PER_KV = Q_PER_KV // HEADS_PER_VREG

T_BLK = 256
TN_DOT = 512
N_CARRY = 2
W_TILES = 16
TN_ADA = 1024

MIB = 1024 * 1024
V7X_VMEM_BYTES = 64 * MIB
VMEM_LIMIT_ADA = 40 * MIB
VMEM_LIMIT_LAYER = V7X_VMEM_BYTES - MIB


def _silu(z):
    hz = 0.5 * z
    return hz + hz * jnp.tanh(hz)


def _ada_kernel(c_ref, wa_top, wa_bot, wf_top, wf_bot, ba_ref, bf_ref, oa_ref, of_ref,
                *, n_a_tiles):
    j = pl.program_id(0)
    c = c_ref[...]
    ca = (c * (1.0 / (1.0 + jnp.exp(-c)))).astype(BF16)
    half = ca.shape[1] // 2

    def mod(w_top, w_bot, b_ref, o_ref):
        acc = jnp.dot(ca[:, :half], w_top[...].astype(BF16), preferred_element_type=F32)
        acc = acc + jnp.dot(ca[:, half:], w_bot[...].astype(BF16),
                            preferred_element_type=F32)
        o_ref[...] = acc + b_ref[...]

    @pl.when(j < n_a_tiles)
    def _():
        mod(wa_top, wa_bot, ba_ref, oa_ref)

    @pl.when(j >= n_a_tiles)
    def _():
        mod(wf_top, wf_bot, bf_ref, of_ref)


def _ada_mod(c, w_a, b_a, w_f, b_f):
    bsz, d = c.shape
    n_a, n_f = w_a.shape[-1], w_f.shape[-1]
    w_a = w_a.reshape(d, n_a)
    w_f = w_f.reshape(d, n_f)
    ta, tf = n_a // TN_ADA, n_f // TN_ADA
    half = d // 2
    a_tile = lambda j: jnp.minimum(j, ta - 1)
    f_tile = lambda j: jnp.maximum(j - ta, 0)
    return pl.pallas_call(
        functools.partial(_ada_kernel, n_a_tiles=ta),
        grid=(ta + tf,),
        in_specs=[
            pl.BlockSpec((bsz, d), lambda j: (0, 0)),
            pl.BlockSpec((half, TN_ADA), lambda j: (0, a_tile(j))),
            pl.BlockSpec((half, TN_ADA), lambda j: (1, a_tile(j))),
            pl.BlockSpec((half, TN_ADA), lambda j: (0, f_tile(j))),
            pl.BlockSpec((half, TN_ADA), lambda j: (1, f_tile(j))),
            pl.BlockSpec((1, TN_ADA), lambda j: (0, a_tile(j))),
            pl.BlockSpec((1, TN_ADA), lambda j: (0, f_tile(j))),
        ],
        out_specs=[
            pl.BlockSpec((bsz, TN_ADA), lambda j: (0, a_tile(j))),
            pl.BlockSpec((bsz, TN_ADA), lambda j: (0, f_tile(j))),
        ],
        out_shape=[jax.ShapeDtypeStruct((bsz, n_a), F32),
                   jax.ShapeDtypeStruct((bsz, n_f), F32)],
        compiler_params=pltpu.CompilerParams(
            dimension_semantics=("arbitrary",), vmem_limit_bytes=VMEM_LIMIT_ADA),
        name="ada_mod",
    )(c, w_a, w_a, w_f, w_f, b_a.reshape(1, n_a), b_f.reshape(1, n_f))


class _MixerBlock:
    def __init__(self, load, store, cos, sin, lng, lnb, ws_ref, bst, sink_ref, kprev, vprev,
                 has_prev, *, d_a, d_b, d_kv):
        self.load, self.store = load, store
        self.cos, self.sin, self.lng, self.lnb = cos, sin, lng, lnb
        q_scale = LOG2E * HEAD_DIM ** -0.5
        self.cos_q, self.sin_q = cos * q_scale, sin * q_scale
        self.ws_ref, self.bst, self.sink_ref = ws_ref, bst, sink_ref
        self.kprev, self.vprev, self.has_prev = kprev, vprev, has_prev
        self.d_a, self.d_b, self.d_kv = d_a, d_b, d_kv
        self.o_u, self.o_v, self.o_za = 0, d_a, 2 * d_a
        self.o_q = 3 * d_a
        self.o_k = self.o_q + d_b
        self.o_vv = self.o_k + d_kv
        self.o_zb = self.o_vv + d_kv
        row = lax.broadcasted_iota(jnp.int32, (CHUNK, CHUNK), 0)
        col = lax.broadcasted_iota(jnp.int32, (CHUNK, CHUNK), 1)
        self.col = col
        self.causal = col <= row
        self.first_half = (col & (HEAD_DIM - 1)) < (HEAD_DIM // 2)
        self.probs = {}

    def _rope(self, xv, query=False):
        cos, sin = (self.cos_q, self.sin_q) if query else (self.cos, self.sin)
        rot = jnp.where(self.first_half,
                        pltpu.roll(xv, LANES - HEAD_DIM // 2, 1),
                        pltpu.roll(xv, HEAD_DIM // 2, 1))
        return xv * cos + rot * sin

    def prep(self):
        va = self.load(self.o_v, self.o_v + self.d_a).astype(F32)
        mu = jnp.mean(va, axis=-1, keepdims=True)
        vc = va - mu
        var = jnp.mean(vc * vc, axis=-1, keepdims=True)
        self.vn = (vc * lax.rsqrt(var + NORM_EPS) * self.lng + self.lnb).astype(BF16)
        self.tril = self.causal.astype(F32)
        low_half = self.col < HEAD_DIM
        self.ke = [None] * N_EXP
        self.ve = [None] * N_EXP
        for c in range(self.d_kv // LANES):
            kc = self._rope(self.load(self.o_k + c * LANES, self.o_k + (c + 1) * LANES)
                            .astype(F32))
            vcol = self.load(self.o_vv + c * LANES, self.o_vv + (c + 1) * LANES).astype(F32)
            kc_sw = pltpu.roll(kc, HEAD_DIM, 1)
            vcol_sw = pltpu.roll(vcol, HEAD_DIM, 1)
            for j in range(HEADS_PER_VREG):
                kv_head = c * HEADS_PER_VREG + j
                for o in range(HEADS_PER_VREG):
                    mask = low_half if o == 0 else jnp.logical_not(low_half)
                    e = kv_head * HEADS_PER_VREG + o
                    self.ke[e] = jnp.where(mask, kc if o == j else kc_sw, 0.0).astype(BF16)
                    self.ve[e] = jnp.where(mask, vcol if o == j else vcol_sw, 0.0).astype(BF16)

    def group_pair(self, g0):
        c0 = g0 * A_GROUP_W
        w = jnp.concatenate([(self.ws_ref[g0] * self.tril).astype(BF16),
                             (self.ws_ref[g0 + 1] * self.tril).astype(BF16)], axis=1)
        zero = jnp.zeros((CHUNK, A_GROUP_W), BF16)
        rhs = jnp.concatenate(
            [jnp.concatenate([self.vn[:, c0:c0 + A_GROUP_W], zero], axis=1),
             jnp.concatenate([zero, self.vn[:, c0 + A_GROUP_W:c0 + 2 * A_GROUP_W]], axis=1)],
            axis=0)
        s2 = jnp.dot(w, rhs, preferred_element_type=F32)
        for k in range(2):
            g = g0 + k
            cg = g * A_GROUP_W
            s = s2[:, k * A_GROUP_W:(k + 1) * A_GROUP_W] + self.bst[:, g:g + 1]
            u = self.load(self.o_u + cg, self.o_u + cg + A_GROUP_W).astype(F32)
            za = self.load(self.o_za + cg, self.o_za + cg + A_GROUP_W).astype(F32)
            self.store(cg, (u * s * _silu(za)).astype(BF16))

    def scores(self, p0):
        kv_head = (p0 * HEADS_PER_VREG) // Q_PER_KV
        kprev = self.kprev()
        q4 = jnp.concatenate(
            [self._rope(self.load(self.o_q + p * LANES, self.o_q + (p + 1) * LANES)
                        .astype(F32), query=True).astype(BF16)
             for p in range(p0, p0 + COLS_PER_KV)], axis=0)
        probs = [[] for _ in range(COLS_PER_KV)]
        for o in range(HEADS_PER_VREG):
            e = kv_head * HEADS_PER_VREG + o
            kband = jnp.concatenate([kprev[e], self.ke[e]], axis=0)
            s4 = lax.dot_general(q4, kband, (((1,), (1,)), ((), ())),
                                 preferred_element_type=F32)
            for a in range(COLS_PER_KV):
                s = s4[a * CHUNK:(a + 1) * CHUNK]
                h = (p0 + a) * HEADS_PER_VREG + o
                s_prev = s[:, :CHUNK]
                if self.has_prev is not True:
                    s_prev = jnp.where(self.has_prev, s_prev, -jnp.inf)
                comb = jnp.where(self.causal, s[:, CHUNK:], s_prev)
                sink = self.sink_ref[h] * LOG2E
                m = jnp.maximum(jnp.max(comb, axis=-1, keepdims=True), sink)
                pexp = jnp.exp2(comb - m)
                denom = jnp.sum(pexp, axis=-1, keepdims=True) + jnp.exp2(sink - m)
                pn = (pexp * (1.0 / denom)).astype(BF16)
                zero = jnp.zeros_like(pn)
                probs[a].append(jnp.where(self.causal, zero, pn))
                probs[a].append(jnp.where(self.causal, pn, zero))
        self.probs[p0] = jnp.concatenate(
            [jnp.concatenate(pa, axis=1) for pa in probs], axis=0)

    def values(self, p0):
        kv_head = (p0 * HEADS_PER_VREG) // Q_PER_KV
        vprev = self.vprev()
        vband = []
        for o in range(HEADS_PER_VREG):
            e = kv_head * HEADS_PER_VREG + o
            vband.append(vprev[e])
            vband.append(self.ve[e])
        out4 = jnp.dot(self.probs.pop(p0), jnp.concatenate(vband, axis=0),
                       preferred_element_type=F32)
        for a in range(COLS_PER_KV):
            p = p0 + a
            zb = self.load(self.o_zb + p * LANES, self.o_zb + (p + 1) * LANES).astype(F32)
            self.store(self.d_a + p * LANES,
                       (out4[a * CHUNK:(a + 1) * CHUNK] * _silu(zb)).astype(BF16))


def _layer_kernel(sink_ref, x_ref, moda_ref, modc_ref, modf_ref, ng_ref, fg_ref,
                  cos_ref, sin_ref, lng_ref, lnb_ref, ws_ref, bst_ref, win32_ref, wout32_ref,
                  o_ref, win_scr, wout_scr, proj0_scr, proj1_scr, y_scr, xprev_scr, h_scr,
                  kprev_scr, vprev_scr,
                  *, d_model, d_in, d_a, d_b, d_kv, blocks_per_batch, n_blk):
    i = pl.program_id(0)

    @pl.when(i < W_TILES)
    def _():
        r_in = pl.multiple_of(i * win32_ref.shape[0], win32_ref.shape[0])
        win_scr[pl.ds(r_in, win32_ref.shape[0]), :] = win32_ref[...].astype(BF16)
        r_out = pl.multiple_of(i * wout32_ref.shape[0], wout32_ref.shape[0])
        wout_scr[pl.ds(r_out, wout32_ref.shape[0]), :] = wout32_ref[...].astype(BF16)

    step = i - W_TILES

    args = (sink_ref, x_ref, moda_ref, modc_ref, modf_ref, ng_ref, fg_ref, cos_ref, sin_ref,
            lng_ref, lnb_ref, ws_ref, bst_ref, win_scr, wout_scr, o_ref)
    scr = (y_scr, xprev_scr, h_scr, kprev_scr, vprev_scr)
    dims = dict(d_model=d_model, d_in=d_in, d_a=d_a, d_b=d_b, d_kv=d_kv,
                blocks_per_batch=blocks_per_batch, n_blk=n_blk)
    bufs = ((proj0_scr, proj1_scr), (proj1_scr, proj0_scr))
    inner = jnp.logical_and(step > 0, step < n_blk)

    @pl.when(step == 0)
    def _():
        kprev_scr[...] = jnp.zeros_like(kprev_scr)
        vprev_scr[...] = jnp.zeros_like(vprev_scr)
        _layer_step(step, *args, *bufs[0], *scr, **dims, do_bc=False)

    @pl.when(jnp.logical_and(inner, step % 2 == 0))
    def _():
        _layer_step(step, *args, *bufs[0], *scr, **dims)

    @pl.when(jnp.logical_and(inner, step % 2 == 1))
    def _():
        _layer_step(step, *args, *bufs[1], *scr, **dims)

    @pl.when(step == n_blk)
    def _():
        _layer_step(step, *args, *bufs[n_blk % 2], *scr, **dims, do_a=False)


def _layer_step(step, sink_ref, x_ref, moda_ref, modc_ref, modf_ref, ng_ref, fg_ref,
                cos_ref, sin_ref, lng_ref, lnb_ref, ws_ref, bst_ref, win_ref, wout_ref,
                o_ref, proj_w, proj_r, y_scr, xprev_scr, h_scr, kprev_scr, vprev_scr,
                *, d_model, d_in, d_a, d_b, d_kv, blocks_per_batch, n_blk,
                do_a=True, do_bc=True):
    blk_c = jnp.clip(step - 1, 0, n_blk - 1)
    first_in_batch = (blk_c % blocks_per_batch) == 0

    chunks = list(range(0, d_in, TN_DOT))
    carried = chunks[len(chunks) - N_CARRY:]
    if do_bc:
        for n0 in carried:
            acc = jnp.dot(h_scr[...], win_ref[:, n0:n0 + TN_DOT], preferred_element_type=F32)
            proj_r[:, n0:n0 + TN_DOT] = acc.astype(BF16)

    if do_a:
        x = x_ref[...]
        ms = jnp.mean(x * x, axis=-1, keepdims=True)
        row_a = pl.ds(jnp.minimum(step, n_blk - 1) // blocks_per_batch, 1)
        shift = moda_ref[row_a, 0:d_model]
        gain = ng_ref[...] * (1.0 + moda_ref[row_a, d_model:2 * d_model])
        h_scr[...] = (x * lax.rsqrt(ms + NORM_EPS) * gain + shift).astype(BF16)

    lng = lng_ref[...]
    lnb = lnb_ref[...]
    bst = bst_ref[...]
    n_sub = T_BLK // CHUNK
    blocks = []
    for sb in range(n_sub):
        r0 = sb * CHUNK

        def load(c0, c1, r0=r0):
            return proj_r[r0:r0 + CHUNK, c0:c1]

        def store(c0, val, r0=r0):
            y_scr[r0:r0 + CHUNK, c0:c0 + val.shape[1]] = val

        if sb == 0:
            kprev = lambda: [kprev_scr[e] for e in range(N_EXP)]
            vprev = lambda: [vprev_scr[e] for e in range(N_EXP)]
            has_prev = jnp.logical_not(first_in_batch)
        else:
            kprev = lambda b=blocks[sb - 1]: b.ke
            vprev = lambda b=blocks[sb - 1]: b.ve
            has_prev = True
        blocks.append(_MixerBlock(
            load, store, cos_ref[r0:r0 + CHUNK, :], sin_ref[r0:r0 + CHUNK, :], lng, lnb,
            ws_ref, bst, sink_ref, kprev, vprev, has_prev, d_a=d_a, d_b=d_b, d_kv=d_kv))

    def proj_chunk(n0):
        acc = jnp.dot(h_scr[...], win_ref[:, n0:n0 + TN_DOT], preferred_element_type=F32)
        proj_w[:, n0:n0 + TN_DOT] = acc.astype(BF16)

    chunks = chunks[:len(chunks) - N_CARRY]
    next_chunk = [0]

    def emit_chunks(n):
        for _ in range(n):
            if do_a and next_chunk[0] < len(chunks):
                proj_chunk(chunks[next_chunk[0]])
                next_chunk[0] += 1

    if not do_bc:
        emit_chunks(len(chunks))
        xprev_scr[...] = x
        return

    for b in blocks:
        b.prep()
    n_pairs = d_b // LANES
    slots = [(b, p0) for b in blocks for p0 in range(0, n_pairs, COLS_PER_KV)]
    group_pairs = [(b, g0) for b in blocks for g0 in range(0, A_GROUPS, 2)]
    pairs_per_slot = -(-len(group_pairs) // len(slots))
    slots[0][0].scores(slots[0][1])
    for j, (b, p0) in enumerate(slots):
        emit_chunks(1)
        b.values(p0)
        if j + 1 < len(slots):
            slots[j + 1][0].scores(slots[j + 1][1])
        for bg, g0 in group_pairs[j * pairs_per_slot:(j + 1) * pairs_per_slot]:
            bg.group_pair(g0)
    for e in range(N_EXP):
        kprev_scr[e] = blocks[-1].ke[e]
        vprev_scr[e] = blocks[-1].ve[e]
    emit_chunks(len(chunks) - next_chunk[0] - 1)

    row_c = pl.ds(blk_c // blocks_per_batch, 1)
    gate = modc_ref[row_c, 2 * d_model:3 * d_model]
    ssq = jnp.zeros((T_BLK, 1), F32)
    for n0 in range(0, d_model, TN_DOT):
        sl = slice(n0, n0 + TN_DOT)
        acc = jnp.dot(y_scr[...], wout_ref[:, sl], preferred_element_type=F32)
        xr = xprev_scr[:, sl] + gate[:, sl] * acc
        ssq = ssq + jnp.sum(xr * xr, axis=-1, keepdims=True)
        o_ref[:, sl] = xr
    emit_chunks(len(chunks))
    inv = lax.rsqrt(ssq * (1.0 / d_model) + NORM_EPS)
    shift_f = modf_ref[row_c, 0:d_model]
    scale_f = modf_ref[row_c, d_model:2 * d_model]
    o_ref[...] = o_ref[...] * inv * (fg_ref[...] * (1.0 + scale_f)) + shift_f

    if do_a:
        xprev_scr[...] = x


def _layer(x2, sinks, mod, mod_f, norm_g, final_g, cos_t, sin_t, ln_g, ln_b, w_sp, b_sp_t,
           w_in, w_out, seq, d_a, d_b, d_kv):
    rows, d_model = x2.shape
    d_in = w_in.shape[-1]
    d_mix = d_a + d_b
    n_blk = rows // T_BLK
    bpb = seq // T_BLK
    assert d_model % W_TILES == 0 and d_mix % W_TILES == 0

    def blk_a(i):
        return jnp.clip(i - W_TILES, 0, n_blk - 1)

    def blk_c(i):
        return jnp.clip(i - W_TILES - 1, 0, n_blk - 1)

    def w_tile(i):
        return (jnp.minimum(i, W_TILES - 1), 0)

    const2 = lambda i: (0, 0)
    kern = functools.partial(_layer_kernel, d_model=d_model, d_in=d_in, d_a=d_a, d_b=d_b,
                             d_kv=d_kv, blocks_per_batch=bpb, n_blk=n_blk)
    return pl.pallas_call(
        kern,
        grid=(W_TILES + n_blk + 1,),
        in_specs=[
            pl.BlockSpec(memory_space=pltpu.SMEM),
            pl.BlockSpec((T_BLK, d_model), lambda i: (blk_a(i), 0)),
            pl.BlockSpec(mod.shape, const2),
            pl.BlockSpec(mod.shape, const2),
            pl.BlockSpec(mod_f.shape, const2),
            pl.BlockSpec((1, d_model), const2),
            pl.BlockSpec((1, d_model), const2),
            pl.BlockSpec((T_BLK, LANES), lambda i: (blk_c(i) % bpb, 0)),
            pl.BlockSpec((T_BLK, LANES), lambda i: (blk_c(i) % bpb, 0)),
            pl.BlockSpec((1, d_a), const2),
            pl.BlockSpec((1, d_a), const2),
            pl.BlockSpec((A_GROUPS, CHUNK, CHUNK), lambda i: (0, 0, 0)),
            pl.BlockSpec((CHUNK, A_GROUPS), const2),
            pl.BlockSpec((d_model // W_TILES, d_in), w_tile),
            pl.BlockSpec((d_mix // W_TILES, d_model), w_tile),
        ],
        out_specs=pl.BlockSpec((T_BLK, d_model), lambda i: (blk_c(i), 0)),
        out_shape=jax.ShapeDtypeStruct((rows, d_model), F32),
        scratch_shapes=[
            pltpu.VMEM((d_model, d_in), BF16),
            pltpu.VMEM((d_mix, d_model), BF16),
            pltpu.VMEM((T_BLK, d_in), BF16),
            pltpu.VMEM((T_BLK, d_in), BF16),
            pltpu.VMEM((T_BLK, d_mix), BF16),
            pltpu.VMEM((T_BLK, d_model), F32),
            pltpu.VMEM((T_BLK, d_model), BF16),
            pltpu.VMEM((N_EXP, CHUNK, LANES), BF16),
            pltpu.VMEM((N_EXP, CHUNK, LANES), BF16),
        ],
        compiler_params=pltpu.CompilerParams(
            dimension_semantics=("arbitrary",), vmem_limit_bytes=VMEM_LIMIT_LAYER),
        name="layer",
    )(sinks, x2, mod, mod, mod_f, norm_g.reshape(1, d_model), final_g.reshape(1, d_model),
      cos_t, sin_t, ln_g, ln_b, w_sp, b_sp_t, w_in, w_out)


def _rope_tables(seq):
    half = HEAD_DIM // 2
    inv_freq = ROPE_THETA ** (-np.arange(0, HEAD_DIM, 2, dtype=np.float64) / HEAD_DIM)
    ang = np.arange(seq, dtype=np.float64)[:, None] * inv_freq[None, :]
    cos = np.cos(ang)
    sin = np.sin(ang)
    cos_t = np.tile(cos, (1, LANES // half))
    sin_t = np.tile(np.concatenate([-sin, sin], axis=1), (1, HEADS_PER_VREG))
    return jnp.asarray(cos_t, F32), jnp.asarray(sin_t, F32)


def kernel(x, c, w_ada, b_ada, norm_g, w_in, ln_v_g, ln_v_b, w_spatial, b_spatial, sinks,
           w_out, w_ada_final, b_ada_final, final_norm_g):
    bsz, seq, d_model = x.shape
    assert w_ada.shape[0] == 1, "single-layer stack"
    d_a = ln_v_g.shape[-1]
    d_mix = w_out.shape[-2]
    d_b = d_mix - d_a
    d_kv = N_KV_HEADS * HEAD_DIM
    d_in = w_in.shape[-1]
    assert d_in == 3 * d_a + 2 * d_b + 2 * d_kv
    assert d_a == A_GROUPS * A_GROUP_W and d_b == N_KV_HEADS * Q_PER_KV * HEAD_DIM
    assert seq % T_BLK == 0 and T_BLK % CHUNK == 0

    x2 = x.reshape(bsz * seq, d_model)
    mod, mod_f = _ada_mod(c, w_ada, b_ada, w_ada_final, b_ada_final)
    cos_t, sin_t = _rope_tables(seq)
    out = _layer(x2, sinks.reshape(-1), mod, mod_f, norm_g, final_norm_g, cos_t, sin_t,
                 ln_v_g.reshape(1, d_a), ln_v_b.reshape(1, d_a),
                 w_spatial.reshape(A_GROUPS, CHUNK, CHUNK),
                 b_spatial.reshape(A_GROUPS, CHUNK).T,
                 w_in.reshape(d_model, d_in), w_out.reshape(d_mix, d_model),
                 seq, d_a, d_b, d_kv)
    return out.reshape(bsz, seq, d_model)
```

```python
import functools

import jax
import jax.numpy as jnp
import numpy as np
from jax import lax
from jax.experimental import pallas as pl
from jax.experimental.pallas import tpu as pltpu

F32 = jnp.float32
BF16 = jnp.bfloat16

CHUNK = 128
A_GROUPS = 8
A_GROUP_W = 128
HEAD_DIM = 64
N_KV_HEADS = 4
Q_PER_KV = 4
ROPE_THETA = 10000.0
NORM_EPS = 1e-5
LOG2E = 1.4426950408889634

LANES = 128
HEADS_PER_VREG = LANES // HEAD_DIM
N_EXP = N_KV_HEADS * HEADS_PER_VREG
COLS_PER_KV = Q_PER_KV // HEADS_PER_VREG

T_BLK = 256
TN_DOT = 512
N_CARRY = 2
W_TILES = 16
TN_ADA = 1024

MIB = 1024 * 1024
V7X_VMEM_BYTES = 64 * MIB
VMEM_LIMIT_ADA = 40 * MIB
VMEM_LIMIT_LAYER = V7X_VMEM_BYTES - MIB


def _silu(z):
    hz = 0.5 * z
    return hz + hz * jnp.tanh(hz)


def _ada_kernel(c_ref, wa_top, wa_bot, wf_top, wf_bot, ba_ref, bf_ref, oa_ref, of_ref,
                *, n_a_tiles):
    j = pl.program_id(0)
    c = c_ref[...]
    ca = (c * (1.0 / (1.0 + jnp.exp(-c)))).astype(BF16)
    half = ca.shape[1] // 2

    def mod(w_top, w_bot, b_ref, o_ref):
        acc = jnp.dot(ca[:, :half], w_top[...].astype(BF16), preferred_element_type=F32)
        acc = acc + jnp.dot(ca[:, half:], w_bot[...].astype(BF16),
                            preferred_element_type=F32)
        o_ref[...] = acc + b_ref[...]

    @pl.when(j < n_a_tiles)
    def _():
        mod(wa_top, wa_bot, ba_ref, oa_ref)

    @pl.when(j >= n_a_tiles)
    def _():
        mod(wf_top, wf_bot, bf_ref, of_ref)


def _ada_mod(c, w_a, b_a, w_f, b_f):
    bsz, d = c.shape
    n_a, n_f = w_a.shape[-1], w_f.shape[-1]
    w_a = w_a.reshape(d, n_a)
    w_f = w_f.reshape(d, n_f)
    ta, tf = n_a // TN_ADA, n_f // TN_ADA
    half = d // 2
    a_tile = lambda j: jnp.minimum(j, ta - 1)
    f_tile = lambda j: jnp.maximum(j - ta, 0)
    return pl.pallas_call(
        functools.partial(_ada_kernel, n_a_tiles=ta),
        grid=(ta + tf,),
        in_specs=[
            pl.BlockSpec((bsz, d), lambda j: (0, 0)),
            pl.BlockSpec((half, TN_ADA), lambda j: (0, a_tile(j))),
            pl.BlockSpec((half, TN_ADA), lambda j: (1, a_tile(j))),
            pl.BlockSpec((half, TN_ADA), lambda j: (0, f_tile(j))),
            pl.BlockSpec((half, TN_ADA), lambda j: (1, f_tile(j))),
            pl.BlockSpec((1, TN_ADA), lambda j: (0, a_tile(j))),
            pl.BlockSpec((1, TN_ADA), lambda j: (0, f_tile(j))),
        ],
        out_specs=[
            pl.BlockSpec((bsz, TN_ADA), lambda j: (0, a_tile(j))),
            pl.BlockSpec((bsz, TN_ADA), lambda j: (0, f_tile(j))),
        ],
        out_shape=[jax.ShapeDtypeStruct((bsz, n_a), F32),
                   jax.ShapeDtypeStruct((bsz, n_f), F32)],
        compiler_params=pltpu.CompilerParams(
            dimension_semantics=("arbitrary",), vmem_limit_bytes=VMEM_LIMIT_ADA),
        name="ada_mod",
    )(c, w_a, w_a, w_f, w_f, b_a.reshape(1, n_a), b_f.reshape(1, n_f))


class _MixerBlock:
    def __init__(self, load, store, cos, sin, lng, lnb, ws_ref, bst, sink_ref, kprev, vprev,
                 has_prev, *, d_a, d_b, d_kv):
        self.load, self.store = load, store
        self.cos, self.sin, self.lng, self.lnb = cos, sin, lng, lnb
        q_scale = LOG2E * HEAD_DIM ** -0.5
        self.cos_q, self.sin_q = cos * q_scale, sin * q_scale
        self.ws_ref, self.bst, self.sink_ref = ws_ref, bst, sink_ref
        self.kprev, self.vprev, self.has_prev = kprev, vprev, has_prev
        self.d_a, self.d_b, self.d_kv = d_a, d_b, d_kv
        self.o_u, self.o_v, self.o_za = 0, d_a, 2 * d_a
        self.o_q = 3 * d_a
        self.o_k = self.o_q + d_b
        self.o_vv = self.o_k + d_kv
        self.o_zb = self.o_vv + d_kv
        row = lax.broadcasted_iota(jnp.int32, (CHUNK, CHUNK), 0)
        col = lax.broadcasted_iota(jnp.int32, (CHUNK, CHUNK), 1)
        self.col = col
        self.causal = col <= row
        self.first_half = (col & (HEAD_DIM - 1)) < (HEAD_DIM // 2)
        self.probs = {}

    def _rope(self, xv, query=False):
        cos, sin = (self.cos_q, self.sin_q) if query else (self.cos, self.sin)
        rot = jnp.where(self.first_half,
                        pltpu.roll(xv, LANES - HEAD_DIM // 2, 1),
                        pltpu.roll(xv, HEAD_DIM // 2, 1))
        return xv * cos + rot * sin

    def prep(self):
        va = self.load(self.o_v, self.o_v + self.d_a).astype(F32)
        mu = jnp.mean(va, axis=-1, keepdims=True)
        vc = va - mu
        var = jnp.mean(vc * vc, axis=-1, keepdims=True)
        self.vn = (vc * lax.rsqrt(var + NORM_EPS) * self.lng + self.lnb).astype(BF16)
        self.tril = self.causal.astype(F32)
        low_half = self.col < HEAD_DIM
        self.ke = [None] * N_EXP
        self.ve = [None] * N_EXP
        for c in range(self.d_kv // LANES):
            kc = self._rope(self.load(self.o_k + c * LANES, self.o_k + (c + 1) * LANES)
                            .astype(F32))
            vcol = self.load(self.o_vv + c * LANES, self.o_vv + (c + 1) * LANES).astype(F32)
            kc_sw = pltpu.roll(kc, HEAD_DIM, 1)
            vcol_sw = pltpu.roll(vcol, HEAD_DIM, 1)
            for j in range(HEADS_PER_VREG):
                kv_head = c * HEADS_PER_VREG + j
                for o in range(HEADS_PER_VREG):
                    mask = low_half if o == 0 else jnp.logical_not(low_half)
                    e = kv_head * HEADS_PER_VREG + o
                    self.ke[e] = jnp.where(mask, kc if o == j else kc_sw, 0.0).astype(BF16)
                    self.ve[e] = jnp.where(mask, vcol if o == j else vcol_sw, 0.0).astype(BF16)

    def group_pair(self, g0):
        c0 = g0 * A_GROUP_W
        w = jnp.concatenate([(self.ws_ref[g0] * self.tril).astype(BF16),
                             (self.ws_ref[g0 + 1] * self.tril).astype(BF16)], axis=1)
        zero = jnp.zeros((CHUNK, A_GROUP_W), BF16)
        rhs = jnp.concatenate(
            [jnp.concatenate([self.vn[:, c0:c0 + A_GROUP_W], zero], axis=1),
             jnp.concatenate([zero, self.vn[:, c0 + A_GROUP_W:c0 + 2 * A_GROUP_W]], axis=1)],
            axis=0)
        s2 = jnp.dot(w, rhs, preferred_element_type=F32)
        for k in range(2):
            g = g0 + k
            cg = g * A_GROUP_W
            s = s2[:, k * A_GROUP_W:(k + 1) * A_GROUP_W] + self.bst[:, g:g + 1]
            u = self.load(self.o_u + cg, self.o_u + cg + A_GROUP_W).astype(F32)
            za = self.load(self.o_za + cg, self.o_za + cg + A_GROUP_W).astype(F32)
            self.store(cg, (u * s * _silu(za)).astype(BF16))

    def scores(self, p0):
        kv_head = (p0 * HEADS_PER_VREG) // Q_PER_KV
        kprev = self.kprev()
        q4 = jnp.concatenate(
            [self._rope(self.load(self.o_q + p * LANES, self.o_q + (p + 1) * LANES)
                        .astype(F32), query=True).astype(BF16)
             for p in range(p0, p0 + COLS_PER_KV)], axis=0)
        probs = [[] for _ in range(COLS_PER_KV)]
        for o in range(HEADS_PER_VREG):
            e = kv_head * HEADS_PER_VREG + o
            kband = jnp.concatenate([kprev[e], self.ke[e]], axis=0)
            s4 = lax.dot_general(q4, kband, (((1,), (1,)), ((), ())),
                                 preferred_element_type=F32)
            for a in range(COLS_PER_KV):
                s = s4[a * CHUNK:(a + 1) * CHUNK]
                h = (p0 + a) * HEADS_PER_VREG + o
                s_prev = s[:, :CHUNK]
                if self.has_prev is not True:
                    s_prev = jnp.where(self.has_prev, s_prev, -jnp.inf)
                comb = jnp.where(self.causal, s[:, CHUNK:], s_prev)
                sink = self.sink_ref[h] * LOG2E
                m = jnp.maximum(jnp.max(comb, axis=-1, keepdims=True), sink)
                pexp = jnp.exp2(comb - m)
                denom = jnp.sum(pexp, axis=-1, keepdims=True) + jnp.exp2(sink - m)
                pn = (pexp * (1.0 / denom)).astype(BF16)
                zero = jnp.zeros_like(pn)
                probs[a].append(jnp.where(self.causal, zero, pn))
                probs[a].append(jnp.where(self.causal, pn, zero))
        self.probs[p0] = jnp.concatenate(
            [jnp.concatenate(pa, axis=1) for pa in probs], axis=0)

    def values(self, p0):
        kv_head = (p0 * HEADS_PER_VREG) // Q_PER_KV
        vprev = self.vprev()
        vband = []
        for o in range(HEADS_PER_VREG):
            e = kv_head * HEADS_PER_VREG + o
            vband.append(vprev[e])
            vband.append(self.ve[e])
        out4 = jnp.dot(self.probs.pop(p0), jnp.concatenate(vband, axis=0),
                       preferred_element_type=F32)
        for a in range(COLS_PER_KV):
            p = p0 + a
            zb = self.load(self.o_zb + p * LANES, self.o_zb + (p + 1) * LANES).astype(F32)
            self.store(self.d_a + p * LANES,
                       (out4[a * CHUNK:(a + 1) * CHUNK] * _silu(zb)).astype(BF16))


def _layer_kernel(sink_ref, x_ref, moda_ref, modc_ref, modf_ref, ng_ref, fg_ref,
                  cos_ref, sin_ref, lng_ref, lnb_ref, ws_ref, bst_ref, win32_ref, wout32_ref,
                  o_ref, win_scr, wout_scr, proj0_scr, proj1_scr, y_scr, xprev_scr, h_scr,
                  kprev_scr, vprev_scr,
                  *, d_model, d_in, d_a, d_b, d_kv, blocks_per_batch, n_blk):
    i = pl.program_id(0)

    @pl.when(i < W_TILES)
    def _():
        r_in = pl.multiple_of(i * win32_ref.shape[0], win32_ref.shape[0])
        win_scr[pl.ds(r_in, win32_ref.shape[0]), :] = win32_ref[...].astype(BF16)
        r_out = pl.multiple_of(i * wout32_ref.shape[0], wout32_ref.shape[0])
        wout_scr[pl.ds(r_out, wout32_ref.shape[0]), :] = wout32_ref[...].astype(BF16)

    step = i - W_TILES

    args = (sink_ref, x_ref, moda_ref, modc_ref, modf_ref, ng_ref, fg_ref, cos_ref, sin_ref,
            lng_ref, lnb_ref, ws_ref, bst_ref, win_scr, wout_scr, o_ref)
    scr = (y_scr, xprev_scr, h_scr, kprev_scr, vprev_scr)
    dims = dict(d_model=d_model, d_in=d_in, d_a=d_a, d_b=d_b, d_kv=d_kv,
                blocks_per_batch=blocks_per_batch, n_blk=n_blk)
    bufs = ((proj0_scr, proj1_scr), (proj1_scr, proj0_scr))
    inner = jnp.logical_and(step > 0, step < n_blk)

    @pl.when(step == 0)
    def _():
        kprev_scr[...] = jnp.zeros_like(kprev_scr)
        vprev_scr[...] = jnp.zeros_like(vprev_scr)
        _layer_step(step, *args, *bufs[0], *scr, **dims, do_bc=False)

    @pl.when(jnp.logical_and(inner, step % 2 == 0))
    def _():
        _layer_step(step, *args, *bufs[0], *scr, **dims)

    @pl.when(jnp.logical_and(inner, step % 2 == 1))
    def _():
        _layer_step(step, *args, *bufs[1], *scr, **dims)

    @pl.when(step == n_blk)
    def _():
        _layer_step(step, *args, *bufs[n_blk % 2], *scr, **dims, do_a=False)


def _layer_step(step, sink_ref, x_ref, moda_ref, modc_ref, modf_ref, ng_ref, fg_ref,
                cos_ref, sin_ref, lng_ref, lnb_ref, ws_ref, bst_ref, win_ref, wout_ref,
                o_ref, proj_w, proj_r, y_scr, xprev_scr, h_scr, kprev_scr, vprev_scr,
                *, d_model, d_in, d_a, d_b, d_kv, blocks_per_batch, n_blk,
                do_a=True, do_bc=True):
    blk_c = jnp.clip(step - 1, 0, n_blk - 1)
    first_in_batch = (blk_c % blocks_per_batch) == 0

    chunks = list(range(0, d_in, TN_DOT))
    carried = chunks[len(chunks) - N_CARRY:]
    if do_bc:
        for n0 in carried:
            acc = jnp.dot(h_scr[...], win_ref[:, n0:n0 + TN_DOT], preferred_element_type=F32)
            proj_r[:, n0:n0 + TN_DOT] = acc.astype(BF16)

    if do_a:
        x = x_ref[...]
        ms = jnp.mean(x * x, axis=-1, keepdims=True)
        row_a = pl.ds(jnp.minimum(step, n_blk - 1) // blocks_per_batch, 1)
        shift = moda_ref[row_a, 0:d_model]
        gain = ng_ref[...] * (1.0 + moda_ref[row_a, d_model:2 * d_model])
        h_scr[...] = (x * lax.rsqrt(ms + NORM_EPS) * gain + shift).astype(BF16)

    lng = lng_ref[...]
    lnb = lnb_ref[...]
    bst = bst_ref[...].T
    n_sub = T_BLK // CHUNK
    blocks = []
    for sb in range(n_sub):
        r0 = sb * CHUNK

        def load(c0, c1, r0=r0):
            return proj_r[r0:r0 + CHUNK, c0:c1]

        def store(c0, val, r0=r0):
            y_scr[r0:r0 + CHUNK, c0:c0 + val.shape[1]] = val

        if sb == 0:
            kprev = lambda: [kprev_scr[e] for e in range(N_EXP)]
            vprev = lambda: [vprev_scr[e] for e in range(N_EXP)]
            has_prev = jnp.logical_not(first_in_batch)
        else:
            kprev = lambda b=blocks[sb - 1]: b.ke
            vprev = lambda b=blocks[sb - 1]: b.ve
            has_prev = True
        blocks.append(_MixerBlock(
            load, store, cos_ref[r0:r0 + CHUNK, :], sin_ref[r0:r0 + CHUNK, :], lng, lnb,
            ws_ref, bst, sink_ref, kprev, vprev, has_prev, d_a=d_a, d_b=d_b, d_kv=d_kv))

    def proj_chunk(n0):
        acc = jnp.dot(h_scr[...], win_ref[:, n0:n0 + TN_DOT], preferred_element_type=F32)
        proj_w[:, n0:n0 + TN_DOT] = acc.astype(BF16)

    chunks = chunks[:len(chunks) - N_CARRY]
    next_chunk = [0]

    def emit_chunks(n):
        for _ in range(n):
            if do_a and next_chunk[0] < len(chunks):
                proj_chunk(chunks[next_chunk[0]])
                next_chunk[0] += 1

    if not do_bc:
        emit_chunks(len(chunks))
        xprev_scr[...] = x
        return

    for b in blocks:
        b.prep()
    n_pairs = d_b // LANES
    slots = [(b, p0) for b in blocks for p0 in range(0, n_pairs, COLS_PER_KV)]
    group_pairs = [(b, g0) for b in blocks for g0 in range(0, A_GROUPS, 2)]
    pairs_per_slot = -(-len(group_pairs) // len(slots))
    slots[0][0].scores(slots[0][1])
    for j, (b, p0) in enumerate(slots):
        emit_chunks(1)
        b.values(p0)
        if j + 1 < len(slots):
            slots[j + 1][0].scores(slots[j + 1][1])
        for bg, g0 in group_pairs[j * pairs_per_slot:(j + 1) * pairs_per_slot]:
            bg.group_pair(g0)
    for e in range(N_EXP):
        kprev_scr[e] = blocks[-1].ke[e]
        vprev_scr[e] = blocks[-1].ve[e]
    emit_chunks(len(chunks) - next_chunk[0] - 1)

    row_c = pl.ds(blk_c // blocks_per_batch, 1)
    gate = modc_ref[row_c, 2 * d_model:3 * d_model]
    ssq = jnp.zeros((T_BLK, 1), F32)
    for n0 in range(0, d_model, TN_DOT):
        sl = slice(n0, n0 + TN_DOT)
        acc = jnp.dot(y_scr[...], wout_ref[:, sl], preferred_element_type=F32)
        xr = xprev_scr[:, sl] + gate[:, sl] * acc
        ssq = ssq + jnp.sum(xr * xr, axis=-1, keepdims=True)
        o_ref[:, sl] = xr
    emit_chunks(len(chunks))
    inv = lax.rsqrt(ssq * (1.0 / d_model) + NORM_EPS)
    shift_f = modf_ref[row_c, 0:d_model]
    scale_f = modf_ref[row_c, d_model:2 * d_model]
    o_ref[...] = o_ref[...] * inv * (fg_ref[...] * (1.0 + scale_f)) + shift_f

    if do_a:
        xprev_scr[...] = x


def _layer(x2, sinks, mod, mod_f, norm_g, final_g, cos_t, sin_t, ln_g, ln_b, w_sp, b_sp,
           w_in, w_out, seq, d_a, d_b, d_kv):
    rows, d_model = x2.shape
    d_in = w_in.shape[-1]
    d_mix = d_a + d_b
    n_blk = rows // T_BLK
    bpb = seq // T_BLK
    assert d_model % W_TILES == 0 and d_mix % W_TILES == 0

    def blk_a(i):
        return jnp.clip(i - W_TILES, 0, n_blk - 1)

    def blk_c(i):
        return jnp.clip(i - W_TILES - 1, 0, n_blk - 1)

    def w_tile(i):
        return (jnp.minimum(i, W_TILES - 1), 0)

    const2 = lambda i: (0, 0)
    kern = functools.partial(_layer_kernel, d_model=d_model, d_in=d_in, d_a=d_a, d_b=d_b,
                             d_kv=d_kv, blocks_per_batch=bpb, n_blk=n_blk)
    return pl.pallas_call(
        kern,
        grid=(W_TILES + n_blk + 1,),
        in_specs=[
            pl.BlockSpec(memory_space=pltpu.SMEM),
            pl.BlockSpec((T_BLK, d_model), lambda i: (blk_a(i), 0)),
            pl.BlockSpec(mod.shape, const2),
            pl.BlockSpec(mod.shape, const2),
            pl.BlockSpec(mod_f.shape, const2),
            pl.BlockSpec((1, d_model), const2),
            pl.BlockSpec((1, d_model), const2),
            pl.BlockSpec((T_BLK, LANES), lambda i: (blk_c(i) % bpb, 0)),
            pl.BlockSpec((T_BLK, LANES), lambda i: (blk_c(i) % bpb, 0)),
            pl.BlockSpec((1, d_a), const2),
            pl.BlockSpec((1, d_a), const2),
            pl.BlockSpec((A_GROUPS, CHUNK, CHUNK), lambda i: (0, 0, 0)),
            pl.BlockSpec((A_GROUPS, CHUNK), const2),
            pl.BlockSpec((d_model // W_TILES, d_in), w_tile),
            pl.BlockSpec((d_mix // W_TILES, d_model), w_tile),
        ],
        out_specs=pl.BlockSpec((T_BLK, d_model), lambda i: (blk_c(i), 0)),
        out_shape=jax.ShapeDtypeStruct((rows, d_model), F32),
        scratch_shapes=[
            pltpu.VMEM((d_model, d_in), BF16),
            pltpu.VMEM((d_mix, d_model), BF16),
            pltpu.VMEM((T_BLK, d_in), BF16),
            pltpu.VMEM((T_BLK, d_in), BF16),
            pltpu.VMEM((T_BLK, d_mix), BF16),
            pltpu.VMEM((T_BLK, d_model), F32),
            pltpu.VMEM((T_BLK, d_model), BF16),
            pltpu.VMEM((N_EXP, CHUNK, LANES), BF16),
            pltpu.VMEM((N_EXP, CHUNK, LANES), BF16),
        ],
        compiler_params=pltpu.CompilerParams(
            dimension_semantics=("arbitrary",), vmem_limit_bytes=VMEM_LIMIT_LAYER),
        name="layer",
    )(sinks, x2, mod, mod, mod_f, norm_g.reshape(1, d_model), final_g.reshape(1, d_model),
      cos_t, sin_t, ln_g, ln_b, w_sp, b_sp, w_in, w_out)


def _rope_tables(seq):
    half = HEAD_DIM // 2
    inv_freq = ROPE_THETA ** (-np.arange(0, HEAD_DIM, 2, dtype=np.float64) / HEAD_DIM)
    ang = np.arange(seq, dtype=np.float64)[:, None] * inv_freq[None, :]
    cos = np.cos(ang)
    sin = np.sin(ang)
    cos_t = np.tile(cos, (1, LANES // half))
    sin_t = np.tile(np.concatenate([-sin, sin], axis=1), (1, HEADS_PER_VREG))
    return jnp.asarray(cos_t, F32), jnp.asarray(sin_t, F32)


def kernel(x, c, w_ada, b_ada, norm_g, w_in, ln_v_g, ln_v_b, w_spatial, b_spatial, sinks,
           w_out, w_ada_final, b_ada_final, final_norm_g):
    bsz, seq, d_model = x.shape
    assert w_ada.shape[0] == 1, "single-layer stack"
    d_a = ln_v_g.shape[-1]
    d_mix = w_out.shape[-2]
    d_b = d_mix - d_a
    d_kv = N_KV_HEADS * HEAD_DIM
    d_in = w_in.shape[-1]
    assert d_in == 3 * d_a + 2 * d_b + 2 * d_kv
    assert d_a == A_GROUPS * A_GROUP_W and d_b == N_KV_HEADS * Q_PER_KV * HEAD_DIM
    assert seq % T_BLK == 0 and T_BLK % CHUNK == 0

    x2 = x.reshape(bsz * seq, d_model)
    mod, mod_f = _ada_mod(c, w_ada, b_ada, w_ada_final, b_ada_final)
    cos_t, sin_t = _rope_tables(seq)
    out = _layer(x2, sinks.reshape(-1), mod, mod_f, norm_g, final_norm_g, cos_t, sin_t,
                 ln_v_g.reshape(1, d_a), ln_v_b.reshape(1, d_a),
                 w_spatial.reshape(A_GROUPS, CHUNK, CHUNK),
                 b_spatial.reshape(A_GROUPS, CHUNK),
                 w_in.reshape(d_model, d_in), w_out.reshape(d_mix, d_model),
                 seq, d_a, d_b, d_kv)
    return out.reshape(bsz, seq, d_model)
```

```python
import functools

import jax
import jax.numpy as jnp
import numpy as np
from jax import lax
from jax.experimental import pallas as pl
from jax.experimental.pallas import tpu as pltpu

F32 = jnp.float32
BF16 = jnp.bfloat16

CHUNK = 128
A_GROUPS = 8
A_GROUP_W = 128
HEAD_DIM = 64
N_KV_HEADS = 4
Q_PER_KV = 4
ROPE_THETA = 10000.0
NORM_EPS = 1e-5
LOG2E = 1.4426950408889634

LANES = 128
HEADS_PER_VREG = LANES // HEAD_DIM
N_EXP = N_KV_HEADS * HEADS_PER_VREG
COLS_PER_KV = Q_PER_KV // HEADS_PER_VREG

T_BLK = 256
TN_DOT = 512
N_CARRY = 2
W_TILES = 16
TN_ADA = 1024
ADA_ROW_SPLITS = 4

MIB = 1024 * 1024
V7X_VMEM_BYTES = 64 * MIB
VMEM_LIMIT_ADA = 40 * MIB
VMEM_LIMIT_LAYER = V7X_VMEM_BYTES - MIB


def _silu(z):
    hz = 0.5 * z
    return hz + hz * jnp.tanh(hz)


def _ada_kernel(c_ref, *refs, n_a_tiles):
    wa = refs[:ADA_ROW_SPLITS]
    wf = refs[ADA_ROW_SPLITS:2 * ADA_ROW_SPLITS]
    ba_ref, bf_ref, oa_ref, of_ref = refs[2 * ADA_ROW_SPLITS:]
    j = pl.program_id(0)
    c = c_ref[...]
    ca = (c * (1.0 / (1.0 + jnp.exp(-c)))).astype(BF16)
    part = ca.shape[1] // ADA_ROW_SPLITS

    def mod(ws, b_ref, o_ref):
        acc = b_ref[...]
        for k, w in enumerate(ws):
            acc = acc + jnp.dot(ca[:, k * part:(k + 1) * part], w[...].astype(BF16),
                                preferred_element_type=F32)
        o_ref[...] = acc

    @pl.when(j < n_a_tiles)
    def _():
        mod(wa, ba_ref, oa_ref)

    @pl.when(j >= n_a_tiles)
    def _():
        mod(wf, bf_ref, of_ref)


def _ada_mod(c, w_a, b_a, w_f, b_f):
    bsz, d = c.shape
    n_a, n_f = w_a.shape[-1], w_f.shape[-1]
    w_a = w_a.reshape(d, n_a)
    w_f = w_f.reshape(d, n_f)
    ta, tf = n_a // TN_ADA, n_f // TN_ADA
    part = d // ADA_ROW_SPLITS
    a_tile = lambda j: jnp.minimum(j, ta - 1)
    f_tile = lambda j: jnp.maximum(j - ta, 0)
    return pl.pallas_call(
        functools.partial(_ada_kernel, n_a_tiles=ta),
        grid=(ta + tf,),
        in_specs=[
            pl.BlockSpec((bsz, d), lambda j: (0, 0)),
            *[pl.BlockSpec((part, TN_ADA), lambda j, k=k: (k, a_tile(j)))
              for k in range(ADA_ROW_SPLITS)],
            *[pl.BlockSpec((part, TN_ADA), lambda j, k=k: (k, f_tile(j)))
              for k in range(ADA_ROW_SPLITS)],
            pl.BlockSpec((1, TN_ADA), lambda j: (0, a_tile(j))),
            pl.BlockSpec((1, TN_ADA), lambda j: (0, f_tile(j))),
        ],
        out_specs=[
            pl.BlockSpec((bsz, TN_ADA), lambda j: (0, a_tile(j))),
            pl.BlockSpec((bsz, TN_ADA), lambda j: (0, f_tile(j))),
        ],
        out_shape=[jax.ShapeDtypeStruct((bsz, n_a), F32),
                   jax.ShapeDtypeStruct((bsz, n_f), F32)],
        compiler_params=pltpu.CompilerParams(
            dimension_semantics=("arbitrary",), vmem_limit_bytes=VMEM_LIMIT_ADA),
        name="ada_mod",
    )(c, *([w_a] * ADA_ROW_SPLITS), *([w_f] * ADA_ROW_SPLITS),
      b_a.reshape(1, n_a), b_f.reshape(1, n_f))


class _MixerBlock:
    def __init__(self, load, store, cos, sin, lng, lnb, ws_ref, bst, sink_ref, kprev, vprev,
                 has_prev, *, d_a, d_b, d_kv):
        self.load, self.store = load, store
        self.cos, self.sin, self.lng, self.lnb = cos, sin, lng, lnb
        q_scale = LOG2E * HEAD_DIM ** -0.5
        self.cos_q, self.sin_q = cos * q_scale, sin * q_scale
        self.ws_ref, self.bst, self.sink_ref = ws_ref, bst, sink_ref
        self.kprev, self.vprev, self.has_prev = kprev, vprev, has_prev
        self.d_a, self.d_b, self.d_kv = d_a, d_b, d_kv
        self.o_u, self.o_v, self.o_za = 0, d_a, 2 * d_a
        self.o_q = 3 * d_a
        self.o_k = self.o_q + d_b
        self.o_vv = self.o_k + d_kv
        self.o_zb = self.o_vv + d_kv
        row = lax.broadcasted_iota(jnp.int32, (CHUNK, CHUNK), 0)
        col = lax.broadcasted_iota(jnp.int32, (CHUNK, CHUNK), 1)
        self.col = col
        self.causal = col <= row
        self.first_half = (col & (HEAD_DIM - 1)) < (HEAD_DIM // 2)
        self.probs = {}

    def _rope(self, xv, query=False):
        cos, sin = (self.cos_q, self.sin_q) if query else (self.cos, self.sin)
        rot = jnp.where(self.first_half,
                        pltpu.roll(xv, LANES - HEAD_DIM // 2, 1),
                        pltpu.roll(xv, HEAD_DIM // 2, 1))
        return xv * cos + rot * sin

    def prep(self):
        va = self.load(self.o_v, self.o_v + self.d_a).astype(F32)
        mu = jnp.mean(va, axis=-1, keepdims=True)
        vc = va - mu
        var = jnp.mean(vc * vc, axis=-1, keepdims=True)
        self.vn = (vc * lax.rsqrt(var + NORM_EPS) * self.lng + self.lnb).astype(BF16)
        self.tril = self.causal.astype(F32)
        low_half = self.col < HEAD_DIM
        self.ke = [None] * N_EXP
        self.ve = [None] * N_EXP
        for c in range(self.d_kv // LANES):
            kc = self._rope(self.load(self.o_k + c * LANES, self.o_k + (c + 1) * LANES)
                            .astype(F32))
            vcol = self.load(self.o_vv + c * LANES, self.o_vv + (c + 1) * LANES).astype(F32)
            kc_sw = pltpu.roll(kc, HEAD_DIM, 1)
            vcol_sw = pltpu.roll(vcol, HEAD_DIM, 1)
            for j in range(HEADS_PER_VREG):
                kv_head = c * HEADS_PER_VREG + j
                for o in range(HEADS_PER_VREG):
                    mask = low_half if o == 0 else jnp.logical_not(low_half)
                    e = kv_head * HEADS_PER_VREG + o
                    self.ke[e] = jnp.where(mask, kc if o == j else kc_sw, 0.0).astype(BF16)
                    self.ve[e] = jnp.where(mask, vcol if o == j else vcol_sw, 0.0).astype(BF16)

    def group_pair(self, g0):
        c0 = g0 * A_GROUP_W
        w = jnp.concatenate([(self.ws_ref[g0] * self.tril).astype(BF16),
                             (self.ws_ref[g0 + 1] * self.tril).astype(BF16)], axis=1)
        zero = jnp.zeros((CHUNK, A_GROUP_W), BF16)
        rhs = jnp.concatenate(
            [jnp.concatenate([self.vn[:, c0:c0 + A_GROUP_W], zero], axis=1),
             jnp.concatenate([zero, self.vn[:, c0 + A_GROUP_W:c0 + 2 * A_GROUP_W]], axis=1)],
            axis=0)
        s2 = jnp.dot(w, rhs, preferred_element_type=F32)
        for k in range(2):
            g = g0 + k
            cg = g * A_GROUP_W
            s = s2[:, k * A_GROUP_W:(k + 1) * A_GROUP_W] + self.bst[:, g:g + 1]
            u = self.load(self.o_u + cg, self.o_u + cg + A_GROUP_W).astype(F32)
            za = self.load(self.o_za + cg, self.o_za + cg + A_GROUP_W).astype(F32)
            self.store(cg, (u * s * _silu(za)).astype(BF16))

    def scores(self, p0):
        kv_head = (p0 * HEADS_PER_VREG) // Q_PER_KV
        kprev = self.kprev()
        q4 = jnp.concatenate(
            [self._rope(self.load(self.o_q + p * LANES, self.o_q + (p + 1) * LANES)
                        .astype(F32), query=True).astype(BF16)
             for p in range(p0, p0 + COLS_PER_KV)], axis=0)
        probs = [[] for _ in range(COLS_PER_KV)]
        for o in range(HEADS_PER_VREG):
            e = kv_head * HEADS_PER_VREG + o
            kband = jnp.concatenate([kprev[e], self.ke[e]], axis=0)
            s4 = lax.dot_general(q4, kband, (((1,), (1,)), ((), ())),
                                 preferred_element_type=F32)
            for a in range(COLS_PER_KV):
                s = s4[a * CHUNK:(a + 1) * CHUNK]
                h = (p0 + a) * HEADS_PER_VREG + o
                s_prev = s[:, :CHUNK]
                if self.has_prev is not True:
                    s_prev = jnp.where(self.has_prev, s_prev, -jnp.inf)
                comb = jnp.where(self.causal, s[:, CHUNK:], s_prev)
                sink = self.sink_ref[h] * LOG2E
                m = jnp.maximum(jnp.max(comb, axis=-1, keepdims=True), sink)
                pexp = jnp.exp2(comb - m)
                denom = jnp.sum(pexp, axis=-1, keepdims=True) + jnp.exp2(sink - m)
                pn = (pexp * (1.0 / denom)).astype(BF16)
                zero = jnp.zeros_like(pn)
                probs[a].append(jnp.where(self.causal, zero, pn))
                probs[a].append(jnp.where(self.causal, pn, zero))
        self.probs[p0] = jnp.concatenate(
            [jnp.concatenate(pa, axis=1) for pa in probs], axis=0)

    def values(self, p0):
        kv_head = (p0 * HEADS_PER_VREG) // Q_PER_KV
        vprev = self.vprev()
        vband = []
        for o in range(HEADS_PER_VREG):
            e = kv_head * HEADS_PER_VREG + o
            vband.append(vprev[e])
            vband.append(self.ve[e])
        out4 = jnp.dot(self.probs.pop(p0), jnp.concatenate(vband, axis=0),
                       preferred_element_type=F32)
        for a in range(COLS_PER_KV):
            p = p0 + a
            zb = self.load(self.o_zb + p * LANES, self.o_zb + (p + 1) * LANES).astype(F32)
            self.store(self.d_a + p * LANES,
                       (out4[a * CHUNK:(a + 1) * CHUNK] * _silu(zb)).astype(BF16))


def _layer_kernel(sink_ref, x_ref, moda_ref, modc_ref, modf_ref, ng_ref, fg_ref,
                  cos_ref, sin_ref, lng_ref, lnb_ref, ws_ref, bst_ref, win32_ref, wout32_ref,
                  o_ref, win_scr, wout_scr, proj0_scr, proj1_scr, y_scr, xprev_scr, h_scr,
                  kprev_scr, vprev_scr,
                  *, d_model, d_in, d_a, d_b, d_kv, blocks_per_batch, n_blk):
    i = pl.program_id(0)

    @pl.when(i < W_TILES)
    def _():
        r_in = pl.multiple_of(i * win32_ref.shape[0], win32_ref.shape[0])
        win_scr[pl.ds(r_in, win32_ref.shape[0]), :] = win32_ref[...].astype(BF16)
        r_out = pl.multiple_of(i * wout32_ref.shape[0], wout32_ref.shape[0])
        wout_scr[pl.ds(r_out, wout32_ref.shape[0]), :] = wout32_ref[...].astype(BF16)

    step = i - W_TILES

    args = (sink_ref, x_ref, moda_ref, modc_ref, modf_ref, ng_ref, fg_ref, cos_ref, sin_ref,
            lng_ref, lnb_ref, ws_ref, bst_ref, win_scr, wout_scr, o_ref)
    scr = (y_scr, xprev_scr, h_scr, kprev_scr, vprev_scr)
    dims = dict(d_model=d_model, d_in=d_in, d_a=d_a, d_b=d_b, d_kv=d_kv,
                blocks_per_batch=blocks_per_batch, n_blk=n_blk)
    bufs = ((proj0_scr, proj1_scr), (proj1_scr, proj0_scr))
    inner = jnp.logical_and(step > 0, step < n_blk)

    @pl.when(step == 0)
    def _():
        kprev_scr[...] = jnp.zeros_like(kprev_scr)
        vprev_scr[...] = jnp.zeros_like(vprev_scr)
        _layer_step(step, *args, *bufs[0], *scr, **dims, do_bc=False)

    @pl.when(jnp.logical_and(inner, step % 2 == 0))
    def _():
        _layer_step(step, *args, *bufs[0], *scr, **dims)

    @pl.when(jnp.logical_and(inner, step % 2 == 1))
    def _():
        _layer_step(step, *args, *bufs[1], *scr, **dims)

    @pl.when(step == n_blk)
    def _():
        _layer_step(step, *args, *bufs[n_blk % 2], *scr, **dims, do_a=False)


def _layer_step(step, sink_ref, x_ref, moda_ref, modc_ref, modf_ref, ng_ref, fg_ref,
                cos_ref, sin_ref, lng_ref, lnb_ref, ws_ref, bst_ref, win_ref, wout_ref,
                o_ref, proj_w, proj_r, y_scr, xprev_scr, h_scr, kprev_scr, vprev_scr,
                *, d_model, d_in, d_a, d_b, d_kv, blocks_per_batch, n_blk,
                do_a=True, do_bc=True):
    blk_c = jnp.clip(step - 1, 0, n_blk - 1)
    first_in_batch = (blk_c % blocks_per_batch) == 0

    chunks = list(range(0, d_in, TN_DOT))
    carried = chunks[len(chunks) - N_CARRY:]
    if do_bc:
        for n0 in carried:
            acc = jnp.dot(h_scr[...], win_ref[:, n0:n0 + TN_DOT], preferred_element_type=F32)
            proj_r[:, n0:n0 + TN_DOT] = acc.astype(BF16)

    if do_a:
        x = x_ref[...]
        ms = jnp.mean(x * x, axis=-1, keepdims=True)
        row_a = pl.ds(jnp.minimum(step, n_blk - 1) // blocks_per_batch, 1)
        shift = moda_ref[row_a, 0:d_model]
        gain = ng_ref[...] * (1.0 + moda_ref[row_a, d_model:2 * d_model])
        h_scr[...] = (x * lax.rsqrt(ms + NORM_EPS) * gain + shift).astype(BF16)

    lng = lng_ref[...]
    lnb = lnb_ref[...]
    bst = bst_ref[...]
    n_sub = T_BLK // CHUNK
    blocks = []
    for sb in range(n_sub):
        r0 = sb * CHUNK

        def load(c0, c1, r0=r0):
            return proj_r[r0:r0 + CHUNK, c0:c1]

        def store(c0, val, r0=r0):
            y_scr[r0:r0 + CHUNK, c0:c0 + val.shape[1]] = val

        if sb == 0:
            kprev = lambda: [kprev_scr[e] for e in range(N_EXP)]
            vprev = lambda: [vprev_scr[e] for e in range(N_EXP)]
            has_prev = jnp.logical_not(first_in_batch)
        else:
            kprev = lambda b=blocks[sb - 1]: b.ke
            vprev = lambda b=blocks[sb - 1]: b.ve
            has_prev = True
        blocks.append(_MixerBlock(
            load, store, cos_ref[r0:r0 + CHUNK, :], sin_ref[r0:r0 + CHUNK, :], lng, lnb,
            ws_ref, bst, sink_ref, kprev, vprev, has_prev, d_a=d_a, d_b=d_b, d_kv=d_kv))

    def proj_chunk(n0):
        acc = jnp.dot(h_scr[...], win_ref[:, n0:n0 + TN_DOT], preferred_element_type=F32)
        proj_w[:, n0:n0 + TN_DOT] = acc.astype(BF16)

    chunks = chunks[:len(chunks) - N_CARRY]
    next_chunk = [0]

    def emit_chunks(n):
        for _ in range(n):
            if do_a and next_chunk[0] < len(chunks):
                proj_chunk(chunks[next_chunk[0]])
                next_chunk[0] += 1

    if not do_bc:
        emit_chunks(len(chunks))
        xprev_scr[...] = x
        return

    for b in blocks:
        b.prep()
    n_pairs = d_b // LANES
    slots = [(b, p0) for b in blocks for p0 in range(0, n_pairs, COLS_PER_KV)]
    group_pairs = [(b, g0) for b in blocks for g0 in range(0, A_GROUPS, 2)]
    pairs_per_slot = -(-len(group_pairs) // len(slots))
    slots[0][0].scores(slots[0][1])
    for j, (b, p0) in enumerate(slots):
        emit_chunks(1)
        b.values(p0)
        if j + 1 < len(slots):
            slots[j + 1][0].scores(slots[j + 1][1])
        for bg, g0 in group_pairs[j * pairs_per_slot:(j + 1) * pairs_per_slot]:
            bg.group_pair(g0)
    for e in range(N_EXP):
        kprev_scr[e] = blocks[-1].ke[e]
        vprev_scr[e] = blocks[-1].ve[e]
    emit_chunks(len(chunks) - next_chunk[0] - 1)

    row_c = pl.ds(blk_c // blocks_per_batch, 1)
    gate = modc_ref[row_c, 2 * d_model:3 * d_model]
    ssq = jnp.zeros((T_BLK, 1), F32)
    for n0 in range(0, d_model, TN_DOT):
        sl = slice(n0, n0 + TN_DOT)
        acc = jnp.dot(y_scr[...], wout_ref[:, sl], preferred_element_type=F32)
        xr = xprev_scr[:, sl] + gate[:, sl] * acc
        ssq = ssq + jnp.sum(xr * xr, axis=-1, keepdims=True)
        o_ref[:, sl] = xr
    emit_chunks(len(chunks))
    inv = lax.rsqrt(ssq * (1.0 / d_model) + NORM_EPS)
    shift_f = modf_ref[row_c, 0:d_model]
    scale_f = modf_ref[row_c, d_model:2 * d_model]
    o_ref[...] = o_ref[...] * inv * (fg_ref[...] * (1.0 + scale_f)) + shift_f

    if do_a:
        xprev_scr[...] = x


def _layer(x2, sinks, mod, mod_f, norm_g, final_g, cos_t, sin_t, ln_g, ln_b, w_sp, b_sp_t,
           w_in, w_out, seq, d_a, d_b, d_kv):
    rows, d_model = x2.shape
    d_in = w_in.shape[-1]
    d_mix = d_a + d_b
    n_blk = rows // T_BLK
    bpb = seq // T_BLK
    assert d_model % W_TILES == 0 and d_mix % W_TILES == 0

    def blk_a(i):
        return jnp.clip(i - W_TILES, 0, n_blk - 1)

    def blk_c(i):
        return jnp.clip(i - W_TILES - 1, 0, n_blk - 1)

    def w_tile(i):
        return (jnp.minimum(i, W_TILES - 1), 0)

    const2 = lambda i: (0, 0)
    kern = functools.partial(_layer_kernel, d_model=d_model, d_in=d_in, d_a=d_a, d_b=d_b,
                             d_kv=d_kv, blocks_per_batch=bpb, n_blk=n_blk)
    return pl.pallas_call(
        kern,
        grid=(W_TILES + n_blk + 1,),
        in_specs=[
            pl.BlockSpec(memory_space=pltpu.SMEM),
            pl.BlockSpec((T_BLK, d_model), lambda i: (blk_a(i), 0)),
            pl.BlockSpec(mod.shape, const2),
            pl.BlockSpec(mod.shape, const2),
            pl.BlockSpec(mod_f.shape, const2),
            pl.BlockSpec((1, d_model), const2),
            pl.BlockSpec((1, d_model), const2),
            pl.BlockSpec((T_BLK, LANES), lambda i: (blk_c(i) % bpb, 0)),
            pl.BlockSpec((T_BLK, LANES), lambda i: (blk_c(i) % bpb, 0)),
            pl.BlockSpec((1, d_a), const2),
            pl.BlockSpec((1, d_a), const2),
            pl.BlockSpec((A_GROUPS, CHUNK, CHUNK), lambda i: (0, 0, 0)),
            pl.BlockSpec((CHUNK, A_GROUPS), const2),
            pl.BlockSpec((d_model // W_TILES, d_in), w_tile),
            pl.BlockSpec((d_mix // W_TILES, d_model), w_tile),
        ],
        out_specs=pl.BlockSpec((T_BLK, d_model), lambda i: (blk_c(i), 0)),
        out_shape=jax.ShapeDtypeStruct((rows, d_model), F32),
        scratch_shapes=[
            pltpu.VMEM((d_model, d_in), BF16),
            pltpu.VMEM((d_mix, d_model), BF16),
            pltpu.VMEM((T_BLK, d_in), BF16),
            pltpu.VMEM((T_BLK, d_in), BF16),
            pltpu.VMEM((T_BLK, d_mix), BF16),
            pltpu.VMEM((T_BLK, d_model), F32),
            pltpu.VMEM((T_BLK, d_model), BF16),
            pltpu.VMEM((N_EXP, CHUNK, LANES), BF16),
            pltpu.VMEM((N_EXP, CHUNK, LANES), BF16),
        ],
        compiler_params=pltpu.CompilerParams(
            dimension_semantics=("arbitrary",), vmem_limit_bytes=VMEM_LIMIT_LAYER),
        name="layer",
    )(sinks, x2, mod, mod, mod_f, norm_g.reshape(1, d_model), final_g.reshape(1, d_model),
      cos_t, sin_t, ln_g, ln_b, w_sp, b_sp_t, w_in, w_out)


def _rope_tables(seq):
    half = HEAD_DIM // 2
    inv_freq = ROPE_THETA ** (-np.arange(0, HEAD_DIM, 2, dtype=np.float64) / HEAD_DIM)
    ang = np.arange(seq, dtype=np.float64)[:, None] * inv_freq[None, :]
    cos = np.cos(ang)
    sin = np.sin(ang)
    cos_t = np.tile(cos, (1, LANES // half))
    sin_t = np.tile(np.concatenate([-sin, sin], axis=1), (1, HEADS_PER_VREG))
    return jnp.asarray(cos_t, F32), jnp.asarray(sin_t, F32)


def kernel(x, c, w_ada, b_ada, norm_g, w_in, ln_v_g, ln_v_b, w_spatial, b_spatial, sinks,
           w_out, w_ada_final, b_ada_final, final_norm_g):
    bsz, seq, d_model = x.shape
    assert w_ada.shape[0] == 1, "single-layer stack"
    d_a = ln_v_g.shape[-1]
    d_mix = w_out.shape[-2]
    d_b = d_mix - d_a
    d_kv = N_KV_HEADS * HEAD_DIM
    d_in = w_in.shape[-1]
    assert d_in == 3 * d_a + 2 * d_b + 2 * d_kv
    assert d_a == A_GROUPS * A_GROUP_W and d_b == N_KV_HEADS * Q_PER_KV * HEAD_DIM
    assert seq % T_BLK == 0 and T_BLK % CHUNK == 0

    x2 = x.reshape(bsz * seq, d_model)
    mod, mod_f = _ada_mod(c, w_ada, b_ada, w_ada_final, b_ada_final)
    cos_t, sin_t = _rope_tables(seq)
    out = _layer(x2, sinks.reshape(-1), mod, mod_f, norm_g, final_norm_g, cos_t, sin_t,
                 ln_v_g.reshape(1, d_a), ln_v_b.reshape(1, d_a),
                 w_spatial.reshape(A_GROUPS, CHUNK, CHUNK),
                 b_spatial.reshape(A_GROUPS, CHUNK).T,
                 w_in.reshape(d_model, d_in), w_out.reshape(d_mix, d_model),
                 seq, d_a, d_b, d_kv)
    return out.reshape(bsz, seq, d_model)
```

```python
import functools

import jax
import jax.numpy as jnp
import numpy as np
from jax import lax
from jax.experimental import pallas as pl
from jax.experimental.pallas import tpu as pltpu

F32 = jnp.float32
BF16 = jnp.bfloat16

CHUNK = 128
A_GROUPS = 8
A_GROUP_W = 128
HEAD_DIM = 64
N_KV_HEADS = 4
Q_PER_KV = 4
ROPE_THETA = 10000.0
NORM_EPS = 1e-5
LOG2E = 1.4426950408889634

LANES = 128
HEADS_PER_VREG = LANES // HEAD_DIM
N_EXP = N_KV_HEADS * HEADS_PER_VREG
COLS_PER_KV = Q_PER_KV // HEADS_PER_VREG

T_BLK = 256
TN_DOT = 512
N_CARRY = 2
W_TILES = 16
TN_ADA = 1024
ADA_ROW_SPLITS = 8

MIB = 1024 * 1024
V7X_VMEM_BYTES = 64 * MIB
VMEM_LIMIT_ADA = 40 * MIB
VMEM_LIMIT_LAYER = V7X_VMEM_BYTES - MIB


def _silu(z):
    hz = 0.5 * z
    return hz + hz * jnp.tanh(hz)


def _ada_kernel(c_ref, *refs, n_a_tiles):
    wa = refs[:ADA_ROW_SPLITS]
    wf = refs[ADA_ROW_SPLITS:2 * ADA_ROW_SPLITS]
    ba_ref, bf_ref, oa_ref, of_ref = refs[2 * ADA_ROW_SPLITS:]
    j = pl.program_id(0)
    c = c_ref[...]
    ca = (c * (1.0 / (1.0 + jnp.exp(-c)))).astype(BF16)
    part = ca.shape[1] // ADA_ROW_SPLITS

    def mod(ws, b_ref, o_ref):
        acc = b_ref[...]
        for k, w in enumerate(ws):
            acc = acc + jnp.dot(ca[:, k * part:(k + 1) * part], w[...].astype(BF16),
                                preferred_element_type=F32)
        o_ref[...] = acc

    @pl.when(j < n_a_tiles)
    def _():
        mod(wa, ba_ref, oa_ref)

    @pl.when(j >= n_a_tiles)
    def _():
        mod(wf, bf_ref, of_ref)


def _ada_mod(c, w_a, b_a, w_f, b_f):
    bsz, d = c.shape
    n_a, n_f = w_a.shape[-1], w_f.shape[-1]
    w_a = w_a.reshape(d, n_a)
    w_f = w_f.reshape(d, n_f)
    ta, tf = n_a // TN_ADA, n_f // TN_ADA
    part = d // ADA_ROW_SPLITS
    a_tile = lambda j: jnp.minimum(j, ta - 1)
    f_tile = lambda j: jnp.maximum(j - ta, 0)
    return pl.pallas_call(
        functools.partial(_ada_kernel, n_a_tiles=ta),
        grid=(ta + tf,),
        in_specs=[
            pl.BlockSpec((bsz, d), lambda j: (0, 0)),
            *[pl.BlockSpec((part, TN_ADA), lambda j, k=k: (k, a_tile(j)))
              for k in range(ADA_ROW_SPLITS)],
            *[pl.BlockSpec((part, TN_ADA), lambda j, k=k: (k, f_tile(j)))
              for k in range(ADA_ROW_SPLITS)],
            pl.BlockSpec((1, TN_ADA), lambda j: (0, a_tile(j))),
            pl.BlockSpec((1, TN_ADA), lambda j: (0, f_tile(j))),
        ],
        out_specs=[
            pl.BlockSpec((bsz, TN_ADA), lambda j: (0, a_tile(j))),
            pl.BlockSpec((bsz, TN_ADA), lambda j: (0, f_tile(j))),
        ],
        out_shape=[jax.ShapeDtypeStruct((bsz, n_a), F32),
                   jax.ShapeDtypeStruct((bsz, n_f), F32)],
        compiler_params=pltpu.CompilerParams(
            dimension_semantics=("arbitrary",), vmem_limit_bytes=VMEM_LIMIT_ADA),
        name="ada_mod",
    )(c, *([w_a] * ADA_ROW_SPLITS), *([w_f] * ADA_ROW_SPLITS),
      b_a.reshape(1, n_a), b_f.reshape(1, n_f))


class _MixerBlock:
    def __init__(self, load, store, cos, sin, lng, lnb, ws_ref, bst, sink_ref, kprev, vprev,
                 has_prev, *, d_a, d_b, d_kv):
        self.load, self.store = load, store
        self.cos, self.sin, self.lng, self.lnb = cos, sin, lng, lnb
        q_scale = LOG2E * HEAD_DIM ** -0.5
        self.cos_q, self.sin_q = cos * q_scale, sin * q_scale
        self.ws_ref, self.bst, self.sink_ref = ws_ref, bst, sink_ref
        self.kprev, self.vprev, self.has_prev = kprev, vprev, has_prev
        self.d_a, self.d_b, self.d_kv = d_a, d_b, d_kv
        self.o_u, self.o_v, self.o_za = 0, d_a, 2 * d_a
        self.o_q = 3 * d_a
        self.o_k = self.o_q + d_b
        self.o_vv = self.o_k + d_kv
        self.o_zb = self.o_vv + d_kv
        row = lax.broadcasted_iota(jnp.int32, (CHUNK, CHUNK), 0)
        col = lax.broadcasted_iota(jnp.int32, (CHUNK, CHUNK), 1)
        self.col = col
        self.causal = col <= row
        self.first_half = (col & (HEAD_DIM - 1)) < (HEAD_DIM // 2)
        self.probs = {}

    def _rope(self, xv, query=False):
        cos, sin = (self.cos_q, self.sin_q) if query else (self.cos, self.sin)
        rot = jnp.where(self.first_half,
                        pltpu.roll(xv, LANES - HEAD_DIM // 2, 1),
                        pltpu.roll(xv, HEAD_DIM // 2, 1))
        return xv * cos + rot * sin

    def prep(self):
        va = self.load(self.o_v, self.o_v + self.d_a).astype(F32)
        mu = jnp.mean(va, axis=-1, keepdims=True)
        vc = va - mu
        var = jnp.mean(vc * vc, axis=-1, keepdims=True)
        self.vn = (vc * lax.rsqrt(var + NORM_EPS) * self.lng + self.lnb).astype(BF16)
        self.tril = self.causal.astype(F32)
        low_half = self.col < HEAD_DIM
        self.ke = [None] * N_EXP
        self.ve = [None] * N_EXP
        for c in range(self.d_kv // LANES):
            kc = self._rope(self.load(self.o_k + c * LANES, self.o_k + (c + 1) * LANES)
                            .astype(F32))
            vcol = self.load(self.o_vv + c * LANES, self.o_vv + (c + 1) * LANES).astype(F32)
            kc_sw = pltpu.roll(kc, HEAD_DIM, 1)
            vcol_sw = pltpu.roll(vcol, HEAD_DIM, 1)
            for j in range(HEADS_PER_VREG):
                kv_head = c * HEADS_PER_VREG + j
                for o in range(HEADS_PER_VREG):
                    mask = low_half if o == 0 else jnp.logical_not(low_half)
                    e = kv_head * HEADS_PER_VREG + o
                    self.ke[e] = jnp.where(mask, kc if o == j else kc_sw, 0.0).astype(BF16)
                    self.ve[e] = jnp.where(mask, vcol if o == j else vcol_sw, 0.0).astype(BF16)

    def group_pair(self, g0):
        c0 = g0 * A_GROUP_W
        w = jnp.concatenate([(self.ws_ref[g0] * self.tril).astype(BF16),
                             (self.ws_ref[g0 + 1] * self.tril).astype(BF16)], axis=1)
        zero = jnp.zeros((CHUNK, A_GROUP_W), BF16)
        rhs = jnp.concatenate(
            [jnp.concatenate([self.vn[:, c0:c0 + A_GROUP_W], zero], axis=1),
             jnp.concatenate([zero, self.vn[:, c0 + A_GROUP_W:c0 + 2 * A_GROUP_W]], axis=1)],
            axis=0)
        s2 = jnp.dot(w, rhs, preferred_element_type=F32)
        for k in range(2):
            g = g0 + k
            cg = g * A_GROUP_W
            s = s2[:, k * A_GROUP_W:(k + 1) * A_GROUP_W] + self.bst[:, g:g + 1]
            u = self.load(self.o_u + cg, self.o_u + cg + A_GROUP_W).astype(F32)
            za = self.load(self.o_za + cg, self.o_za + cg + A_GROUP_W).astype(F32)
            self.store(cg, (u * s * _silu(za)).astype(BF16))

    def scores(self, p0):
        kv_head = (p0 * HEADS_PER_VREG) // Q_PER_KV
        kprev = self.kprev()
        q4 = jnp.concatenate(
            [self._rope(self.load(self.o_q + p * LANES, self.o_q + (p + 1) * LANES)
                        .astype(F32), query=True).astype(BF16)
             for p in range(p0, p0 + COLS_PER_KV)], axis=0)
        probs = [[] for _ in range(COLS_PER_KV)]
        for o in range(HEADS_PER_VREG):
            e = kv_head * HEADS_PER_VREG + o
            kband = jnp.concatenate([kprev[e], self.ke[e]], axis=0)
            s4 = lax.dot_general(q4, kband, (((1,), (1,)), ((), ())),
                                 preferred_element_type=F32)
            for a in range(COLS_PER_KV):
                s = s4[a * CHUNK:(a + 1) * CHUNK]
                h = (p0 + a) * HEADS_PER_VREG + o
                s_prev = s[:, :CHUNK]
                if self.has_prev is not True:
                    s_prev = jnp.where(self.has_prev, s_prev, -jnp.inf)
                comb = jnp.where(self.causal, s[:, CHUNK:], s_prev)
                sink = self.sink_ref[h] * LOG2E
                m = jnp.maximum(jnp.max(comb, axis=-1, keepdims=True), sink)
                pexp = jnp.exp2(comb - m)
                denom = jnp.sum(pexp, axis=-1, keepdims=True) + jnp.exp2(sink - m)
                pn = (pexp * (1.0 / denom)).astype(BF16)
                zero = jnp.zeros_like(pn)
                probs[a].append(jnp.where(self.causal, zero, pn))
                probs[a].append(jnp.where(self.causal, pn, zero))
        self.probs[p0] = jnp.concatenate(
            [jnp.concatenate(pa, axis=1) for pa in probs], axis=0)

    def values(self, p0):
        kv_head = (p0 * HEADS_PER_VREG) // Q_PER_KV
        vprev = self.vprev()
        vband = []
        for o in range(HEADS_PER_VREG):
            e = kv_head * HEADS_PER_VREG + o
            vband.append(vprev[e])
            vband.append(self.ve[e])
        out4 = jnp.dot(self.probs.pop(p0), jnp.concatenate(vband, axis=0),
                       preferred_element_type=F32)
        for a in range(COLS_PER_KV):
            p = p0 + a
            zb = self.load(self.o_zb + p * LANES, self.o_zb + (p + 1) * LANES).astype(F32)
            self.store(self.d_a + p * LANES,
                       (out4[a * CHUNK:(a + 1) * CHUNK] * _silu(zb)).astype(BF16))


def _layer_kernel(sink_ref, x_ref, moda_ref, modc_ref, modf_ref, ng_ref, fg_ref,
                  cos_ref, sin_ref, lng_ref, lnb_ref, ws_ref, bst_ref, win32_ref, wout32_ref,
                  o_ref, win_scr, wout_scr, proj0_scr, proj1_scr, y_scr, xprev_scr, h_scr,
                  kprev_scr, vprev_scr,
                  *, d_model, d_in, d_a, d_b, d_kv, blocks_per_batch, n_blk):
    i = pl.program_id(0)

    @pl.when(i < W_TILES)
    def _():
        r_in = pl.multiple_of(i * win32_ref.shape[0], win32_ref.shape[0])
        win_scr[pl.ds(r_in, win32_ref.shape[0]), :] = win32_ref[...].astype(BF16)
        r_out = pl.multiple_of(i * wout32_ref.shape[0], wout32_ref.shape[0])
        wout_scr[pl.ds(r_out, wout32_ref.shape[0]), :] = wout32_ref[...].astype(BF16)

    step = i - W_TILES

    args = (sink_ref, x_ref, moda_ref, modc_ref, modf_ref, ng_ref, fg_ref, cos_ref, sin_ref,
            lng_ref, lnb_ref, ws_ref, bst_ref, win_scr, wout_scr, o_ref)
    scr = (y_scr, xprev_scr, h_scr, kprev_scr, vprev_scr)
    dims = dict(d_model=d_model, d_in=d_in, d_a=d_a, d_b=d_b, d_kv=d_kv,
                blocks_per_batch=blocks_per_batch, n_blk=n_blk)
    bufs = ((proj0_scr, proj1_scr), (proj1_scr, proj0_scr))
    inner = jnp.logical_and(step > 0, step < n_blk)

    @pl.when(step == 0)
    def _():
        kprev_scr[...] = jnp.zeros_like(kprev_scr)
        vprev_scr[...] = jnp.zeros_like(vprev_scr)
        _layer_step(step, *args, *bufs[0], *scr, **dims, do_bc=False)

    @pl.when(jnp.logical_and(inner, step % 2 == 0))
    def _():
        _layer_step(step, *args, *bufs[0], *scr, **dims)

    @pl.when(jnp.logical_and(inner, step % 2 == 1))
    def _():
        _layer_step(step, *args, *bufs[1], *scr, **dims)

    @pl.when(step == n_blk)
    def _():
        _layer_step(step, *args, *bufs[n_blk % 2], *scr, **dims, do_a=False)


def _layer_step(step, sink_ref, x_ref, moda_ref, modc_ref, modf_ref, ng_ref, fg_ref,
                cos_ref, sin_ref, lng_ref, lnb_ref, ws_ref, bst_ref, win_ref, wout_ref,
                o_ref, proj_w, proj_r, y_scr, xprev_scr, h_scr, kprev_scr, vprev_scr,
                *, d_model, d_in, d_a, d_b, d_kv, blocks_per_batch, n_blk,
                do_a=True, do_bc=True):
    blk_c = jnp.clip(step - 1, 0, n_blk - 1)
    first_in_batch = (blk_c % blocks_per_batch) == 0

    chunks = list(range(0, d_in, TN_DOT))
    carried = chunks[len(chunks) - N_CARRY:]
    if do_bc:
        for n0 in carried:
            acc = jnp.dot(h_scr[...], win_ref[:, n0:n0 + TN_DOT], preferred_element_type=F32)
            proj_r[:, n0:n0 + TN_DOT] = acc.astype(BF16)

    if do_a:
        x = x_ref[...]
        ms = jnp.mean(x * x, axis=-1, keepdims=True)
        row_a = pl.ds(jnp.minimum(step, n_blk - 1) // blocks_per_batch, 1)
        shift = moda_ref[row_a, 0:d_model]
        gain = ng_ref[...] * (1.0 + moda_ref[row_a, d_model:2 * d_model])
        h_scr[...] = (x * lax.rsqrt(ms + NORM_EPS) * gain + shift).astype(BF16)

    lng = lng_ref[...]
    lnb = lnb_ref[...]
    bst = bst_ref[...]
    n_sub = T_BLK // CHUNK
    blocks = []
    for sb in range(n_sub):
        r0 = sb * CHUNK

        def load(c0, c1, r0=r0):
            return proj_r[r0:r0 + CHUNK, c0:c1]

        def store(c0, val, r0=r0):
            y_scr[r0:r0 + CHUNK, c0:c0 + val.shape[1]] = val

        if sb == 0:
            kprev = lambda: [kprev_scr[e] for e in range(N_EXP)]
            vprev = lambda: [vprev_scr[e] for e in range(N_EXP)]
            has_prev = jnp.logical_not(first_in_batch)
        else:
            kprev = lambda b=blocks[sb - 1]: b.ke
            vprev = lambda b=blocks[sb - 1]: b.ve
            has_prev = True
        blocks.append(_MixerBlock(
            load, store, cos_ref[r0:r0 + CHUNK, :], sin_ref[r0:r0 + CHUNK, :], lng, lnb,
            ws_ref, bst, sink_ref, kprev, vprev, has_prev, d_a=d_a, d_b=d_b, d_kv=d_kv))

    def proj_chunk(n0):
        acc = jnp.dot(h_scr[...], win_ref[:, n0:n0 + TN_DOT], preferred_element_type=F32)
        proj_w[:, n0:n0 + TN_DOT] = acc.astype(BF16)

    chunks = chunks[:len(chunks) - N_CARRY]
    next_chunk = [0]

    def emit_chunks(n):
        for _ in range(n):
            if do_a and next_chunk[0] < len(chunks):
                proj_chunk(chunks[next_chunk[0]])
                next_chunk[0] += 1

    if not do_bc:
        emit_chunks(len(chunks))
        xprev_scr[...] = x
        return

    for b in blocks:
        b.prep()
    n_pairs = d_b // LANES
    slots = [(b, p0) for b in blocks for p0 in range(0, n_pairs, COLS_PER_KV)]
    group_pairs = [(b, g0) for b in blocks for g0 in range(0, A_GROUPS, 2)]
    pairs_per_slot = -(-len(group_pairs) // len(slots))
    slots[0][0].scores(slots[0][1])
    for j, (b, p0) in enumerate(slots):
        emit_chunks(1)
        b.values(p0)
        if j + 1 < len(slots):
            slots[j + 1][0].scores(slots[j + 1][1])
        for bg, g0 in group_pairs[j * pairs_per_slot:(j + 1) * pairs_per_slot]:
            bg.group_pair(g0)
    for e in range(N_EXP):
        kprev_scr[e] = blocks[-1].ke[e]
        vprev_scr[e] = blocks[-1].ve[e]
    emit_chunks(len(chunks) - next_chunk[0] - 1)

    row_c = pl.ds(blk_c // blocks_per_batch, 1)
    gate = modc_ref[row_c, 2 * d_model:3 * d_model]
    ssq = jnp.zeros((T_BLK, 1), F32)
    for n0 in range(0, d_model, TN_DOT):
        sl = slice(n0, n0 + TN_DOT)
        acc = jnp.dot(y_scr[...], wout_ref[:, sl], preferred_element_type=F32)
        xr = xprev_scr[:, sl] + gate[:, sl] * acc
        ssq = ssq + jnp.sum(xr * xr, axis=-1, keepdims=True)
        o_ref[:, sl] = xr
    emit_chunks(len(chunks))
    inv = lax.rsqrt(ssq * (1.0 / d_model) + NORM_EPS)
    shift_f = modf_ref[row_c, 0:d_model]
    scale_f = modf_ref[row_c, d_model:2 * d_model]
    o_ref[...] = o_ref[...] * inv * (fg_ref[...] * (1.0 + scale_f)) + shift_f

    if do_a:
        xprev_scr[...] = x


def _layer(x2, sinks, mod, mod_f, norm_g, final_g, cos_t, sin_t, ln_g, ln_b, w_sp, b_sp_t,
           w_in, w_out, seq, d_a, d_b, d_kv):
    rows, d_model = x2.shape
    d_in = w_in.shape[-1]
    d_mix = d_a + d_b
    n_blk = rows // T_BLK
    bpb = seq // T_BLK
    assert d_model % W_TILES == 0 and d_mix % W_TILES == 0

    def blk_a(i):
        return jnp.clip(i - W_TILES, 0, n_blk - 1)

    def blk_c(i):
        return jnp.clip(i - W_TILES - 1, 0, n_blk - 1)

    def w_tile(i):
        return (jnp.minimum(i, W_TILES - 1), 0)

    const2 = lambda i: (0, 0)
    kern = functools.partial(_layer_kernel, d_model=d_model, d_in=d_in, d_a=d_a, d_b=d_b,
                             d_kv=d_kv, blocks_per_batch=bpb, n_blk=n_blk)
    return pl.pallas_call(
        kern,
        grid=(W_TILES + n_blk + 1,),
        in_specs=[
            pl.BlockSpec(memory_space=pltpu.SMEM),
            pl.BlockSpec((T_BLK, d_model), lambda i: (blk_a(i), 0)),
            pl.BlockSpec(mod.shape, const2),
            pl.BlockSpec(mod.shape, const2),
            pl.BlockSpec(mod_f.shape, const2),
            pl.BlockSpec((1, d_model), const2),
            pl.BlockSpec((1, d_model), const2),
            pl.BlockSpec((T_BLK, LANES), lambda i: (blk_c(i) % bpb, 0)),
            pl.BlockSpec((T_BLK, LANES), lambda i: (blk_c(i) % bpb, 0)),
            pl.BlockSpec((1, d_a), const2),
            pl.BlockSpec((1, d_a), const2),
            pl.BlockSpec((A_GROUPS, CHUNK, CHUNK), lambda i: (0, 0, 0)),
            pl.BlockSpec((CHUNK, A_GROUPS), const2),
            pl.BlockSpec((d_model // W_TILES, d_in), w_tile),
            pl.BlockSpec((d_mix // W_TILES, d_model), w_tile),
        ],
        out_specs=pl.BlockSpec((T_BLK, d_model), lambda i: (blk_c(i), 0)),
        out_shape=jax.ShapeDtypeStruct((rows, d_model), F32),
        scratch_shapes=[
            pltpu.VMEM((d_model, d_in), BF16),
            pltpu.VMEM((d_mix, d_model), BF16),
            pltpu.VMEM((T_BLK, d_in), BF16),
            pltpu.VMEM((T_BLK, d_in), BF16),
            pltpu.VMEM((T_BLK, d_mix), BF16),
            pltpu.VMEM((T_BLK, d_model), F32),
            pltpu.VMEM((T_BLK, d_model), BF16),
            pltpu.VMEM((N_EXP, CHUNK, LANES), BF16),
            pltpu.VMEM((N_EXP, CHUNK, LANES), BF16),
        ],
        compiler_params=pltpu.CompilerParams(
            dimension_semantics=("arbitrary",), vmem_limit_bytes=VMEM_LIMIT_LAYER),
        name="layer",
    )(sinks, x2, mod, mod, mod_f, norm_g.reshape(1, d_model), final_g.reshape(1, d_model),
      cos_t, sin_t, ln_g, ln_b, w_sp, b_sp_t, w_in, w_out)


def _rope_tables(seq):
    half = HEAD_DIM // 2
    inv_freq = ROPE_THETA ** (-np.arange(0, HEAD_DIM, 2, dtype=np.float64) / HEAD_DIM)
    ang = np.arange(seq, dtype=np.float64)[:, None] * inv_freq[None, :]
    cos = np.cos(ang)
    sin = np.sin(ang)
    cos_t = np.tile(cos, (1, LANES // half))
    sin_t = np.tile(np.concatenate([-sin, sin], axis=1), (1, HEADS_PER_VREG))
    return jnp.asarray(cos_t, F32), jnp.asarray(sin_t, F32)


def kernel(x, c, w_ada, b_ada, norm_g, w_in, ln_v_g, ln_v_b, w_spatial, b_spatial, sinks,
           w_out, w_ada_final, b_ada_final, final_norm_g):
    bsz, seq, d_model = x.shape
    assert w_ada.shape[0] == 1, "single-layer stack"
    d_a = ln_v_g.shape[-1]
    d_mix = w_out.shape[-2]
    d_b = d_mix - d_a
    d_kv = N_KV_HEADS * HEAD_DIM
    d_in = w_in.shape[-1]
    assert d_in == 3 * d_a + 2 * d_b + 2 * d_kv
    assert d_a == A_GROUPS * A_GROUP_W and d_b == N_KV_HEADS * Q_PER_KV * HEAD_DIM
    assert seq % T_BLK == 0 and T_BLK % CHUNK == 0

    x2 = x.reshape(bsz * seq, d_model)
    mod, mod_f = _ada_mod(c, w_ada, b_ada, w_ada_final, b_ada_final)
    cos_t, sin_t = _rope_tables(seq)
    out = _layer(x2, sinks.reshape(-1), mod, mod_f, norm_g, final_norm_g, cos_t, sin_t,
                 ln_v_g.reshape(1, d_a), ln_v_b.reshape(1, d_a),
                 w_spatial.reshape(A_GROUPS, CHUNK, CHUNK),
                 b_spatial.reshape(A_GROUPS, CHUNK).T,
                 w_in.reshape(d_model, d_in), w_out.reshape(d_mix, d_model),
                 seq, d_a, d_b, d_kv)
    return out.reshape(bsz, seq, d_model)
```

```python
import functools

import jax
import jax.numpy as jnp
import numpy as np
from jax import lax
from jax.experimental import pallas as pl
from jax.experimental.pallas import tpu as pltpu

F32 = jnp.float32
BF16 = jnp.bfloat16

CHUNK = 128
A_GROUPS = 8
A_GROUP_W = 128
HEAD_DIM = 64
N_KV_HEADS = 4
Q_PER_KV = 4
ROPE_THETA = 10000.0
NORM_EPS = 1e-5
LOG2E = 1.4426950408889634

LANES = 128
HEADS_PER_VREG = LANES // HEAD_DIM
N_EXP = N_KV_HEADS * HEADS_PER_VREG
COLS_PER_KV = Q_PER_KV // HEADS_PER_VREG

T_BLK = 256
TN_DOT = 512
N_CARRY = 2
W_TILES = 16
W_IN_PARTS = 4
TN_ADA = 1024
ADA_ROW_SPLITS = 8

MIB = 1024 * 1024
V7X_VMEM_BYTES = 64 * MIB
VMEM_LIMIT_ADA = 40 * MIB
VMEM_LIMIT_LAYER = V7X_VMEM_BYTES - MIB


def _silu(z):
    hz = 0.5 * z
    return hz + hz * jnp.tanh(hz)


def _ada_kernel(c_ref, *refs, n_a_tiles):
    wa = refs[:ADA_ROW_SPLITS]
    wf = refs[ADA_ROW_SPLITS:2 * ADA_ROW_SPLITS]
    ba_ref, bf_ref, oa_ref, of_ref = refs[2 * ADA_ROW_SPLITS:]
    j = pl.program_id(0)
    c = c_ref[...]
    ca = (c * (1.0 / (1.0 + jnp.exp(-c)))).astype(BF16)
    part = ca.shape[1] // ADA_ROW_SPLITS

    def mod(ws, b_ref, o_ref):
        acc = b_ref[...]
        for k, w in enumerate(ws):
            acc = acc + jnp.dot(ca[:, k * part:(k + 1) * part], w[...].astype(BF16),
                                preferred_element_type=F32)
        o_ref[...] = acc

    @pl.when(j < n_a_tiles)
    def _():
        mod(wa, ba_ref, oa_ref)

    @pl.when(j >= n_a_tiles)
    def _():
        mod(wf, bf_ref, of_ref)


def _ada_mod(c, w_a, b_a, w_f, b_f):
    bsz, d = c.shape
    n_a, n_f = w_a.shape[-1], w_f.shape[-1]
    w_a = w_a.reshape(d, n_a)
    w_f = w_f.reshape(d, n_f)
    ta, tf = n_a // TN_ADA, n_f // TN_ADA
    part = d // ADA_ROW_SPLITS
    a_tile = lambda j: jnp.minimum(j, ta - 1)
    f_tile = lambda j: jnp.maximum(j - ta, 0)
    return pl.pallas_call(
        functools.partial(_ada_kernel, n_a_tiles=ta),
        grid=(ta + tf,),
        in_specs=[
            pl.BlockSpec((bsz, d), lambda j: (0, 0)),
            *[pl.BlockSpec((part, TN_ADA), lambda j, k=k: (k, a_tile(j)))
              for k in range(ADA_ROW_SPLITS)],
            *[pl.BlockSpec((part, TN_ADA), lambda j, k=k: (k, f_tile(j)))
              for k in range(ADA_ROW_SPLITS)],
            pl.BlockSpec((1, TN_ADA), lambda j: (0, a_tile(j))),
            pl.BlockSpec((1, TN_ADA), lambda j: (0, f_tile(j))),
        ],
        out_specs=[
            pl.BlockSpec((bsz, TN_ADA), lambda j: (0, a_tile(j))),
            pl.BlockSpec((bsz, TN_ADA), lambda j: (0, f_tile(j))),
        ],
        out_shape=[jax.ShapeDtypeStruct((bsz, n_a), F32),
                   jax.ShapeDtypeStruct((bsz, n_f), F32)],
        compiler_params=pltpu.CompilerParams(
            dimension_semantics=("arbitrary",), vmem_limit_bytes=VMEM_LIMIT_ADA),
        name="ada_mod",
    )(c, *([w_a] * ADA_ROW_SPLITS), *([w_f] * ADA_ROW_SPLITS),
      b_a.reshape(1, n_a), b_f.reshape(1, n_f))


class _MixerBlock:
    def __init__(self, load, store, cos, sin, lng, lnb, ws_ref, bst, sink_ref, kprev, vprev,
                 has_prev, *, d_a, d_b, d_kv):
        self.load, self.store = load, store
        self.cos, self.sin, self.lng, self.lnb = cos, sin, lng, lnb
        q_scale = LOG2E * HEAD_DIM ** -0.5
        self.cos_q, self.sin_q = cos * q_scale, sin * q_scale
        self.ws_ref, self.bst, self.sink_ref = ws_ref, bst, sink_ref
        self.kprev, self.vprev, self.has_prev = kprev, vprev, has_prev
        self.d_a, self.d_b, self.d_kv = d_a, d_b, d_kv
        self.o_u, self.o_v, self.o_za = 0, d_a, 2 * d_a
        self.o_q = 3 * d_a
        self.o_k = self.o_q + d_b
        self.o_vv = self.o_k + d_kv
        self.o_zb = self.o_vv + d_kv
        row = lax.broadcasted_iota(jnp.int32, (CHUNK, CHUNK), 0)
        col = lax.broadcasted_iota(jnp.int32, (CHUNK, CHUNK), 1)
        self.col = col
        self.causal = col <= row
        self.first_half = (col & (HEAD_DIM - 1)) < (HEAD_DIM // 2)
        self.probs = {}

    def _rope(self, xv, query=False):
        cos, sin = (self.cos_q, self.sin_q) if query else (self.cos, self.sin)
        rot = jnp.where(self.first_half,
                        pltpu.roll(xv, LANES - HEAD_DIM // 2, 1),
                        pltpu.roll(xv, HEAD_DIM // 2, 1))
        return xv * cos + rot * sin

    def prep(self):
        va = self.load(self.o_v, self.o_v + self.d_a).astype(F32)
        mu = jnp.mean(va, axis=-1, keepdims=True)
        vc = va - mu
        var = jnp.mean(vc * vc, axis=-1, keepdims=True)
        self.vn = (vc * lax.rsqrt(var + NORM_EPS) * self.lng + self.lnb).astype(BF16)
        self.tril = self.causal.astype(F32)
        low_half = self.col < HEAD_DIM
        self.ke = [None] * N_EXP
        self.ve = [None] * N_EXP
        for c in range(self.d_kv // LANES):
            kc = self._rope(self.load(self.o_k + c * LANES, self.o_k + (c + 1) * LANES)
                            .astype(F32))
            vcol = self.load(self.o_vv + c * LANES, self.o_vv + (c + 1) * LANES).astype(F32)
            kc_sw = pltpu.roll(kc, HEAD_DIM, 1)
            vcol_sw = pltpu.roll(vcol, HEAD_DIM, 1)
            for j in range(HEADS_PER_VREG):
                kv_head = c * HEADS_PER_VREG + j
                for o in range(HEADS_PER_VREG):
                    mask = low_half if o == 0 else jnp.logical_not(low_half)
                    e = kv_head * HEADS_PER_VREG + o
                    self.ke[e] = jnp.where(mask, kc if o == j else kc_sw, 0.0).astype(BF16)
                    self.ve[e] = jnp.where(mask, vcol if o == j else vcol_sw, 0.0).astype(BF16)

    def group_pair(self, g0):
        c0 = g0 * A_GROUP_W
        w = jnp.concatenate([(self.ws_ref[g0] * self.tril).astype(BF16),
                             (self.ws_ref[g0 + 1] * self.tril).astype(BF16)], axis=1)
        zero = jnp.zeros((CHUNK, A_GROUP_W), BF16)
        rhs = jnp.concatenate(
            [jnp.concatenate([self.vn[:, c0:c0 + A_GROUP_W], zero], axis=1),
             jnp.concatenate([zero, self.vn[:, c0 + A_GROUP_W:c0 + 2 * A_GROUP_W]], axis=1)],
            axis=0)
        s2 = jnp.dot(w, rhs, preferred_element_type=F32)
        for k in range(2):
            g = g0 + k
            cg = g * A_GROUP_W
            s = s2[:, k * A_GROUP_W:(k + 1) * A_GROUP_W] + self.bst[:, g:g + 1]
            u = self.load(self.o_u + cg, self.o_u + cg + A_GROUP_W).astype(F32)
            za = self.load(self.o_za + cg, self.o_za + cg + A_GROUP_W).astype(F32)
            self.store(cg, (u * s * _silu(za)).astype(BF16))

    def scores(self, p0):
        kv_head = (p0 * HEADS_PER_VREG) // Q_PER_KV
        kprev = self.kprev()
        q4 = jnp.concatenate(
            [self._rope(self.load(self.o_q + p * LANES, self.o_q + (p + 1) * LANES)
                        .astype(F32), query=True).astype(BF16)
             for p in range(p0, p0 + COLS_PER_KV)], axis=0)
        probs = [[] for _ in range(COLS_PER_KV)]
        for o in range(HEADS_PER_VREG):
            e = kv_head * HEADS_PER_VREG + o
            kband = jnp.concatenate([kprev[e], self.ke[e]], axis=0)
            s4 = lax.dot_general(q4, kband, (((1,), (1,)), ((), ())),
                                 preferred_element_type=F32)
            for a in range(COLS_PER_KV):
                s = s4[a * CHUNK:(a + 1) * CHUNK]
                h = (p0 + a) * HEADS_PER_VREG + o
                s_prev = s[:, :CHUNK]
                if self.has_prev is not True:
                    s_prev = jnp.where(self.has_prev, s_prev, -jnp.inf)
                comb = jnp.where(self.causal, s[:, CHUNK:], s_prev)
                sink = self.sink_ref[h] * LOG2E
                m = jnp.maximum(jnp.max(comb, axis=-1, keepdims=True), sink)
                pexp = jnp.exp2(comb - m)
                denom = jnp.sum(pexp, axis=-1, keepdims=True) + jnp.exp2(sink - m)
                pn = (pexp * (1.0 / denom)).astype(BF16)
                zero = jnp.zeros_like(pn)
                probs[a].append(jnp.where(self.causal, zero, pn))
                probs[a].append(jnp.where(self.causal, pn, zero))
        self.probs[p0] = jnp.concatenate(
            [jnp.concatenate(pa, axis=1) for pa in probs], axis=0)

    def values(self, p0):
        kv_head = (p0 * HEADS_PER_VREG) // Q_PER_KV
        vprev = self.vprev()
        vband = []
        for o in range(HEADS_PER_VREG):
            e = kv_head * HEADS_PER_VREG + o
            vband.append(vprev[e])
            vband.append(self.ve[e])
        out4 = jnp.dot(self.probs.pop(p0), jnp.concatenate(vband, axis=0),
                       preferred_element_type=F32)
        for a in range(COLS_PER_KV):
            p = p0 + a
            zb = self.load(self.o_zb + p * LANES, self.o_zb + (p + 1) * LANES).astype(F32)
            self.store(self.d_a + p * LANES,
                       (out4[a * CHUNK:(a + 1) * CHUNK] * _silu(zb)).astype(BF16))


def _layer_kernel(sink_ref, x_ref, moda_ref, modc_ref, modf_ref, ng_ref, fg_ref,
                  cos_ref, sin_ref, lng_ref, lnb_ref, ws_ref, bst_ref, *refs,
                  d_model, d_in, d_a, d_b, d_kv, blocks_per_batch, n_blk):
    win32_refs = refs[:W_IN_PARTS]
    (wout32_ref, o_ref, win_scr, wout_scr, proj0_scr, proj1_scr, y_scr, xprev_scr, h_scr,
     kprev_scr, vprev_scr) = refs[W_IN_PARTS:]
    i = pl.program_id(0)

    @pl.when(i < W_TILES)
    def _():
        rows = win32_refs[0].shape[0]
        for k, win32_ref in enumerate(win32_refs):
            r_in = pl.multiple_of((i * W_IN_PARTS + k) * rows, rows)
            win_scr[pl.ds(r_in, rows), :] = win32_ref[...].astype(BF16)
        r_out = pl.multiple_of(i * wout32_ref.shape[0], wout32_ref.shape[0])
        wout_scr[pl.ds(r_out, wout32_ref.shape[0]), :] = wout32_ref[...].astype(BF16)

    step = i - W_TILES

    args = (sink_ref, x_ref, moda_ref, modc_ref, modf_ref, ng_ref, fg_ref, cos_ref, sin_ref,
            lng_ref, lnb_ref, ws_ref, bst_ref, win_scr, wout_scr, o_ref)
    scr = (y_scr, xprev_scr, h_scr, kprev_scr, vprev_scr)
    dims = dict(d_model=d_model, d_in=d_in, d_a=d_a, d_b=d_b, d_kv=d_kv,
                blocks_per_batch=blocks_per_batch, n_blk=n_blk)
    bufs = ((proj0_scr, proj1_scr), (proj1_scr, proj0_scr))
    inner = jnp.logical_and(step > 0, step < n_blk)

    @pl.when(step == 0)
    def _():
        kprev_scr[...] = jnp.zeros_like(kprev_scr)
        vprev_scr[...] = jnp.zeros_like(vprev_scr)
        _layer_step(step, *args, *bufs[0], *scr, **dims, do_bc=False)

    @pl.when(jnp.logical_and(inner, step % 2 == 0))
    def _():
        _layer_step(step, *args, *bufs[0], *scr, **dims)

    @pl.when(jnp.logical_and(inner, step % 2 == 1))
    def _():
        _layer_step(step, *args, *bufs[1], *scr, **dims)

    @pl.when(step == n_blk)
    def _():
        _layer_step(step, *args, *bufs[n_blk % 2], *scr, **dims, do_a=False)


def _layer_step(step, sink_ref, x_ref, moda_ref, modc_ref, modf_ref, ng_ref, fg_ref,
                cos_ref, sin_ref, lng_ref, lnb_ref, ws_ref, bst_ref, win_ref, wout_ref,
                o_ref, proj_w, proj_r, y_scr, xprev_scr, h_scr, kprev_scr, vprev_scr,
                *, d_model, d_in, d_a, d_b, d_kv, blocks_per_batch, n_blk,
                do_a=True, do_bc=True):
    blk_c = jnp.clip(step - 1, 0, n_blk - 1)
    first_in_batch = (blk_c % blocks_per_batch) == 0

    chunks = list(range(0, d_in, TN_DOT))
    carried = chunks[len(chunks) - N_CARRY:]
    if do_bc:
        for n0 in carried:
            acc = jnp.dot(h_scr[...], win_ref[:, n0:n0 + TN_DOT], preferred_element_type=F32)
            proj_r[:, n0:n0 + TN_DOT] = acc.astype(BF16)

    if do_a:
        x = x_ref[...]
        ms = jnp.mean(x * x, axis=-1, keepdims=True)
        row_a = pl.ds(jnp.minimum(step, n_blk - 1) // blocks_per_batch, 1)
        shift = moda_ref[row_a, 0:d_model]
        gain = ng_ref[...] * (1.0 + moda_ref[row_a, d_model:2 * d_model])
        h_scr[...] = (x * lax.rsqrt(ms + NORM_EPS) * gain + shift).astype(BF16)

    lng = lng_ref[...]
    lnb = lnb_ref[...]
    bst = bst_ref[...]
    n_sub = T_BLK // CHUNK
    blocks = []
    for sb in range(n_sub):
        r0 = sb * CHUNK

        def load(c0, c1, r0=r0):
            return proj_r[r0:r0 + CHUNK, c0:c1]

        def store(c0, val, r0=r0):
            y_scr[r0:r0 + CHUNK, c0:c0 + val.shape[1]] = val

        if sb == 0:
            kprev = lambda: [kprev_scr[e] for e in range(N_EXP)]
            vprev = lambda: [vprev_scr[e] for e in range(N_EXP)]
            has_prev = jnp.logical_not(first_in_batch)
        else:
            kprev = lambda b=blocks[sb - 1]: b.ke
            vprev = lambda b=blocks[sb - 1]: b.ve
            has_prev = True
        blocks.append(_MixerBlock(
            load, store, cos_ref[r0:r0 + CHUNK, :], sin_ref[r0:r0 + CHUNK, :], lng, lnb,
            ws_ref, bst, sink_ref, kprev, vprev, has_prev, d_a=d_a, d_b=d_b, d_kv=d_kv))

    def proj_chunk(n0):
        acc = jnp.dot(h_scr[...], win_ref[:, n0:n0 + TN_DOT], preferred_element_type=F32)
        proj_w[:, n0:n0 + TN_DOT] = acc.astype(BF16)

    chunks = chunks[:len(chunks) - N_CARRY]
    next_chunk = [0]

    def emit_chunks(n):
        for _ in range(n):
            if do_a and next_chunk[0] < len(chunks):
                proj_chunk(chunks[next_chunk[0]])
                next_chunk[0] += 1

    if not do_bc:
        emit_chunks(len(chunks))
        xprev_scr[...] = x
        return

    for b in blocks:
        b.prep()
    n_pairs = d_b // LANES
    slots = [(b, p0) for b in blocks for p0 in range(0, n_pairs, COLS_PER_KV)]
    group_pairs = [(b, g0) for b in blocks for g0 in range(0, A_GROUPS, 2)]
    pairs_per_slot = -(-len(group_pairs) // len(slots))
    slots[0][0].scores(slots[0][1])
    for j, (b, p0) in enumerate(slots):
        emit_chunks(1)
        b.values(p0)
        if j + 1 < len(slots):
            slots[j + 1][0].scores(slots[j + 1][1])
        for bg, g0 in group_pairs[j * pairs_per_slot:(j + 1) * pairs_per_slot]:
            bg.group_pair(g0)
    for e in range(N_EXP):
        kprev_scr[e] = blocks[-1].ke[e]
        vprev_scr[e] = blocks[-1].ve[e]
    emit_chunks(len(chunks) - next_chunk[0] - 1)

    row_c = pl.ds(blk_c // blocks_per_batch, 1)
    gate = modc_ref[row_c, 2 * d_model:3 * d_model]
    ssq = jnp.zeros((T_BLK, 1), F32)
    for n0 in range(0, d_model, TN_DOT):
        sl = slice(n0, n0 + TN_DOT)
        acc = jnp.dot(y_scr[...], wout_ref[:, sl], preferred_element_type=F32)
        xr = xprev_scr[:, sl] + gate[:, sl] * acc
        ssq = ssq + jnp.sum(xr * xr, axis=-1, keepdims=True)
        o_ref[:, sl] = xr
    emit_chunks(len(chunks))
    inv = lax.rsqrt(ssq * (1.0 / d_model) + NORM_EPS)
    shift_f = modf_ref[row_c, 0:d_model]
    scale_f = modf_ref[row_c, d_model:2 * d_model]
    o_ref[...] = o_ref[...] * inv * (fg_ref[...] * (1.0 + scale_f)) + shift_f

    if do_a:
        xprev_scr[...] = x


def _layer(x2, sinks, mod, mod_f, norm_g, final_g, cos_t, sin_t, ln_g, ln_b, w_sp, b_sp_t,
           w_in, w_out, seq, d_a, d_b, d_kv):
    rows, d_model = x2.shape
    d_in = w_in.shape[-1]
    d_mix = d_a + d_b
    n_blk = rows // T_BLK
    bpb = seq // T_BLK
    assert d_model % (W_TILES * W_IN_PARTS) == 0 and d_mix % W_TILES == 0

    def blk_a(i):
        return jnp.clip(i - W_TILES, 0, n_blk - 1)

    def blk_c(i):
        return jnp.clip(i - W_TILES - 1, 0, n_blk - 1)

    def w_tile(i):
        return (jnp.minimum(i, W_TILES - 1), 0)

    const2 = lambda i: (0, 0)
    kern = functools.partial(_layer_kernel, d_model=d_model, d_in=d_in, d_a=d_a, d_b=d_b,
                             d_kv=d_kv, blocks_per_batch=bpb, n_blk=n_blk)
    return pl.pallas_call(
        kern,
        grid=(W_TILES + n_blk + 1,),
        in_specs=[
            pl.BlockSpec(memory_space=pltpu.SMEM),
            pl.BlockSpec((T_BLK, d_model), lambda i: (blk_a(i), 0)),
            pl.BlockSpec(mod.shape, const2),
            pl.BlockSpec(mod.shape, const2),
            pl.BlockSpec(mod_f.shape, const2),
            pl.BlockSpec((1, d_model), const2),
            pl.BlockSpec((1, d_model), const2),
            pl.BlockSpec((T_BLK, LANES), lambda i: (blk_c(i) % bpb, 0)),
            pl.BlockSpec((T_BLK, LANES), lambda i: (blk_c(i) % bpb, 0)),
            pl.BlockSpec((1, d_a), const2),
            pl.BlockSpec((1, d_a), const2),
            pl.BlockSpec((A_GROUPS, CHUNK, CHUNK), lambda i: (0, 0, 0)),
            pl.BlockSpec((CHUNK, A_GROUPS), const2),
            *[pl.BlockSpec((d_model // (W_TILES * W_IN_PARTS), d_in),
                           lambda i, k=k: (w_tile(i)[0] * W_IN_PARTS + k, 0))
              for k in range(W_IN_PARTS)],
            pl.BlockSpec((d_mix // W_TILES, d_model), w_tile),
        ],
        out_specs=pl.BlockSpec((T_BLK, d_model), lambda i: (blk_c(i), 0)),
        out_shape=jax.ShapeDtypeStruct((rows, d_model), F32),
        scratch_shapes=[
            pltpu.VMEM((d_model, d_in), BF16),
            pltpu.VMEM((d_mix, d_model), BF16),
            pltpu.VMEM((T_BLK, d_in), BF16),
            pltpu.VMEM((T_BLK, d_in), BF16),
            pltpu.VMEM((T_BLK, d_mix), BF16),
            pltpu.VMEM((T_BLK, d_model), F32),
            pltpu.VMEM((T_BLK, d_model), BF16),
            pltpu.VMEM((N_EXP, CHUNK, LANES), BF16),
            pltpu.VMEM((N_EXP, CHUNK, LANES), BF16),
        ],
        compiler_params=pltpu.CompilerParams(
            dimension_semantics=("arbitrary",), vmem_limit_bytes=VMEM_LIMIT_LAYER),
        name="layer",
    )(sinks, x2, mod, mod, mod_f, norm_g.reshape(1, d_model), final_g.reshape(1, d_model),
      cos_t, sin_t, ln_g, ln_b, w_sp, b_sp_t, *([w_in] * W_IN_PARTS), w_out)


def _rope_tables(seq):
    half = HEAD_DIM // 2
    inv_freq = ROPE_THETA ** (-np.arange(0, HEAD_DIM, 2, dtype=np.float64) / HEAD_DIM)
    ang = np.arange(seq, dtype=np.float64)[:, None] * inv_freq[None, :]
    cos = np.cos(ang)
    sin = np.sin(ang)
    cos_t = np.tile(cos, (1, LANES // half))
    sin_t = np.tile(np.concatenate([-sin, sin], axis=1), (1, HEADS_PER_VREG))
    return jnp.asarray(cos_t, F32), jnp.asarray(sin_t, F32)


def kernel(x, c, w_ada, b_ada, norm_g, w_in, ln_v_g, ln_v_b, w_spatial, b_spatial, sinks,
           w_out, w_ada_final, b_ada_final, final_norm_g):
    bsz, seq, d_model = x.shape
    assert w_ada.shape[0] == 1, "single-layer stack"
    d_a = ln_v_g.shape[-1]
    d_mix = w_out.shape[-2]
    d_b = d_mix - d_a
    d_kv = N_KV_HEADS * HEAD_DIM
    d_in = w_in.shape[-1]
    assert d_in == 3 * d_a + 2 * d_b + 2 * d_kv
    assert d_a == A_GROUPS * A_GROUP_W and d_b == N_KV_HEADS * Q_PER_KV * HEAD_DIM
    assert seq % T_BLK == 0 and T_BLK % CHUNK == 0

    x2 = x.reshape(bsz * seq, d_model)
    mod, mod_f = _ada_mod(c, w_ada, b_ada, w_ada_final, b_ada_final)
    cos_t, sin_t = _rope_tables(seq)
    out = _layer(x2, sinks.reshape(-1), mod, mod_f, norm_g, final_norm_g, cos_t, sin_t,
                 ln_v_g.reshape(1, d_a), ln_v_b.reshape(1, d_a),
                 w_spatial.reshape(A_GROUPS, CHUNK, CHUNK),
                 b_spatial.reshape(A_GROUPS, CHUNK).T,
                 w_in.reshape(d_model, d_in), w_out.reshape(d_mix, d_model),
                 seq, d_a, d_b, d_kv)
    return out.reshape(bsz, seq, d_model)
```

```python
import functools

import jax
import jax.numpy as jnp
import numpy as np
from jax import lax
from jax.experimental import pallas as pl
from jax.experimental.pallas import tpu as pltpu

F32 = jnp.float32
BF16 = jnp.bfloat16

CHUNK = 128
A_GROUPS = 8
A_GROUP_W = 128
HEAD_DIM = 64
N_KV_HEADS = 4
Q_PER_KV = 4
ROPE_THETA = 10000.0
NORM_EPS = 1e-5
LOG2E = 1.4426950408889634

LANES = 128
HEADS_PER_VREG = LANES // HEAD_DIM
N_EXP = N_KV_HEADS * HEADS_PER_VREG
COLS_PER_KV = Q_PER_KV // HEADS_PER_VREG

T_BLK = 256
TN_DOT = 512
N_CARRY = 2
W_TILES = 16
TN_ADA = 1024
ADA_ROW_SPLITS = 8

MIB = 1024 * 1024
V7X_VMEM_BYTES = 64 * MIB
VMEM_LIMIT_ADA = 40 * MIB
VMEM_LIMIT_LAYER = V7X_VMEM_BYTES - MIB


def _silu(z):
    hz = 0.5 * z
    return hz + hz * jnp.tanh(hz)


def _ada_kernel(c_ref, *refs, n_a_tiles):
    wa = refs[:ADA_ROW_SPLITS]
    wf = refs[ADA_ROW_SPLITS:2 * ADA_ROW_SPLITS]
    ba_ref, bf_ref, oa_ref, of_ref = refs[2 * ADA_ROW_SPLITS:]
    j = pl.program_id(0)
    c = c_ref[...]
    ca = (c * (1.0 / (1.0 + jnp.exp(-c)))).astype(BF16)
    part = ca.shape[1] // ADA_ROW_SPLITS

    def mod(ws, b_ref, o_ref):
        acc = b_ref[...]
        for k, w in enumerate(ws):
            acc = acc + jnp.dot(ca[:, k * part:(k + 1) * part], w[...].astype(BF16),
                                preferred_element_type=F32)
        o_ref[...] = acc

    @pl.when(j < n_a_tiles)
    def _():
        mod(wa, ba_ref, oa_ref)

    @pl.when(j >= n_a_tiles)
    def _():
        mod(wf, bf_ref, of_ref)


def _ada_mod(c, w_a, b_a, w_f, b_f):
    bsz, d = c.shape
    n_a, n_f = w_a.shape[-1], w_f.shape[-1]
    w_a = w_a.reshape(d, n_a)
    w_f = w_f.reshape(d, n_f)
    ta, tf = n_a // TN_ADA, n_f // TN_ADA
    part = d // ADA_ROW_SPLITS
    a_tile = lambda j: jnp.minimum(j, ta - 1)
    f_tile = lambda j: jnp.maximum(j - ta, 0)
    return pl.pallas_call(
        functools.partial(_ada_kernel, n_a_tiles=ta),
        grid=(ta + tf,),
        in_specs=[
            pl.BlockSpec((bsz, d), lambda j: (0, 0)),
            *[pl.BlockSpec((part, TN_ADA), lambda j, k=k: (k, a_tile(j)))
              for k in range(ADA_ROW_SPLITS)],
            *[pl.BlockSpec((part, TN_ADA), lambda j, k=k: (k, f_tile(j)))
              for k in range(ADA_ROW_SPLITS)],
            pl.BlockSpec((1, TN_ADA), lambda j: (0, a_tile(j))),
            pl.BlockSpec((1, TN_ADA), lambda j: (0, f_tile(j))),
        ],
        out_specs=[
            pl.BlockSpec((bsz, TN_ADA), lambda j: (0, a_tile(j))),
            pl.BlockSpec((bsz, TN_ADA), lambda j: (0, f_tile(j))),
        ],
        out_shape=[jax.ShapeDtypeStruct((bsz, n_a), F32),
                   jax.ShapeDtypeStruct((bsz, n_f), F32)],
        compiler_params=pltpu.CompilerParams(
            dimension_semantics=("arbitrary",), vmem_limit_bytes=VMEM_LIMIT_ADA),
        name="ada_mod",
    )(c, *([w_a] * ADA_ROW_SPLITS), *([w_f] * ADA_ROW_SPLITS),
      b_a.reshape(1, n_a), b_f.reshape(1, n_f))


class _MixerBlock:
    def __init__(self, load, store, cos, sin, lng, lnb, ws_ref, bst, sink_ref, kprev, vprev,
                 has_prev, *, d_a, d_b, d_kv):
        self.load, self.store = load, store
        self.cos, self.sin, self.lng, self.lnb = cos, sin, lng, lnb
        q_scale = LOG2E * HEAD_DIM ** -0.5
        self.cos_q, self.sin_q = cos * q_scale, sin * q_scale
        self.ws_ref, self.bst, self.sink_ref = ws_ref, bst, sink_ref
        self.kprev, self.vprev, self.has_prev = kprev, vprev, has_prev
        self.d_a, self.d_b, self.d_kv = d_a, d_b, d_kv
        self.o_u, self.o_v, self.o_za = 0, d_a, 2 * d_a
        self.o_q = 3 * d_a
        self.o_k = self.o_q + d_b
        self.o_vv = self.o_k + d_kv
        self.o_zb = self.o_vv + d_kv
        row = lax.broadcasted_iota(jnp.int32, (CHUNK, CHUNK), 0)
        col = lax.broadcasted_iota(jnp.int32, (CHUNK, CHUNK), 1)
        self.col = col
        self.causal = col <= row
        self.first_half = (col & (HEAD_DIM - 1)) < (HEAD_DIM // 2)
        self.probs = {}

    def _rope(self, xv, query=False):
        cos, sin = (self.cos_q, self.sin_q) if query else (self.cos, self.sin)
        rot = jnp.where(self.first_half,
                        pltpu.roll(xv, LANES - HEAD_DIM // 2, 1),
                        pltpu.roll(xv, HEAD_DIM // 2, 1))
        return xv * cos + rot * sin

    def prep(self):
        va = self.load(self.o_v, self.o_v + self.d_a).astype(F32)
        mu = jnp.mean(va, axis=-1, keepdims=True)
        vc = va - mu
        var = jnp.mean(vc * vc, axis=-1, keepdims=True)
        self.vn = (vc * lax.rsqrt(var + NORM_EPS) * self.lng + self.lnb).astype(BF16)
        self.tril = self.causal.astype(F32)
        low_half = self.col < HEAD_DIM
        self.ke = [None] * N_EXP
        self.ve = [None] * N_EXP
        for c in range(self.d_kv // LANES):
            kc = self._rope(self.load(self.o_k + c * LANES, self.o_k + (c + 1) * LANES)
                            .astype(F32))
            vcol = self.load(self.o_vv + c * LANES, self.o_vv + (c + 1) * LANES).astype(F32)
            kc_sw = pltpu.roll(kc, HEAD_DIM, 1)
            vcol_sw = pltpu.roll(vcol, HEAD_DIM, 1)
            for j in range(HEADS_PER_VREG):
                kv_head = c * HEADS_PER_VREG + j
                for o in range(HEADS_PER_VREG):
                    mask = low_half if o == 0 else jnp.logical_not(low_half)
                    e = kv_head * HEADS_PER_VREG + o
                    self.ke[e] = jnp.where(mask, kc if o == j else kc_sw, 0.0).astype(BF16)
                    self.ve[e] = jnp.where(mask, vcol if o == j else vcol_sw, 0.0).astype(BF16)

    def group_pair(self, g0):
        c0 = g0 * A_GROUP_W
        w = jnp.concatenate([(self.ws_ref[g0] * self.tril).astype(BF16),
                             (self.ws_ref[g0 + 1] * self.tril).astype(BF16)], axis=1)
        zero = jnp.zeros((CHUNK, A_GROUP_W), BF16)
        rhs = jnp.concatenate(
            [jnp.concatenate([self.vn[:, c0:c0 + A_GROUP_W], zero], axis=1),
             jnp.concatenate([zero, self.vn[:, c0 + A_GROUP_W:c0 + 2 * A_GROUP_W]], axis=1)],
            axis=0)
        s2 = jnp.dot(w, rhs, preferred_element_type=F32)
        for k in range(2):
            g = g0 + k
            cg = g * A_GROUP_W
            s = s2[:, k * A_GROUP_W:(k + 1) * A_GROUP_W] + self.bst[:, g:g + 1]
            u = self.load(self.o_u + cg, self.o_u + cg + A_GROUP_W).astype(F32)
            za = self.load(self.o_za + cg, self.o_za + cg + A_GROUP_W).astype(F32)
            self.store(cg, (u * s * _silu(za)).astype(BF16))

    def scores(self, p0):
        kv_head = (p0 * HEADS_PER_VREG) // Q_PER_KV
        kprev = self.kprev()
        q4 = jnp.concatenate(
            [self._rope(self.load(self.o_q + p * LANES, self.o_q + (p + 1) * LANES)
                        .astype(F32), query=True).astype(BF16)
             for p in range(p0, p0 + COLS_PER_KV)], axis=0)
        probs = [[] for _ in range(COLS_PER_KV)]
        for o in range(HEADS_PER_VREG):
            e = kv_head * HEADS_PER_VREG + o
            kband = jnp.concatenate([kprev[e], self.ke[e]], axis=0)
            s4 = lax.dot_general(q4, kband, (((1,), (1,)), ((), ())),
                                 preferred_element_type=F32)
            for a in range(COLS_PER_KV):
                s = s4[a * CHUNK:(a + 1) * CHUNK]
                h = (p0 + a) * HEADS_PER_VREG + o
                s_prev = s[:, :CHUNK]
                if self.has_prev is not True:
                    s_prev = jnp.where(self.has_prev, s_prev, -jnp.inf)
                comb = jnp.where(self.causal, s[:, CHUNK:], s_prev)
                sink = self.sink_ref[h] * LOG2E
                m = jnp.maximum(jnp.max(comb, axis=-1, keepdims=True), sink)
                pexp = jnp.exp2(comb - m)
                denom = jnp.sum(pexp, axis=-1, keepdims=True) + jnp.exp2(sink - m)
                pn = (pexp * (1.0 / denom)).astype(BF16)
                zero = jnp.zeros_like(pn)
                probs[a].append(jnp.where(self.causal, zero, pn))
                probs[a].append(jnp.where(self.causal, pn, zero))
        self.probs[p0] = jnp.concatenate(
            [jnp.concatenate(pa, axis=1) for pa in probs], axis=0)

    def values(self, p0):
        kv_head = (p0 * HEADS_PER_VREG) // Q_PER_KV
        vprev = self.vprev()
        vband = []
        for o in range(HEADS_PER_VREG):
            e = kv_head * HEADS_PER_VREG + o
            vband.append(vprev[e])
            vband.append(self.ve[e])
        out4 = jnp.dot(self.probs.pop(p0), jnp.concatenate(vband, axis=0),
                       preferred_element_type=F32)
        for a in range(COLS_PER_KV):
            p = p0 + a
            zb = self.load(self.o_zb + p * LANES, self.o_zb + (p + 1) * LANES).astype(F32)
            self.store(self.d_a + p * LANES,
                       (out4[a * CHUNK:(a + 1) * CHUNK] * _silu(zb)).astype(BF16))


def _layer_kernel(sink_ref, x_ref, mod_ref, modf_ref, ng_ref, fg_ref,
                  rope_ref, lng_ref, lnb_ref, ws_ref, bst_ref, win32_ref, wout32_ref,
                  o_ref, win_scr, wout_scr, proj0_scr, proj1_scr, y_scr, xprev_scr, h_scr,
                  kprev_scr, vprev_scr,
                  *, d_model, d_in, d_a, d_b, d_kv, blocks_per_batch, n_blk):
    i = pl.program_id(0)

    @pl.when(i < W_TILES)
    def _():
        r_in = pl.multiple_of(i * win32_ref.shape[0], win32_ref.shape[0])
        win_scr[pl.ds(r_in, win32_ref.shape[0]), :] = win32_ref[...].astype(BF16)
        r_out = pl.multiple_of(i * wout32_ref.shape[0], wout32_ref.shape[0])
        wout_scr[pl.ds(r_out, wout32_ref.shape[0]), :] = wout32_ref[...].astype(BF16)

    step = i - W_TILES

    cos_ref = rope_ref.at[:, 0:LANES]
    sin_ref = rope_ref.at[:, LANES:2 * LANES]
    args = (sink_ref, x_ref, mod_ref, mod_ref, modf_ref, ng_ref, fg_ref, cos_ref, sin_ref,
            lng_ref, lnb_ref, ws_ref, bst_ref, win_scr, wout_scr, o_ref)
    scr = (y_scr, xprev_scr, h_scr, kprev_scr, vprev_scr)
    dims = dict(d_model=d_model, d_in=d_in, d_a=d_a, d_b=d_b, d_kv=d_kv,
                blocks_per_batch=blocks_per_batch, n_blk=n_blk)
    bufs = ((proj0_scr, proj1_scr), (proj1_scr, proj0_scr))
    inner = jnp.logical_and(step > 0, step < n_blk)

    @pl.when(step == 0)
    def _():
        kprev_scr[...] = jnp.zeros_like(kprev_scr)
        vprev_scr[...] = jnp.zeros_like(vprev_scr)
        _layer_step(step, *args, *bufs[0], *scr, **dims, do_bc=False)

    @pl.when(jnp.logical_and(inner, step % 2 == 0))
    def _():
        _layer_step(step, *args, *bufs[0], *scr, **dims)

    @pl.when(jnp.logical_and(inner, step % 2 == 1))
    def _():
        _layer_step(step, *args, *bufs[1], *scr, **dims)

    @pl.when(step == n_blk)
    def _():
        _layer_step(step, *args, *bufs[n_blk % 2], *scr, **dims, do_a=False)


def _layer_step(step, sink_ref, x_ref, moda_ref, modc_ref, modf_ref, ng_ref, fg_ref,
                cos_ref, sin_ref, lng_ref, lnb_ref, ws_ref, bst_ref, win_ref, wout_ref,
                o_ref, proj_w, proj_r, y_scr, xprev_scr, h_scr, kprev_scr, vprev_scr,
                *, d_model, d_in, d_a, d_b, d_kv, blocks_per_batch, n_blk,
                do_a=True, do_bc=True):
    blk_c = jnp.clip(step - 1, 0, n_blk - 1)
    first_in_batch = (blk_c % blocks_per_batch) == 0

    chunks = list(range(0, d_in, TN_DOT))
    carried = chunks[len(chunks) - N_CARRY:]
    if do_bc:
        for n0 in carried:
            acc = jnp.dot(h_scr[...], win_ref[:, n0:n0 + TN_DOT], preferred_element_type=F32)
            proj_r[:, n0:n0 + TN_DOT] = acc.astype(BF16)

    if do_a:
        x = x_ref[...]
        ms = jnp.mean(x * x, axis=-1, keepdims=True)
        row_a = pl.ds(jnp.minimum(step, n_blk - 1) // blocks_per_batch, 1)
        shift = moda_ref[row_a, 0:d_model]
        gain = ng_ref[...] * (1.0 + moda_ref[row_a, d_model:2 * d_model])
        h_scr[...] = (x * lax.rsqrt(ms + NORM_EPS) * gain + shift).astype(BF16)

    lng = lng_ref[...]
    lnb = lnb_ref[...]
    bst = bst_ref[...]
    n_sub = T_BLK // CHUNK
    blocks = []
    for sb in range(n_sub):
        r0 = sb * CHUNK

        def load(c0, c1, r0=r0):
            return proj_r[r0:r0 + CHUNK, c0:c1]

        def store(c0, val, r0=r0):
            y_scr[r0:r0 + CHUNK, c0:c0 + val.shape[1]] = val

        if sb == 0:
            kprev = lambda: [kprev_scr[e] for e in range(N_EXP)]
            vprev = lambda: [vprev_scr[e] for e in range(N_EXP)]
            has_prev = jnp.logical_not(first_in_batch)
        else:
            kprev = lambda b=blocks[sb - 1]: b.ke
            vprev = lambda b=blocks[sb - 1]: b.ve
            has_prev = True
        blocks.append(_MixerBlock(
            load, store, cos_ref[r0:r0 + CHUNK, :], sin_ref[r0:r0 + CHUNK, :], lng, lnb,
            ws_ref, bst, sink_ref, kprev, vprev, has_prev, d_a=d_a, d_b=d_b, d_kv=d_kv))

    def proj_chunk(n0):
        acc = jnp.dot(h_scr[...], win_ref[:, n0:n0 + TN_DOT], preferred_element_type=F32)
        proj_w[:, n0:n0 + TN_DOT] = acc.astype(BF16)

    chunks = chunks[:len(chunks) - N_CARRY]
    next_chunk = [0]

    def emit_chunks(n):
        for _ in range(n):
            if do_a and next_chunk[0] < len(chunks):
                proj_chunk(chunks[next_chunk[0]])
                next_chunk[0] += 1

    if not do_bc:
        emit_chunks(len(chunks))
        xprev_scr[...] = x
        return

    for b in blocks:
        b.prep()
    n_pairs = d_b // LANES
    slots = [(b, p0) for b in blocks for p0 in range(0, n_pairs, COLS_PER_KV)]
    group_pairs = [(b, g0) for b in blocks for g0 in range(0, A_GROUPS, 2)]
    pairs_per_slot = -(-len(group_pairs) // len(slots))
    slots[0][0].scores(slots[0][1])
    for j, (b, p0) in enumerate(slots):
        emit_chunks(1)
        b.values(p0)
        if j + 1 < len(slots):
            slots[j + 1][0].scores(slots[j + 1][1])
        for bg, g0 in group_pairs[j * pairs_per_slot:(j + 1) * pairs_per_slot]:
            bg.group_pair(g0)
    for e in range(N_EXP):
        kprev_scr[e] = blocks[-1].ke[e]
        vprev_scr[e] = blocks[-1].ve[e]
    emit_chunks(len(chunks) - next_chunk[0] - 1)

    row_c = pl.ds(blk_c // blocks_per_batch, 1)
    gate = modc_ref[row_c, 2 * d_model:3 * d_model]
    ssq = jnp.zeros((T_BLK, 1), F32)
    for n0 in range(0, d_model, TN_DOT):
        sl = slice(n0, n0 + TN_DOT)
        acc = jnp.dot(y_scr[...], wout_ref[:, sl], preferred_element_type=F32)
        xr = xprev_scr[:, sl] + gate[:, sl] * acc
        ssq = ssq + jnp.sum(xr * xr, axis=-1, keepdims=True)
        o_ref[:, sl] = xr
    emit_chunks(len(chunks))
    inv = lax.rsqrt(ssq * (1.0 / d_model) + NORM_EPS)
    shift_f = modf_ref[row_c, 0:d_model]
    scale_f = modf_ref[row_c, d_model:2 * d_model]
    o_ref[...] = o_ref[...] * inv * (fg_ref[...] * (1.0 + scale_f)) + shift_f

    if do_a:
        xprev_scr[...] = x


def _layer(x2, sinks, mod, mod_f, norm_g, final_g, rope_t, ln_g, ln_b, w_sp, b_sp_t,
           w_in, w_out, seq, d_a, d_b, d_kv):
    rows, d_model = x2.shape
    d_in = w_in.shape[-1]
    d_mix = d_a + d_b
    n_blk = rows // T_BLK
    bpb = seq // T_BLK
    assert d_model % W_TILES == 0 and d_mix % W_TILES == 0

    def blk_a(i):
        return jnp.clip(i - W_TILES, 0, n_blk - 1)

    def blk_c(i):
        return jnp.clip(i - W_TILES - 1, 0, n_blk - 1)

    def w_tile(i):
        return (jnp.minimum(i, W_TILES - 1), 0)

    const2 = lambda i: (0, 0)
    kern = functools.partial(_layer_kernel, d_model=d_model, d_in=d_in, d_a=d_a, d_b=d_b,
                             d_kv=d_kv, blocks_per_batch=bpb, n_blk=n_blk)
    return pl.pallas_call(
        kern,
        grid=(W_TILES + n_blk + 1,),
        in_specs=[
            pl.BlockSpec(memory_space=pltpu.SMEM),
            pl.BlockSpec((T_BLK, d_model), lambda i: (blk_a(i), 0)),
            pl.BlockSpec(mod.shape, const2),
            pl.BlockSpec(mod_f.shape, const2),
            pl.BlockSpec((1, d_model), const2),
            pl.BlockSpec((1, d_model), const2),
            pl.BlockSpec((T_BLK, 2 * LANES), lambda i: (blk_c(i) % bpb, 0)),
            pl.BlockSpec((1, d_a), const2),
            pl.BlockSpec((1, d_a), const2),
            pl.BlockSpec((A_GROUPS, CHUNK, CHUNK), lambda i: (0, 0, 0)),
            pl.BlockSpec((CHUNK, A_GROUPS), const2),
            pl.BlockSpec((d_model // W_TILES, d_in), w_tile),
            pl.BlockSpec((d_mix // W_TILES, d_model), w_tile),
        ],
        out_specs=pl.BlockSpec((T_BLK, d_model), lambda i: (blk_c(i), 0)),
        out_shape=jax.ShapeDtypeStruct((rows, d_model), F32),
        scratch_shapes=[
            pltpu.VMEM((d_model, d_in), BF16),
            pltpu.VMEM((d_mix, d_model), BF16),
            pltpu.VMEM((T_BLK, d_in), BF16),
            pltpu.VMEM((T_BLK, d_in), BF16),
            pltpu.VMEM((T_BLK, d_mix), BF16),
            pltpu.VMEM((T_BLK, d_model), F32),
            pltpu.VMEM((T_BLK, d_model), BF16),
            pltpu.VMEM((N_EXP, CHUNK, LANES), BF16),
            pltpu.VMEM((N_EXP, CHUNK, LANES), BF16),
        ],
        compiler_params=pltpu.CompilerParams(
            dimension_semantics=("arbitrary",), vmem_limit_bytes=VMEM_LIMIT_LAYER),
        name="layer",
    )(sinks, x2, mod, mod_f, norm_g.reshape(1, d_model), final_g.reshape(1, d_model),
      rope_t, ln_g, ln_b, w_sp, b_sp_t, w_in, w_out)


def _rope_tables(seq):
    half = HEAD_DIM // 2
    inv_freq = ROPE_THETA ** (-np.arange(0, HEAD_DIM, 2, dtype=np.float64) / HEAD_DIM)
    ang = np.arange(seq, dtype=np.float64)[:, None] * inv_freq[None, :]
    cos = np.cos(ang)
    sin = np.sin(ang)
    cos_t = np.tile(cos, (1, LANES // half))
    sin_t = np.tile(np.concatenate([-sin, sin], axis=1), (1, HEADS_PER_VREG))
    return jnp.asarray(np.concatenate([cos_t, sin_t], axis=1), F32)


def kernel(x, c, w_ada, b_ada, norm_g, w_in, ln_v_g, ln_v_b, w_spatial, b_spatial, sinks,
           w_out, w_ada_final, b_ada_final, final_norm_g):
    bsz, seq, d_model = x.shape
    assert w_ada.shape[0] == 1, "single-layer stack"
    d_a = ln_v_g.shape[-1]
    d_mix = w_out.shape[-2]
    d_b = d_mix - d_a
    d_kv = N_KV_HEADS * HEAD_DIM
    d_in = w_in.shape[-1]
    assert d_in == 3 * d_a + 2 * d_b + 2 * d_kv
    assert d_a == A_GROUPS * A_GROUP_W and d_b == N_KV_HEADS * Q_PER_KV * HEAD_DIM
    assert seq % T_BLK == 0 and T_BLK % CHUNK == 0

    x2 = x.reshape(bsz * seq, d_model)
    mod, mod_f = _ada_mod(c, w_ada, b_ada, w_ada_final, b_ada_final)
    out = _layer(x2, sinks.reshape(-1), mod, mod_f, norm_g, final_norm_g, _rope_tables(seq),
                 ln_v_g.reshape(1, d_a), ln_v_b.reshape(1, d_a),
                 w_spatial.reshape(A_GROUPS, CHUNK, CHUNK),
                 b_spatial.reshape(A_GROUPS, CHUNK).T,
                 w_in.reshape(d_model, d_in), w_out.reshape(d_mix, d_model),
                 seq, d_a, d_b, d_kv)
    return out.reshape(bsz, seq, d_model)
```

```python
import functools

import jax
import jax.numpy as jnp
import numpy as np
from jax import lax
from jax.experimental import pallas as pl
from jax.experimental.pallas import tpu as pltpu

F32 = jnp.float32
BF16 = jnp.bfloat16

CHUNK = 128
A_GROUPS = 8
A_GROUP_W = 128
HEAD_DIM = 64
N_KV_HEADS = 4
Q_PER_KV = 4
ROPE_THETA = 10000.0
NORM_EPS = 1e-5
LOG2E = 1.4426950408889634

LANES = 128
HEADS_PER_VREG = LANES // HEAD_DIM
N_EXP = N_KV_HEADS * HEADS_PER_VREG
COLS_PER_KV = Q_PER_KV // HEADS_PER_VREG

T_BLK = 256
TN_DOT = 512
N_CARRY = 2
W_TILES = 16
TN_ADA = 1024
ADA_ROW_SPLITS = 8
ADA_BUFFERS = 3

MIB = 1024 * 1024
V7X_VMEM_BYTES = 64 * MIB
VMEM_LIMIT_ADA = 40 * MIB
VMEM_LIMIT_LAYER = V7X_VMEM_BYTES - MIB


def _silu(z):
    hz = 0.5 * z
    return hz + hz * jnp.tanh(hz)


def _ada_kernel(c_ref, wa_hbm, wf_hbm, ba_hbm, bf_hbm, oa_hbm, of_hbm, ca_scr):
    c = c_ref[...]
    ca_scr[...] = (c * (1.0 / (1.0 + jnp.exp(-c)))).astype(BF16)
    bsz, d = ca_scr.shape
    part = d // ADA_ROW_SPLITS

    def tile(*refs):
        ws = refs[:ADA_ROW_SPLITS]
        b_ref, o_ref = refs[ADA_ROW_SPLITS:]
        acc = b_ref[...]
        for k, w in enumerate(ws):
            acc = acc + jnp.dot(ca_scr[:, k * part:(k + 1) * part], w[...].astype(BF16),
                                preferred_element_type=F32)
        o_ref[...] = acc

    for w_hbm, b_hbm, o_hbm in ((wa_hbm, ba_hbm, oa_hbm), (wf_hbm, bf_hbm, of_hbm)):
        pltpu.emit_pipeline(
            tile,
            grid=(w_hbm.shape[1] // TN_ADA,),
            in_specs=[
                *[pl.BlockSpec((part, TN_ADA), lambda j, k=k: (k, j),
                               pipeline_mode=pl.Buffered(ADA_BUFFERS))
                  for k in range(ADA_ROW_SPLITS)],
                pl.BlockSpec((1, TN_ADA), lambda j: (0, j)),
            ],
            out_specs=[pl.BlockSpec((bsz, TN_ADA), lambda j: (0, j))],
        )(*([w_hbm] * ADA_ROW_SPLITS), b_hbm, o_hbm)


def _ada_mod(c, w_a, b_a, w_f, b_f):
    bsz, d = c.shape
    n_a, n_f = w_a.shape[-1], w_f.shape[-1]
    any_spec = pl.BlockSpec(memory_space=pl.ANY)
    return pl.pallas_call(
        _ada_kernel,
        in_specs=[pl.BlockSpec(memory_space=pltpu.VMEM)] + [any_spec] * 4,
        out_specs=[any_spec, any_spec],
        out_shape=[jax.ShapeDtypeStruct((bsz, n_a), F32),
                   jax.ShapeDtypeStruct((bsz, n_f), F32)],
        scratch_shapes=[pltpu.VMEM((bsz, d), BF16)],
        compiler_params=pltpu.CompilerParams(vmem_limit_bytes=VMEM_LIMIT_ADA),
        name="ada_mod",
    )(c, w_a.reshape(d, n_a), w_f.reshape(d, n_f), b_a.reshape(1, n_a), b_f.reshape(1, n_f))


class _MixerBlock:
    def __init__(self, load, store, cos, sin, lng, lnb, ws_ref, bst, sink_ref, kprev, vprev,
                 has_prev, *, d_a, d_b, d_kv):
        self.load, self.store = load, store
        self.cos, self.sin, self.lng, self.lnb = cos, sin, lng, lnb
        q_scale = LOG2E * HEAD_DIM ** -0.5
        self.cos_q, self.sin_q = cos * q_scale, sin * q_scale
        self.ws_ref, self.bst, self.sink_ref = ws_ref, bst, sink_ref
        self.kprev, self.vprev, self.has_prev = kprev, vprev, has_prev
        self.d_a, self.d_b, self.d_kv = d_a, d_b, d_kv
        self.o_u, self.o_v, self.o_za = 0, d_a, 2 * d_a
        self.o_q = 3 * d_a
        self.o_k = self.o_q + d_b
        self.o_vv = self.o_k + d_kv
        self.o_zb = self.o_vv + d_kv
        row = lax.broadcasted_iota(jnp.int32, (CHUNK, CHUNK), 0)
        col = lax.broadcasted_iota(jnp.int32, (CHUNK, CHUNK), 1)
        self.col = col
        self.causal = col <= row
        self.first_half = (col & (HEAD_DIM - 1)) < (HEAD_DIM // 2)
        self.probs = {}

    def _rope(self, xv, query=False):
        cos, sin = (self.cos_q, self.sin_q) if query else (self.cos, self.sin)
        rot = jnp.where(self.first_half,
                        pltpu.roll(xv, LANES - HEAD_DIM // 2, 1),
                        pltpu.roll(xv, HEAD_DIM // 2, 1))
        return xv * cos + rot * sin

    def prep(self):
        va = self.load(self.o_v, self.o_v + self.d_a).astype(F32)
        mu = jnp.mean(va, axis=-1, keepdims=True)
        vc = va - mu
        var = jnp.mean(vc * vc, axis=-1, keepdims=True)
        self.vn = (vc * lax.rsqrt(var + NORM_EPS) * self.lng + self.lnb).astype(BF16)
        self.tril = self.causal.astype(F32)
        low_half = self.col < HEAD_DIM
        self.ke = [None] * N_EXP
        self.ve = [None] * N_EXP
        for c in range(self.d_kv // LANES):
            kc = self._rope(self.load(self.o_k + c * LANES, self.o_k + (c + 1) * LANES)
                            .astype(F32))
            vcol = self.load(self.o_vv + c * LANES, self.o_vv + (c + 1) * LANES).astype(F32)
            kc_sw = pltpu.roll(kc, HEAD_DIM, 1)
            vcol_sw = pltpu.roll(vcol, HEAD_DIM, 1)
            for j in range(HEADS_PER_VREG):
                kv_head = c * HEADS_PER_VREG + j
                for o in range(HEADS_PER_VREG):
                    mask = low_half if o == 0 else jnp.logical_not(low_half)
                    e = kv_head * HEADS_PER_VREG + o
                    self.ke[e] = jnp.where(mask, kc if o == j else kc_sw, 0.0).astype(BF16)
                    self.ve[e] = jnp.where(mask, vcol if o == j else vcol_sw, 0.0).astype(BF16)

    def group_pair(self, g0):
        c0 = g0 * A_GROUP_W
        w = jnp.concatenate([(self.ws_ref[g0] * self.tril).astype(BF16),
                             (self.ws_ref[g0 + 1] * self.tril).astype(BF16)], axis=1)
        zero = jnp.zeros((CHUNK, A_GROUP_W), BF16)
        rhs = jnp.concatenate(
            [jnp.concatenate([self.vn[:, c0:c0 + A_GROUP_W], zero], axis=1),
             jnp.concatenate([zero, self.vn[:, c0 + A_GROUP_W:c0 + 2 * A_GROUP_W]], axis=1)],
            axis=0)
        s2 = jnp.dot(w, rhs, preferred_element_type=F32)
        for k in range(2):
            g = g0 + k
            cg = g * A_GROUP_W
            s = s2[:, k * A_GROUP_W:(k + 1) * A_GROUP_W] + self.bst[:, g:g + 1]
            u = self.load(self.o_u + cg, self.o_u + cg + A_GROUP_W).astype(F32)
            za = self.load(self.o_za + cg, self.o_za + cg + A_GROUP_W).astype(F32)
            self.store(cg, (u * s * _silu(za)).astype(BF16))

    def scores(self, p0):
        kv_head = (p0 * HEADS_PER_VREG) // Q_PER_KV
        kprev = self.kprev()
        q4 = jnp.concatenate(
            [self._rope(self.load(self.o_q + p * LANES, self.o_q + (p + 1) * LANES)
                        .astype(F32), query=True).astype(BF16)
             for p in range(p0, p0 + COLS_PER_KV)], axis=0)
        probs = [[] for _ in range(COLS_PER_KV)]
        for o in range(HEADS_PER_VREG):
            e = kv_head * HEADS_PER_VREG + o
            kband = jnp.concatenate([kprev[e], self.ke[e]], axis=0)
            s4 = lax.dot_general(q4, kband, (((1,), (1,)), ((), ())),
                                 preferred_element_type=F32)
            for a in range(COLS_PER_KV):
                s = s4[a * CHUNK:(a + 1) * CHUNK]
                h = (p0 + a) * HEADS_PER_VREG + o
                s_prev = s[:, :CHUNK]
                if self.has_prev is not True:
                    s_prev = jnp.where(self.has_prev, s_prev, -jnp.inf)
                comb = jnp.where(self.causal, s[:, CHUNK:], s_prev)
                sink = self.sink_ref[h] * LOG2E
                m = jnp.maximum(jnp.max(comb, axis=-1, keepdims=True), sink)
                pexp = jnp.exp2(comb - m)
                denom = jnp.sum(pexp, axis=-1, keepdims=True) + jnp.exp2(sink - m)
                pn = (pexp * (1.0 / denom)).astype(BF16)
                zero = jnp.zeros_like(pn)
                probs[a].append(jnp.where(self.causal, zero, pn))
                probs[a].append(jnp.where(self.causal, pn, zero))
        self.probs[p0] = jnp.concatenate(
            [jnp.concatenate(pa, axis=1) for pa in probs], axis=0)

    def values(self, p0):
        kv_head = (p0 * HEADS_PER_VREG) // Q_PER_KV
        vprev = self.vprev()
        vband = []
        for o in range(HEADS_PER_VREG):
            e = kv_head * HEADS_PER_VREG + o
            vband.append(vprev[e])
            vband.append(self.ve[e])
        out4 = jnp.dot(self.probs.pop(p0), jnp.concatenate(vband, axis=0),
                       preferred_element_type=F32)
        for a in range(COLS_PER_KV):
            p = p0 + a
            zb = self.load(self.o_zb + p * LANES, self.o_zb + (p + 1) * LANES).astype(F32)
            self.store(self.d_a + p * LANES,
                       (out4[a * CHUNK:(a + 1) * CHUNK] * _silu(zb)).astype(BF16))


def _layer_kernel(sink_ref, x_ref, moda_ref, modc_ref, modf_ref, ng_ref, fg_ref,
                  cos_ref, sin_ref, lng_ref, lnb_ref, ws_ref, bst_ref, win32_ref, wout32_ref,
                  o_ref, win_scr, wout_scr, proj0_scr, proj1_scr, y_scr, xprev_scr, h_scr,
                  kprev_scr, vprev_scr,
                  *, d_model, d_in, d_a, d_b, d_kv, blocks_per_batch, n_blk):
    i = pl.program_id(0)

    @pl.when(i < W_TILES)
    def _():
        r_in = pl.multiple_of(i * win32_ref.shape[0], win32_ref.shape[0])
        win_scr[pl.ds(r_in, win32_ref.shape[0]), :] = win32_ref[...].astype(BF16)
        r_out = pl.multiple_of(i * wout32_ref.shape[0], wout32_ref.shape[0])
        wout_scr[pl.ds(r_out, wout32_ref.shape[0]), :] = wout32_ref[...].astype(BF16)

    step = i - W_TILES

    args = (sink_ref, x_ref, moda_ref, modc_ref, modf_ref, ng_ref, fg_ref, cos_ref, sin_ref,
            lng_ref, lnb_ref, ws_ref, bst_ref, win_scr, wout_scr, o_ref)
    scr = (y_scr, xprev_scr, h_scr, kprev_scr, vprev_scr)
    dims = dict(d_model=d_model, d_in=d_in, d_a=d_a, d_b=d_b, d_kv=d_kv,
                blocks_per_batch=blocks_per_batch, n_blk=n_blk)
    bufs = ((proj0_scr, proj1_scr), (proj1_scr, proj0_scr))
    inner = jnp.logical_and(step > 0, step < n_blk)

    @pl.when(step == 0)
    def _():
        kprev_scr[...] = jnp.zeros_like(kprev_scr)
        vprev_scr[...] = jnp.zeros_like(vprev_scr)
        _layer_step(step, *args, *bufs[0], *scr, **dims, do_bc=False)

    @pl.when(jnp.logical_and(inner, step % 2 == 0))
    def _():
        _layer_step(step, *args, *bufs[0], *scr, **dims)

    @pl.when(jnp.logical_and(inner, step % 2 == 1))
    def _():
        _layer_step(step, *args, *bufs[1], *scr, **dims)

    @pl.when(step == n_blk)
    def _():
        _layer_step(step, *args, *bufs[n_blk % 2], *scr, **dims, do_a=False)


def _layer_step(step, sink_ref, x_ref, moda_ref, modc_ref, modf_ref, ng_ref, fg_ref,
                cos_ref, sin_ref, lng_ref, lnb_ref, ws_ref, bst_ref, win_ref, wout_ref,
                o_ref, proj_w, proj_r, y_scr, xprev_scr, h_scr, kprev_scr, vprev_scr,
                *, d_model, d_in, d_a, d_b, d_kv, blocks_per_batch, n_blk,
                do_a=True, do_bc=True):
    blk_c = jnp.clip(step - 1, 0, n_blk - 1)
    first_in_batch = (blk_c % blocks_per_batch) == 0

    chunks = list(range(0, d_in, TN_DOT))
    carried = chunks[len(chunks) - N_CARRY:]
    if do_bc:
        for n0 in carried:
            acc = jnp.dot(h_scr[...], win_ref[:, n0:n0 + TN_DOT], preferred_element_type=F32)
            proj_r[:, n0:n0 + TN_DOT] = acc.astype(BF16)

    if do_a:
        x = x_ref[...]
        ms = jnp.mean(x * x, axis=-1, keepdims=True)
        row_a = pl.ds(jnp.minimum(step, n_blk - 1) // blocks_per_batch, 1)
        shift = moda_ref[row_a, 0:d_model]
        gain = ng_ref[...] * (1.0 + moda_ref[row_a, d_model:2 * d_model])
        h_scr[...] = (x * lax.rsqrt(ms + NORM_EPS) * gain + shift).astype(BF16)

    lng = lng_ref[...]
    lnb = lnb_ref[...]
    bst = bst_ref[...]
    n_sub = T_BLK // CHUNK
    blocks = []
    for sb in range(n_sub):
        r0 = sb * CHUNK

        def load(c0, c1, r0=r0):
            return proj_r[r0:r0 + CHUNK, c0:c1]

        def store(c0, val, r0=r0):
            y_scr[r0:r0 + CHUNK, c0:c0 + val.shape[1]] = val

        if sb == 0:
            kprev = lambda: [kprev_scr[e] for e in range(N_EXP)]
            vprev = lambda: [vprev_scr[e] for e in range(N_EXP)]
            has_prev = jnp.logical_not(first_in_batch)
        else:
            kprev = lambda b=blocks[sb - 1]: b.ke
            vprev = lambda b=blocks[sb - 1]: b.ve
            has_prev = True
        blocks.append(_MixerBlock(
            load, store, cos_ref[r0:r0 + CHUNK, :], sin_ref[r0:r0 + CHUNK, :], lng, lnb,
            ws_ref, bst, sink_ref, kprev, vprev, has_prev, d_a=d_a, d_b=d_b, d_kv=d_kv))

    def proj_chunk(n0):
        acc = jnp.dot(h_scr[...], win_ref[:, n0:n0 + TN_DOT], preferred_element_type=F32)
        proj_w[:, n0:n0 + TN_DOT] = acc.astype(BF16)

    chunks = chunks[:len(chunks) - N_CARRY]
    next_chunk = [0]

    def emit_chunks(n):
        for _ in range(n):
            if do_a and next_chunk[0] < len(chunks):
                proj_chunk(chunks[next_chunk[0]])
                next_chunk[0] += 1

    if not do_bc:
        emit_chunks(len(chunks))
        xprev_scr[...] = x
        return

    for b in blocks:
        b.prep()
    n_pairs = d_b // LANES
    slots = [(b, p0) for b in blocks for p0 in range(0, n_pairs, COLS_PER_KV)]
    group_pairs = [(b, g0) for b in blocks for g0 in range(0, A_GROUPS, 2)]
    pairs_per_slot = -(-len(group_pairs) // len(slots))
    slots[0][0].scores(slots[0][1])
    for j, (b, p0) in enumerate(slots):
        emit_chunks(1)
        b.values(p0)
        if j + 1 < len(slots):
            slots[j + 1][0].scores(slots[j + 1][1])
        for bg, g0 in group_pairs[j * pairs_per_slot:(j + 1) * pairs_per_slot]:
            bg.group_pair(g0)
    for e in range(N_EXP):
        kprev_scr[e] = blocks[-1].ke[e]
        vprev_scr[e] = blocks[-1].ve[e]
    emit_chunks(len(chunks) - next_chunk[0] - 1)

    row_c = pl.ds(blk_c // blocks_per_batch, 1)
    gate = modc_ref[row_c, 2 * d_model:3 * d_model]
    ssq = jnp.zeros((T_BLK, 1), F32)
    for n0 in range(0, d_model, TN_DOT):
        sl = slice(n0, n0 + TN_DOT)
        acc = jnp.dot(y_scr[...], wout_ref[:, sl], preferred_element_type=F32)
        xr = xprev_scr[:, sl] + gate[:, sl] * acc
        ssq = ssq + jnp.sum(xr * xr, axis=-1, keepdims=True)
        o_ref[:, sl] = xr
    emit_chunks(len(chunks))
    inv = lax.rsqrt(ssq * (1.0 / d_model) + NORM_EPS)
    shift_f = modf_ref[row_c, 0:d_model]
    scale_f = modf_ref[row_c, d_model:2 * d_model]
    o_ref[...] = o_ref[...] * inv * (fg_ref[...] * (1.0 + scale_f)) + shift_f

    if do_a:
        xprev_scr[...] = x


def _layer(x2, sinks, mod, mod_f, norm_g, final_g, cos_t, sin_t, ln_g, ln_b, w_sp, b_sp_t,
           w_in, w_out, seq, d_a, d_b, d_kv):
    rows, d_model = x2.shape
    d_in = w_in.shape[-1]
    d_mix = d_a + d_b
    n_blk = rows // T_BLK
    bpb = seq // T_BLK
    assert d_model % W_TILES == 0 and d_mix % W_TILES == 0

    def blk_a(i):
        return jnp.clip(i - W_TILES, 0, n_blk - 1)

    def blk_c(i):
        return jnp.clip(i - W_TILES - 1, 0, n_blk - 1)

    def w_tile(i):
        return (jnp.minimum(i, W_TILES - 1), 0)

    const2 = lambda i: (0, 0)
    kern = functools.partial(_layer_kernel, d_model=d_model, d_in=d_in, d_a=d_a, d_b=d_b,
                             d_kv=d_kv, blocks_per_batch=bpb, n_blk=n_blk)
    return pl.pallas_call(
        kern,
        grid=(W_TILES + n_blk + 1,),
        in_specs=[
            pl.BlockSpec(memory_space=pltpu.SMEM),
            pl.BlockSpec((T_BLK, d_model), lambda i: (blk_a(i), 0)),
            pl.BlockSpec(mod.shape, const2),
            pl.BlockSpec(mod.shape, const2),
            pl.BlockSpec(mod_f.shape, const2),
            pl.BlockSpec((1, d_model), const2),
            pl.BlockSpec((1, d_model), const2),
            pl.BlockSpec((T_BLK, LANES), lambda i: (blk_c(i) % bpb, 0)),
            pl.BlockSpec((T_BLK, LANES), lambda i: (blk_c(i) % bpb, 0)),
            pl.BlockSpec((1, d_a), const2),
            pl.BlockSpec((1, d_a), const2),
            pl.BlockSpec((A_GROUPS, CHUNK, CHUNK), lambda i: (0, 0, 0)),
            pl.BlockSpec((CHUNK, A_GROUPS), const2),
            pl.BlockSpec((d_model // W_TILES, d_in), w_tile),
            pl.BlockSpec((d_mix // W_TILES, d_model), w_tile),
        ],
        out_specs=pl.BlockSpec((T_BLK, d_model), lambda i: (blk_c(i), 0)),
        out_shape=jax.ShapeDtypeStruct((rows, d_model), F32),
        scratch_shapes=[
            pltpu.VMEM((d_model, d_in), BF16),
            pltpu.VMEM((d_mix, d_model), BF16),
            pltpu.VMEM((T_BLK, d_in), BF16),
            pltpu.VMEM((T_BLK, d_in), BF16),
            pltpu.VMEM((T_BLK, d_mix), BF16),
            pltpu.VMEM((T_BLK, d_model), F32),
            pltpu.VMEM((T_BLK, d_model), BF16),
            pltpu.VMEM((N_EXP, CHUNK, LANES), BF16),
            pltpu.VMEM((N_EXP, CHUNK, LANES), BF16),
        ],
        compiler_params=pltpu.CompilerParams(
            dimension_semantics=("arbitrary",), vmem_limit_bytes=VMEM_LIMIT_LAYER),
        name="layer",
    )(sinks, x2, mod, mod, mod_f, norm_g.reshape(1, d_model), final_g.reshape(1, d_model),
      cos_t, sin_t, ln_g, ln_b, w_sp, b_sp_t, w_in, w_out)


def _rope_tables(seq):
    half = HEAD_DIM // 2
    inv_freq = ROPE_THETA ** (-np.arange(0, HEAD_DIM, 2, dtype=np.float64) / HEAD_DIM)
    ang = np.arange(seq, dtype=np.float64)[:, None] * inv_freq[None, :]
    cos = np.cos(ang)
    sin = np.sin(ang)
    cos_t = np.tile(cos, (1, LANES // half))
    sin_t = np.tile(np.concatenate([-sin, sin], axis=1), (1, HEADS_PER_VREG))
    return jnp.asarray(cos_t, F32), jnp.asarray(sin_t, F32)


def kernel(x, c, w_ada, b_ada, norm_g, w_in, ln_v_g, ln_v_b, w_spatial, b_spatial, sinks,
           w_out, w_ada_final, b_ada_final, final_norm_g):
    bsz, seq, d_model = x.shape
    assert w_ada.shape[0] == 1, "single-layer stack"
    d_a = ln_v_g.shape[-1]
    d_mix = w_out.shape[-2]
    d_b = d_mix - d_a
    d_kv = N_KV_HEADS * HEAD_DIM
    d_in = w_in.shape[-1]
    assert d_in == 3 * d_a + 2 * d_b + 2 * d_kv
    assert d_a == A_GROUPS * A_GROUP_W and d_b == N_KV_HEADS * Q_PER_KV * HEAD_DIM
    assert seq % T_BLK == 0 and T_BLK % CHUNK == 0

    x2 = x.reshape(bsz * seq, d_model)
    mod, mod_f = _ada_mod(c, w_ada, b_ada, w_ada_final, b_ada_final)
    cos_t, sin_t = _rope_tables(seq)
    out = _layer(x2, sinks.reshape(-1), mod, mod_f, norm_g, final_norm_g, cos_t, sin_t,
                 ln_v_g.reshape(1, d_a), ln_v_b.reshape(1, d_a),
                 w_spatial.reshape(A_GROUPS, CHUNK, CHUNK),
                 b_spatial.reshape(A_GROUPS, CHUNK).T,
                 w_in.reshape(d_model, d_in), w_out.reshape(d_mix, d_model),
                 seq, d_a, d_b, d_kv)
    return out.reshape(bsz, seq, d_model)
```
